```python
import math
import jax, jax.numpy as jnp
from jax import lax
import numpy as np

D_MODEL = 1024
BATCH = 8
SEQ = 2048
DEPTH = 2

GRID_W = 64
CTX_LEN = 256
HEAD_DIM = 64
BLOCK = 128
A_Q_HEADS = 8
A_KV_HEADS = 2
WINDOW = 128
B_HEADS = 4
B_V_DIM = 2 * HEAD_DIM
A_Q_W = A_Q_HEADS * HEAD_DIM
A_KV_W = A_KV_HEADS * HEAD_DIM
B_QK_W = B_HEADS * 2 * HEAD_DIM
B_V_W = B_HEADS * B_V_DIM
ATTN_IN = A_Q_W + 2 * A_KV_W + 2 * B_QK_W + B_V_W
ATTN_OUT = A_Q_W + B_V_W
S5_WIDTH = D_MODEL
S5_GROUP = 16
S5_GROUPS = S5_WIDTH // S5_GROUP
S5_STATE = 64
D_FF = 4 * D_MODEL
ROPE_BASE = 10000.0
EPS = 1e-6
NEG_INF = -1e30
F32 = jnp.float32

kernel_name = "hybrid_prefix_dit_swa_diff_s5"


def rms_norm(x, g):
    xf = x.astype(F32)
    y = xf * lax.rsqrt(jnp.mean(xf * xf, axis=-1, keepdims=True) + EPS)
    return (y * g.astype(F32)).astype(x.dtype)


def axial_rope_tables(rows_n):
    row = jnp.repeat(jnp.arange(rows_n, dtype=F32), GRID_W)
    col = jnp.tile(jnp.arange(GRID_W, dtype=F32), rows_n)
    n_freq = HEAD_DIM // 4
    inv = ROPE_BASE ** (-jnp.arange(n_freq, dtype=F32) / n_freq)
    ang = jnp.concatenate([row[:, None] * inv, col[:, None] * inv], axis=-1)
    return jnp.cos(ang), jnp.sin(ang)


def apply_rope(x, cos, sin):
    shape = (cos.shape[0],) + (1,) * (x.ndim - 3) + (cos.shape[1],)
    cs = cos.reshape(shape).astype(x.dtype)
    sn = sin.reshape(shape).astype(x.dtype)
    x1, x2 = jnp.split(x, 2, axis=-1)
    return jnp.concatenate([x1 * cs - x2 * sn, x2 * cs + x1 * sn], axis=-1)


def split_attn_proj(z):
    bn, t, _ = z.shape
    cuts = np.cumsum([A_Q_W, A_KV_W, A_KV_W, B_QK_W, B_QK_W]).tolist()
    qa, ka, va, qb, kb, vb = jnp.split(z, cuts, axis=-1)
    return (qa.reshape(bn, t, A_Q_HEADS, HEAD_DIM),
            ka.reshape(bn, t, A_KV_HEADS, HEAD_DIM),
            va.reshape(bn, t, A_KV_HEADS, HEAD_DIM),
            qb.reshape(bn, t, B_HEADS, 2, HEAD_DIM),
            kb.reshape(bn, t, B_HEADS, 2, HEAD_DIM),
            vb.reshape(bn, t, B_HEADS, B_V_DIM))


def window_attention(q, k, v, kc, vc, sink):
    bn, L, hq, d = q.shape
    hkv = k.shape[2]
    g = hq // hkv
    nb = L // BLOCK
    scale = d ** -0.5
    qb = q.reshape(bn, nb, BLOCK, hkv, g, d)
    pad = ((0, 0), (BLOCK, BLOCK), (0, 0), (0, 0))
    kp = jnp.pad(k, pad).reshape(bn, nb + 2, BLOCK, hkv, d)
    vp = jnp.pad(v, pad).reshape(bn, nb + 2, BLOCK, hkv, d)
    kw = jnp.concatenate([kp[:, :-2], kp[:, 1:-1], kp[:, 2:]], axis=2)
    vw = jnp.concatenate([vp[:, :-2], vp[:, 1:-1], vp[:, 2:]], axis=2)
    s_win = jnp.einsum('bnqhgd,bnkhd->bnhgqk', qb, kw).astype(F32) * scale
    qi = jnp.arange(BLOCK)[:, None]
    kj = jnp.arange(3 * BLOCK)[None, :]
    jpos = jnp.arange(nb)[:, None, None] * BLOCK + kj[None] - BLOCK
    valid = (jnp.abs(kj - BLOCK - qi) <= WINDOW)[None] & (jpos >= 0) & (jpos < L)
    s_win = jnp.where(valid[None, :, None, None], s_win, NEG_INF)
    s_ctx = jnp.einsum('bnqhgd,bkhd->bnhgqk', qb, kc).astype(F32) * scale
    s_sink = jnp.broadcast_to(sink.astype(F32).reshape(1, 1, hkv, g, 1, 1), s_win.shape[:-1] + (1,))
    p = jax.nn.softmax(jnp.concatenate([s_win, s_ctx, s_sink], axis=-1), axis=-1).astype(v.dtype)
    nw = 3 * BLOCK
    nc = kc.shape[1]
    o = (jnp.einsum('bnhgqk,bnkhd->bnqhgd', p[..., :nw], vw)
         + jnp.einsum('bnhgqk,bkhd->bnqhgd', p[..., nw:nw + nc], vc))
    return o.reshape(bn, L, hq * d)


def gqa_sink_dense(q, k, v, sink):
    bn, t, hq, d = q.shape
    hkv = k.shape[2]
    g = hq // hkv
    qg = q.reshape(bn, t, hkv, g, d)
    s = jnp.einsum('bqhgd,bkhd->bhgqk', qg, k).astype(F32) * d ** -0.5
    sk = jnp.broadcast_to(sink.astype(F32).reshape(1, hkv, g, 1, 1), s.shape[:-1] + (1,))
    p = jax.nn.softmax(jnp.concatenate([s, sk], axis=-1), axis=-1)[..., :-1].astype(v.dtype)
    o = jnp.einsum('bhgqk,bkhd->bqhgd', p, v)
    return o.reshape(bn, t, hq * d)


def diff_attend(q, k, v, lam):
    s = jnp.einsum('bqhmd,bkhmd->bhmqk', q, k).astype(F32) * HEAD_DIM ** -0.5
    p = jax.nn.softmax(s, axis=-1)
    w = p[:, :, 0] - lam * p[:, :, 1]
    return jnp.einsum('bhqk,bkhe->bqhe', w.astype(v.dtype), v)


def diff_attention_latent(q, k, v, kc, vc, lam):
    bn, L, h, _, d = q.shape
    nb = L // BLOCK
    k_all = jnp.concatenate([kc, k], axis=1)
    v_all = jnp.concatenate([vc, v], axis=1)
    qb = jnp.moveaxis(q.reshape(bn, nb, BLOCK, h, 2, d), 1, 0)
    o = lax.map(lambda qblk: diff_attend(qblk, k_all, v_all, lam), qb)
    return jnp.moveaxis(o, 0, 1).reshape(bn, L, h, v.shape[-1])


def attn_mixer(h_lat, h_ctx, w_in, w_out, a_qn, a_kn, a_sink, b_qn, b_kn, lq1, lk1, lq2, lk2, subln,
               cos, sin, lambda_init, need_ctx):
    qa, ka, va, qb, kb, vb = split_attn_proj(h_lat @ w_in)
    qa_c, ka_c, va_c, qb_c, kb_c, vb_c = split_attn_proj(h_ctx @ w_in)
    qa = apply_rope(rms_norm(qa, a_qn), cos, sin)
    ka = apply_rope(rms_norm(ka, a_kn), cos, sin)
    qb = apply_rope(rms_norm(qb, b_qn), cos, sin)
    kb = apply_rope(rms_norm(kb, b_kn), cos, sin)
    ka_c = rms_norm(ka_c, a_kn)
    kb_c = rms_norm(kb_c, b_kn)
    lam = (jnp.exp(jnp.sum(lq1.astype(F32) * lk1.astype(F32)))
           - jnp.exp(jnp.sum(lq2.astype(F32) * lk2.astype(F32))) + lambda_init)

    def merge(ya, yb):
        yb = rms_norm(yb, subln) * (1.0 - lambda_init)
        yb = yb.reshape(yb.shape[0], yb.shape[1], B_V_W)
        return jnp.concatenate([ya, yb], axis=-1) @ w_out

    y_lat = merge(window_attention(qa, ka, va, ka_c, va_c, a_sink),
                  diff_attention_latent(qb, kb, vb, kb_c, vb_c, lam))
    y_ctx = None
    if need_ctx:
        qa_c = rms_norm(qa_c, a_qn)
        qb_c = rms_norm(qb_c, b_qn)
        y_ctx = merge(gqa_sink_dense(qa_c, ka_c, va_c, a_sink), diff_attend(qb_c, kb_c, vb_c, lam))
    return y_lat, y_ctx


def s5_discretize(lam_re, lam_im, log_step, b_re, b_im):
    lam_re = lam_re.astype(F32)
    lam_im = lam_im.astype(F32)
    b_re = b_re.astype(F32)
    b_im = b_im.astype(F32)
    dt = jnp.exp(log_step.astype(F32))[:, None]
    mag = jnp.exp(lam_re * dt)
    ab_re = mag * jnp.cos(lam_im * dt)
    ab_im = mag * jnp.sin(lam_im * dt)
    den = lam_re * lam_re + lam_im * lam_im
    nr = ab_re - 1.0
    f_re = (nr * lam_re + ab_im * lam_im) / den
    f_im = (ab_im * lam_re - nr * lam_im) / den
    bb_re = f_re[..., None] * b_re - f_im[..., None] * b_im
    bb_im = f_re[..., None] * b_im + f_im[..., None] * b_re
    return ab_re, ab_im, bb_re, bb_im


def _cplx_combine(e1, e2):
    a1r, a1i, b1r, b1i = e1
    a2r, a2i, b2r, b2i = e2
    return (a2r * a1r - a2i * a1i,
            a2r * a1i + a2i * a1r,
            a2r * b1r - a2i * b1i + b2r,
            a2r * b1i + a2i * b1r + b2i)


def s5_scan(u, ab_re, ab_im, bb_re, bb_im, h0, reverse):
    t = u.shape[1]
    bu_re = jnp.einsum('btgh,gph->btgp', u, bb_re)
    bu_im = jnp.einsum('btgh,gph->btgp', u, bb_im)
    if h0 is not None:
        h0_re, h0_im = h0
        t0 = t - 1 if reverse else 0
        bu_re = bu_re.at[:, t0].add(ab_re * h0_re - ab_im * h0_im)
        bu_im = bu_im.at[:, t0].add(ab_re * h0_im + ab_im * h0_re)
    a_re = jnp.broadcast_to(ab_re[None, None], (1, t) + ab_re.shape)
    a_im = jnp.broadcast_to(ab_im[None, None], (1, t) + ab_im.shape)
    _, _, h_re, h_im = lax.associative_scan(_cplx_combine, (a_re, a_im, bu_re, bu_im), reverse=reverse, axis=1)
    return h_re, h_im


def s5_readout(h_re, h_im, c_re, c_im):
    return jnp.einsum('btgp,ghp->btgh', h_re, c_re) - jnp.einsum('btgp,ghp->btgh', h_im, c_im)


def s5_mixer(h_lat, h_ctx, w_in, lam_re, lam_im, log_step, b_re, b_im, c_re, c_im, d_skip, glu_w, glu_b, w_out,
             need_ctx):
    bn, L, _ = h_lat.shape
    nc = h_ctx.shape[1]
    u_lat = (h_lat @ w_in).astype(F32)
    u_ctx = (h_ctx @ w_in).astype(F32)
    ul = u_lat.reshape(bn, L, S5_GROUPS, S5_GROUP)
    uc = u_ctx.reshape(bn, nc, S5_GROUPS, S5_GROUP)
    dsk = d_skip.astype(F32)
    y_lat = u_lat * dsk
    y_ctx = u_ctx * dsk if need_ctx else None
    for direction in range(2):
        reverse = direction == 1
        ab_re, ab_im, bb_re, bb_im = s5_discretize(lam_re[direction], lam_im[direction], log_step[direction],
                                                   b_re[direction], b_im[direction])
        cr = c_re[direction].astype(F32)
        ci = c_im[direction].astype(F32)
        hc_re, hc_im = s5_scan(uc, ab_re, ab_im, bb_re, bb_im, None, reverse)
        t_end = 0 if reverse else nc - 1
        hl_re, hl_im = s5_scan(ul, ab_re, ab_im, bb_re, bb_im, (hc_re[:, t_end], hc_im[:, t_end]), reverse)
        y_lat = y_lat + s5_readout(hl_re, hl_im, cr, ci).reshape(bn, L, S5_WIDTH)
        if need_ctx:
            y_ctx = y_ctx + s5_readout(hc_re, hc_im, cr, ci).reshape(bn, nc, S5_WIDTH)

    def out_map(y):
        g = jax.nn.gelu(y).astype(h_lat.dtype)
        g = g * jax.nn.sigmoid(g @ glu_w + glu_b)
        return g @ w_out

    return out_map(y_lat), (out_map(y_ctx) if need_ctx else None)


def sq_relu_mlp(h, w1, w2):
    return jnp.square(jax.nn.relu(h @ w1)) @ w2


def setup_inputs(seed: int = 0) -> dict:
    key = jax.random.key(seed)
    ks = iter(jax.random.split(key, 40))
    n_attn = (DEPTH + 1) // 2
    n_ssm = DEPTH // 2

    def nrm(shape, scale):
        return scale * jax.random.normal(next(ks), shape, F32)

    def gain(shape):
        return 1.0 + nrm(shape, 0.02)

    n_idx = jnp.arange(S5_STATE, dtype=F32)
    s5_state_shape = (n_ssm, 2, S5_GROUPS, S5_STATE)
    return {
        "x": nrm((BATCH, SEQ, D_MODEL), 1.0),
        "c": nrm((BATCH, D_MODEL), 1.0),
        "ctx": nrm((BATCH, CTX_LEN, D_MODEL), 1.0),
        "c_ctx": nrm((D_MODEL,), 1.0),
        "norm1_g": gain((DEPTH, D_MODEL)),
        "norm2_g": gain((DEPTH, D_MODEL)),
        "mod_w": nrm((DEPTH, D_MODEL, 6 * D_MODEL), 0.5 * D_MODEL ** -0.5),
        "mod_b": nrm((DEPTH, 6 * D_MODEL), 0.02),
        "mlp_w1": nrm((DEPTH, D_MODEL, D_FF), D_MODEL ** -0.5),
        "mlp_w2": nrm((DEPTH, D_FF, D_MODEL), D_FF ** -0.5),
        "attn_w_in": nrm((n_attn, D_MODEL, ATTN_IN), D_MODEL ** -0.5),
        "attn_w_out": nrm((n_attn, ATTN_OUT, D_MODEL), ATTN_OUT ** -0.5),
        "a_q_norm": gain((n_attn, HEAD_DIM)),
        "a_k_norm": gain((n_attn, HEAD_DIM)),
        "a_sink": nrm((n_attn, A_Q_HEADS), 0.5),
        "b_q_norm": gain((n_attn, HEAD_DIM)),
        "b_k_norm": gain((n_attn, HEAD_DIM)),
        "b_lq1": nrm((n_attn, HEAD_DIM), 0.1),
        "b_lk1": nrm((n_attn, HEAD_DIM), 0.1),
        "b_lq2": nrm((n_attn, HEAD_DIM), 0.1),
        "b_lk2": nrm((n_attn, HEAD_DIM), 0.1),
        "b_subln": gain((n_attn, B_V_DIM)),
        "s5_w_in": nrm((n_ssm, D_MODEL, S5_WIDTH), D_MODEL ** -0.5),
        "s5_lambda_re": -0.5 + nrm(s5_state_shape, 0.01),
        "s5_lambda_im": jnp.pi * n_idx + nrm(s5_state_shape, 0.01),
        "s5_log_step": jax.random.uniform(next(ks), (n_ssm, 2, S5_GROUPS), dtype=F32,
                                          minval=math.log(1e-3), maxval=math.log(1e-1)),
        "s5_b_re": nrm((n_ssm, 2, S5_GROUPS, S5_STATE, S5_GROUP), (2 * S5_GROUP) ** -0.5),
        "s5_b_im": nrm((n_ssm, 2, S5_GROUPS, S5_STATE, S5_GROUP), (2 * S5_GROUP) ** -0.5),
        "s5_c_re": nrm((n_ssm, 2, S5_GROUPS, S5_GROUP, S5_STATE), (2 * S5_STATE) ** -0.5),
        "s5_c_im": nrm((n_ssm, 2, S5_GROUPS, S5_GROUP, S5_STATE), (2 * S5_STATE) ** -0.5),
        "s5_d": nrm((n_ssm, S5_WIDTH), 1.0),
        "s5_glu_w": nrm((n_ssm, S5_WIDTH, S5_WIDTH), S5_WIDTH ** -0.5),
        "s5_glu_b": nrm((n_ssm, S5_WIDTH), 0.02),
        "s5_w_out": nrm((n_ssm, S5_WIDTH, D_MODEL), S5_WIDTH ** -0.5),
    }


def reference(x, c, ctx, c_ctx, norm1_g, norm2_g, mod_w, mod_b, mlp_w1, mlp_w2, attn_w_in, attn_w_out,
              a_q_norm, a_k_norm, a_sink, b_q_norm, b_k_norm, b_lq1, b_lk1, b_lq2, b_lk2, b_subln,
              s5_w_in, s5_lambda_re, s5_lambda_im, s5_log_step, s5_b_re, s5_b_im, s5_c_re, s5_c_im, s5_d,
              s5_glu_w, s5_glu_b, s5_w_out):
    bn, L, _ = x.shape
    ROWS = L // GRID_W
    cos, sin = axial_rope_tables(ROWS)
    s_lat = jax.nn.silu(c)
    s_ctx = jax.nn.silu(c_ctx)
    h_lat, h_ctx = x, ctx
    for i in range(DEPTH):
        last = i == DEPTH - 1
        j = i // 2
        m_lat = jnp.split((s_lat @ mod_w[i] + mod_b[i])[:, None, :], 6, axis=-1)
        m_ctx = jnp.split((s_ctx @ mod_w[i] + mod_b[i])[None, None, :], 6, axis=-1)
        a_lat = rms_norm(h_lat, norm1_g[i]) * (1.0 + m_lat[1]) + m_lat[0]
        a_ctx = rms_norm(h_ctx, norm1_g[i]) * (1.0 + m_ctx[1]) + m_ctx[0]
        if i % 2 == 0:
            lambda_init = 0.8 - 0.6 * math.exp(-0.3 * i)
            y_lat, y_ctx = attn_mixer(a_lat, a_ctx, attn_w_in[j], attn_w_out[j], a_q_norm[j], a_k_norm[j],
                                      a_sink[j], b_q_norm[j], b_k_norm[j], b_lq1[j], b_lk1[j], b_lq2[j],
                                      b_lk2[j], b_subln[j], cos, sin, lambda_init, not last)
        else:
            y_lat, y_ctx = s5_mixer(a_lat, a_ctx, s5_w_in[j], s5_lambda_re[j], s5_lambda_im[j], s5_log_step[j],
                                    s5_b_re[j], s5_b_im[j], s5_c_re[j], s5_c_im[j], s5_d[j], s5_glu_w[j],
                                    s5_glu_b[j], s5_w_out[j], not last)
        h_lat = h_lat + m_lat[2] * y_lat
        f_lat = rms_norm(h_lat, norm2_g[i]) * (1.0 + m_lat[4]) + m_lat[3]
        h_lat = h_lat + m_lat[5] * sq_relu_mlp(f_lat, mlp_w1[i], mlp_w2[i])
        if not last:
            h_ctx = h_ctx + m_ctx[2] * y_ctx
            f_ctx = rms_norm(h_ctx, norm2_g[i]) * (1.0 + m_ctx[4]) + m_ctx[3]
            h_ctx = h_ctx + m_ctx[5] * sq_relu_mlp(f_ctx, mlp_w1[i], mlp_w2[i])
    return h_lat
```

```python
import functools
import math

import jax
import jax.numpy as jnp
from jax import lax
from jax.experimental import pallas as pl
from jax.experimental.pallas import tpu as pltpu

F32 = jnp.float32
BF16 = jnp.bfloat16

D_MODEL = 1024
BATCH = 8
SEQ = 2048
DEPTH = 2
GRID_W = 64
CTX_LEN = 256
HEAD_DIM = 64
WINDOW = 128
A_Q_HEADS = 8
A_KV_HEADS = 2
B_HEADS = 4
A_Q_W = A_Q_HEADS * HEAD_DIM
A_KV_W = A_KV_HEADS * HEAD_DIM
B_QK_W = B_HEADS * 2 * HEAD_DIM
B_V_W = B_HEADS * 2 * HEAD_DIM
ATTN_IN = A_Q_W + 2 * A_KV_W + 2 * B_QK_W + B_V_W
S5_GROUP = 16
S5_GROUPS = D_MODEL // S5_GROUP
S5_STATE = 64
D_FF = 4 * D_MODEL
ROPE_BASE = 10000.0
EPS = 1e-6
NEG_INF = -1e30

TT = CTX_LEN + SEQ
TM = 256
N_TILES = TT // TM
LANES = 128
S5_CHUNK = 16
S5_CW = S5_CHUNK * S5_GROUP
N_CHUNKS = TT // S5_CHUNK
N_CTX_CHUNKS = CTX_LEN // S5_CHUNK
VMEM_LIMIT = 56 * 1024 * 1024


def _dot(a, b):
    return jnp.dot(a, b, preferred_element_type=F32)


def _dot_nt(a, b):
    return lax.dot_general(a, b, (((1,), (1,)), ((), ())), preferred_element_type=F32)


def _rms(x):
    return x * lax.rsqrt(jnp.mean(x * x, axis=-1, keepdims=True) + EPS)


def _modnorm(x, g, shift, scale):
    return _rms(x) * g * (1.0 + scale) + shift


def _params(**kw):
    return pltpu.CompilerParams(vmem_limit_bytes=VMEM_LIMIT, **kw)


def _mod_kernel(s_ref, w_ref, b_ref, o_ref):
    s = s_ref[...]
    s = s * jax.nn.sigmoid(s)
    o_ref[...] = _dot(s.astype(BF16), w_ref[...].astype(BF16)) + b_ref[...]


def _modulation(s_rows, mod_w, mod_b):
    return pl.pallas_call(
        _mod_kernel,
        out_shape=jax.ShapeDtypeStruct((DEPTH, 16, 6 * D_MODEL), F32),
        grid=(DEPTH, 6),
        in_specs=[
            pl.BlockSpec((16, D_MODEL), lambda i, j: (0, 0)),
            pl.BlockSpec((None, D_MODEL, D_MODEL), lambda i, j: (i, 0, j)),
            pl.BlockSpec((None, 1, D_MODEL), lambda i, j: (i, 0, j)),
        ],
        out_specs=pl.BlockSpec((None, 16, D_MODEL), lambda i, j: (i, 0, j)),
        compiler_params=_params(),
        name="modulation",
    )(s_rows, mod_w, mod_b.reshape(DEPTH, 1, 6 * D_MODEL))


def _mod_spec():
    return pl.BlockSpec((None, None, 6, D_MODEL), lambda b, t: (jnp.minimum(t, 1), b, 0, 0))


def _attn_in_kernel(h_ref, mod_ref, g_ref, w_ref, gain_ref, cos_ref, sin_ref, e_ref,
                    qa_ref, k2a_ref, v2a_ref, qb_ref, kb_ref, vb_ref):
    mod = mod_ref[...]
    a = _modnorm(h_ref[...], g_ref[...], mod[0:1], mod[1:2]).astype(BF16)
    z = _dot(a, w_ref[...])
    cos_t = cos_ref[...]
    sin_s = sin_ref[...]
    e = e_ref[...]
    gains = gain_ref[...]
    lane = lax.broadcasted_iota(jnp.int32, (TM, LANES), 1)
    first_half = (lane & (HEAD_DIM - 1)) < HEAD_DIM // 2
    lo = lane < HEAD_DIM

    def norm_rope(c, gain):
        ms = _dot((c * c).astype(BF16), e)
        cn = c * lax.rsqrt(ms + EPS) * gain
        r_fwd = pltpu.roll(cn, HEAD_DIM // 2, 1)
        r_bwd = pltpu.roll(cn, LANES - HEAD_DIM // 2, 1)
        return cn * cos_t + jnp.where(first_half, r_bwd, r_fwd) * sin_s

    def dup_halves(x, ref):
        sw = pltpu.roll(x, HEAD_DIM, 1)
        ref[:, 0:LANES] = jnp.where(lo, x, sw).astype(BF16)
        ref[:, LANES:2 * LANES] = jnp.where(lo, sw, x).astype(BF16)

    q_scale = HEAD_DIM ** -0.5
    off = 0
    for c in range(A_Q_W // LANES):
        qa_ref[:, c * LANES:(c + 1) * LANES] = (
            norm_rope(z[:, off:off + LANES], gains[0:1]) * q_scale).astype(BF16)
        off += LANES
    dup_halves(norm_rope(z[:, off:off + LANES], gains[1:2]), k2a_ref)
    off += LANES
    dup_halves(z[:, off:off + LANES], v2a_ref)
    off += LANES
    for c in range(B_QK_W // LANES):
        qb_ref[:, c * LANES:(c + 1) * LANES] = (
            norm_rope(z[:, off:off + LANES], gains[2:3]) * q_scale).astype(BF16)
        off += LANES
    for c in range(B_QK_W // LANES):
        kb_ref[:, c * LANES:(c + 1) * LANES] = norm_rope(z[:, off:off + LANES], gains[3:4]).astype(BF16)
        off += LANES
    vb_ref[...] = z[:, off:off + B_V_W].astype(BF16)


def _attn_in(h, mods, g, w_in, gains, cos_t, sin_s, e_blk):
    def tok(width):
        return pl.BlockSpec((None, TM, width), lambda b, t: (b, t, 0))

    def full(shape):
        return pl.BlockSpec(shape, lambda b, t: (0,) * len(shape))

    out_shapes = [jax.ShapeDtypeStruct((BATCH, TT, w), BF16)
                  for w in (A_Q_W, 2 * A_KV_W, 2 * A_KV_W, B_QK_W, B_QK_W, B_V_W)]
    return pl.pallas_call(
        _attn_in_kernel,
        out_shape=out_shapes,
        grid=(BATCH, N_TILES),
        in_specs=[
            tok(D_MODEL), _mod_spec(), full((1, D_MODEL)), full((D_MODEL, ATTN_IN)), full((4, LANES)),
            pl.BlockSpec((TM, LANES), lambda b, t: (t, 0)), pl.BlockSpec((TM, LANES), lambda b, t: (t, 0)),
            full((LANES, LANES)),
        ],
        out_specs=[tok(A_Q_W), tok(2 * A_KV_W), tok(2 * A_KV_W), tok(B_QK_W), tok(B_QK_W), tok(B_V_W)],
        compiler_params=_params(),
        name="attn_in_proj",
    )(h, mods, g, w_in, gains, cos_t, sin_s, e_blk)


QB = 128


def _win_attn_kernel(sink_ref, q_ref, k2_ref, v2_ref, o_ref):
    t = pl.program_id(1)
    n = t - CTX_LEN // QB
    ws = jnp.clip((n - 1) * QB, 0, SEQ - 3 * QB) + CTX_LEN
    ws = pl.multiple_of(ws, QB)
    q = q_ref[...]
    lane = lax.broadcasted_iota(jnp.int32, (QB, LANES), 1)
    lo = lane < HEAD_DIM
    rows = 4 * QB
    row = lax.broadcasted_iota(jnp.int32, (rows, 3 * QB), 0)
    col = lax.broadcasted_iota(jnp.int32, (rows, 3 * QB), 1)
    qpos = n * QB + (row & (QB - 1))
    kpos = (ws - CTX_LEN) + col
    valid = (jnp.abs(qpos - kpos) <= WINDOW) & (n >= 0)
    row_head = lax.broadcasted_iota(jnp.int32, (rows, 1), 0) // QB
    zero = jnp.zeros((QB, LANES), BF16)
    for g in range(A_KV_HEADS):
        pieces = []
        for p in range(2):
            qp = q[:, g * 2 * LANES + p * LANES: g * 2 * LANES + (p + 1) * LANES]
            pieces.append(jnp.where(lo, qp, zero))
            pieces.append(jnp.where(lo, zero, qp))
        qs = jnp.concatenate(pieces, axis=0)
        kc = k2_ref[0:CTX_LEN, g * LANES:(g + 1) * LANES]
        kw = k2_ref[pl.ds(ws, 3 * QB), g * LANES:(g + 1) * LANES]
        vc = v2_ref[0:CTX_LEN, g * LANES:(g + 1) * LANES]
        vw = v2_ref[pl.ds(ws, 3 * QB), g * LANES:(g + 1) * LANES]
        s_c = _dot_nt(qs, kc)
        s_w = jnp.where(valid, _dot_nt(qs, kw), NEG_INF)
        sk = jnp.full((rows, 1), sink_ref[4 * g + 3], F32)
        for hh in range(3):
            sk = jnp.where(row_head == hh, sink_ref[4 * g + hh], sk)
        m = jnp.maximum(jnp.maximum(jnp.max(s_c, axis=-1, keepdims=True),
                                    jnp.max(s_w, axis=-1, keepdims=True)), sk)
        p_c = jnp.exp(s_c - m)
        p_w = jnp.exp(s_w - m)
        denom = (jnp.sum(p_c, axis=-1, keepdims=True) + jnp.sum(p_w, axis=-1, keepdims=True)
                 + jnp.exp(sk - m))
        o = (_dot(p_c.astype(BF16), vc) + _dot(p_w.astype(BF16), vw)) / denom
        for p in range(2):
            o_ref[:, g * 2 * LANES + p * LANES: g * 2 * LANES + (p + 1) * LANES] = jnp.where(
                lo, o[2 * p * QB:(2 * p + 1) * QB], o[(2 * p + 1) * QB:(2 * p + 2) * QB]).astype(BF16)


def _win_attn(sink, qa, k2a, v2a):
    return pl.pallas_call(
        _win_attn_kernel,
        out_shape=jax.ShapeDtypeStruct((BATCH, TT, A_Q_W), BF16),
        grid=(BATCH, TT // QB),
        in_specs=[
            pl.BlockSpec(memory_space=pltpu.SMEM),
            pl.BlockSpec((None, QB, A_Q_W), lambda b, t: (b, t, 0)),
            pl.BlockSpec((None, TT, 2 * A_KV_W), lambda b, t: (b, 0, 0)),
            pl.BlockSpec((None, TT, 2 * A_KV_W), lambda b, t: (b, 0, 0)),
        ],
        out_specs=pl.BlockSpec((None, QB, A_Q_W), lambda b, t: (b, t, 0)),
        compiler_params=_params(),
        name="window_attention",
    )(sink, qa, k2a, v2a)


def _diff_attn_kernel(lpar_ref, subln_ref, q_ref, k_ref, v_ref, o_ref, *, lambda_init):
    t = pl.program_id(2)
    lp = lpar_ref[...]
    lam = (jnp.exp(jnp.sum(lp[0:1] * lp[1:2], axis=-1, keepdims=True))
           - jnp.exp(jnp.sum(lp[2:3] * lp[3:4], axis=-1, keepdims=True)) + lambda_init)
    q = q_ref[...]
    lane = lax.broadcasted_iota(jnp.int32, (TM, LANES), 1)
    lo = lane < HEAD_DIM
    zero = jnp.zeros((TM, LANES), BF16)
    qs = jnp.concatenate([jnp.where(lo, q, zero), jnp.where(lo, zero, q)], axis=0)

    def attend(n_keys):
        s = _dot_nt(qs, k_ref[0:n_keys, :])
        p = jnp.exp(s - jnp.max(s, axis=-1, keepdims=True))
        p = p / jnp.sum(p, axis=-1, keepdims=True)
        w = (p[0:TM] - lam * p[TM:2 * TM]).astype(BF16)
        y = _dot(w, v_ref[0:n_keys, :])
        o_ref[...] = (_rms(y) * subln_ref[...] * (1.0 - lambda_init)).astype(BF16)

    @pl.when(t == 0)
    def _():
        attend(CTX_LEN)

    @pl.when(t > 0)
    def _():
        attend(TT)


def _diff_attn(lpar, subln, qb, kb, vb, lambda_init):
    return pl.pallas_call(
        functools.partial(_diff_attn_kernel, lambda_init=lambda_init),
        out_shape=jax.ShapeDtypeStruct((BATCH, TT, B_V_W), BF16),
        grid=(BATCH, B_HEADS, N_TILES),
        in_specs=[
            pl.BlockSpec((4, HEAD_DIM), lambda b, h, t: (0, 0)),
            pl.BlockSpec((1, LANES), lambda b, h, t: (0, 0)),
            pl.BlockSpec((None, TM, LANES), lambda b, h, t: (b, t, h)),
            pl.BlockSpec((None, TT, LANES), lambda b, h, t: (b, 0, h)),
            pl.BlockSpec((None, TT, LANES), lambda b, h, t: (b, 0, h)),
        ],
        out_specs=pl.BlockSpec((None, TM, LANES), lambda b, h, t: (b, t, h)),
        compiler_params=_params(),
        name="diff_attention",
    )(lpar, subln, qb, kb, vb)


FF_CHUNK = 1024


def _mix_mlp_kernel(h_ref, ua_ref, ub_ref, mod_ref, g_ref, wo_ref, w1_ref, w2_ref, o_ref):
    mod = mod_ref[...]
    half = D_MODEL // 2
    y = _dot(ua_ref[...], wo_ref[0:half, :]) + _dot(ub_ref[...], wo_ref[half:D_MODEL, :])
    h1 = h_ref[...] + mod[2:3] * y
    f = _modnorm(h1, g_ref[...], mod[3:4], mod[4:5]).astype(BF16)
    acc = jnp.zeros((TM, D_MODEL), F32)
    for c in range(D_FF // FF_CHUNK):
        hid = jnp.maximum(_dot(f, w1_ref[:, c * FF_CHUNK:(c + 1) * FF_CHUNK]), 0.0)
        acc = acc + _dot((hid * hid).astype(BF16), w2_ref[c * FF_CHUNK:(c + 1) * FF_CHUNK, :])
    o_ref[...] = h1 + mod[5:6] * acc


def _mix_mlp(h, ua, ub, mods, g, wo, w1, w2, *, latent_only):
    n_tiles = SEQ // TM if latent_only else N_TILES
    h_off = N_TILES - n_tiles
    half = D_MODEL // 2
    ub_col = 1 if ub.shape[-1] == D_MODEL else 0

    def full(shape):
        return pl.BlockSpec(shape, lambda b, t: (0,) * len(shape), pipeline_mode=pl.Buffered(1))

    return pl.pallas_call(
        _mix_mlp_kernel,
        out_shape=jax.ShapeDtypeStruct((BATCH, n_tiles * TM, D_MODEL), F32),
        grid=(BATCH, n_tiles),
        in_specs=[
            pl.BlockSpec((None, TM, D_MODEL), lambda b, t: (b, t + h_off, 0)),
            pl.BlockSpec((None, TM, half), lambda b, t: (b, t, 0)),
            pl.BlockSpec((None, TM, half), lambda b, t: (b, t, ub_col)),
            pl.BlockSpec((None, None, 6, D_MODEL), lambda b, t: (jnp.minimum(t + h_off, 1), b, 0, 0)),
            pl.BlockSpec((1, D_MODEL), lambda b, t: (0, 0)),
            full((D_MODEL, D_MODEL)), full((D_MODEL, D_FF)), full((D_FF, D_MODEL)),
        ],
        out_specs=pl.BlockSpec((None, TM, D_MODEL), lambda b, t: (b, t, 0)),
        compiler_params=_params(),
        name="mixer_out_mlp",
    )(h, ua, ub, mods, g, wo, w1, w2)


def _s5_in_kernel(h_ref, mod_ref, g_ref, w_ref, u_ref, ub_ref):
    mod = mod_ref[...]
    a = _modnorm(h_ref[...], g_ref[...], mod[0:1], mod[1:2]).astype(BF16)
    u = _dot(a, w_ref[...])
    u_ref[...] = u
    ub_ref[...] = u.astype(BF16)


def _s5_in(h, mods, g, w_in):
    tok = pl.BlockSpec((None, TM, D_MODEL), lambda b, t: (b, t, 0))
    return pl.pallas_call(
        _s5_in_kernel,
        out_shape=[jax.ShapeDtypeStruct((BATCH, TT, D_MODEL), F32),
                   jax.ShapeDtypeStruct((BATCH, TT, D_MODEL), BF16)],
        grid=(BATCH, N_TILES),
        in_specs=[tok, _mod_spec(), pl.BlockSpec((1, D_MODEL), lambda b, t: (0, 0)),
                  pl.BlockSpec((D_MODEL, D_MODEL), lambda b, t: (0, 0))],
        out_specs=[tok, tok],
        compiler_params=_params(),
        name="s5_in_proj",
    )(h, mods, g, w_in)


def _s5_ops_kernel(lr_c_ref, li_c_ref, ls_c_ref, lr_r_ref, li_r_ref, ls_r_ref,
                   btr_ref, bti_ref, btile_r_ref, btile_i_ref, ctile_r_ref, ctile_i_ref,
                   m_ref, q_ref, n_ref, a_ref):
    P2 = 2 * S5_STATE

    def discretize(lr, li, ls):
        dt = jnp.exp(ls)
        mag = jnp.exp(lr * dt)
        ab_re = mag * jnp.cos(li * dt)
        ab_im = mag * jnp.sin(li * dt)
        den = lr * lr + li * li
        nr = ab_re - 1.0
        f_re = (nr * lr + ab_im * li) / den
        f_im = (ab_im * lr - nr * li) / den
        return dt, f_re, f_im

    lr_c, li_c = lr_c_ref[...], li_c_ref[...]
    dt_c, f_re_c, f_im_c = discretize(lr_c, li_c, ls_c_ref[...])
    lag = (lax.broadcasted_iota(jnp.int32, (P2, S5_CW), 1) // S5_GROUP).astype(F32)
    is_fwd = lax.broadcasted_iota(jnp.int32, (P2, S5_CW), 0) < S5_STATE
    e_k = jnp.where(is_fwd, lag, (S5_CHUNK - 1) - lag)

    def a_pow(k):
        mag = jnp.exp(lr_c * dt_c * k)
        th = li_c * dt_c * k
        return mag * jnp.cos(th), mag * jnp.sin(th)

    c_re, c_im = ctile_r_ref[...], ctile_i_ref[...]
    p_re, p_im = a_pow(e_k)
    cp_re = c_re * p_re - c_im * p_im
    cp_im = c_re * p_im + c_im * p_re

    lr_r, li_r = lr_r_ref[...], li_r_ref[...]
    dt_r, f_re_r, f_im_r = discretize(lr_r, li_r, ls_r_ref[...])
    bt_re = f_re_r * btr_ref[...] - f_im_r * bti_ref[...]
    bt_im = f_re_r * bti_ref[...] + f_im_r * btr_ref[...]
    lane_fwd = lax.broadcasted_iota(jnp.int32, (S5_GROUP, P2), 1) < S5_STATE

    def lag_kernels(keep):
        br = jnp.where(keep, bt_re, 0.0)
        bi = jnp.where(keep, bt_im, 0.0)
        hi = lax.Precision.HIGHEST
        return (jnp.dot(br, cp_re, precision=hi, preferred_element_type=F32)
                - jnp.dot(bi, cp_im, precision=hi, preferred_element_type=F32))

    kt_f = lag_kernels(lane_fwd)
    kt_b = lag_kernels(jnp.logical_not(lane_fwd))
    lane_w = lax.broadcasted_iota(jnp.int32, (S5_GROUP, S5_CW), 1)
    for s in range(S5_CHUNK):
        f_part = kt_f if s == 0 else jnp.where(lane_w >= S5_GROUP * s, pltpu.roll(kt_f, S5_GROUP * s, 1), 0.0)
        sh = (S5_GROUP * (s + 1)) % S5_CW
        b_roll = kt_b if sh == 0 else pltpu.roll(kt_b, sh, 1)
        b_part = jnp.where(lane_w < S5_GROUP * (s + 1), b_roll, 0.0)
        m_ref[s * S5_GROUP:(s + 1) * S5_GROUP, :] = (f_part + b_part).astype(BF16)

    bb_re = f_re_c * btile_r_ref[...] - f_im_c * btile_i_ref[...]
    bb_im = f_re_c * btile_i_ref[...] + f_im_c * btile_r_ref[...]
    p_re, p_im = a_pow((S5_CHUNK - 1) - e_k)
    q_ref[0:P2, :] = (p_re * bb_re - p_im * bb_im).astype(BF16)
    q_ref[P2:2 * P2, :] = (p_re * bb_im + p_im * bb_re).astype(BF16)

    p_re, p_im = a_pow(e_k + 1.0)
    n_ref[0:P2, :] = (c_re * p_re - c_im * p_im).astype(BF16)
    n_ref[P2:2 * P2, :] = (-(c_re * p_im + c_im * p_re)).astype(BF16)

    mag = jnp.exp(lr_r * dt_r * float(S5_CHUNK))
    th = li_r * dt_r * float(S5_CHUNK)
    a_ref[0:1, :] = mag * jnp.cos(th)
    a_ref[1:2, :] = mag * jnp.sin(th)


def _s5_ops(col_params, row_params, bt, btile, ctile):
    P2 = 2 * S5_STATE
    col = pl.BlockSpec((None, P2, 1), lambda g: (g, 0, 0))
    row = pl.BlockSpec((None, 1, P2), lambda g: (g, 0, 0))
    btspec = pl.BlockSpec((None, S5_GROUP, P2), lambda g: (g, 0, 0))
    tile = pl.BlockSpec((None, P2, S5_CW), lambda g: (g, 0, 0))
    sq = pl.BlockSpec((None, S5_CW, S5_CW), lambda g: (g, 0, 0))
    return pl.pallas_call(
        _s5_ops_kernel,
        out_shape=[jax.ShapeDtypeStruct((S5_GROUPS, S5_CW, S5_CW), BF16)] * 3
        + [jax.ShapeDtypeStruct((S5_GROUPS, 2, P2), F32)],
        grid=(S5_GROUPS,),
        in_specs=[col, col, col, row, row, row, btspec, btspec, tile, tile, tile, tile],
        out_specs=[sq, sq, sq, pl.BlockSpec((None, 2, P2), lambda g: (g, 0, 0))],
        compiler_params=_params(),
        name="s5_chunk_operators",
    )(*col_params, *row_params, *bt, *btile, *ctile)


ROWS = N_CHUNKS * BATCH


def _s5_scan_kernel(u_ref, m_ref, q_ref, n_ref, a_ref, y_ref, pu_ref, sp_ref):
    u = u_ref[...]
    pu_ref[...] = _dot_nt(u, q_ref[...])
    a_re = a_ref[0:1, :]
    a_im = a_ref[1:2, :]
    lane = lax.broadcasted_iota(jnp.int32, (BATCH, 2 * S5_STATE), 1)
    lo = lane < S5_STATE
    S = S5_STATE

    def step(k, carry):
        s_re, s_im = carry
        cf = pl.multiple_of(k * BATCH, BATCH)
        cb = pl.multiple_of(jnp.where(k < N_CTX_CHUNKS, N_CTX_CHUNKS - 1 - k,
                                      N_CHUNKS + N_CTX_CHUNKS - 1 - k) * BATCH, BATCH)
        sp_ref[pl.ds(cf, BATCH), 0:S] = s_re[:, 0:S]
        sp_ref[pl.ds(cb, BATCH), S:2 * S] = s_re[:, S:2 * S]
        sp_ref[pl.ds(cf, BATCH), 2 * S:3 * S] = s_im[:, 0:S]
        sp_ref[pl.ds(cb, BATCH), 3 * S:4 * S] = s_im[:, S:2 * S]
        x_re = jnp.where(lo, pu_ref[pl.ds(cf, BATCH), 0:2 * S], pu_ref[pl.ds(cb, BATCH), 0:2 * S])
        x_im = jnp.where(lo, pu_ref[pl.ds(cf, BATCH), 2 * S:4 * S], pu_ref[pl.ds(cb, BATCH), 2 * S:4 * S])
        return (a_re * s_re - a_im * s_im + x_re, a_re * s_im + a_im * s_re + x_im)

    zero = jnp.zeros((BATCH, 2 * S), F32)
    lax.fori_loop(0, N_CHUNKS, step, (zero, zero))
    y_ref[...] = _dot(u, m_ref[...]) + _dot(sp_ref[...].astype(BF16), n_ref[...])


def _s5_scan(u_g, m_op, q_op, n_op, a_vec):
    sq = pl.BlockSpec((None, S5_CW, S5_CW), lambda g: (g, 0, 0))
    rows = pl.BlockSpec((None, ROWS, S5_CW), lambda g: (g, 0, 0))
    return pl.pallas_call(
        _s5_scan_kernel,
        out_shape=jax.ShapeDtypeStruct((S5_GROUPS, ROWS, S5_CW), F32),
        grid=(S5_GROUPS,),
        in_specs=[rows, sq, sq, sq, pl.BlockSpec((None, 2, 2 * S5_STATE), lambda g: (g, 0, 0))],
        out_specs=rows,
        scratch_shapes=[pltpu.VMEM((ROWS, S5_CW), F32), pltpu.VMEM((ROWS, S5_CW), F32)],
        compiler_params=_params(),
        name="s5_scan",
    )(u_g, m_op, q_op, n_op, a_vec)


def _s5_out_kernel(u_ref, ys_ref, d_ref, gw_ref, gb_ref, o_ref):
    y = u_ref[...] * d_ref[...] + ys_ref[...]
    g = jax.nn.gelu(y)
    gate = jax.nn.sigmoid(_dot(g.astype(BF16), gw_ref[...]) + gb_ref[...])
    o_ref[...] = (g * gate).astype(BF16)


def _s5_out(u, ys, d_skip, glu_w, glu_b):
    n_tiles = SEQ // TM
    vec = pl.BlockSpec((1, D_MODEL), lambda b, t: (0, 0))
    return pl.pallas_call(
        _s5_out_kernel,
        out_shape=jax.ShapeDtypeStruct((BATCH, SEQ, D_MODEL), BF16),
        grid=(BATCH, n_tiles),
        in_specs=[pl.BlockSpec((None, TM, D_MODEL), lambda b, t: (b, t + 1, 0)),
                  pl.BlockSpec((None, TM, D_MODEL), lambda b, t: (b, t, 0)),
                  vec, pl.BlockSpec((D_MODEL, D_MODEL), lambda b, t: (0, 0)), vec],
        out_specs=pl.BlockSpec((None, TM, D_MODEL), lambda b, t: (b, t, 0)),
        compiler_params=_params(),
        name="s5_gelu_glu",
    )(u, ys, d_skip, glu_w, glu_b)


def _rope_tables():
    rows_n = SEQ // GRID_W
    row = jnp.repeat(jnp.arange(rows_n, dtype=F32), GRID_W)
    col = jnp.tile(jnp.arange(GRID_W, dtype=F32), rows_n)
    n_freq = HEAD_DIM // 4
    inv = ROPE_BASE ** (-jnp.arange(n_freq, dtype=F32) / n_freq)
    ang = jnp.concatenate([row[:, None] * inv, col[:, None] * inv], axis=-1)
    reps = LANES // (HEAD_DIM // 2)
    cos_t = jnp.tile(jnp.cos(ang), (1, reps))
    sin_t = jnp.tile(jnp.sin(ang), (1, reps))
    sign = jnp.where((jnp.arange(LANES) % HEAD_DIM) < HEAD_DIM // 2, -1.0, 1.0).astype(F32)
    cos_t = jnp.concatenate([jnp.ones((CTX_LEN, LANES), F32), cos_t], axis=0)
    sin_s = jnp.concatenate([jnp.zeros((CTX_LEN, LANES), F32), sin_t * sign], axis=0)
    return cos_t, sin_s


def _fb_cols(x):
    return jnp.transpose(x, (1, 0, 2)).reshape(S5_GROUPS, 2 * S5_STATE, 1)


def _fb_rows(x):
    return jnp.transpose(x, (1, 0, 2)).reshape(S5_GROUPS, 1, 2 * S5_STATE)


def _s5_layout(lam_re, lam_im, log_step, b_re, b_im, c_re, c_im):
    ls = jnp.broadcast_to(log_step[:, :, None], lam_re.shape)
    cols = [_fb_cols(v) for v in (lam_re, lam_im, ls)]
    rows = [_fb_rows(v) for v in (lam_re, lam_im, ls)]

    def bt_of(b):
        return jnp.transpose(b, (1, 3, 0, 2)).reshape(S5_GROUPS, S5_GROUP, 2 * S5_STATE)

    def btile_of(b):
        t = jnp.transpose(b, (1, 0, 2, 3)).reshape(S5_GROUPS, 2 * S5_STATE, S5_GROUP)
        return jnp.tile(t, (1, 1, S5_CHUNK))

    def ctile_of(c):
        t = jnp.transpose(c, (1, 0, 3, 2)).reshape(S5_GROUPS, 2 * S5_STATE, S5_GROUP)
        return jnp.tile(t, (1, 1, S5_CHUNK))

    return cols, rows, [bt_of(b_re), bt_of(b_im)], [btile_of(b_re), btile_of(b_im)], [ctile_of(c_re), ctile_of(c_im)]


def kernel(x, c, ctx, c_ctx, norm1_g, norm2_g, mod_w, mod_b, mlp_w1, mlp_w2, attn_w_in, attn_w_out, a_q_norm, a_k_norm, a_sink, b_q_norm, b_k_norm, b_lq1, b_lk1, b_lq2, b_lk2, b_subln, s5_w_in, s5_lambda_re, s5_lambda_im, s5_log_step, s5_b_re, s5_b_im, s5_c_re, s5_c_im, s5_d, s5_glu_w, s5_glu_b, s5_w_out):
    assert x.shape == (BATCH, SEQ, D_MODEL) and ctx.shape == (BATCH, CTX_LEN, D_MODEL)
    h = jnp.concatenate([ctx, x], axis=1)
    s_rows = jnp.concatenate([c, c_ctx[None], jnp.zeros((16 - BATCH - 1, D_MODEL), F32)], axis=0)
    m_all = _modulation(s_rows, mod_w, mod_b)
    cos_t, sin_s = _rope_tables()
    e_blk = (jnp.kron(jnp.eye(LANES // HEAD_DIM, dtype=F32), jnp.ones((HEAD_DIM, HEAD_DIM), F32))
             / HEAD_DIM).astype(BF16)

    for i in range(DEPTH):
        last = i == DEPTH - 1
        j = i // 2
        m_lat = m_all[i, :BATCH].reshape(BATCH, 6, D_MODEL)
        m_ctx = jnp.broadcast_to(m_all[i, BATCH].reshape(1, 6, D_MODEL), (BATCH, 6, D_MODEL))
        mods = jnp.stack([m_ctx, m_lat])
        g1 = norm1_g[i].reshape(1, D_MODEL)
        g2 = norm2_g[i].reshape(1, D_MODEL)
        if i % 2 == 0:
            lambda_init = 0.8 - 0.6 * math.exp(-0.3 * i)
            gains = jnp.stack([jnp.tile(v[j], LANES // HEAD_DIM) for v in (a_q_norm, a_k_norm, b_q_norm, b_k_norm)])
            qa, k2a, v2a, qb, kb, vb = _attn_in(h, mods, g1, attn_w_in[j].astype(BF16), gains, cos_t, sin_s, e_blk)
            ya = _win_attn(a_sink[j], qa, k2a, v2a)
            lpar = jnp.stack([b_lq1[j], b_lk1[j], b_lq2[j], b_lk2[j]])
            yb = _diff_attn(lpar, b_subln[j].reshape(1, LANES), qb, kb, vb, lambda_init)
            if last:
                ya, yb = ya[:, CTX_LEN:], yb[:, CTX_LEN:]
            ua, ub, wo = ya, yb, attn_w_out[j]
        else:
            u, u_bf = _s5_in(h, mods, g1, s5_w_in[j].astype(BF16))
            ops_in = _s5_layout(s5_lambda_re[j], s5_lambda_im[j], s5_log_step[j], s5_b_re[j], s5_b_im[j],
                                s5_c_re[j], s5_c_im[j])
            m_op, q_op, n_op, a_vec = _s5_ops(*ops_in)
            u_g = u_bf.reshape(BATCH, N_CHUNKS, S5_CHUNK, S5_GROUPS, S5_GROUP)
            u_g = jnp.transpose(u_g, (3, 1, 0, 2, 4)).reshape(S5_GROUPS, ROWS, S5_CW)
            y_g = _s5_scan(u_g, m_op, q_op, n_op, a_vec)
            assert last, "S5 layers before the last one would also need the context rows of the readout"
            y_g = y_g[:, N_CTX_CHUNKS * BATCH:].reshape(S5_GROUPS, SEQ // S5_CHUNK, BATCH, S5_CHUNK, S5_GROUP)
            ys = jnp.transpose(y_g, (2, 1, 3, 0, 4)).reshape(BATCH, SEQ, D_MODEL)
            gated = _s5_out(u, ys, s5_d[j].reshape(1, D_MODEL), s5_glu_w[j].astype(BF16),
                            s5_glu_b[j].reshape(1, D_MODEL))
            ua, ub, wo = gated, gated, s5_w_out[j]
        h = _mix_mlp(h, ua, ub, mods, g2, wo.astype(BF16), mlp_w1[i].astype(BF16), mlp_w2[i].astype(BF16),
                     latent_only=last)
    return h
```

```python
import functools
import math

import jax
import jax.numpy as jnp
import numpy as np
from jax import lax
from jax.experimental import pallas as pl
from jax.experimental.pallas import tpu as pltpu

F32 = jnp.float32
BF16 = jnp.bfloat16

D_MODEL = 1024
BATCH = 8
SEQ = 2048
DEPTH = 2
GRID_W = 64
CTX_LEN = 256
HEAD_DIM = 64
WINDOW = 128
A_Q_HEADS = 8
A_KV_HEADS = 2
B_HEADS = 4
A_Q_W = A_Q_HEADS * HEAD_DIM
A_KV_W = A_KV_HEADS * HEAD_DIM
B_QK_W = B_HEADS * 2 * HEAD_DIM
B_V_W = B_HEADS * 2 * HEAD_DIM
ATTN_IN = A_Q_W + 2 * A_KV_W + 2 * B_QK_W + B_V_W
S5_GROUP = 16
S5_GROUPS = D_MODEL // S5_GROUP
S5_STATE = 64
D_FF = 4 * D_MODEL
ROPE_BASE = 10000.0
EPS = 1e-6
NEG_INF = -1e30

TT = CTX_LEN + SEQ
TM = 256
N_TILES = TT // TM
LANES = 128
S5_CHUNK = 16
S5_CW = S5_CHUNK * S5_GROUP
N_CHUNKS = TT // S5_CHUNK
N_CTX_CHUNKS = CTX_LEN // S5_CHUNK
VMEM_LIMIT = 56 * 1024 * 1024


def _dot(a, b):
    return jnp.dot(a, b, preferred_element_type=F32)


def _dot_nt(a, b):
    return lax.dot_general(a, b, (((1,), (1,)), ((), ())), preferred_element_type=F32)


def _rms(x):
    return x * lax.rsqrt(jnp.mean(x * x, axis=-1, keepdims=True) + EPS)


def _modnorm(x, g, shift, scale):
    return _rms(x) * g * (1.0 + scale) + shift


def _params(**kw):
    return pltpu.CompilerParams(vmem_limit_bytes=VMEM_LIMIT, **kw)


def _mod_kernel(s_ref, w_ref, b_ref, o_ref):
    s = s_ref[...]
    s = s * jax.nn.sigmoid(s)
    o_ref[...] = _dot(s.astype(BF16), w_ref[...].astype(BF16)) + b_ref[...]


def _modulation(s_rows, mod_w, mod_b):
    return pl.pallas_call(
        _mod_kernel,
        out_shape=jax.ShapeDtypeStruct((DEPTH, 16, 6 * D_MODEL), F32),
        grid=(DEPTH, 6),
        in_specs=[
            pl.BlockSpec((16, D_MODEL), lambda i, j: (0, 0)),
            pl.BlockSpec((None, D_MODEL, D_MODEL), lambda i, j: (i, 0, j)),
            pl.BlockSpec((None, 1, D_MODEL), lambda i, j: (i, 0, j)),
        ],
        out_specs=pl.BlockSpec((None, 16, D_MODEL), lambda i, j: (i, 0, j)),
        compiler_params=_params(),
        name="modulation",
    )(s_rows, mod_w, mod_b.reshape(DEPTH, 1, 6 * D_MODEL))


def _mod_spec():
    return pl.BlockSpec((None, None, 6, D_MODEL), lambda b, t: (jnp.minimum(t, 1), b, 0, 0))


def _attn_in_kernel(h_ref, mod_ref, g_ref, w_ref, gain_ref, cos_ref, sin_ref, e_ref,
                    qa_ref, k2a_ref, v2a_ref, qb_ref, kb_ref, vb_ref):
    mod = mod_ref[...]
    a = _modnorm(h_ref[...], g_ref[...], mod[0:1], mod[1:2]).astype(BF16)
    z = _dot(a, w_ref[...])
    cos_t = cos_ref[...]
    sin_s = sin_ref[...]
    e = e_ref[...]
    gains = gain_ref[...]
    lane = lax.broadcasted_iota(jnp.int32, (TM, LANES), 1)
    first_half = (lane & (HEAD_DIM - 1)) < HEAD_DIM // 2
    lo = lane < HEAD_DIM

    def norm_rope(c, gain):
        ms = _dot((c * c).astype(BF16), e)
        cn = c * lax.rsqrt(ms + EPS) * gain
        r_fwd = pltpu.roll(cn, HEAD_DIM // 2, 1)
        r_bwd = pltpu.roll(cn, LANES - HEAD_DIM // 2, 1)
        return cn * cos_t + jnp.where(first_half, r_bwd, r_fwd) * sin_s

    def dup_halves(x, ref):
        sw = pltpu.roll(x, HEAD_DIM, 1)
        ref[:, 0:LANES] = jnp.where(lo, x, sw).astype(BF16)
        ref[:, LANES:2 * LANES] = jnp.where(lo, sw, x).astype(BF16)

    q_scale = HEAD_DIM ** -0.5
    off = 0
    for c in range(A_Q_W // LANES):
        qa_ref[:, c * LANES:(c + 1) * LANES] = (
            norm_rope(z[:, off:off + LANES], gains[0:1]) * q_scale).astype(BF16)
        off += LANES
    dup_halves(norm_rope(z[:, off:off + LANES], gains[1:2]), k2a_ref)
    off += LANES
    dup_halves(z[:, off:off + LANES], v2a_ref)
    off += LANES
    for c in range(B_QK_W // LANES):
        qb_ref[:, c * LANES:(c + 1) * LANES] = (
            norm_rope(z[:, off:off + LANES], gains[2:3]) * q_scale).astype(BF16)
        off += LANES
    for c in range(B_QK_W // LANES):
        kb_ref[:, c * LANES:(c + 1) * LANES] = norm_rope(z[:, off:off + LANES], gains[3:4]).astype(BF16)
        off += LANES
    vb_ref[...] = z[:, off:off + B_V_W].astype(BF16)


def _attn_in(h, mods, g, w_in, gains, cos_t, sin_s, e_blk):
    def tok(width):
        return pl.BlockSpec((None, TM, width), lambda b, t: (b, t, 0))

    def full(shape):
        return pl.BlockSpec(shape, lambda b, t: (0,) * len(shape))

    out_shapes = [jax.ShapeDtypeStruct((BATCH, TT, w), BF16)
                  for w in (A_Q_W, 2 * A_KV_W, 2 * A_KV_W, B_QK_W, B_QK_W, B_V_W)]
    return pl.pallas_call(
        _attn_in_kernel,
        out_shape=out_shapes,
        grid=(BATCH, N_TILES),
        in_specs=[
            tok(D_MODEL), _mod_spec(), full((1, D_MODEL)), full((D_MODEL, ATTN_IN)), full((4, LANES)),
            pl.BlockSpec((TM, LANES), lambda b, t: (t, 0)), pl.BlockSpec((TM, LANES), lambda b, t: (t, 0)),
            full((LANES, LANES)),
        ],
        out_specs=[tok(A_Q_W), tok(2 * A_KV_W), tok(2 * A_KV_W), tok(B_QK_W), tok(B_QK_W), tok(B_V_W)],
        compiler_params=_params(),
        name="attn_in_proj",
    )(h, mods, g, w_in, gains, cos_t, sin_s, e_blk)


QB = 128


def _win_attn_kernel(sink_ref, q_ref, k2_ref, v2_ref, o_ref):
    t = pl.program_id(1)
    n = t - CTX_LEN // QB
    ws = jnp.clip((n - 1) * QB, 0, SEQ - 3 * QB) + CTX_LEN
    ws = pl.multiple_of(ws, QB)
    q = q_ref[...]
    lane = lax.broadcasted_iota(jnp.int32, (QB, LANES), 1)
    lo = lane < HEAD_DIM
    rows = 4 * QB
    row = lax.broadcasted_iota(jnp.int32, (rows, 3 * QB), 0)
    col = lax.broadcasted_iota(jnp.int32, (rows, 3 * QB), 1)
    qpos = n * QB + (row & (QB - 1))
    kpos = (ws - CTX_LEN) + col
    valid = (jnp.abs(qpos - kpos) <= WINDOW) & (n >= 0)
    row_head = lax.broadcasted_iota(jnp.int32, (rows, 1), 0) // QB
    zero = jnp.zeros((QB, LANES), BF16)
    for g in range(A_KV_HEADS):
        pieces = []
        for p in range(2):
            qp = q[:, g * 2 * LANES + p * LANES: g * 2 * LANES + (p + 1) * LANES]
            pieces.append(jnp.where(lo, qp, zero))
            pieces.append(jnp.where(lo, zero, qp))
        qs = jnp.concatenate(pieces, axis=0)
        kc = k2_ref[0:CTX_LEN, g * LANES:(g + 1) * LANES]
        kw = k2_ref[pl.ds(ws, 3 * QB), g * LANES:(g + 1) * LANES]
        vc = v2_ref[0:CTX_LEN, g * LANES:(g + 1) * LANES]
        vw = v2_ref[pl.ds(ws, 3 * QB), g * LANES:(g + 1) * LANES]
        s_c = _dot_nt(qs, kc)
        s_w = jnp.where(valid, _dot_nt(qs, kw), NEG_INF)
        sk = jnp.full((rows, 1), sink_ref[4 * g + 3], F32)
        for hh in range(3):
            sk = jnp.where(row_head == hh, sink_ref[4 * g + hh], sk)
        m = jnp.maximum(jnp.maximum(jnp.max(s_c, axis=-1, keepdims=True),
                                    jnp.max(s_w, axis=-1, keepdims=True)), sk)
        p_c = jnp.exp(s_c - m)
        p_w = jnp.exp(s_w - m)
        denom = (jnp.sum(p_c, axis=-1, keepdims=True) + jnp.sum(p_w, axis=-1, keepdims=True)
                 + jnp.exp(sk - m))
        o = (_dot(p_c.astype(BF16), vc) + _dot(p_w.astype(BF16), vw)) / denom
        for p in range(2):
            o_ref[:, g * 2 * LANES + p * LANES: g * 2 * LANES + (p + 1) * LANES] = jnp.where(
                lo, o[2 * p * QB:(2 * p + 1) * QB], o[(2 * p + 1) * QB:(2 * p + 2) * QB]).astype(BF16)


def _win_attn(sink, qa, k2a, v2a):
    return pl.pallas_call(
        _win_attn_kernel,
        out_shape=jax.ShapeDtypeStruct((BATCH, TT, A_Q_W), BF16),
        grid=(BATCH, TT // QB),
        in_specs=[
            pl.BlockSpec(memory_space=pltpu.SMEM),
            pl.BlockSpec((None, QB, A_Q_W), lambda b, t: (b, t, 0)),
            pl.BlockSpec((None, TT, 2 * A_KV_W), lambda b, t: (b, 0, 0)),
            pl.BlockSpec((None, TT, 2 * A_KV_W), lambda b, t: (b, 0, 0)),
        ],
        out_specs=pl.BlockSpec((None, QB, A_Q_W), lambda b, t: (b, t, 0)),
        compiler_params=_params(),
        name="window_attention",
    )(sink, qa, k2a, v2a)


def _diff_attn_kernel(lpar_ref, subln_ref, q_ref, k_ref, v_ref, o_ref, *, lambda_init):
    t = pl.program_id(2)
    lp = lpar_ref[...]
    lam = (jnp.exp(jnp.sum(lp[0:1] * lp[1:2], axis=-1, keepdims=True))
           - jnp.exp(jnp.sum(lp[2:3] * lp[3:4], axis=-1, keepdims=True)) + lambda_init)
    q = q_ref[...]
    lane = lax.broadcasted_iota(jnp.int32, (TM, LANES), 1)
    lo = lane < HEAD_DIM
    zero = jnp.zeros((TM, LANES), BF16)
    qs = jnp.concatenate([jnp.where(lo, q, zero), jnp.where(lo, zero, q)], axis=0)

    def attend(n_keys):
        s = _dot_nt(qs, k_ref[0:n_keys, :])
        p = jnp.exp(s - jnp.max(s, axis=-1, keepdims=True))
        p = p / jnp.sum(p, axis=-1, keepdims=True)
        w = (p[0:TM] - lam * p[TM:2 * TM]).astype(BF16)
        y = _dot(w, v_ref[0:n_keys, :])
        o_ref[...] = (_rms(y) * subln_ref[...] * (1.0 - lambda_init)).astype(BF16)

    @pl.when(t == 0)
    def _():
        attend(CTX_LEN)

    @pl.when(t > 0)
    def _():
        attend(TT)


def _diff_attn(lpar, subln, qb, kb, vb, lambda_init):
    return pl.pallas_call(
        functools.partial(_diff_attn_kernel, lambda_init=lambda_init),
        out_shape=jax.ShapeDtypeStruct((BATCH, TT, B_V_W), BF16),
        grid=(BATCH, B_HEADS, N_TILES),
        in_specs=[
            pl.BlockSpec((4, HEAD_DIM), lambda b, h, t: (0, 0)),
            pl.BlockSpec((1, LANES), lambda b, h, t: (0, 0)),
            pl.BlockSpec((None, TM, LANES), lambda b, h, t: (b, t, h)),
            pl.BlockSpec((None, TT, LANES), lambda b, h, t: (b, 0, h)),
            pl.BlockSpec((None, TT, LANES), lambda b, h, t: (b, 0, h)),
        ],
        out_specs=pl.BlockSpec((None, TM, LANES), lambda b, h, t: (b, t, h)),
        compiler_params=_params(),
        name="diff_attention",
    )(lpar, subln, qb, kb, vb)


FF_CHUNK = 1024


def _mix_mlp_kernel(h_ref, ua_ref, ub_ref, mod_ref, g_ref, wo_ref, w1_ref, w2_ref, o_ref):
    mod = mod_ref[...]
    half = D_MODEL // 2
    y = _dot(ua_ref[...], wo_ref[0:half, :]) + _dot(ub_ref[...], wo_ref[half:D_MODEL, :])
    h1 = h_ref[...] + mod[2:3] * y
    f = _modnorm(h1, g_ref[...], mod[3:4], mod[4:5]).astype(BF16)
    acc = jnp.zeros((TM, D_MODEL), F32)
    for c in range(D_FF // FF_CHUNK):
        hid = jnp.maximum(_dot(f, w1_ref[:, c * FF_CHUNK:(c + 1) * FF_CHUNK]), 0.0)
        acc = acc + _dot((hid * hid).astype(BF16), w2_ref[c * FF_CHUNK:(c + 1) * FF_CHUNK, :])
    o_ref[...] = h1 + mod[5:6] * acc


def _mix_mlp(h, ua, ub, mods, g, wo, w1, w2, *, latent_only):
    n_tiles = SEQ // TM if latent_only else N_TILES
    h_off = N_TILES - n_tiles
    half = D_MODEL // 2
    ub_col = 1 if ub.shape[-1] == D_MODEL else 0

    def full(shape):
        return pl.BlockSpec(shape, lambda b, t: (0,) * len(shape), pipeline_mode=pl.Buffered(1))

    return pl.pallas_call(
        _mix_mlp_kernel,
        out_shape=jax.ShapeDtypeStruct((BATCH, n_tiles * TM, D_MODEL), F32),
        grid=(BATCH, n_tiles),
        in_specs=[
            pl.BlockSpec((None, TM, D_MODEL), lambda b, t: (b, t + h_off, 0)),
            pl.BlockSpec((None, TM, half), lambda b, t: (b, t, 0)),
            pl.BlockSpec((None, TM, half), lambda b, t: (b, t, ub_col)),
            pl.BlockSpec((None, None, 6, D_MODEL), lambda b, t: (jnp.minimum(t + h_off, 1), b, 0, 0)),
            pl.BlockSpec((1, D_MODEL), lambda b, t: (0, 0)),
            full((D_MODEL, D_MODEL)), full((D_MODEL, D_FF)), full((D_FF, D_MODEL)),
        ],
        out_specs=pl.BlockSpec((None, TM, D_MODEL), lambda b, t: (b, t, 0)),
        compiler_params=_params(),
        name="mixer_out_mlp",
    )(h, ua, ub, mods, g, wo, w1, w2)


S5_PAIR = 2
S5_TOK = S5_PAIR * S5_CHUNK
S5_STEPS = N_CHUNKS // S5_PAIR
GROUPS_PER_VREG = LANES // S5_GROUP
STEPS_PER_VREG = LANES // S5_GROUP


def _lane_block():
    return lax.broadcasted_iota(jnp.int32, (BATCH, LANES), 1) // S5_GROUP


def _s5_in_kernel(h_ref, mod_ref, g_ref, perm_ref, w_ref, u_ref, z_ref, u_scr):
    mod = mod_ref[...]
    x = h_ref[...]
    a = _rms(x) * g_ref[...] * (1.0 + mod[:, 1:2, :]) + mod[:, 0:1, :]
    a = a.reshape(BATCH * S5_TOK, D_MODEL).astype(BF16)
    a = _dot(perm_ref[...], a).astype(BF16)
    u = _dot(a, w_ref[...])
    u_ref[...] = u.reshape(S5_TOK, BATCH, D_MODEL)
    for k in range(D_MODEL // LANES):
        u_scr[k] = u[:, k * LANES:(k + 1) * LANES]
    lane_blk = _lane_block()
    for g in range(S5_GROUPS):
        k, r = divmod(g, GROUPS_PER_VREG)
        for hh in range(S5_CHUNK // STEPS_PER_VREG):
            halves = []
            for c2 in range(S5_PAIR):
                acc = None
                for m in range(STEPS_PER_VREG):
                    tok = c2 * S5_CHUNK + hh * STEPS_PER_VREG + m
                    piece = u_scr[k, tok * BATCH:(tok + 1) * BATCH, :]
                    sh = (S5_GROUP * (m - r)) % LANES
                    if sh:
                        piece = pltpu.roll(piece, sh, 1)
                    acc = piece if acc is None else jnp.where(lane_blk == m, piece, acc)
                halves.append(acc)
            z_ref[g, :, hh * LANES:(hh + 1) * LANES] = jnp.concatenate(halves, axis=0).astype(BF16)


def _s5_in(h, mods, g, perm, w_in):
    ctx_steps = N_CTX_CHUNKS // S5_PAIR
    n_rows = BATCH * S5_TOK
    return pl.pallas_call(
        _s5_in_kernel,
        out_shape=[jax.ShapeDtypeStruct((TT, BATCH, D_MODEL), F32),
                   jax.ShapeDtypeStruct((S5_GROUPS, ROWS, S5_CW), BF16)],
        grid=(S5_STEPS,),
        in_specs=[pl.BlockSpec((BATCH, S5_TOK, D_MODEL), lambda p: (0, p, 0)),
                  pl.BlockSpec((None, BATCH, 6, D_MODEL), lambda p: (jnp.minimum(p // ctx_steps, 1), 0, 0, 0)),
                  pl.BlockSpec((1, D_MODEL), lambda p: (0, 0)),
                  pl.BlockSpec((n_rows, n_rows), lambda p: (0, 0)),
                  pl.BlockSpec((D_MODEL, D_MODEL), lambda p: (0, 0))],
        out_specs=[pl.BlockSpec((S5_TOK, BATCH, D_MODEL), lambda p: (p, 0, 0)),
                   pl.BlockSpec((S5_GROUPS, S5_PAIR * BATCH, S5_CW), lambda p: (0, p, 0))],
        scratch_shapes=[pltpu.VMEM((D_MODEL // LANES, n_rows, LANES), F32)],
        compiler_params=_params(),
        name="s5_in_proj",
    )(h, mods, g, perm, w_in)


def _s5_ops_kernel(lr_c_ref, li_c_ref, ls_c_ref, lr_r_ref, li_r_ref, ls_r_ref,
                   btr_ref, bti_ref, btile_r_ref, btile_i_ref, ctile_r_ref, ctile_i_ref,
                   m_ref, q_ref, n_ref, a_ref):
    P2 = 2 * S5_STATE

    def discretize(lr, li, ls):
        dt = jnp.exp(ls)
        mag = jnp.exp(lr * dt)
        ab_re = mag * jnp.cos(li * dt)
        ab_im = mag * jnp.sin(li * dt)
        den = lr * lr + li * li
        nr = ab_re - 1.0
        f_re = (nr * lr + ab_im * li) / den
        f_im = (ab_im * lr - nr * li) / den
        return dt, f_re, f_im

    lr_c, li_c = lr_c_ref[...], li_c_ref[...]
    dt_c, f_re_c, f_im_c = discretize(lr_c, li_c, ls_c_ref[...])
    lag = (lax.broadcasted_iota(jnp.int32, (P2, S5_CW), 1) // S5_GROUP).astype(F32)
    is_fwd = lax.broadcasted_iota(jnp.int32, (P2, S5_CW), 0) < S5_STATE
    e_k = jnp.where(is_fwd, lag, (S5_CHUNK - 1) - lag)

    def a_pow(k):
        mag = jnp.exp(lr_c * dt_c * k)
        th = li_c * dt_c * k
        return mag * jnp.cos(th), mag * jnp.sin(th)

    c_re, c_im = ctile_r_ref[...], ctile_i_ref[...]
    p_re, p_im = a_pow(e_k)
    cp_re = c_re * p_re - c_im * p_im
    cp_im = c_re * p_im + c_im * p_re

    lr_r, li_r = lr_r_ref[...], li_r_ref[...]
    dt_r, f_re_r, f_im_r = discretize(lr_r, li_r, ls_r_ref[...])
    bt_re = f_re_r * btr_ref[...] - f_im_r * bti_ref[...]
    bt_im = f_re_r * bti_ref[...] + f_im_r * btr_ref[...]
    lane_fwd = lax.broadcasted_iota(jnp.int32, (S5_GROUP, P2), 1) < S5_STATE

    def lag_kernels(keep):
        br = jnp.where(keep, bt_re, 0.0)
        bi = jnp.where(keep, bt_im, 0.0)
        hi = lax.Precision.HIGHEST
        return (jnp.dot(br, cp_re, precision=hi, preferred_element_type=F32)
                - jnp.dot(bi, cp_im, precision=hi, preferred_element_type=F32))

    kt_f = lag_kernels(lane_fwd)
    kt_b = lag_kernels(jnp.logical_not(lane_fwd))
    lane_w = lax.broadcasted_iota(jnp.int32, (S5_GROUP, S5_CW), 1)
    for s in range(S5_CHUNK):
        f_part = kt_f if s == 0 else jnp.where(lane_w >= S5_GROUP * s, pltpu.roll(kt_f, S5_GROUP * s, 1), 0.0)
        sh = (S5_GROUP * (s + 1)) % S5_CW
        b_roll = kt_b if sh == 0 else pltpu.roll(kt_b, sh, 1)
        b_part = jnp.where(lane_w < S5_GROUP * (s + 1), b_roll, 0.0)
        m_ref[s * S5_GROUP:(s + 1) * S5_GROUP, :] = (f_part + b_part).astype(BF16)

    bb_re = f_re_c * btile_r_ref[...] - f_im_c * btile_i_ref[...]
    bb_im = f_re_c * btile_i_ref[...] + f_im_c * btile_r_ref[...]
    p_re, p_im = a_pow((S5_CHUNK - 1) - e_k)
    q_ref[0:P2, :] = (p_re * bb_re - p_im * bb_im).astype(BF16)
    q_ref[P2:2 * P2, :] = (p_re * bb_im + p_im * bb_re).astype(BF16)

    p_re, p_im = a_pow(e_k + 1.0)
    n_ref[0:P2, :] = (c_re * p_re - c_im * p_im).astype(BF16)
    n_ref[P2:2 * P2, :] = (-(c_re * p_im + c_im * p_re)).astype(BF16)

    mag = jnp.exp(lr_r * dt_r * float(S5_CHUNK))
    th = li_r * dt_r * float(S5_CHUNK)
    a_ref[0:1, :] = mag * jnp.cos(th)
    a_ref[1:2, :] = mag * jnp.sin(th)


def _s5_ops(col_params, row_params, bt, btile, ctile):
    P2 = 2 * S5_STATE
    col = pl.BlockSpec((None, P2, 1), lambda g: (g, 0, 0))
    row = pl.BlockSpec((None, 1, P2), lambda g: (g, 0, 0))
    btspec = pl.BlockSpec((None, S5_GROUP, P2), lambda g: (g, 0, 0))
    tile = pl.BlockSpec((None, P2, S5_CW), lambda g: (g, 0, 0))
    sq = pl.BlockSpec((None, S5_CW, S5_CW), lambda g: (g, 0, 0))
    return pl.pallas_call(
        _s5_ops_kernel,
        out_shape=[jax.ShapeDtypeStruct((S5_GROUPS, S5_CW, S5_CW), BF16)] * 3
        + [jax.ShapeDtypeStruct((S5_GROUPS, 2, P2), F32)],
        grid=(S5_GROUPS,),
        in_specs=[col, col, col, row, row, row, btspec, btspec, tile, tile, tile, tile],
        out_specs=[sq, sq, sq, pl.BlockSpec((None, 2, P2), lambda g: (g, 0, 0))],
        compiler_params=_params(),
        name="s5_chunk_operators",
    )(*col_params, *row_params, *bt, *btile, *ctile)


ROWS = N_CHUNKS * BATCH


def _s5_scan_kernel(u_ref, m_ref, q_ref, n_ref, a_ref, y_ref, pu_ref, sp_ref):
    u = u_ref[...]
    pu_ref[...] = _dot_nt(u, q_ref[...])
    a_re = a_ref[0:1, :]
    a_im = a_ref[1:2, :]
    lane = lax.broadcasted_iota(jnp.int32, (BATCH, 2 * S5_STATE), 1)
    lo = lane < S5_STATE
    S = S5_STATE

    def step(k, carry):
        s_re, s_im = carry
        cf = pl.multiple_of(k * BATCH, BATCH)
        cb = pl.multiple_of(jnp.where(k < N_CTX_CHUNKS, N_CTX_CHUNKS - 1 - k,
                                      N_CHUNKS + N_CTX_CHUNKS - 1 - k) * BATCH, BATCH)
        sp_ref[pl.ds(cf, BATCH), 0:S] = s_re[:, 0:S]
        sp_ref[pl.ds(cb, BATCH), S:2 * S] = s_re[:, S:2 * S]
        sp_ref[pl.ds(cf, BATCH), 2 * S:3 * S] = s_im[:, 0:S]
        sp_ref[pl.ds(cb, BATCH), 3 * S:4 * S] = s_im[:, S:2 * S]
        x_re = jnp.where(lo, pu_ref[pl.ds(cf, BATCH), 0:2 * S], pu_ref[pl.ds(cb, BATCH), 0:2 * S])
        x_im = jnp.where(lo, pu_ref[pl.ds(cf, BATCH), 2 * S:4 * S], pu_ref[pl.ds(cb, BATCH), 2 * S:4 * S])
        return (a_re * s_re - a_im * s_im + x_re, a_re * s_im + a_im * s_re + x_im)

    zero = jnp.zeros((BATCH, 2 * S), F32)
    lax.fori_loop(0, N_CHUNKS, step, (zero, zero))
    y_ref[...] = _dot(u, m_ref[...]) + _dot(sp_ref[...].astype(BF16), n_ref[...])


def _s5_scan(u_g, m_op, q_op, n_op, a_vec):
    sq = pl.BlockSpec((None, S5_CW, S5_CW), lambda g: (g, 0, 0))
    rows = pl.BlockSpec((None, ROWS, S5_CW), lambda g: (g, 0, 0))
    return pl.pallas_call(
        _s5_scan_kernel,
        out_shape=jax.ShapeDtypeStruct((S5_GROUPS, ROWS, S5_CW), F32),
        grid=(S5_GROUPS,),
        in_specs=[rows, sq, sq, sq, pl.BlockSpec((None, 2, 2 * S5_STATE), lambda g: (g, 0, 0))],
        out_specs=rows,
        scratch_shapes=[pltpu.VMEM((ROWS, S5_CW), F32), pltpu.VMEM((ROWS, S5_CW), F32)],
        compiler_params=_params(),
        name="s5_scan",
    )(u_g, m_op, q_op, n_op, a_vec)


def _s5_out_kernel(u_ref, y_ref, d_ref, gw_ref, gb_ref, perm_ref, o_ref, ys_scr):
    lane_blk = _lane_block()
    for k in range(S5_GROUPS // GROUPS_PER_VREG):
        for c2 in range(S5_PAIR):
            for t in range(S5_CHUNK):
                hh, m = divmod(t, STEPS_PER_VREG)
                acc = None
                for r in range(GROUPS_PER_VREG):
                    piece = y_ref[k * GROUPS_PER_VREG + r, c2 * BATCH:(c2 + 1) * BATCH, hh * LANES:(hh + 1) * LANES]
                    sh = (S5_GROUP * (r - m)) % LANES
                    if sh:
                        piece = pltpu.roll(piece, sh, 1)
                    acc = piece if acc is None else jnp.where(lane_blk == r, piece, acc)
                tok = c2 * S5_CHUNK + t
                ys_scr[k, tok * BATCH:(tok + 1) * BATCH, :] = acc
    ys = jnp.concatenate([ys_scr[k] for k in range(D_MODEL // LANES)], axis=1)
    y = u_ref[...].reshape(BATCH * S5_TOK, D_MODEL) * d_ref[...] + ys
    g = jax.nn.gelu(y)
    gate = jax.nn.sigmoid(_dot(g.astype(BF16), gw_ref[...]) + gb_ref[...])
    gated = (g * gate).astype(BF16)
    gated = _dot(perm_ref[...], gated).astype(BF16)
    o_ref[...] = gated.reshape(BATCH, S5_TOK, D_MODEL)


def _s5_out(u, y_g, d_skip, glu_w, glu_b, perm_t):
    ctx_steps = N_CTX_CHUNKS // S5_PAIR
    n_rows = BATCH * S5_TOK
    vec = pl.BlockSpec((1, D_MODEL), lambda p: (0, 0))
    return pl.pallas_call(
        _s5_out_kernel,
        out_shape=jax.ShapeDtypeStruct((BATCH, SEQ, D_MODEL), BF16),
        grid=(S5_STEPS - ctx_steps,),
        in_specs=[pl.BlockSpec((S5_TOK, BATCH, D_MODEL), lambda p: (p + ctx_steps, 0, 0)),
                  pl.BlockSpec((S5_GROUPS, S5_PAIR * BATCH, S5_CW), lambda p: (0, p + ctx_steps, 0)),
                  vec, pl.BlockSpec((D_MODEL, D_MODEL), lambda p: (0, 0)), vec,
                  pl.BlockSpec((n_rows, n_rows), lambda p: (0, 0))],
        out_specs=pl.BlockSpec((BATCH, S5_TOK, D_MODEL), lambda p: (0, p, 0)),
        scratch_shapes=[pltpu.VMEM((D_MODEL // LANES, n_rows, LANES), F32)],
        compiler_params=_params(),
        name="s5_gelu_glu",
    )(u, y_g, d_skip, glu_w, glu_b, perm_t)


def _rope_tables():
    rows_n = SEQ // GRID_W
    row = jnp.repeat(jnp.arange(rows_n, dtype=F32), GRID_W)
    col = jnp.tile(jnp.arange(GRID_W, dtype=F32), rows_n)
    n_freq = HEAD_DIM // 4
    inv = ROPE_BASE ** (-jnp.arange(n_freq, dtype=F32) / n_freq)
    ang = jnp.concatenate([row[:, None] * inv, col[:, None] * inv], axis=-1)
    reps = LANES // (HEAD_DIM // 2)
    cos_t = jnp.tile(jnp.cos(ang), (1, reps))
    sin_t = jnp.tile(jnp.sin(ang), (1, reps))
    sign = jnp.where((jnp.arange(LANES) % HEAD_DIM) < HEAD_DIM // 2, -1.0, 1.0).astype(F32)
    cos_t = jnp.concatenate([jnp.ones((CTX_LEN, LANES), F32), cos_t], axis=0)
    sin_s = jnp.concatenate([jnp.zeros((CTX_LEN, LANES), F32), sin_t * sign], axis=0)
    return cos_t, sin_s


def _fb_cols(x):
    return jnp.transpose(x, (1, 0, 2)).reshape(S5_GROUPS, 2 * S5_STATE, 1)


def _fb_rows(x):
    return jnp.transpose(x, (1, 0, 2)).reshape(S5_GROUPS, 1, 2 * S5_STATE)


def _s5_layout(lam_re, lam_im, log_step, b_re, b_im, c_re, c_im):
    ls = jnp.broadcast_to(log_step[:, :, None], lam_re.shape)
    cols = [_fb_cols(v) for v in (lam_re, lam_im, ls)]
    rows = [_fb_rows(v) for v in (lam_re, lam_im, ls)]

    def bt_of(b):
        return jnp.transpose(b, (1, 3, 0, 2)).reshape(S5_GROUPS, S5_GROUP, 2 * S5_STATE)

    def btile_of(b):
        t = jnp.transpose(b, (1, 0, 2, 3)).reshape(S5_GROUPS, 2 * S5_STATE, S5_GROUP)
        return jnp.tile(t, (1, 1, S5_CHUNK))

    def ctile_of(c):
        t = jnp.transpose(c, (1, 0, 3, 2)).reshape(S5_GROUPS, 2 * S5_STATE, S5_GROUP)
        return jnp.tile(t, (1, 1, S5_CHUNK))

    return cols, rows, [bt_of(b_re), bt_of(b_im)], [btile_of(b_re), btile_of(b_im)], [ctile_of(c_re), ctile_of(c_im)]


def kernel(x, c, ctx, c_ctx, norm1_g, norm2_g, mod_w, mod_b, mlp_w1, mlp_w2, attn_w_in, attn_w_out, a_q_norm, a_k_norm, a_sink, b_q_norm, b_k_norm, b_lq1, b_lk1, b_lq2, b_lk2, b_subln, s5_w_in, s5_lambda_re, s5_lambda_im, s5_log_step, s5_b_re, s5_b_im, s5_c_re, s5_c_im, s5_d, s5_glu_w, s5_glu_b, s5_w_out):
    assert x.shape == (BATCH, SEQ, D_MODEL) and ctx.shape == (BATCH, CTX_LEN, D_MODEL)
    h = jnp.concatenate([ctx, x], axis=1)
    s_rows = jnp.concatenate([c, c_ctx[None], jnp.zeros((16 - BATCH - 1, D_MODEL), F32)], axis=0)
    m_all = _modulation(s_rows, mod_w, mod_b)
    cos_t, sin_s = _rope_tables()
    e_blk = (jnp.kron(jnp.eye(LANES // HEAD_DIM, dtype=F32), jnp.ones((HEAD_DIM, HEAD_DIM), F32))
             / HEAD_DIM).astype(BF16)

    for i in range(DEPTH):
        last = i == DEPTH - 1
        j = i // 2
        m_lat = m_all[i, :BATCH].reshape(BATCH, 6, D_MODEL)
        m_ctx = jnp.broadcast_to(m_all[i, BATCH].reshape(1, 6, D_MODEL), (BATCH, 6, D_MODEL))
        mods = jnp.stack([m_ctx, m_lat])
        g1 = norm1_g[i].reshape(1, D_MODEL)
        g2 = norm2_g[i].reshape(1, D_MODEL)
        if i % 2 == 0:
            lambda_init = 0.8 - 0.6 * math.exp(-0.3 * i)
            gains = jnp.stack([jnp.tile(v[j], LANES // HEAD_DIM) for v in (a_q_norm, a_k_norm, b_q_norm, b_k_norm)])
            qa, k2a, v2a, qb, kb, vb = _attn_in(h, mods, g1, attn_w_in[j].astype(BF16), gains, cos_t, sin_s, e_blk)
            ya = _win_attn(a_sink[j], qa, k2a, v2a)
            lpar = jnp.stack([b_lq1[j], b_lk1[j], b_lq2[j], b_lk2[j]])
            yb = _diff_attn(lpar, b_subln[j].reshape(1, LANES), qb, kb, vb, lambda_init)
            if last:
                ya, yb = ya[:, CTX_LEN:], yb[:, CTX_LEN:]
            ua, ub, wo = ya, yb, attn_w_out[j]
        else:
            src = np.arange(BATCH * S5_TOK).reshape(BATCH, S5_TOK).T.reshape(-1)
            perm = jnp.asarray(np.eye(BATCH * S5_TOK, dtype=np.float32)[src], BF16)
            u, u_g = _s5_in(h, mods, g1, perm, s5_w_in[j].astype(BF16))
            ops_in = _s5_layout(s5_lambda_re[j], s5_lambda_im[j], s5_log_step[j], s5_b_re[j], s5_b_im[j],
                                s5_c_re[j], s5_c_im[j])
            m_op, q_op, n_op, a_vec = _s5_ops(*ops_in)
            y_g = _s5_scan(u_g, m_op, q_op, n_op, a_vec)
            assert last, "S5 layers before the last one would also need the context rows of the readout"
            gated = _s5_out(u, y_g, s5_d[j].reshape(1, D_MODEL), s5_glu_w[j].astype(BF16),
                            s5_glu_b[j].reshape(1, D_MODEL), perm.T)
            ua, ub, wo = gated, gated, s5_w_out[j]
        h = _mix_mlp(h, ua, ub, mods, g2, wo.astype(BF16), mlp_w1[i].astype(BF16), mlp_w2[i].astype(BF16),
                     latent_only=last)
    return h
```

```python
import functools
import math

import jax
import jax.numpy as jnp
import numpy as np
from jax import lax
from jax.experimental import pallas as pl
from jax.experimental.pallas import tpu as pltpu

F32 = jnp.float32
BF16 = jnp.bfloat16

D_MODEL = 1024
BATCH = 8
SEQ = 2048
DEPTH = 2
GRID_W = 64
CTX_LEN = 256
HEAD_DIM = 64
WINDOW = 128
A_Q_HEADS = 8
A_KV_HEADS = 2
B_HEADS = 4
A_Q_W = A_Q_HEADS * HEAD_DIM
A_KV_W = A_KV_HEADS * HEAD_DIM
B_QK_W = B_HEADS * 2 * HEAD_DIM
B_V_W = B_HEADS * 2 * HEAD_DIM
ATTN_IN = A_Q_W + 2 * A_KV_W + 2 * B_QK_W + B_V_W
S5_GROUP = 16
S5_GROUPS = D_MODEL // S5_GROUP
S5_STATE = 64
D_FF = 4 * D_MODEL
ROPE_BASE = 10000.0
EPS = 1e-6
NEG_INF = -1e30
LOG2E = math.log2(math.e)

TT = CTX_LEN + SEQ
TM = 256
N_TILES = TT // TM
LANES = 128
S5_CHUNK = 16
S5_CW = S5_CHUNK * S5_GROUP
N_CHUNKS = TT // S5_CHUNK
N_CTX_CHUNKS = CTX_LEN // S5_CHUNK
VMEM_LIMIT = 56 * 1024 * 1024


def _dot(a, b):
    return jnp.dot(a, b, preferred_element_type=F32)


def _dot_nt(a, b):
    return lax.dot_general(a, b, (((1,), (1,)), ((), ())), preferred_element_type=F32)


def _rms(x):
    return x * lax.rsqrt(jnp.mean(x * x, axis=-1, keepdims=True) + EPS)


def _modnorm(x, g, shift, scale):
    return _rms(x) * g * (1.0 + scale) + shift


def _params(**kw):
    return pltpu.CompilerParams(vmem_limit_bytes=VMEM_LIMIT, **kw)


def _mod_kernel(s_ref, w_ref, b_ref, o_ref):
    s = s_ref[...]
    s = s * jax.nn.sigmoid(s)
    o_ref[...] = _dot(s.astype(BF16), w_ref[...].astype(BF16)) + b_ref[...]


def _modulation(s_rows, mod_w, mod_b):
    return pl.pallas_call(
        _mod_kernel,
        out_shape=jax.ShapeDtypeStruct((DEPTH, 16, 6 * D_MODEL), F32),
        grid=(DEPTH, 6),
        in_specs=[
            pl.BlockSpec((16, D_MODEL), lambda i, j: (0, 0)),
            pl.BlockSpec((None, D_MODEL, D_MODEL), lambda i, j: (i, 0, j)),
            pl.BlockSpec((None, 1, D_MODEL), lambda i, j: (i, 0, j)),
        ],
        out_specs=pl.BlockSpec((None, 16, D_MODEL), lambda i, j: (i, 0, j)),
        compiler_params=_params(),
        name="modulation",
    )(s_rows, mod_w, mod_b.reshape(DEPTH, 1, 6 * D_MODEL))


def _mod_spec():
    return pl.BlockSpec((None, None, 6, D_MODEL), lambda b, t: (jnp.minimum(t, 1), b, 0, 0))


def _attn_in_kernel(h_ref, mod_ref, g_ref, w_ref, gain_ref, cos_ref, sin_ref, e_ref,
                    qa_ref, k2a_ref, v2a_ref, qb_ref, kb_ref, vb_ref):
    mod = mod_ref[...]
    a = _modnorm(h_ref[...], g_ref[...], mod[0:1], mod[1:2]).astype(BF16)
    z = _dot(a, w_ref[...])
    cos_t = cos_ref[...]
    sin_s = sin_ref[...]
    e = e_ref[...]
    gains = gain_ref[...]
    lane = lax.broadcasted_iota(jnp.int32, (TM, LANES), 1)
    first_half = (lane & (HEAD_DIM - 1)) < HEAD_DIM // 2
    lo = lane < HEAD_DIM

    def norm_rope(c, gain):
        ms = _dot((c * c).astype(BF16), e)
        cn = c * lax.rsqrt(ms + EPS) * gain
        r_fwd = pltpu.roll(cn, HEAD_DIM // 2, 1)
        r_bwd = pltpu.roll(cn, LANES - HEAD_DIM // 2, 1)
        return cn * cos_t + jnp.where(first_half, r_bwd, r_fwd) * sin_s

    def dup_halves(x, ref):
        sw = pltpu.roll(x, HEAD_DIM, 1)
        ref[:, 0:LANES] = jnp.where(lo, x, sw).astype(BF16)
        ref[:, LANES:2 * LANES] = jnp.where(lo, sw, x).astype(BF16)

    q_scale = HEAD_DIM ** -0.5 * LOG2E
    off = 0
    for c in range(A_Q_W // LANES):
        qa_ref[:, c * LANES:(c + 1) * LANES] = (
            norm_rope(z[:, off:off + LANES], gains[0:1]) * q_scale).astype(BF16)
        off += LANES
    dup_halves(norm_rope(z[:, off:off + LANES], gains[1:2]), k2a_ref)
    off += LANES
    dup_halves(z[:, off:off + LANES], v2a_ref)
    off += LANES
    for c in range(B_QK_W // LANES):
        qb_ref[:, c * LANES:(c + 1) * LANES] = (
            norm_rope(z[:, off:off + LANES], gains[2:3]) * q_scale).astype(BF16)
        off += LANES
    for c in range(B_QK_W // LANES):
        kb_ref[:, c * LANES:(c + 1) * LANES] = norm_rope(z[:, off:off + LANES], gains[3:4]).astype(BF16)
        off += LANES
    vb_ref[...] = z[:, off:off + B_V_W].astype(BF16)


def _attn_in(h, mods, g, w_in, gains, cos_t, sin_s, e_blk):
    def tok(width):
        return pl.BlockSpec((None, TM, width), lambda b, t: (b, t, 0))

    def full(shape):
        return pl.BlockSpec(shape, lambda b, t: (0,) * len(shape))

    out_shapes = [jax.ShapeDtypeStruct((BATCH, TT, w), BF16)
                  for w in (A_Q_W, 2 * A_KV_W, 2 * A_KV_W, B_QK_W, B_QK_W, B_V_W)]
    return pl.pallas_call(
        _attn_in_kernel,
        out_shape=out_shapes,
        grid=(BATCH, N_TILES),
        in_specs=[
            tok(D_MODEL), _mod_spec(), full((1, D_MODEL)), full((D_MODEL, ATTN_IN)), full((4, LANES)),
            pl.BlockSpec((TM, LANES), lambda b, t: (t, 0)), pl.BlockSpec((TM, LANES), lambda b, t: (t, 0)),
            full((LANES, LANES)),
        ],
        out_specs=[tok(A_Q_W), tok(2 * A_KV_W), tok(2 * A_KV_W), tok(B_QK_W), tok(B_QK_W), tok(B_V_W)],
        compiler_params=_params(),
        name="attn_in_proj",
    )(h, mods, g, w_in, gains, cos_t, sin_s, e_blk)


QB = 128


def _win_attn_kernel(sink_ref, q_ref, k2_ref, v2_ref, o_ref):
    t = pl.program_id(1)
    lane = lax.broadcasted_iota(jnp.int32, (QB, LANES), 1)
    lo = lane < HEAD_DIM
    rows = 4 * QB
    row = lax.broadcasted_iota(jnp.int32, (rows, 3 * QB), 0)
    col = lax.broadcasted_iota(jnp.int32, (rows, 3 * QB), 1)
    row_head = lax.broadcasted_iota(jnp.int32, (rows, 1), 0) // QB
    zero = jnp.zeros((QB, LANES), BF16)
    blocks = [(qb, g) for qb in range(TM // QB) for g in range(A_KV_HEADS)]

    def window_start(qb):
        n = (t - 1) * (TM // QB) + qb
        ws = jnp.clip((n - 1) * QB, 0, SEQ - 3 * QB)
        return n, ws

    def scores(qb, g, with_window):
        pieces = []
        for p in range(2):
            qp = q_ref[qb * QB:(qb + 1) * QB, g * 2 * LANES + p * LANES: g * 2 * LANES + (p + 1) * LANES]
            pieces.append(jnp.where(lo, qp, zero))
            pieces.append(jnp.where(lo, zero, qp))
        qs = jnp.concatenate(pieces, axis=0)
        s_c = _dot_nt(qs, k2_ref[0:CTX_LEN, g * LANES:(g + 1) * LANES])
        if not with_window:
            return s_c, None
        n, ws = window_start(qb)
        kw = k2_ref[pl.ds(pl.multiple_of(ws + CTX_LEN, QB), 3 * QB), g * LANES:(g + 1) * LANES]
        valid = jnp.abs(n * QB + (row & (QB - 1)) - (ws + col)) <= WINDOW
        return s_c, jnp.where(valid, _dot_nt(qs, kw), NEG_INF)

    def finish(qb, g, s_c, s_w):
        sk = jnp.full((rows, 1), sink_ref[4 * g + 3], F32)
        for hh in range(3):
            sk = jnp.where(row_head == hh, sink_ref[4 * g + hh], sk)
        sk = sk * LOG2E
        m = jnp.maximum(jnp.max(s_c, axis=-1, keepdims=True), sk)
        if s_w is not None:
            m = jnp.maximum(m, jnp.max(s_w, axis=-1, keepdims=True))
        p_c = jnp.exp2(s_c - m)
        denom = jnp.sum(p_c, axis=-1, keepdims=True) + jnp.exp2(sk - m)
        o = _dot(p_c.astype(BF16), v2_ref[0:CTX_LEN, g * LANES:(g + 1) * LANES])
        if s_w is not None:
            _, ws = window_start(qb)
            p_w = jnp.exp2(s_w - m)
            denom = denom + jnp.sum(p_w, axis=-1, keepdims=True)
            vw = v2_ref[pl.ds(pl.multiple_of(ws + CTX_LEN, QB), 3 * QB), g * LANES:(g + 1) * LANES]
            o = o + _dot(p_w.astype(BF16), vw)
        o = o / denom
        for p in range(2):
            o_ref[qb * QB:(qb + 1) * QB, g * 2 * LANES + p * LANES: g * 2 * LANES + (p + 1) * LANES] = jnp.where(
                lo, o[2 * p * QB:(2 * p + 1) * QB], o[(2 * p + 1) * QB:(2 * p + 2) * QB]).astype(BF16)

    def attend(with_window):
        s_next = scores(*blocks[0], with_window)
        for i, blk in enumerate(blocks):
            s_cur = s_next
            if i + 1 < len(blocks):
                s_next = scores(*blocks[i + 1], with_window)
            finish(*blk, *s_cur)

    @pl.when(t == 0)
    def _():
        attend(False)

    @pl.when(t > 0)
    def _():
        attend(True)


def _win_attn(sink, qa, k2a, v2a):
    return pl.pallas_call(
        _win_attn_kernel,
        out_shape=jax.ShapeDtypeStruct((BATCH, TT, A_Q_W), BF16),
        grid=(BATCH, N_TILES),
        in_specs=[
            pl.BlockSpec(memory_space=pltpu.SMEM),
            pl.BlockSpec((None, TM, A_Q_W), lambda b, t: (b, t, 0)),
            pl.BlockSpec((None, TT, 2 * A_KV_W), lambda b, t: (b, 0, 0)),
            pl.BlockSpec((None, TT, 2 * A_KV_W), lambda b, t: (b, 0, 0)),
        ],
        out_specs=pl.BlockSpec((None, TM, A_Q_W), lambda b, t: (b, t, 0)),
        compiler_params=_params(),
        name="window_attention",
    )(sink, qa, k2a, v2a)


DIFF_ROWS = 128


def _diff_attn_kernel(lpar_ref, subln_ref, q_ref, k_ref, v_ref, o_ref, *, lambda_init):
    t = pl.program_id(1)
    lp = lpar_ref[...]
    lam = (jnp.exp(jnp.sum(lp[0:1] * lp[1:2], axis=-1, keepdims=True))
           - jnp.exp(jnp.sum(lp[2:3] * lp[3:4], axis=-1, keepdims=True)) + lambda_init)
    R = DIFF_ROWS
    lane = lax.broadcasted_iota(jnp.int32, (R, LANES), 1)
    lo = lane < HEAD_DIM
    zero = jnp.zeros((R, LANES), BF16)

    def attend(n_keys):
        blocks = [(slice(rb * R, (rb + 1) * R), slice(h * LANES, (h + 1) * LANES))
                  for h in range(B_HEADS) for rb in range(TM // R)]

        def scores(rows, cols):
            q = q_ref[rows, cols]
            qs = jnp.concatenate([jnp.where(lo, q, zero), jnp.where(lo, zero, q)], axis=0)
            return _dot_nt(qs, k_ref[0:n_keys, cols])

        def finish(rows, cols, s):
            p = jnp.exp2(s - jnp.max(s, axis=-1, keepdims=True))
            den = jnp.sum(p, axis=-1, keepdims=True)
            coef = lam * den[0:R] / den[R:2 * R]
            w = (p[0:R] - coef * p[R:2 * R]).astype(BF16)
            y = _dot(w, v_ref[0:n_keys, cols]) / den[0:R]
            o_ref[rows, cols] = (_rms(y) * subln_ref[...] * (1.0 - lambda_init)).astype(BF16)

        s_next = scores(*blocks[0])
        for i, blk in enumerate(blocks):
            s_cur = s_next
            if i + 1 < len(blocks):
                s_next = scores(*blocks[i + 1])
            finish(*blk, s_cur)

    @pl.when(t == 0)
    def _():
        attend(CTX_LEN)

    @pl.when(t > 0)
    def _():
        attend(TT)


def _diff_attn(lpar, subln, qb, kb, vb, lambda_init):
    return pl.pallas_call(
        functools.partial(_diff_attn_kernel, lambda_init=lambda_init),
        out_shape=jax.ShapeDtypeStruct((BATCH, TT, B_V_W), BF16),
        grid=(BATCH, N_TILES),
        in_specs=[
            pl.BlockSpec((4, HEAD_DIM), lambda b, t: (0, 0)),
            pl.BlockSpec((1, LANES), lambda b, t: (0, 0)),
            pl.BlockSpec((None, TM, B_QK_W), lambda b, t: (b, t, 0)),
            pl.BlockSpec((None, TT, B_QK_W), lambda b, t: (b, 0, 0)),
            pl.BlockSpec((None, TT, B_V_W), lambda b, t: (b, 0, 0)),
        ],
        out_specs=pl.BlockSpec((None, TM, B_V_W), lambda b, t: (b, t, 0)),
        compiler_params=_params(),
        name="diff_attention",
    )(lpar, subln, qb, kb, vb)


FF_CHUNK = 1024


def _mix_mlp_kernel(h_ref, ua_ref, ub_ref, mod_ref, g_ref, wo_ref, w1_ref, w2_ref, o_ref):
    mod = mod_ref[...]
    half = D_MODEL // 2
    y = _dot(ua_ref[...], wo_ref[0:half, :]) + _dot(ub_ref[...], wo_ref[half:D_MODEL, :])
    h1 = h_ref[...] + mod[2:3] * y
    f = _modnorm(h1, g_ref[...], mod[3:4], mod[4:5]).astype(BF16)
    acc = jnp.zeros((TM, D_MODEL), F32)
    for c in range(D_FF // FF_CHUNK):
        hid = jnp.maximum(_dot(f, w1_ref[:, c * FF_CHUNK:(c + 1) * FF_CHUNK]), 0.0)
        acc = acc + _dot((hid * hid).astype(BF16), w2_ref[c * FF_CHUNK:(c + 1) * FF_CHUNK, :])
    o_ref[...] = h1 + mod[5:6] * acc


def _mix_mlp(h, ua, ub, mods, g, wo, w1, w2, *, latent_only):
    n_tiles = SEQ // TM if latent_only else N_TILES
    h_off = N_TILES - n_tiles
    half = D_MODEL // 2
    ub_col = 1 if ub.shape[-1] == D_MODEL else 0

    def full(shape):
        return pl.BlockSpec(shape, lambda b, t: (0,) * len(shape), pipeline_mode=pl.Buffered(1))

    return pl.pallas_call(
        _mix_mlp_kernel,
        out_shape=jax.ShapeDtypeStruct((BATCH, n_tiles * TM, D_MODEL), F32),
        grid=(BATCH, n_tiles),
        in_specs=[
            pl.BlockSpec((None, TM, D_MODEL), lambda b, t: (b, t + h_off, 0)),
            pl.BlockSpec((None, TM, half), lambda b, t: (b, t, 0)),
            pl.BlockSpec((None, TM, half), lambda b, t: (b, t, ub_col)),
            pl.BlockSpec((None, None, 6, D_MODEL), lambda b, t: (jnp.minimum(t + h_off, 1), b, 0, 0)),
            pl.BlockSpec((1, D_MODEL), lambda b, t: (0, 0)),
            full((D_MODEL, D_MODEL)), full((D_MODEL, D_FF)), full((D_FF, D_MODEL)),
        ],
        out_specs=pl.BlockSpec((None, TM, D_MODEL), lambda b, t: (b, t, 0)),
        compiler_params=_params(),
        name="mixer_out_mlp",
    )(h, ua, ub, mods, g, wo, w1, w2)


S5_PAIR = 2
S5_TOK = S5_PAIR * S5_CHUNK
S5_STEPS = N_CHUNKS // S5_PAIR
GROUPS_PER_VREG = LANES // S5_GROUP
STEPS_PER_VREG = LANES // S5_GROUP


def _lane_block():
    return lax.broadcasted_iota(jnp.int32, (BATCH, LANES), 1) // S5_GROUP


def _s5_in_kernel(h_ref, mod_ref, g_ref, perm_ref, w_ref, u_ref, z_ref, u_scr):
    mod = mod_ref[...]
    x = h_ref[...]
    a = _rms(x) * g_ref[...] * (1.0 + mod[:, 1:2, :]) + mod[:, 0:1, :]
    a = a.reshape(BATCH * S5_TOK, D_MODEL).astype(BF16)
    a = _dot(perm_ref[...], a).astype(BF16)
    u = _dot(a, w_ref[...])
    u_ref[...] = u.reshape(S5_TOK, BATCH, D_MODEL)
    for k in range(D_MODEL // LANES):
        u_scr[k] = u[:, k * LANES:(k + 1) * LANES]
    lane_blk = _lane_block()
    for g in range(S5_GROUPS):
        k, r = divmod(g, GROUPS_PER_VREG)
        for hh in range(S5_CHUNK // STEPS_PER_VREG):
            halves = []
            for c2 in range(S5_PAIR):
                acc = None
                for m in range(STEPS_PER_VREG):
                    tok = c2 * S5_CHUNK + hh * STEPS_PER_VREG + m
                    piece = u_scr[k, tok * BATCH:(tok + 1) * BATCH, :]
                    sh = (S5_GROUP * (m - r)) % LANES
                    if sh:
                        piece = pltpu.roll(piece, sh, 1)
                    acc = piece if acc is None else jnp.where(lane_blk == m, piece, acc)
                halves.append(acc)
            z_ref[g, :, hh * LANES:(hh + 1) * LANES] = jnp.concatenate(halves, axis=0).astype(BF16)


def _s5_in(h, mods, g, perm, w_in):
    ctx_steps = N_CTX_CHUNKS // S5_PAIR
    n_rows = BATCH * S5_TOK
    return pl.pallas_call(
        _s5_in_kernel,
        out_shape=[jax.ShapeDtypeStruct((TT, BATCH, D_MODEL), F32),
                   jax.ShapeDtypeStruct((S5_GROUPS, ROWS, S5_CW), BF16)],
        grid=(S5_STEPS,),
        in_specs=[pl.BlockSpec((BATCH, S5_TOK, D_MODEL), lambda p: (0, p, 0)),
                  pl.BlockSpec((None, BATCH, 6, D_MODEL), lambda p: (jnp.minimum(p // ctx_steps, 1), 0, 0, 0)),
                  pl.BlockSpec((1, D_MODEL), lambda p: (0, 0)),
                  pl.BlockSpec((n_rows, n_rows), lambda p: (0, 0)),
                  pl.BlockSpec((D_MODEL, D_MODEL), lambda p: (0, 0))],
        out_specs=[pl.BlockSpec((S5_TOK, BATCH, D_MODEL), lambda p: (p, 0, 0)),
                   pl.BlockSpec((S5_GROUPS, S5_PAIR * BATCH, S5_CW), lambda p: (0, p, 0))],
        scratch_shapes=[pltpu.VMEM((D_MODEL // LANES, n_rows, LANES), F32)],
        compiler_params=_params(),
        name="s5_in_proj",
    )(h, mods, g, perm, w_in)


def _s5_ops_kernel(lr_c_ref, li_c_ref, ls_c_ref, lr_r_ref, li_r_ref, ls_r_ref,
                   btr_ref, bti_ref, btile_r_ref, btile_i_ref, ctile_r_ref, ctile_i_ref,
                   m_ref, q_ref, n_ref, a_ref):
    P2 = 2 * S5_STATE

    def discretize(lr, li, ls):
        dt = jnp.exp(ls)
        mag = jnp.exp(lr * dt)
        ab_re = mag * jnp.cos(li * dt)
        ab_im = mag * jnp.sin(li * dt)
        den = lr * lr + li * li
        nr = ab_re - 1.0
        f_re = (nr * lr + ab_im * li) / den
        f_im = (ab_im * lr - nr * li) / den
        return dt, f_re, f_im

    lr_c, li_c = lr_c_ref[...], li_c_ref[...]
    dt_c, f_re_c, f_im_c = discretize(lr_c, li_c, ls_c_ref[...])
    lag = (lax.broadcasted_iota(jnp.int32, (P2, S5_CW), 1) // S5_GROUP).astype(F32)
    is_fwd = lax.broadcasted_iota(jnp.int32, (P2, S5_CW), 0) < S5_STATE
    e_k = jnp.where(is_fwd, lag, (S5_CHUNK - 1) - lag)

    def a_pow(k):
        mag = jnp.exp(lr_c * dt_c * k)
        th = li_c * dt_c * k
        return mag * jnp.cos(th), mag * jnp.sin(th)

    c_re, c_im = ctile_r_ref[...], ctile_i_ref[...]
    p_re, p_im = a_pow(e_k)
    cp_re = c_re * p_re - c_im * p_im
    cp_im = c_re * p_im + c_im * p_re

    lr_r, li_r = lr_r_ref[...], li_r_ref[...]
    dt_r, f_re_r, f_im_r = discretize(lr_r, li_r, ls_r_ref[...])
    bt_re = f_re_r * btr_ref[...] - f_im_r * bti_ref[...]
    bt_im = f_re_r * bti_ref[...] + f_im_r * btr_ref[...]
    lane_fwd = lax.broadcasted_iota(jnp.int32, (S5_GROUP, P2), 1) < S5_STATE

    def lag_kernels(keep):
        br = jnp.where(keep, bt_re, 0.0)
        bi = jnp.where(keep, bt_im, 0.0)
        hi = lax.Precision.HIGHEST
        return (jnp.dot(br, cp_re, precision=hi, preferred_element_type=F32)
                - jnp.dot(bi, cp_im, precision=hi, preferred_element_type=F32))

    kt_f = lag_kernels(lane_fwd)
    kt_b = lag_kernels(jnp.logical_not(lane_fwd))
    lane_w = lax.broadcasted_iota(jnp.int32, (S5_GROUP, S5_CW), 1)
    for s in range(S5_CHUNK):
        f_part = kt_f if s == 0 else jnp.where(lane_w >= S5_GROUP * s, pltpu.roll(kt_f, S5_GROUP * s, 1), 0.0)
        sh = (S5_GROUP * (s + 1)) % S5_CW
        b_roll = kt_b if sh == 0 else pltpu.roll(kt_b, sh, 1)
        b_part = jnp.where(lane_w < S5_GROUP * (s + 1), b_roll, 0.0)
        m_ref[s * S5_GROUP:(s + 1) * S5_GROUP, :] = (f_part + b_part).astype(BF16)

    bb_re = f_re_c * btile_r_ref[...] - f_im_c * btile_i_ref[...]
    bb_im = f_re_c * btile_i_ref[...] + f_im_c * btile_r_ref[...]
    p_re, p_im = a_pow((S5_CHUNK - 1) - e_k)
    q_ref[0:P2, :] = (p_re * bb_re - p_im * bb_im).astype(BF16)
    q_ref[P2:2 * P2, :] = (p_re * bb_im + p_im * bb_re).astype(BF16)

    p_re, p_im = a_pow(e_k + 1.0)
    n_ref[0:P2, :] = (c_re * p_re - c_im * p_im).astype(BF16)
    n_ref[P2:2 * P2, :] = (-(c_re * p_im + c_im * p_re)).astype(BF16)

    mag = jnp.exp(lr_r * dt_r * float(S5_CHUNK))
    th = li_r * dt_r * float(S5_CHUNK)
    a_ref[0:1, :] = mag * jnp.cos(th)
    a_ref[1:2, :] = mag * jnp.sin(th)


def _s5_ops(col_params, row_params, bt, btile, ctile):
    P2 = 2 * S5_STATE
    col = pl.BlockSpec((None, P2, 1), lambda g: (g, 0, 0))
    row = pl.BlockSpec((None, 1, P2), lambda g: (g, 0, 0))
    btspec = pl.BlockSpec((None, S5_GROUP, P2), lambda g: (g, 0, 0))
    tile = pl.BlockSpec((None, P2, S5_CW), lambda g: (g, 0, 0))
    sq = pl.BlockSpec((None, S5_CW, S5_CW), lambda g: (g, 0, 0))
    return pl.pallas_call(
        _s5_ops_kernel,
        out_shape=[jax.ShapeDtypeStruct((S5_GROUPS, S5_CW, S5_CW), BF16)] * 3
        + [jax.ShapeDtypeStruct((S5_GROUPS, 2, P2), F32)],
        grid=(S5_GROUPS,),
        in_specs=[col, col, col, row, row, row, btspec, btspec, tile, tile, tile, tile],
        out_specs=[sq, sq, sq, pl.BlockSpec((None, 2, P2), lambda g: (g, 0, 0))],
        compiler_params=_params(),
        name="s5_chunk_operators",
    )(*col_params, *row_params, *bt, *btile, *ctile)


ROWS = N_CHUNKS * BATCH


def _s5_scan_kernel(u_ref, m_ref, q_ref, n_ref, a_ref, y_ref, pu_ref, sp_ref):
    u = u_ref[...]
    pu_ref[...] = _dot_nt(u, q_ref[...])
    a_re = a_ref[0:1, :]
    a_im = a_ref[1:2, :]
    lane = lax.broadcasted_iota(jnp.int32, (BATCH, 2 * S5_STATE), 1)
    lo = lane < S5_STATE
    S = S5_STATE

    def step(k, carry):
        s_re, s_im = carry
        cf = pl.multiple_of(k * BATCH, BATCH)
        cb = pl.multiple_of(jnp.where(k < N_CTX_CHUNKS, N_CTX_CHUNKS - 1 - k,
                                      N_CHUNKS + N_CTX_CHUNKS - 1 - k) * BATCH, BATCH)
        sp_ref[pl.ds(cf, BATCH), 0:S] = s_re[:, 0:S]
        sp_ref[pl.ds(cb, BATCH), S:2 * S] = s_re[:, S:2 * S]
        sp_ref[pl.ds(cf, BATCH), 2 * S:3 * S] = s_im[:, 0:S]
        sp_ref[pl.ds(cb, BATCH), 3 * S:4 * S] = s_im[:, S:2 * S]
        x_re = jnp.where(lo, pu_ref[pl.ds(cf, BATCH), 0:2 * S], pu_ref[pl.ds(cb, BATCH), 0:2 * S])
        x_im = jnp.where(lo, pu_ref[pl.ds(cf, BATCH), 2 * S:4 * S], pu_ref[pl.ds(cb, BATCH), 2 * S:4 * S])
        return (a_re * s_re - a_im * s_im + x_re, a_re * s_im + a_im * s_re + x_im)

    zero = jnp.zeros((BATCH, 2 * S), F32)
    lax.fori_loop(0, N_CHUNKS, step, (zero, zero))
    y_ref[...] = _dot(u, m_ref[...]) + _dot(sp_ref[...].astype(BF16), n_ref[...])


def _s5_scan(u_g, m_op, q_op, n_op, a_vec):
    sq = pl.BlockSpec((None, S5_CW, S5_CW), lambda g: (g, 0, 0))
    rows = pl.BlockSpec((None, ROWS, S5_CW), lambda g: (g, 0, 0))
    return pl.pallas_call(
        _s5_scan_kernel,
        out_shape=jax.ShapeDtypeStruct((S5_GROUPS, ROWS, S5_CW), F32),
        grid=(S5_GROUPS,),
        in_specs=[rows, sq, sq, sq, pl.BlockSpec((None, 2, 2 * S5_STATE), lambda g: (g, 0, 0))],
        out_specs=rows,
        scratch_shapes=[pltpu.VMEM((ROWS, S5_CW), F32), pltpu.VMEM((ROWS, S5_CW), F32)],
        compiler_params=_params(),
        name="s5_scan",
    )(u_g, m_op, q_op, n_op, a_vec)


def _s5_out_kernel(u_ref, y_ref, d_ref, gw_ref, gb_ref, perm_ref, o_ref, ys_scr):
    lane_blk = _lane_block()
    for k in range(S5_GROUPS // GROUPS_PER_VREG):
        for c2 in range(S5_PAIR):
            for t in range(S5_CHUNK):
                hh, m = divmod(t, STEPS_PER_VREG)
                acc = None
                for r in range(GROUPS_PER_VREG):
                    piece = y_ref[k * GROUPS_PER_VREG + r, c2 * BATCH:(c2 + 1) * BATCH, hh * LANES:(hh + 1) * LANES]
                    sh = (S5_GROUP * (r - m)) % LANES
                    if sh:
                        piece = pltpu.roll(piece, sh, 1)
                    acc = piece if acc is None else jnp.where(lane_blk == r, piece, acc)
                tok = c2 * S5_CHUNK + t
                ys_scr[k, tok * BATCH:(tok + 1) * BATCH, :] = acc
    ys = jnp.concatenate([ys_scr[k] for k in range(D_MODEL // LANES)], axis=1)
    y = u_ref[...].reshape(BATCH * S5_TOK, D_MODEL) * d_ref[...] + ys
    g = jax.nn.gelu(y)
    gate = jax.nn.sigmoid(_dot(g.astype(BF16), gw_ref[...]) + gb_ref[...])
    gated = (g * gate).astype(BF16)
    gated = _dot(perm_ref[...], gated).astype(BF16)
    o_ref[...] = gated.reshape(BATCH, S5_TOK, D_MODEL)


def _s5_out(u, y_g, d_skip, glu_w, glu_b, perm_t):
    ctx_steps = N_CTX_CHUNKS // S5_PAIR
    n_rows = BATCH * S5_TOK
    vec = pl.BlockSpec((1, D_MODEL), lambda p: (0, 0))
    return pl.pallas_call(
        _s5_out_kernel,
        out_shape=jax.ShapeDtypeStruct((BATCH, SEQ, D_MODEL), BF16),
        grid=(S5_STEPS - ctx_steps,),
        in_specs=[pl.BlockSpec((S5_TOK, BATCH, D_MODEL), lambda p: (p + ctx_steps, 0, 0)),
                  pl.BlockSpec((S5_GROUPS, S5_PAIR * BATCH, S5_CW), lambda p: (0, p + ctx_steps, 0)),
                  vec, pl.BlockSpec((D_MODEL, D_MODEL), lambda p: (0, 0)), vec,
                  pl.BlockSpec((n_rows, n_rows), lambda p: (0, 0))],
        out_specs=pl.BlockSpec((BATCH, S5_TOK, D_MODEL), lambda p: (0, p, 0)),
        scratch_shapes=[pltpu.VMEM((D_MODEL // LANES, n_rows, LANES), F32)],
        compiler_params=_params(),
        name="s5_gelu_glu",
    )(u, y_g, d_skip, glu_w, glu_b, perm_t)


def _rope_tables():
    rows_n = SEQ // GRID_W
    row = jnp.repeat(jnp.arange(rows_n, dtype=F32), GRID_W)
    col = jnp.tile(jnp.arange(GRID_W, dtype=F32), rows_n)
    n_freq = HEAD_DIM // 4
    inv = ROPE_BASE ** (-jnp.arange(n_freq, dtype=F32) / n_freq)
    ang = jnp.concatenate([row[:, None] * inv, col[:, None] * inv], axis=-1)
    reps = LANES // (HEAD_DIM // 2)
    cos_t = jnp.tile(jnp.cos(ang), (1, reps))
    sin_t = jnp.tile(jnp.sin(ang), (1, reps))
    sign = jnp.where((jnp.arange(LANES) % HEAD_DIM) < HEAD_DIM // 2, -1.0, 1.0).astype(F32)
    cos_t = jnp.concatenate([jnp.ones((CTX_LEN, LANES), F32), cos_t], axis=0)
    sin_s = jnp.concatenate([jnp.zeros((CTX_LEN, LANES), F32), sin_t * sign], axis=0)
    return cos_t, sin_s


def _fb_cols(x):
    return jnp.transpose(x, (1, 0, 2)).reshape(S5_GROUPS, 2 * S5_STATE, 1)


def _fb_rows(x):
    return jnp.transpose(x, (1, 0, 2)).reshape(S5_GROUPS, 1, 2 * S5_STATE)


def _s5_layout(lam_re, lam_im, log_step, b_re, b_im, c_re, c_im):
    ls = jnp.broadcast_to(log_step[:, :, None], lam_re.shape)
    cols = [_fb_cols(v) for v in (lam_re, lam_im, ls)]
    rows = [_fb_rows(v) for v in (lam_re, lam_im, ls)]

    def bt_of(b):
        return jnp.transpose(b, (1, 3, 0, 2)).reshape(S5_GROUPS, S5_GROUP, 2 * S5_STATE)

    def btile_of(b):
        t = jnp.transpose(b, (1, 0, 2, 3)).reshape(S5_GROUPS, 2 * S5_STATE, S5_GROUP)
        return jnp.tile(t, (1, 1, S5_CHUNK))

    def ctile_of(c):
        t = jnp.transpose(c, (1, 0, 3, 2)).reshape(S5_GROUPS, 2 * S5_STATE, S5_GROUP)
        return jnp.tile(t, (1, 1, S5_CHUNK))

    return cols, rows, [bt_of(b_re), bt_of(b_im)], [btile_of(b_re), btile_of(b_im)], [ctile_of(c_re), ctile_of(c_im)]


def kernel(x, c, ctx, c_ctx, norm1_g, norm2_g, mod_w, mod_b, mlp_w1, mlp_w2, attn_w_in, attn_w_out, a_q_norm, a_k_norm, a_sink, b_q_norm, b_k_norm, b_lq1, b_lk1, b_lq2, b_lk2, b_subln, s5_w_in, s5_lambda_re, s5_lambda_im, s5_log_step, s5_b_re, s5_b_im, s5_c_re, s5_c_im, s5_d, s5_glu_w, s5_glu_b, s5_w_out):
    assert x.shape == (BATCH, SEQ, D_MODEL) and ctx.shape == (BATCH, CTX_LEN, D_MODEL)
    h = jnp.concatenate([ctx, x], axis=1)
    s_rows = jnp.concatenate([c, c_ctx[None], jnp.zeros((16 - BATCH - 1, D_MODEL), F32)], axis=0)
    m_all = _modulation(s_rows, mod_w, mod_b)
    cos_t, sin_s = _rope_tables()
    e_blk = (jnp.kron(jnp.eye(LANES // HEAD_DIM, dtype=F32), jnp.ones((HEAD_DIM, HEAD_DIM), F32))
             / HEAD_DIM).astype(BF16)

    for i in range(DEPTH):
        last = i == DEPTH - 1
        j = i // 2
        m_lat = m_all[i, :BATCH].reshape(BATCH, 6, D_MODEL)
        m_ctx = jnp.broadcast_to(m_all[i, BATCH].reshape(1, 6, D_MODEL), (BATCH, 6, D_MODEL))
        mods = jnp.stack([m_ctx, m_lat])
        g1 = norm1_g[i].reshape(1, D_MODEL)
        g2 = norm2_g[i].reshape(1, D_MODEL)
        if i % 2 == 0:
            lambda_init = 0.8 - 0.6 * math.exp(-0.3 * i)
            gains = jnp.stack([jnp.tile(v[j], LANES // HEAD_DIM) for v in (a_q_norm, a_k_norm, b_q_norm, b_k_norm)])
            qa, k2a, v2a, qb, kb, vb = _attn_in(h, mods, g1, attn_w_in[j].astype(BF16), gains, cos_t, sin_s, e_blk)
            ya = _win_attn(a_sink[j], qa, k2a, v2a)
            lpar = jnp.stack([b_lq1[j], b_lk1[j], b_lq2[j], b_lk2[j]])
            yb = _diff_attn(lpar, b_subln[j].reshape(1, LANES), qb, kb, vb, lambda_init)
            if last:
                ya, yb = ya[:, CTX_LEN:], yb[:, CTX_LEN:]
            ua, ub, wo = ya, yb, attn_w_out[j]
        else:
            src = np.arange(BATCH * S5_TOK).reshape(BATCH, S5_TOK).T.reshape(-1)
            perm = jnp.asarray(np.eye(BATCH * S5_TOK, dtype=np.float32)[src], BF16)
            u, u_g = _s5_in(h, mods, g1, perm, s5_w_in[j].astype(BF16))
            ops_in = _s5_layout(s5_lambda_re[j], s5_lambda_im[j], s5_log_step[j], s5_b_re[j], s5_b_im[j],
                                s5_c_re[j], s5_c_im[j])
            m_op, q_op, n_op, a_vec = _s5_ops(*ops_in)
            y_g = _s5_scan(u_g, m_op, q_op, n_op, a_vec)
            assert last, "S5 layers before the last one would also need the context rows of the readout"
            gated = _s5_out(u, y_g, s5_d[j].reshape(1, D_MODEL), s5_glu_w[j].astype(BF16),
                            s5_glu_b[j].reshape(1, D_MODEL), perm.T)
            ua, ub, wo = gated, gated, s5_w_out[j]
        h = _mix_mlp(h, ua, ub, mods, g2, wo.astype(BF16), mlp_w1[i].astype(BF16), mlp_w2[i].astype(BF16),
                     latent_only=last)
    return h
```

```python
import functools
import math

import jax
import jax.numpy as jnp
import numpy as np
from jax import lax
from jax.experimental import pallas as pl
from jax.experimental.pallas import tpu as pltpu

F32 = jnp.float32
BF16 = jnp.bfloat16

D_MODEL = 1024
BATCH = 8
SEQ = 2048
DEPTH = 2
GRID_W = 64
CTX_LEN = 256
HEAD_DIM = 64
WINDOW = 128
A_Q_HEADS = 8
A_KV_HEADS = 2
B_HEADS = 4
A_Q_W = A_Q_HEADS * HEAD_DIM
A_KV_W = A_KV_HEADS * HEAD_DIM
B_QK_W = B_HEADS * 2 * HEAD_DIM
B_V_W = B_HEADS * 2 * HEAD_DIM
ATTN_IN = A_Q_W + 2 * A_KV_W + 2 * B_QK_W + B_V_W
S5_GROUP = 16
S5_GROUPS = D_MODEL // S5_GROUP
S5_STATE = 64
D_FF = 4 * D_MODEL
ROPE_BASE = 10000.0
EPS = 1e-6
NEG_INF = -1e30
LOG2E = math.log2(math.e)

TT = CTX_LEN + SEQ
TM = 256
N_TILES = TT // TM
LANES = 128
S5_CHUNK = 16
S5_CW = S5_CHUNK * S5_GROUP
N_CHUNKS = TT // S5_CHUNK
N_CTX_CHUNKS = CTX_LEN // S5_CHUNK
VMEM_LIMIT = 56 * 1024 * 1024


def _dot(a, b):
    return jnp.dot(a, b, preferred_element_type=F32)


def _dot_nt(a, b):
    return lax.dot_general(a, b, (((1,), (1,)), ((), ())), preferred_element_type=F32)


def _rms(x):
    return x * lax.rsqrt(jnp.mean(x * x, axis=-1, keepdims=True) + EPS)


def _modnorm(x, g, shift, scale):
    return _rms(x) * g * (1.0 + scale) + shift


def _params(**kw):
    return pltpu.CompilerParams(vmem_limit_bytes=VMEM_LIMIT, **kw)


def _mod_kernel(s_ref, w_ref, b_ref, o_ref):
    s = s_ref[...]
    s = s * jax.nn.sigmoid(s)
    o_ref[...] = _dot(s.astype(BF16), w_ref[...].astype(BF16)) + b_ref[...]


def _modulation(s_rows, mod_w, mod_b):
    return pl.pallas_call(
        _mod_kernel,
        out_shape=jax.ShapeDtypeStruct((DEPTH, 16, 6 * D_MODEL), F32),
        grid=(DEPTH, 6),
        in_specs=[
            pl.BlockSpec((16, D_MODEL), lambda i, j: (0, 0)),
            pl.BlockSpec((None, D_MODEL, D_MODEL), lambda i, j: (i, 0, j)),
            pl.BlockSpec((None, 1, D_MODEL), lambda i, j: (i, 0, j)),
        ],
        out_specs=pl.BlockSpec((None, 16, D_MODEL), lambda i, j: (i, 0, j)),
        compiler_params=_params(),
        name="modulation",
    )(s_rows, mod_w, mod_b.reshape(DEPTH, 1, 6 * D_MODEL))


def _mod_spec():
    return pl.BlockSpec((None, None, 6, D_MODEL), lambda b, t: (jnp.minimum(t, 1), b, 0, 0))


def _stream_specs(stream, h_off):
    unified = stream[0].shape[1] == TT
    first_lat = 1 if unified else 0
    ctx_spec = pl.BlockSpec((None, TM, D_MODEL), lambda b, t: (b, 0, 0))
    lat_spec = pl.BlockSpec((None, TM, D_MODEL),
                            lambda b, t: (b, jnp.maximum(t + h_off - 1 + first_lat, first_lat), 0))
    return [ctx_spec, lat_spec]


def _stream_tile(hc_ref, hl_ref, h_off):
    if h_off > 0:
        return hl_ref[...]
    return jnp.where(pl.program_id(1) == 0, hc_ref[...], hl_ref[...])


def _attn_in_kernel(hc_ref, hl_ref, mod_ref, g_ref, w_ref, gain_ref, cos_ref, sin_ref, e_ref,
                    qa_ref, k2a_ref, v2a_ref, qb_ref, kb_ref, vb_ref):
    mod = mod_ref[...]
    a = _modnorm(_stream_tile(hc_ref, hl_ref, 0), g_ref[...], mod[0:1], mod[1:2]).astype(BF16)
    z = _dot(a, w_ref[...])
    cos_t = cos_ref[...]
    sin_s = sin_ref[...]
    e = e_ref[...]
    gains = gain_ref[...]
    lane = lax.broadcasted_iota(jnp.int32, (TM, LANES), 1)
    first_half = (lane & (HEAD_DIM - 1)) < HEAD_DIM // 2
    lo = lane < HEAD_DIM

    def norm_rope(c, gain):
        ms = _dot((c * c).astype(BF16), e)
        cn = c * lax.rsqrt(ms + EPS) * gain
        r_fwd = pltpu.roll(cn, HEAD_DIM // 2, 1)
        r_bwd = pltpu.roll(cn, LANES - HEAD_DIM // 2, 1)
        return cn * cos_t + jnp.where(first_half, r_bwd, r_fwd) * sin_s

    def dup_halves(x, ref):
        sw = pltpu.roll(x, HEAD_DIM, 1)
        ref[:, 0:LANES] = jnp.where(lo, x, sw).astype(BF16)
        ref[:, LANES:2 * LANES] = jnp.where(lo, sw, x).astype(BF16)

    q_scale = HEAD_DIM ** -0.5 * LOG2E
    off = 0
    for c in range(A_Q_W // LANES):
        qa_ref[:, c * LANES:(c + 1) * LANES] = (
            norm_rope(z[:, off:off + LANES], gains[0:1]) * q_scale).astype(BF16)
        off += LANES
    dup_halves(norm_rope(z[:, off:off + LANES], gains[1:2]), k2a_ref)
    off += LANES
    dup_halves(z[:, off:off + LANES], v2a_ref)
    off += LANES
    for c in range(B_QK_W // LANES):
        qb_ref[:, c * LANES:(c + 1) * LANES] = (
            norm_rope(z[:, off:off + LANES], gains[2:3]) * q_scale).astype(BF16)
        off += LANES
    for c in range(B_QK_W // LANES):
        kb_ref[:, c * LANES:(c + 1) * LANES] = norm_rope(z[:, off:off + LANES], gains[3:4]).astype(BF16)
        off += LANES
    vb_ref[...] = z[:, off:off + B_V_W].astype(BF16)


def _attn_in(stream, mods, g, w_in, gains, cos_t, sin_s, e_blk):
    def tok(width):
        return pl.BlockSpec((None, TM, width), lambda b, t: (b, t, 0))

    def full(shape):
        return pl.BlockSpec(shape, lambda b, t: (0,) * len(shape))

    out_shapes = [jax.ShapeDtypeStruct((BATCH, TT, w), BF16)
                  for w in (A_Q_W, 2 * A_KV_W, 2 * A_KV_W, B_QK_W, B_QK_W, B_V_W)]
    return pl.pallas_call(
        _attn_in_kernel,
        out_shape=out_shapes,
        grid=(BATCH, N_TILES),
        in_specs=[
            *_stream_specs(stream, 0), _mod_spec(), full((1, D_MODEL)), full((D_MODEL, ATTN_IN)), full((4, LANES)),
            pl.BlockSpec((TM, LANES), lambda b, t: (t, 0)), pl.BlockSpec((TM, LANES), lambda b, t: (t, 0)),
            full((LANES, LANES)),
        ],
        out_specs=[tok(A_Q_W), tok(2 * A_KV_W), tok(2 * A_KV_W), tok(B_QK_W), tok(B_QK_W), tok(B_V_W)],
        compiler_params=_params(),
        name="attn_in_proj",
    )(*stream, mods, g, w_in, gains, cos_t, sin_s, e_blk)


QB = 128


def _win_attn_kernel(sink_ref, q_ref, k2_ref, v2_ref, o_ref):
    t = pl.program_id(1)
    lane = lax.broadcasted_iota(jnp.int32, (QB, LANES), 1)
    lo = lane < HEAD_DIM
    rows = 4 * QB
    row = lax.broadcasted_iota(jnp.int32, (rows, 3 * QB), 0)
    col = lax.broadcasted_iota(jnp.int32, (rows, 3 * QB), 1)
    row_head = lax.broadcasted_iota(jnp.int32, (rows, 1), 0) // QB
    zero = jnp.zeros((QB, LANES), BF16)
    blocks = [(qb, g) for qb in range(TM // QB) for g in range(A_KV_HEADS)]

    def window_start(qb):
        n = (t - 1) * (TM // QB) + qb
        ws = jnp.clip((n - 1) * QB, 0, SEQ - 3 * QB)
        return n, ws

    def scores(qb, g, with_window):
        pieces = []
        for p in range(2):
            qp = q_ref[qb * QB:(qb + 1) * QB, g * 2 * LANES + p * LANES: g * 2 * LANES + (p + 1) * LANES]
            pieces.append(jnp.where(lo, qp, zero))
            pieces.append(jnp.where(lo, zero, qp))
        qs = jnp.concatenate(pieces, axis=0)
        s_c = _dot_nt(qs, k2_ref[0:CTX_LEN, g * LANES:(g + 1) * LANES])
        if not with_window:
            return s_c, None
        n, ws = window_start(qb)
        kw = k2_ref[pl.ds(pl.multiple_of(ws + CTX_LEN, QB), 3 * QB), g * LANES:(g + 1) * LANES]
        valid = jnp.abs(n * QB + (row & (QB - 1)) - (ws + col)) <= WINDOW
        return s_c, jnp.where(valid, _dot_nt(qs, kw), NEG_INF)

    def finish(qb, g, s_c, s_w):
        sk = jnp.full((rows, 1), sink_ref[4 * g + 3], F32)
        for hh in range(3):
            sk = jnp.where(row_head == hh, sink_ref[4 * g + hh], sk)
        sk = sk * LOG2E
        m = jnp.maximum(jnp.max(s_c, axis=-1, keepdims=True), sk)
        if s_w is not None:
            m = jnp.maximum(m, jnp.max(s_w, axis=-1, keepdims=True))
        p_c = jnp.exp2(s_c - m)
        denom = jnp.sum(p_c, axis=-1, keepdims=True) + jnp.exp2(sk - m)
        o = _dot(p_c.astype(BF16), v2_ref[0:CTX_LEN, g * LANES:(g + 1) * LANES])
        if s_w is not None:
            _, ws = window_start(qb)
            p_w = jnp.exp2(s_w - m)
            denom = denom + jnp.sum(p_w, axis=-1, keepdims=True)
            vw = v2_ref[pl.ds(pl.multiple_of(ws + CTX_LEN, QB), 3 * QB), g * LANES:(g + 1) * LANES]
            o = o + _dot(p_w.astype(BF16), vw)
        o = o / denom
        for p in range(2):
            o_ref[qb * QB:(qb + 1) * QB, g * 2 * LANES + p * LANES: g * 2 * LANES + (p + 1) * LANES] = jnp.where(
                lo, o[2 * p * QB:(2 * p + 1) * QB], o[(2 * p + 1) * QB:(2 * p + 2) * QB]).astype(BF16)

    def attend(with_window):
        s_next = scores(*blocks[0], with_window)
        for i, blk in enumerate(blocks):
            s_cur = s_next
            if i + 1 < len(blocks):
                s_next = scores(*blocks[i + 1], with_window)
            finish(*blk, *s_cur)

    @pl.when(t == 0)
    def _():
        attend(False)

    @pl.when(t > 0)
    def _():
        attend(True)


def _win_attn(sink, qa, k2a, v2a):
    return pl.pallas_call(
        _win_attn_kernel,
        out_shape=jax.ShapeDtypeStruct((BATCH, TT, A_Q_W), BF16),
        grid=(BATCH, N_TILES),
        in_specs=[
            pl.BlockSpec(memory_space=pltpu.SMEM),
            pl.BlockSpec((None, TM, A_Q_W), lambda b, t: (b, t, 0)),
            pl.BlockSpec((None, TT, 2 * A_KV_W), lambda b, t: (b, 0, 0)),
            pl.BlockSpec((None, TT, 2 * A_KV_W), lambda b, t: (b, 0, 0)),
        ],
        out_specs=pl.BlockSpec((None, TM, A_Q_W), lambda b, t: (b, t, 0)),
        compiler_params=_params(),
        name="window_attention",
    )(sink, qa, k2a, v2a)


DIFF_ROWS = 128


def _diff_attn_kernel(lpar_ref, subln_ref, q_ref, k_ref, v_ref, o_ref, *, lambda_init):
    t = pl.program_id(1)
    lp = lpar_ref[...]
    lam = (jnp.exp(jnp.sum(lp[0:1] * lp[1:2], axis=-1, keepdims=True))
           - jnp.exp(jnp.sum(lp[2:3] * lp[3:4], axis=-1, keepdims=True)) + lambda_init)
    R = DIFF_ROWS
    lane = lax.broadcasted_iota(jnp.int32, (R, LANES), 1)
    lo = lane < HEAD_DIM
    zero = jnp.zeros((R, LANES), BF16)

    def attend(n_keys):
        blocks = [(slice(rb * R, (rb + 1) * R), slice(h * LANES, (h + 1) * LANES))
                  for h in range(B_HEADS) for rb in range(TM // R)]

        def scores(rows, cols):
            q = q_ref[rows, cols]
            qs = jnp.concatenate([jnp.where(lo, q, zero), jnp.where(lo, zero, q)], axis=0)
            return _dot_nt(qs, k_ref[0:n_keys, cols])

        def finish(rows, cols, s):
            p = jnp.exp2(s - jnp.max(s, axis=-1, keepdims=True))
            den = jnp.sum(p, axis=-1, keepdims=True)
            coef = lam * den[0:R] / den[R:2 * R]
            w = (p[0:R] - coef * p[R:2 * R]).astype(BF16)
            y = _dot(w, v_ref[0:n_keys, cols]) / den[0:R]
            o_ref[rows, cols] = (_rms(y) * subln_ref[...] * (1.0 - lambda_init)).astype(BF16)

        s_next = scores(*blocks[0])
        for i, blk in enumerate(blocks):
            s_cur = s_next
            if i + 1 < len(blocks):
                s_next = scores(*blocks[i + 1])
            finish(*blk, s_cur)

    @pl.when(t == 0)
    def _():
        attend(CTX_LEN)

    @pl.when(t > 0)
    def _():
        attend(TT)


def _diff_attn(lpar, subln, qb, kb, vb, lambda_init):
    return pl.pallas_call(
        functools.partial(_diff_attn_kernel, lambda_init=lambda_init),
        out_shape=jax.ShapeDtypeStruct((BATCH, TT, B_V_W), BF16),
        grid=(BATCH, N_TILES),
        in_specs=[
            pl.BlockSpec((4, HEAD_DIM), lambda b, t: (0, 0)),
            pl.BlockSpec((1, LANES), lambda b, t: (0, 0)),
            pl.BlockSpec((None, TM, B_QK_W), lambda b, t: (b, t, 0)),
            pl.BlockSpec((None, TT, B_QK_W), lambda b, t: (b, 0, 0)),
            pl.BlockSpec((None, TT, B_V_W), lambda b, t: (b, 0, 0)),
        ],
        out_specs=pl.BlockSpec((None, TM, B_V_W), lambda b, t: (b, t, 0)),
        compiler_params=_params(),
        name="diff_attention",
    )(lpar, subln, qb, kb, vb)


FF_CHUNK = 1024


def _mix_mlp_kernel(hc_ref, hl_ref, ua_ref, ub_ref, mod_ref, g_ref, wo_ref, w1_ref, w2_ref, o_ref, *, h_off):
    mod = mod_ref[...]
    half = D_MODEL // 2
    y = _dot(ua_ref[...], wo_ref[0:half, :]) + _dot(ub_ref[...], wo_ref[half:D_MODEL, :])
    h1 = _stream_tile(hc_ref, hl_ref, h_off) + mod[2:3] * y
    f = _modnorm(h1, g_ref[...], mod[3:4], mod[4:5]).astype(BF16)
    acc = jnp.zeros((TM, D_MODEL), F32)
    for c in range(D_FF // FF_CHUNK):
        hid = jnp.maximum(_dot(f, w1_ref[:, c * FF_CHUNK:(c + 1) * FF_CHUNK]), 0.0)
        acc = acc + _dot((hid * hid).astype(BF16), w2_ref[c * FF_CHUNK:(c + 1) * FF_CHUNK, :])
    o_ref[...] = h1 + mod[5:6] * acc


def _mix_mlp(stream, ua, ub, mods, g, wo, w1, w2, *, latent_only):
    n_tiles = SEQ // TM if latent_only else N_TILES
    h_off = N_TILES - n_tiles
    half = D_MODEL // 2
    ub_col = 1 if ub.shape[-1] == D_MODEL else 0

    def full(shape):
        return pl.BlockSpec(shape, lambda b, t: (0,) * len(shape), pipeline_mode=pl.Buffered(1))

    return pl.pallas_call(
        functools.partial(_mix_mlp_kernel, h_off=h_off),
        out_shape=jax.ShapeDtypeStruct((BATCH, n_tiles * TM, D_MODEL), F32),
        grid=(BATCH, n_tiles),
        in_specs=[
            *_stream_specs(stream, h_off),
            pl.BlockSpec((None, TM, half), lambda b, t: (b, t, 0)),
            pl.BlockSpec((None, TM, half), lambda b, t: (b, t, ub_col)),
            pl.BlockSpec((None, None, 6, D_MODEL), lambda b, t: (jnp.minimum(t + h_off, 1), b, 0, 0)),
            pl.BlockSpec((1, D_MODEL), lambda b, t: (0, 0)),
            full((D_MODEL, D_MODEL)), full((D_MODEL, D_FF)), full((D_FF, D_MODEL)),
        ],
        out_specs=pl.BlockSpec((None, TM, D_MODEL), lambda b, t: (b, t, 0)),
        compiler_params=_params(),
        name="mixer_out_mlp",
    )(*stream, ua, ub, mods, g, wo, w1, w2)


S5_PAIR = 2
S5_TOK = S5_PAIR * S5_CHUNK
S5_STEPS = N_CHUNKS // S5_PAIR
GROUPS_PER_VREG = LANES // S5_GROUP
STEPS_PER_VREG = LANES // S5_GROUP


def _lane_block():
    return lax.broadcasted_iota(jnp.int32, (BATCH, LANES), 1) // S5_GROUP


def _s5_in_kernel(h_ref, mod_ref, g_ref, perm_ref, w_ref, u_ref, z_ref, u_scr):
    mod = mod_ref[...]
    x = h_ref[...]
    a = _rms(x) * g_ref[...] * (1.0 + mod[:, 1:2, :]) + mod[:, 0:1, :]
    a = a.reshape(BATCH * S5_TOK, D_MODEL).astype(BF16)
    a = _dot(perm_ref[...], a).astype(BF16)
    u = _dot(a, w_ref[...])
    u_ref[...] = u.reshape(S5_TOK, BATCH, D_MODEL)
    for k in range(D_MODEL // LANES):
        u_scr[k] = u[:, k * LANES:(k + 1) * LANES]
    lane_blk = _lane_block()
    for g in range(S5_GROUPS):
        k, r = divmod(g, GROUPS_PER_VREG)
        for hh in range(S5_CHUNK // STEPS_PER_VREG):
            halves = []
            for c2 in range(S5_PAIR):
                acc = None
                for m in range(STEPS_PER_VREG):
                    tok = c2 * S5_CHUNK + hh * STEPS_PER_VREG + m
                    piece = u_scr[k, tok * BATCH:(tok + 1) * BATCH, :]
                    sh = (S5_GROUP * (m - r)) % LANES
                    if sh:
                        piece = pltpu.roll(piece, sh, 1)
                    acc = piece if acc is None else jnp.where(lane_blk == m, piece, acc)
                halves.append(acc)
            z_ref[g, :, hh * LANES:(hh + 1) * LANES] = jnp.concatenate(halves, axis=0).astype(BF16)


def _s5_in(h, mods, g, perm, w_in):
    ctx_steps = N_CTX_CHUNKS // S5_PAIR
    n_rows = BATCH * S5_TOK
    return pl.pallas_call(
        _s5_in_kernel,
        out_shape=[jax.ShapeDtypeStruct((TT, BATCH, D_MODEL), F32),
                   jax.ShapeDtypeStruct((S5_GROUPS, ROWS, S5_CW), BF16)],
        grid=(S5_STEPS,),
        in_specs=[pl.BlockSpec((BATCH, S5_TOK, D_MODEL), lambda p: (0, p, 0)),
                  pl.BlockSpec((None, BATCH, 6, D_MODEL), lambda p: (jnp.minimum(p // ctx_steps, 1), 0, 0, 0)),
                  pl.BlockSpec((1, D_MODEL), lambda p: (0, 0)),
                  pl.BlockSpec((n_rows, n_rows), lambda p: (0, 0)),
                  pl.BlockSpec((D_MODEL, D_MODEL), lambda p: (0, 0))],
        out_specs=[pl.BlockSpec((S5_TOK, BATCH, D_MODEL), lambda p: (p, 0, 0)),
                   pl.BlockSpec((S5_GROUPS, S5_PAIR * BATCH, S5_CW), lambda p: (0, p, 0))],
        scratch_shapes=[pltpu.VMEM((D_MODEL // LANES, n_rows, LANES), F32)],
        compiler_params=_params(),
        name="s5_in_proj",
    )(h, mods, g, perm, w_in)


def _s5_ops_kernel(lr_c_ref, li_c_ref, ls_c_ref, lr_r_ref, li_r_ref, ls_r_ref,
                   btr_ref, bti_ref, btile_r_ref, btile_i_ref, ctile_r_ref, ctile_i_ref,
                   m_ref, q_ref, n_ref, a_ref):
    P2 = 2 * S5_STATE

    def discretize(lr, li, ls):
        dt = jnp.exp(ls)
        mag = jnp.exp(lr * dt)
        ab_re = mag * jnp.cos(li * dt)
        ab_im = mag * jnp.sin(li * dt)
        den = lr * lr + li * li
        nr = ab_re - 1.0
        f_re = (nr * lr + ab_im * li) / den
        f_im = (ab_im * lr - nr * li) / den
        return ab_re, ab_im, f_re, f_im

    def cmul(xr, xi, yr, yi):
        return xr * yr - xi * yi, xr * yi + xi * yr

    ar_c, ai_c, f_re_c, f_im_c = discretize(lr_c_ref[...], li_c_ref[...], ls_c_ref[...])
    lag = lax.broadcasted_iota(jnp.int32, (P2, S5_CW), 1) // S5_GROUP
    is_fwd = lax.broadcasted_iota(jnp.int32, (P2, S5_CW), 0) < S5_STATE
    e_k = jnp.where(is_fwd, lag, (S5_CHUNK - 1) - lag)
    squares = [(ar_c, ai_c)]
    while 2 ** len(squares) < S5_CHUNK:
        squares.append(cmul(*squares[-1], *squares[-1]))

    def a_pow(e):
        pr = jnp.ones((P2, S5_CW), F32)
        pi = jnp.zeros((P2, S5_CW), F32)
        for b, (br, bi) in enumerate(squares):
            bit = ((e >> b) & 1) == 1
            tr, ti = cmul(pr, pi, br, bi)
            pr = jnp.where(bit, tr, pr)
            pi = jnp.where(bit, ti, pi)
        return pr, pi

    c_re, c_im = ctile_r_ref[...], ctile_i_ref[...]
    pk_re, pk_im = a_pow(e_k)
    cp_re, cp_im = cmul(c_re, c_im, pk_re, pk_im)

    ar_r, ai_r, f_re_r, f_im_r = discretize(lr_r_ref[...], li_r_ref[...], ls_r_ref[...])
    bt_re = f_re_r * btr_ref[...] - f_im_r * bti_ref[...]
    bt_im = f_re_r * bti_ref[...] + f_im_r * btr_ref[...]
    lane_fwd = lax.broadcasted_iota(jnp.int32, (S5_GROUP, P2), 1) < S5_STATE

    def lag_kernels(keep):
        br = jnp.where(keep, bt_re, 0.0)
        bi = jnp.where(keep, bt_im, 0.0)
        hi = lax.Precision.HIGHEST
        return (jnp.dot(br, cp_re, precision=hi, preferred_element_type=F32)
                - jnp.dot(bi, cp_im, precision=hi, preferred_element_type=F32))

    kt_f = lag_kernels(lane_fwd)
    kt_b = lag_kernels(jnp.logical_not(lane_fwd))
    lane_w = lax.broadcasted_iota(jnp.int32, (S5_GROUP, S5_CW), 1)
    for s in range(S5_CHUNK):
        f_part = kt_f if s == 0 else jnp.where(lane_w >= S5_GROUP * s, pltpu.roll(kt_f, S5_GROUP * s, 1), 0.0)
        sh = (S5_GROUP * (s + 1)) % S5_CW
        b_roll = kt_b if sh == 0 else pltpu.roll(kt_b, sh, 1)
        b_part = jnp.where(lane_w < S5_GROUP * (s + 1), b_roll, 0.0)
        m_ref[s * S5_GROUP:(s + 1) * S5_GROUP, :] = (f_part + b_part).astype(BF16)

    bb_re = f_re_c * btile_r_ref[...] - f_im_c * btile_i_ref[...]
    bb_im = f_re_c * btile_i_ref[...] + f_im_c * btile_r_ref[...]
    q_re, q_im = cmul(*a_pow((S5_CHUNK - 1) - e_k), bb_re, bb_im)
    q_ref[0:P2, :] = q_re.astype(BF16)
    q_ref[P2:2 * P2, :] = q_im.astype(BF16)

    n_re, n_im = cmul(cp_re, cp_im, ar_c, ai_c)
    n_ref[0:P2, :] = n_re.astype(BF16)
    n_ref[P2:2 * P2, :] = (-n_im).astype(BF16)

    a_re, a_im = ar_r, ai_r
    for _ in range(S5_CHUNK.bit_length() - 1):
        a_re, a_im = cmul(a_re, a_im, a_re, a_im)
    a_ref[0:1, :] = a_re
    a_ref[1:2, :] = a_im


def _s5_ops(col_params, row_params, bt, btile, ctile):
    P2 = 2 * S5_STATE
    col = pl.BlockSpec((None, P2, 1), lambda g: (g, 0, 0))
    row = pl.BlockSpec((None, 1, P2), lambda g: (g, 0, 0))
    btspec = pl.BlockSpec((None, S5_GROUP, P2), lambda g: (g, 0, 0))
    tile = pl.BlockSpec((None, P2, S5_CW), lambda g: (g, 0, 0))
    sq = pl.BlockSpec((None, S5_CW, S5_CW), lambda g: (g, 0, 0))
    return pl.pallas_call(
        _s5_ops_kernel,
        out_shape=[jax.ShapeDtypeStruct((S5_GROUPS, S5_CW, S5_CW), BF16)] * 3
        + [jax.ShapeDtypeStruct((S5_GROUPS, 2, P2), F32)],
        grid=(S5_GROUPS,),
        in_specs=[col, col, col, row, row, row, btspec, btspec, tile, tile, tile, tile],
        out_specs=[sq, sq, sq, pl.BlockSpec((None, 2, P2), lambda g: (g, 0, 0))],
        compiler_params=_params(),
        name="s5_chunk_operators",
    )(*col_params, *row_params, *bt, *btile, *ctile)


ROWS = N_CHUNKS * BATCH


def _s5_scan_kernel(u_ref, m_ref, q_ref, n_ref, a_ref, y_ref, pu_ref, sp_ref):
    u = u_ref[...]
    pu_ref[...] = _dot_nt(u, q_ref[...])
    a_re = a_ref[0:1, :]
    a_im = a_ref[1:2, :]
    lane = lax.broadcasted_iota(jnp.int32, (BATCH, 2 * S5_STATE), 1)
    lo = lane < S5_STATE
    S = S5_STATE

    def step(k, carry):
        s_re, s_im = carry
        cf = pl.multiple_of(k * BATCH, BATCH)
        cb = pl.multiple_of(jnp.where(k < N_CTX_CHUNKS, N_CTX_CHUNKS - 1 - k,
                                      N_CHUNKS + N_CTX_CHUNKS - 1 - k) * BATCH, BATCH)
        sp_ref[pl.ds(cf, BATCH), 0:S] = s_re[:, 0:S]
        sp_ref[pl.ds(cb, BATCH), S:2 * S] = s_re[:, S:2 * S]
        sp_ref[pl.ds(cf, BATCH), 2 * S:3 * S] = s_im[:, 0:S]
        sp_ref[pl.ds(cb, BATCH), 3 * S:4 * S] = s_im[:, S:2 * S]
        x_re = jnp.where(lo, pu_ref[pl.ds(cf, BATCH), 0:2 * S], pu_ref[pl.ds(cb, BATCH), 0:2 * S])
        x_im = jnp.where(lo, pu_ref[pl.ds(cf, BATCH), 2 * S:4 * S], pu_ref[pl.ds(cb, BATCH), 2 * S:4 * S])
        return (a_re * s_re - a_im * s_im + x_re, a_re * s_im + a_im * s_re + x_im)

    zero = jnp.zeros((BATCH, 2 * S), F32)
    lax.fori_loop(0, N_CHUNKS, step, (zero, zero))
    y_ref[...] = _dot(u, m_ref[...]) + _dot(sp_ref[...].astype(BF16), n_ref[...])


def _s5_scan(u_g, m_op, q_op, n_op, a_vec):
    sq = pl.BlockSpec((None, S5_CW, S5_CW), lambda g: (g, 0, 0))
    rows = pl.BlockSpec((None, ROWS, S5_CW), lambda g: (g, 0, 0))
    return pl.pallas_call(
        _s5_scan_kernel,
        out_shape=jax.ShapeDtypeStruct((S5_GROUPS, ROWS, S5_CW), F32),
        grid=(S5_GROUPS,),
        in_specs=[rows, sq, sq, sq, pl.BlockSpec((None, 2, 2 * S5_STATE), lambda g: (g, 0, 0))],
        out_specs=rows,
        scratch_shapes=[pltpu.VMEM((ROWS, S5_CW), F32), pltpu.VMEM((ROWS, S5_CW), F32)],
        compiler_params=_params(),
        name="s5_scan",
    )(u_g, m_op, q_op, n_op, a_vec)


def _s5_out_kernel(u_ref, y_ref, d_ref, gw_ref, gb_ref, perm_ref, o_ref, ys_scr):
    lane_blk = _lane_block()
    for k in range(S5_GROUPS // GROUPS_PER_VREG):
        for c2 in range(S5_PAIR):
            for t in range(S5_CHUNK):
                hh, m = divmod(t, STEPS_PER_VREG)
                acc = None
                for r in range(GROUPS_PER_VREG):
                    piece = y_ref[k * GROUPS_PER_VREG + r, c2 * BATCH:(c2 + 1) * BATCH, hh * LANES:(hh + 1) * LANES]
                    sh = (S5_GROUP * (r - m)) % LANES
                    if sh:
                        piece = pltpu.roll(piece, sh, 1)
                    acc = piece if acc is None else jnp.where(lane_blk == r, piece, acc)
                tok = c2 * S5_CHUNK + t
                ys_scr[k, tok * BATCH:(tok + 1) * BATCH, :] = acc
    ys = jnp.concatenate([ys_scr[k] for k in range(D_MODEL // LANES)], axis=1)
    y = u_ref[...].reshape(BATCH * S5_TOK, D_MODEL) * d_ref[...] + ys
    g = jax.nn.gelu(y)
    gate = jax.nn.sigmoid(_dot(g.astype(BF16), gw_ref[...]) + gb_ref[...])
    gated = (g * gate).astype(BF16)
    gated = _dot(perm_ref[...], gated).astype(BF16)
    o_ref[...] = gated.reshape(BATCH, S5_TOK, D_MODEL)


def _s5_out(u, y_g, d_skip, glu_w, glu_b, perm_t):
    ctx_steps = N_CTX_CHUNKS // S5_PAIR
    n_rows = BATCH * S5_TOK
    vec = pl.BlockSpec((1, D_MODEL), lambda p: (0, 0))
    return pl.pallas_call(
        _s5_out_kernel,
        out_shape=jax.ShapeDtypeStruct((BATCH, SEQ, D_MODEL), BF16),
        grid=(S5_STEPS - ctx_steps,),
        in_specs=[pl.BlockSpec((S5_TOK, BATCH, D_MODEL), lambda p: (p + ctx_steps, 0, 0)),
                  pl.BlockSpec((S5_GROUPS, S5_PAIR * BATCH, S5_CW), lambda p: (0, p + ctx_steps, 0)),
                  vec, pl.BlockSpec((D_MODEL, D_MODEL), lambda p: (0, 0)), vec,
                  pl.BlockSpec((n_rows, n_rows), lambda p: (0, 0))],
        out_specs=pl.BlockSpec((BATCH, S5_TOK, D_MODEL), lambda p: (0, p, 0)),
        scratch_shapes=[pltpu.VMEM((D_MODEL // LANES, n_rows, LANES), F32)],
        compiler_params=_params(),
        name="s5_gelu_glu",
    )(u, y_g, d_skip, glu_w, glu_b, perm_t)


def _rope_tables():
    rows_n = SEQ // GRID_W
    row = jnp.repeat(jnp.arange(rows_n, dtype=F32), GRID_W)
    col = jnp.tile(jnp.arange(GRID_W, dtype=F32), rows_n)
    n_freq = HEAD_DIM // 4
    inv = ROPE_BASE ** (-jnp.arange(n_freq, dtype=F32) / n_freq)
    ang = jnp.concatenate([row[:, None] * inv, col[:, None] * inv], axis=-1)
    reps = LANES // (HEAD_DIM // 2)
    cos_t = jnp.tile(jnp.cos(ang), (1, reps))
    sin_t = jnp.tile(jnp.sin(ang), (1, reps))
    sign = jnp.where((jnp.arange(LANES) % HEAD_DIM) < HEAD_DIM // 2, -1.0, 1.0).astype(F32)
    cos_t = jnp.concatenate([jnp.ones((CTX_LEN, LANES), F32), cos_t], axis=0)
    sin_s = jnp.concatenate([jnp.zeros((CTX_LEN, LANES), F32), sin_t * sign], axis=0)
    return cos_t, sin_s


def _fb_cols(x):
    return jnp.transpose(x, (1, 0, 2)).reshape(S5_GROUPS, 2 * S5_STATE, 1)


def _fb_rows(x):
    return jnp.transpose(x, (1, 0, 2)).reshape(S5_GROUPS, 1, 2 * S5_STATE)


def _s5_layout(lam_re, lam_im, log_step, b_re, b_im, c_re, c_im):
    ls = jnp.broadcast_to(log_step[:, :, None], lam_re.shape)
    cols = [_fb_cols(v) for v in (lam_re, lam_im, ls)]
    rows = [_fb_rows(v) for v in (lam_re, lam_im, ls)]

    def bt_of(b):
        return jnp.transpose(b, (1, 3, 0, 2)).reshape(S5_GROUPS, S5_GROUP, 2 * S5_STATE)

    def btile_of(b):
        t = jnp.transpose(b, (1, 0, 2, 3)).reshape(S5_GROUPS, 2 * S5_STATE, S5_GROUP)
        return jnp.tile(t, (1, 1, S5_CHUNK))

    def ctile_of(c):
        t = jnp.transpose(c, (1, 0, 3, 2)).reshape(S5_GROUPS, 2 * S5_STATE, S5_GROUP)
        return jnp.tile(t, (1, 1, S5_CHUNK))

    return cols, rows, [bt_of(b_re), bt_of(b_im)], [btile_of(b_re), btile_of(b_im)], [ctile_of(c_re), ctile_of(c_im)]


def kernel(x, c, ctx, c_ctx, norm1_g, norm2_g, mod_w, mod_b, mlp_w1, mlp_w2, attn_w_in, attn_w_out, a_q_norm, a_k_norm, a_sink, b_q_norm, b_k_norm, b_lq1, b_lk1, b_lq2, b_lk2, b_subln, s5_w_in, s5_lambda_re, s5_lambda_im, s5_log_step, s5_b_re, s5_b_im, s5_c_re, s5_c_im, s5_d, s5_glu_w, s5_glu_b, s5_w_out):
    assert x.shape == (BATCH, SEQ, D_MODEL) and ctx.shape == (BATCH, CTX_LEN, D_MODEL)
    stream = (ctx, x)
    s_rows = jnp.concatenate([c, c_ctx[None], jnp.zeros((16 - BATCH - 1, D_MODEL), F32)], axis=0)
    m_all = _modulation(s_rows, mod_w, mod_b)
    cos_t, sin_s = _rope_tables()
    e_blk = (jnp.kron(jnp.eye(LANES // HEAD_DIM, dtype=F32), jnp.ones((HEAD_DIM, HEAD_DIM), F32))
             / HEAD_DIM).astype(BF16)

    for i in range(DEPTH):
        last = i == DEPTH - 1
        j = i // 2
        m_lat = m_all[i, :BATCH].reshape(BATCH, 6, D_MODEL)
        m_ctx = jnp.broadcast_to(m_all[i, BATCH].reshape(1, 6, D_MODEL), (BATCH, 6, D_MODEL))
        mods = jnp.stack([m_ctx, m_lat])
        g1 = norm1_g[i].reshape(1, D_MODEL)
        g2 = norm2_g[i].reshape(1, D_MODEL)
        if i % 2 == 0:
            lambda_init = 0.8 - 0.6 * math.exp(-0.3 * i)
            gains = jnp.stack([jnp.tile(v[j], LANES // HEAD_DIM) for v in (a_q_norm, a_k_norm, b_q_norm, b_k_norm)])
            qa, k2a, v2a, qb, kb, vb = _attn_in(stream, mods, g1, attn_w_in[j].astype(BF16), gains, cos_t, sin_s, e_blk)
            ya = _win_attn(a_sink[j], qa, k2a, v2a)
            lpar = jnp.stack([b_lq1[j], b_lk1[j], b_lq2[j], b_lk2[j]])
            yb = _diff_attn(lpar, b_subln[j].reshape(1, LANES), qb, kb, vb, lambda_init)
            if last:
                ya, yb = ya[:, CTX_LEN:], yb[:, CTX_LEN:]
            ua, ub, wo = ya, yb, attn_w_out[j]
        else:
            src = np.arange(BATCH * S5_TOK).reshape(BATCH, S5_TOK).T.reshape(-1)
            perm = jnp.asarray(np.eye(BATCH * S5_TOK, dtype=np.float32)[src], BF16)
            h_all = stream[0] if stream[0].shape[1] == TT else jnp.concatenate(stream, axis=1)
            u, u_g = _s5_in(h_all, mods, g1, perm, s5_w_in[j].astype(BF16))
            ops_in = _s5_layout(s5_lambda_re[j], s5_lambda_im[j], s5_log_step[j], s5_b_re[j], s5_b_im[j],
                                s5_c_re[j], s5_c_im[j])
            m_op, q_op, n_op, a_vec = _s5_ops(*ops_in)
            y_g = _s5_scan(u_g, m_op, q_op, n_op, a_vec)
            assert last, "S5 layers before the last one would also need the context rows of the readout"
            gated = _s5_out(u, y_g, s5_d[j].reshape(1, D_MODEL), s5_glu_w[j].astype(BF16),
                            s5_glu_b[j].reshape(1, D_MODEL), perm.T)
            ua, ub, wo = gated, gated, s5_w_out[j]
        h = _mix_mlp(stream, ua, ub, mods, g2, wo.astype(BF16), mlp_w1[i].astype(BF16), mlp_w2[i].astype(BF16),
                     latent_only=last)
        stream = (h, h)
    return h
```

```python
import functools
import math

import jax
import jax.numpy as jnp
import numpy as np
from jax import lax
from jax.experimental import pallas as pl
from jax.experimental.pallas import tpu as pltpu

F32 = jnp.float32
BF16 = jnp.bfloat16

D_MODEL = 1024
BATCH = 8
SEQ = 2048
DEPTH = 2
GRID_W = 64
CTX_LEN = 256
HEAD_DIM = 64
WINDOW = 128
A_Q_HEADS = 8
A_KV_HEADS = 2
B_HEADS = 4
A_Q_W = A_Q_HEADS * HEAD_DIM
A_KV_W = A_KV_HEADS * HEAD_DIM
B_QK_W = B_HEADS * 2 * HEAD_DIM
B_V_W = B_HEADS * 2 * HEAD_DIM
ATTN_IN = A_Q_W + 2 * A_KV_W + 2 * B_QK_W + B_V_W
S5_GROUP = 16
S5_GROUPS = D_MODEL // S5_GROUP
S5_STATE = 64
D_FF = 4 * D_MODEL
ROPE_BASE = 10000.0
EPS = 1e-6
NEG_INF = -1e30
LOG2E = math.log2(math.e)

TT = CTX_LEN + SEQ
TM = 256
N_TILES = TT // TM
LANES = 128
S5_CHUNK = 16
S5_CW = S5_CHUNK * S5_GROUP
N_CHUNKS = TT // S5_CHUNK
N_CTX_CHUNKS = CTX_LEN // S5_CHUNK
VMEM_LIMIT = 56 * 1024 * 1024


def _dot(a, b):
    return jnp.dot(a, b, preferred_element_type=F32)


def _dot_nt(a, b):
    return lax.dot_general(a, b, (((1,), (1,)), ((), ())), preferred_element_type=F32)


def _rms(x):
    return x * lax.rsqrt(jnp.mean(x * x, axis=-1, keepdims=True) + EPS)


def _modnorm(x, g, shift, scale):
    return _rms(x) * g * (1.0 + scale) + shift


def _params(**kw):
    return pltpu.CompilerParams(vmem_limit_bytes=VMEM_LIMIT, **kw)


def _mod_kernel(s_ref, w_ref, b_ref, o_ref):
    s = s_ref[...]
    s = s * jax.nn.sigmoid(s)
    o_ref[...] = _dot(s.astype(BF16), w_ref[...].astype(BF16)) + b_ref[...]


def _modulation(s_rows, mod_w, mod_b):
    return pl.pallas_call(
        _mod_kernel,
        out_shape=jax.ShapeDtypeStruct((DEPTH, 16, 6 * D_MODEL), F32),
        grid=(DEPTH, 6),
        in_specs=[
            pl.BlockSpec((16, D_MODEL), lambda i, j: (0, 0)),
            pl.BlockSpec((None, D_MODEL, D_MODEL), lambda i, j: (i, 0, j)),
            pl.BlockSpec((None, 1, D_MODEL), lambda i, j: (i, 0, j)),
        ],
        out_specs=pl.BlockSpec((None, 16, D_MODEL), lambda i, j: (i, 0, j)),
        compiler_params=_params(),
        name="modulation",
    )(s_rows, mod_w, mod_b.reshape(DEPTH, 1, 6 * D_MODEL))


def _mod_spec():
    return pl.BlockSpec((None, None, 6, D_MODEL), lambda b, t: (jnp.minimum(t, 1), b, 0, 0))


def _stream_specs(stream, h_off):
    unified = stream[0].shape[1] == TT
    first_lat = 1 if unified else 0
    ctx_spec = pl.BlockSpec((None, TM, D_MODEL), lambda b, t: (b, 0, 0))
    lat_spec = pl.BlockSpec((None, TM, D_MODEL),
                            lambda b, t: (b, jnp.maximum(t + h_off - 1 + first_lat, first_lat), 0))
    return [ctx_spec, lat_spec]


def _stream_tile(hc_ref, hl_ref, h_off):
    if h_off > 0:
        return hl_ref[...]
    return jnp.where(pl.program_id(1) == 0, hc_ref[...], hl_ref[...])


def _attn_in_kernel(hc_ref, hl_ref, mod_ref, g_ref, w_ref, gain_ref, cos_ref, sin_ref, e_ref,
                    qa_ref, k2a_ref, v2a_ref, qb_ref, kb_ref, vb_ref):
    mod = mod_ref[...]
    a = _modnorm(_stream_tile(hc_ref, hl_ref, 0), g_ref[...], mod[0:1], mod[1:2]).astype(BF16)
    cos_t = cos_ref[...]
    sin_s = sin_ref[...]
    e = e_ref[...]
    gains = gain_ref[...]
    lane = lax.broadcasted_iota(jnp.int32, (TM, LANES), 1)
    first_half = (lane & (HEAD_DIM - 1)) < HEAD_DIM // 2
    lo = lane < HEAD_DIM

    def norm_rope(c, gain):
        ms = _dot((c * c).astype(BF16), e)
        cn = c * lax.rsqrt(ms + EPS) * gain
        r_fwd = pltpu.roll(cn, HEAD_DIM // 2, 1)
        r_bwd = pltpu.roll(cn, LANES - HEAD_DIM // 2, 1)
        return cn * cos_t + jnp.where(first_half, r_bwd, r_fwd) * sin_s

    def dup_halves(x, ref):
        sw = pltpu.roll(x, HEAD_DIM, 1)
        ref[:, 0:LANES] = jnp.where(lo, x, sw).astype(BF16)
        ref[:, LANES:2 * LANES] = jnp.where(lo, sw, x).astype(BF16)

    q_scale = HEAD_DIM ** -0.5 * LOG2E

    def finish_q(z, ref, gain):
        for c in range(z.shape[1] // LANES):
            ref[:, c * LANES:(c + 1) * LANES] = (
                norm_rope(z[:, c * LANES:(c + 1) * LANES], gain) * q_scale).astype(BF16)

    def finish_kv_a(z):
        dup_halves(norm_rope(z[:, 0:LANES], gains[1:2]), k2a_ref)
        dup_halves(z[:, LANES:2 * LANES], v2a_ref)

    def finish_kb(z):
        for c in range(B_QK_W // LANES):
            kb_ref[:, c * LANES:(c + 1) * LANES] = norm_rope(z[:, c * LANES:(c + 1) * LANES], gains[3:4]).astype(BF16)

    def finish_vb(z):
        vb_ref[...] = z.astype(BF16)

    segments = [(A_Q_W, lambda z: finish_q(z, qa_ref, gains[0:1])), (2 * A_KV_W, finish_kv_a),
                (B_QK_W, lambda z: finish_q(z, qb_ref, gains[2:3])), (B_QK_W, finish_kb), (B_V_W, finish_vb)]
    off = 0
    pending = None
    for width, finish in segments:
        z = _dot(a, w_ref[:, off:off + width])
        off += width
        if pending is not None:
            pending[1](pending[0])
        pending = (z, finish)
    pending[1](pending[0])


def _attn_in(stream, mods, g, w_in, gains, cos_t, sin_s, e_blk):
    def tok(width):
        return pl.BlockSpec((None, TM, width), lambda b, t: (b, t, 0))

    def full(shape):
        return pl.BlockSpec(shape, lambda b, t: (0,) * len(shape))

    out_shapes = [jax.ShapeDtypeStruct((BATCH, TT, w), BF16)
                  for w in (A_Q_W, 2 * A_KV_W, 2 * A_KV_W, B_QK_W, B_QK_W, B_V_W)]
    return pl.pallas_call(
        _attn_in_kernel,
        out_shape=out_shapes,
        grid=(BATCH, N_TILES),
        in_specs=[
            *_stream_specs(stream, 0), _mod_spec(), full((1, D_MODEL)), full((D_MODEL, ATTN_IN)), full((4, LANES)),
            pl.BlockSpec((TM, LANES), lambda b, t: (t, 0)), pl.BlockSpec((TM, LANES), lambda b, t: (t, 0)),
            full((LANES, LANES)),
        ],
        out_specs=[tok(A_Q_W), tok(2 * A_KV_W), tok(2 * A_KV_W), tok(B_QK_W), tok(B_QK_W), tok(B_V_W)],
        compiler_params=_params(),
        name="attn_in_proj",
    )(*stream, mods, g, w_in, gains, cos_t, sin_s, e_blk)


QB = 128


def _win_attn_kernel(sink_ref, q_ref, k2_ref, v2_ref, o_ref):
    t = pl.program_id(1)
    lane = lax.broadcasted_iota(jnp.int32, (QB, LANES), 1)
    lo = lane < HEAD_DIM
    rows = 4 * QB
    row = lax.broadcasted_iota(jnp.int32, (rows, 3 * QB), 0)
    col = lax.broadcasted_iota(jnp.int32, (rows, 3 * QB), 1)
    row_head = lax.broadcasted_iota(jnp.int32, (rows, 1), 0) // QB
    zero = jnp.zeros((QB, LANES), BF16)
    blocks = [(qb, g) for qb in range(TM // QB) for g in range(A_KV_HEADS)]

    def window_start(qb):
        n = (t - 1) * (TM // QB) + qb
        ws = jnp.clip((n - 1) * QB, 0, SEQ - 3 * QB)
        return n, ws

    def scores(qb, g, with_window):
        pieces = []
        for p in range(2):
            qp = q_ref[qb * QB:(qb + 1) * QB, g * 2 * LANES + p * LANES: g * 2 * LANES + (p + 1) * LANES]
            pieces.append(jnp.where(lo, qp, zero))
            pieces.append(jnp.where(lo, zero, qp))
        qs = jnp.concatenate(pieces, axis=0)
        s_c = _dot_nt(qs, k2_ref[0:CTX_LEN, g * LANES:(g + 1) * LANES])
        if not with_window:
            return s_c, None
        n, ws = window_start(qb)
        kw = k2_ref[pl.ds(pl.multiple_of(ws + CTX_LEN, QB), 3 * QB), g * LANES:(g + 1) * LANES]
        valid = jnp.abs(n * QB + (row & (QB - 1)) - (ws + col)) <= WINDOW
        return s_c, jnp.where(valid, _dot_nt(qs, kw), NEG_INF)

    def finish(qb, g, s_c, s_w):
        sk = jnp.full((rows, 1), sink_ref[4 * g + 3], F32)
        for hh in range(3):
            sk = jnp.where(row_head == hh, sink_ref[4 * g + hh], sk)
        sk = sk * LOG2E
        m = jnp.maximum(jnp.max(s_c, axis=-1, keepdims=True), sk)
        if s_w is not None:
            m = jnp.maximum(m, jnp.max(s_w, axis=-1, keepdims=True))
        p_c = jnp.exp2(s_c - m)
        denom = jnp.sum(p_c, axis=-1, keepdims=True) + jnp.exp2(sk - m)
        o = _dot(p_c.astype(BF16), v2_ref[0:CTX_LEN, g * LANES:(g + 1) * LANES])
        if s_w is not None:
            _, ws = window_start(qb)
            p_w = jnp.exp2(s_w - m)
            denom = denom + jnp.sum(p_w, axis=-1, keepdims=True)
            vw = v2_ref[pl.ds(pl.multiple_of(ws + CTX_LEN, QB), 3 * QB), g * LANES:(g + 1) * LANES]
            o = o + _dot(p_w.astype(BF16), vw)
        o = o / denom
        for p in range(2):
            o_ref[qb * QB:(qb + 1) * QB, g * 2 * LANES + p * LANES: g * 2 * LANES + (p + 1) * LANES] = jnp.where(
                lo, o[2 * p * QB:(2 * p + 1) * QB], o[(2 * p + 1) * QB:(2 * p + 2) * QB]).astype(BF16)

    def attend(with_window):
        s_next = scores(*blocks[0], with_window)
        for i, blk in enumerate(blocks):
            s_cur = s_next
            if i + 1 < len(blocks):
                s_next = scores(*blocks[i + 1], with_window)
            finish(*blk, *s_cur)

    @pl.when(t == 0)
    def _():
        attend(False)

    @pl.when(t > 0)
    def _():
        attend(True)


def _win_attn(sink, qa, k2a, v2a):
    return pl.pallas_call(
        _win_attn_kernel,
        out_shape=jax.ShapeDtypeStruct((BATCH, TT, A_Q_W), BF16),
        grid=(BATCH, N_TILES),
        in_specs=[
            pl.BlockSpec(memory_space=pltpu.SMEM),
            pl.BlockSpec((None, TM, A_Q_W), lambda b, t: (b, t, 0)),
            pl.BlockSpec((None, TT, 2 * A_KV_W), lambda b, t: (b, 0, 0)),
            pl.BlockSpec((None, TT, 2 * A_KV_W), lambda b, t: (b, 0, 0)),
        ],
        out_specs=pl.BlockSpec((None, TM, A_Q_W), lambda b, t: (b, t, 0)),
        compiler_params=_params(),
        name="window_attention",
    )(sink, qa, k2a, v2a)


DIFF_ROWS = 128


def _diff_attn_kernel(lpar_ref, subln_ref, q_ref, k_ref, v_ref, o_ref, *, lambda_init):
    t = pl.program_id(1)
    lp = lpar_ref[...]
    lam = (jnp.exp(jnp.sum(lp[0:1] * lp[1:2], axis=-1, keepdims=True))
           - jnp.exp(jnp.sum(lp[2:3] * lp[3:4], axis=-1, keepdims=True)) + lambda_init)
    R = DIFF_ROWS
    lane = lax.broadcasted_iota(jnp.int32, (R, LANES), 1)
    lo = lane < HEAD_DIM
    zero = jnp.zeros((R, LANES), BF16)

    def attend(n_keys):
        blocks = [(slice(rb * R, (rb + 1) * R), slice(h * LANES, (h + 1) * LANES))
                  for h in range(B_HEADS) for rb in range(TM // R)]

        def scores(rows, cols):
            q = q_ref[rows, cols]
            qs = jnp.concatenate([jnp.where(lo, q, zero), jnp.where(lo, zero, q)], axis=0)
            return _dot_nt(qs, k_ref[0:n_keys, cols])

        def finish(rows, cols, s):
            p = jnp.exp2(s - jnp.max(s, axis=-1, keepdims=True))
            den = jnp.sum(p, axis=-1, keepdims=True)
            coef = lam * den[0:R] / den[R:2 * R]
            w = (p[0:R] - coef * p[R:2 * R]).astype(BF16)
            y = _dot(w, v_ref[0:n_keys, cols]) / den[0:R]
            o_ref[rows, cols] = (_rms(y) * subln_ref[...] * (1.0 - lambda_init)).astype(BF16)

        s_next = scores(*blocks[0])
        for i, blk in enumerate(blocks):
            s_cur = s_next
            if i + 1 < len(blocks):
                s_next = scores(*blocks[i + 1])
            finish(*blk, s_cur)

    @pl.when(t == 0)
    def _():
        attend(CTX_LEN)

    @pl.when(t > 0)
    def _():
        attend(TT)


def _diff_attn(lpar, subln, qb, kb, vb, lambda_init):
    return pl.pallas_call(
        functools.partial(_diff_attn_kernel, lambda_init=lambda_init),
        out_shape=jax.ShapeDtypeStruct((BATCH, TT, B_V_W), BF16),
        grid=(BATCH, N_TILES),
        in_specs=[
            pl.BlockSpec((4, HEAD_DIM), lambda b, t: (0, 0)),
            pl.BlockSpec((1, LANES), lambda b, t: (0, 0)),
            pl.BlockSpec((None, TM, B_QK_W), lambda b, t: (b, t, 0)),
            pl.BlockSpec((None, TT, B_QK_W), lambda b, t: (b, 0, 0)),
            pl.BlockSpec((None, TT, B_V_W), lambda b, t: (b, 0, 0)),
        ],
        out_specs=pl.BlockSpec((None, TM, B_V_W), lambda b, t: (b, t, 0)),
        compiler_params=_params(),
        name="diff_attention",
    )(lpar, subln, qb, kb, vb)


FF_CHUNK = 1024


def _mix_mlp_kernel(hc_ref, hl_ref, ua_ref, ub_ref, mod_ref, g_ref, wo_ref, w1_ref, w2_ref, o_ref, *, h_off):
    mod = mod_ref[...]
    half = D_MODEL // 2
    y = _dot(ua_ref[...], wo_ref[0:half, :]) + _dot(ub_ref[...], wo_ref[half:D_MODEL, :])
    h1 = _stream_tile(hc_ref, hl_ref, h_off) + mod[2:3] * y
    f = _modnorm(h1, g_ref[...], mod[3:4], mod[4:5]).astype(BF16)
    acc = jnp.zeros((TM, D_MODEL), F32)
    for c in range(D_FF // FF_CHUNK):
        hid = jnp.maximum(_dot(f, w1_ref[:, c * FF_CHUNK:(c + 1) * FF_CHUNK]), 0.0)
        acc = acc + _dot((hid * hid).astype(BF16), w2_ref[c * FF_CHUNK:(c + 1) * FF_CHUNK, :])
    o_ref[...] = h1 + mod[5:6] * acc


def _mix_mlp(stream, ua, ub, mods, g, wo, w1, w2, *, latent_only):
    n_tiles = SEQ // TM if latent_only else N_TILES
    h_off = N_TILES - n_tiles
    half = D_MODEL // 2
    ub_col = 1 if ub.shape[-1] == D_MODEL else 0

    def full(shape):
        return pl.BlockSpec(shape, lambda b, t: (0,) * len(shape), pipeline_mode=pl.Buffered(1))

    return pl.pallas_call(
        functools.partial(_mix_mlp_kernel, h_off=h_off),
        out_shape=jax.ShapeDtypeStruct((BATCH, n_tiles * TM, D_MODEL), F32),
        grid=(BATCH, n_tiles),
        in_specs=[
            *_stream_specs(stream, h_off),
            pl.BlockSpec((None, TM, half), lambda b, t: (b, t, 0)),
            pl.BlockSpec((None, TM, half), lambda b, t: (b, t, ub_col)),
            pl.BlockSpec((None, None, 6, D_MODEL), lambda b, t: (jnp.minimum(t + h_off, 1), b, 0, 0)),
            pl.BlockSpec((1, D_MODEL), lambda b, t: (0, 0)),
            full((D_MODEL, D_MODEL)), full((D_MODEL, D_FF)), full((D_FF, D_MODEL)),
        ],
        out_specs=pl.BlockSpec((None, TM, D_MODEL), lambda b, t: (b, t, 0)),
        compiler_params=_params(),
        name="mixer_out_mlp",
    )(*stream, ua, ub, mods, g, wo, w1, w2)


S5_PAIR = 2
S5_TOK = S5_PAIR * S5_CHUNK
S5_STEPS = N_CHUNKS // S5_PAIR
GROUPS_PER_VREG = LANES // S5_GROUP
STEPS_PER_VREG = LANES // S5_GROUP
S5_SEG = 2 * LANES


def _lane_block():
    return lax.broadcasted_iota(jnp.int32, (BATCH, LANES), 1) // S5_GROUP


def _s5_in_kernel(h_ref, mod_ref, g_ref, perm_ref, w_ref, u_ref, z_ref, u_scr):
    mod = mod_ref[...]
    x = h_ref[...]
    a = _rms(x) * g_ref[...] * (1.0 + mod[:, 1:2, :]) + mod[:, 0:1, :]
    a = a.reshape(BATCH * S5_TOK, D_MODEL).astype(BF16)
    a = _dot(perm_ref[...], a).astype(BF16)
    lane_blk = _lane_block()

    def regroup(k):
        for r in range(GROUPS_PER_VREG):
            for hh in range(S5_CHUNK // STEPS_PER_VREG):
                halves = []
                for c2 in range(S5_PAIR):
                    acc = None
                    for m in range(STEPS_PER_VREG):
                        tok = c2 * S5_CHUNK + hh * STEPS_PER_VREG + m
                        piece = u_scr[k, tok * BATCH:(tok + 1) * BATCH, :]
                        sh = (S5_GROUP * (m - r)) % LANES
                        if sh:
                            piece = pltpu.roll(piece, sh, 1)
                        acc = piece if acc is None else jnp.where(lane_blk == m, piece, acc)
                    halves.append(acc)
                z_ref[k * GROUPS_PER_VREG + r, :, hh * LANES:(hh + 1) * LANES] = (
                    jnp.concatenate(halves, axis=0).astype(BF16))

    def finish(s, u_seg):
        u_ref[:, :, s * S5_SEG:(s + 1) * S5_SEG] = u_seg.reshape(S5_TOK, BATCH, S5_SEG)
        for kk in range(S5_SEG // LANES):
            k = s * (S5_SEG // LANES) + kk
            u_scr[k] = u_seg[:, kk * LANES:(kk + 1) * LANES]
            regroup(k)

    pending = None
    for s in range(D_MODEL // S5_SEG):
        u_seg = _dot(a, w_ref[:, s * S5_SEG:(s + 1) * S5_SEG])
        if pending is not None:
            finish(*pending)
        pending = (s, u_seg)
    finish(*pending)


def _s5_in(h, mods, g, perm, w_in):
    ctx_steps = N_CTX_CHUNKS // S5_PAIR
    n_rows = BATCH * S5_TOK
    return pl.pallas_call(
        _s5_in_kernel,
        out_shape=[jax.ShapeDtypeStruct((TT, BATCH, D_MODEL), F32),
                   jax.ShapeDtypeStruct((S5_GROUPS, ROWS, S5_CW), BF16)],
        grid=(S5_STEPS,),
        in_specs=[pl.BlockSpec((BATCH, S5_TOK, D_MODEL), lambda p: (0, p, 0)),
                  pl.BlockSpec((None, BATCH, 6, D_MODEL), lambda p: (jnp.minimum(p // ctx_steps, 1), 0, 0, 0)),
                  pl.BlockSpec((1, D_MODEL), lambda p: (0, 0)),
                  pl.BlockSpec((n_rows, n_rows), lambda p: (0, 0)),
                  pl.BlockSpec((D_MODEL, D_MODEL), lambda p: (0, 0))],
        out_specs=[pl.BlockSpec((S5_TOK, BATCH, D_MODEL), lambda p: (p, 0, 0)),
                   pl.BlockSpec((S5_GROUPS, S5_PAIR * BATCH, S5_CW), lambda p: (0, p, 0))],
        scratch_shapes=[pltpu.VMEM((D_MODEL // LANES, n_rows, LANES), F32)],
        compiler_params=_params(),
        name="s5_in_proj",
    )(h, mods, g, perm, w_in)


def _s5_ops_kernel(lr_c_ref, li_c_ref, ls_c_ref, lr_r_ref, li_r_ref, ls_r_ref,
                   btr_ref, bti_ref, btile_r_ref, btile_i_ref, ctile_r_ref, ctile_i_ref,
                   m_ref, q_ref, n_ref, a_ref):
    P2 = 2 * S5_STATE

    def discretize(lr, li, ls):
        dt = jnp.exp(ls)
        mag = jnp.exp(lr * dt)
        ab_re = mag * jnp.cos(li * dt)
        ab_im = mag * jnp.sin(li * dt)
        den = lr * lr + li * li
        nr = ab_re - 1.0
        f_re = (nr * lr + ab_im * li) / den
        f_im = (ab_im * lr - nr * li) / den
        return ab_re, ab_im, f_re, f_im

    def cmul(xr, xi, yr, yi):
        return xr * yr - xi * yi, xr * yi + xi * yr

    ar_c, ai_c, f_re_c, f_im_c = discretize(lr_c_ref[...], li_c_ref[...], ls_c_ref[...])
    lag = lax.broadcasted_iota(jnp.int32, (P2, S5_CW), 1) // S5_GROUP
    is_fwd = lax.broadcasted_iota(jnp.int32, (P2, S5_CW), 0) < S5_STATE
    e_k = jnp.where(is_fwd, lag, (S5_CHUNK - 1) - lag)
    squares = [(ar_c, ai_c)]
    while 2 ** len(squares) < S5_CHUNK:
        squares.append(cmul(*squares[-1], *squares[-1]))

    def a_pow(e):
        pr = jnp.ones((P2, S5_CW), F32)
        pi = jnp.zeros((P2, S5_CW), F32)
        for b, (br, bi) in enumerate(squares):
            bit = ((e >> b) & 1) == 1
            tr, ti = cmul(pr, pi, br, bi)
            pr = jnp.where(bit, tr, pr)
            pi = jnp.where(bit, ti, pi)
        return pr, pi

    c_re, c_im = ctile_r_ref[...], ctile_i_ref[...]
    pk_re, pk_im = a_pow(e_k)
    cp_re, cp_im = cmul(c_re, c_im, pk_re, pk_im)

    ar_r, ai_r, f_re_r, f_im_r = discretize(lr_r_ref[...], li_r_ref[...], ls_r_ref[...])
    bt_re = f_re_r * btr_ref[...] - f_im_r * bti_ref[...]
    bt_im = f_re_r * bti_ref[...] + f_im_r * btr_ref[...]
    lane_fwd = lax.broadcasted_iota(jnp.int32, (S5_GROUP, P2), 1) < S5_STATE

    def lag_kernels(keep):
        br = jnp.where(keep, bt_re, 0.0)
        bi = jnp.where(keep, bt_im, 0.0)
        hi = lax.Precision.HIGHEST
        return (jnp.dot(br, cp_re, precision=hi, preferred_element_type=F32)
                - jnp.dot(bi, cp_im, precision=hi, preferred_element_type=F32))

    kt_f = lag_kernels(lane_fwd)
    kt_b = lag_kernels(jnp.logical_not(lane_fwd))
    lane_w = lax.broadcasted_iota(jnp.int32, (S5_GROUP, S5_CW), 1)
    for s in range(S5_CHUNK):
        f_part = kt_f if s == 0 else jnp.where(lane_w >= S5_GROUP * s, pltpu.roll(kt_f, S5_GROUP * s, 1), 0.0)
        sh = (S5_GROUP * (s + 1)) % S5_CW
        b_roll = kt_b if sh == 0 else pltpu.roll(kt_b, sh, 1)
        b_part = jnp.where(lane_w < S5_GROUP * (s + 1), b_roll, 0.0)
        m_ref[s * S5_GROUP:(s + 1) * S5_GROUP, :] = (f_part + b_part).astype(BF16)

    bb_re = f_re_c * btile_r_ref[...] - f_im_c * btile_i_ref[...]
    bb_im = f_re_c * btile_i_ref[...] + f_im_c * btile_r_ref[...]
    q_re, q_im = cmul(*a_pow((S5_CHUNK - 1) - e_k), bb_re, bb_im)
    q_ref[0:P2, :] = q_re.astype(BF16)
    q_ref[P2:2 * P2, :] = q_im.astype(BF16)

    n_re, n_im = cmul(cp_re, cp_im, ar_c, ai_c)
    n_ref[0:P2, :] = n_re.astype(BF16)
    n_ref[P2:2 * P2, :] = (-n_im).astype(BF16)

    a_re, a_im = ar_r, ai_r
    for _ in range(S5_CHUNK.bit_length() - 1):
        a_re, a_im = cmul(a_re, a_im, a_re, a_im)
    a_ref[0:1, :] = a_re
    a_ref[1:2, :] = a_im


def _s5_ops(col_params, row_params, bt, btile, ctile):
    P2 = 2 * S5_STATE
    col = pl.BlockSpec((None, P2, 1), lambda g: (g, 0, 0))
    row = pl.BlockSpec((None, 1, P2), lambda g: (g, 0, 0))
    btspec = pl.BlockSpec((None, S5_GROUP, P2), lambda g: (g, 0, 0))
    tile = pl.BlockSpec((None, P2, S5_CW), lambda g: (g, 0, 0))
    sq = pl.BlockSpec((None, S5_CW, S5_CW), lambda g: (g, 0, 0))
    return pl.pallas_call(
        _s5_ops_kernel,
        out_shape=[jax.ShapeDtypeStruct((S5_GROUPS, S5_CW, S5_CW), BF16)] * 3
        + [jax.ShapeDtypeStruct((S5_GROUPS, 2, P2), F32)],
        grid=(S5_GROUPS,),
        in_specs=[col, col, col, row, row, row, btspec, btspec, tile, tile, tile, tile],
        out_specs=[sq, sq, sq, pl.BlockSpec((None, 2, P2), lambda g: (g, 0, 0))],
        compiler_params=_params(),
        name="s5_chunk_operators",
    )(*col_params, *row_params, *bt, *btile, *ctile)


ROWS = N_CHUNKS * BATCH


SCAN_G = 4


def _s5_scan_kernel(u_ref, m_ref, q_ref, n_ref, a_ref, y_ref, pu_ref, sp_ref):
    S = S5_STATE
    lane = lax.broadcasted_iota(jnp.int32, (BATCH, 2 * S), 1)
    lo = lane < S
    for j in range(SCAN_G):
        pu_ref[j] = _dot_nt(u_ref[j], q_ref[j])
    for j in range(SCAN_G):
        y_ref[j] = _dot(u_ref[j], m_ref[j])
    decay = [(a_ref[j, 0:1, :], a_ref[j, 1:2, :]) for j in range(SCAN_G)]
    zero = jnp.zeros((BATCH, 2 * S), F32)
    state = [(zero, zero)] * SCAN_G
    for k in range(N_CHUNKS):
        cf = k * BATCH
        cb = (N_CTX_CHUNKS - 1 - k if k < N_CTX_CHUNKS else N_CHUNKS + N_CTX_CHUNKS - 1 - k) * BATCH
        for j in range(SCAN_G):
            s_re, s_im = state[j]
            a_re, a_im = decay[j]
            sp_ref[j, cf:cf + BATCH, 0:S] = s_re[:, 0:S]
            sp_ref[j, cb:cb + BATCH, S:2 * S] = s_re[:, S:2 * S]
            sp_ref[j, cf:cf + BATCH, 2 * S:3 * S] = s_im[:, 0:S]
            sp_ref[j, cb:cb + BATCH, 3 * S:4 * S] = s_im[:, S:2 * S]
            x_re = jnp.where(lo, pu_ref[j, cf:cf + BATCH, 0:2 * S], pu_ref[j, cb:cb + BATCH, 0:2 * S])
            x_im = jnp.where(lo, pu_ref[j, cf:cf + BATCH, 2 * S:4 * S], pu_ref[j, cb:cb + BATCH, 2 * S:4 * S])
            state[j] = (a_re * s_re - a_im * s_im + x_re, a_re * s_im + a_im * s_re + x_im)
    for j in range(SCAN_G):
        y_ref[j] += _dot(sp_ref[j].astype(BF16), n_ref[j])


def _s5_scan(u_g, m_op, q_op, n_op, a_vec):
    sq = pl.BlockSpec((SCAN_G, S5_CW, S5_CW), lambda g: (g, 0, 0))
    rows = pl.BlockSpec((SCAN_G, ROWS, S5_CW), lambda g: (g, 0, 0))
    return pl.pallas_call(
        _s5_scan_kernel,
        out_shape=jax.ShapeDtypeStruct((S5_GROUPS, ROWS, S5_CW), F32),
        grid=(S5_GROUPS // SCAN_G,),
        in_specs=[rows, sq, sq, sq, pl.BlockSpec((SCAN_G, 2, 2 * S5_STATE), lambda g: (g, 0, 0))],
        out_specs=rows,
        scratch_shapes=[pltpu.VMEM((SCAN_G, ROWS, S5_CW), F32), pltpu.VMEM((SCAN_G, ROWS, S5_CW), F32)],
        compiler_params=_params(),
        name="s5_scan",
    )(u_g, m_op, q_op, n_op, a_vec)


def _s5_out_kernel(u_ref, y_ref, d_ref, gw_ref, gb_ref, perm_ref, o_ref, ys_scr):
    lane_blk = _lane_block()

    def regroup(k):
        for c2 in range(S5_PAIR):
            for t in range(S5_CHUNK):
                hh, m = divmod(t, STEPS_PER_VREG)
                acc = None
                for r in range(GROUPS_PER_VREG):
                    piece = y_ref[k * GROUPS_PER_VREG + r, c2 * BATCH:(c2 + 1) * BATCH, hh * LANES:(hh + 1) * LANES]
                    sh = (S5_GROUP * (r - m)) % LANES
                    if sh:
                        piece = pltpu.roll(piece, sh, 1)
                    acc = piece if acc is None else jnp.where(lane_blk == r, piece, acc)
                tok = c2 * S5_CHUNK + t
                ys_scr[k, tok * BATCH:(tok + 1) * BATCH, :] = acc

    n_seg = D_MODEL // S5_SEG
    per_seg = S5_SEG // LANES
    for kk in range(per_seg):
        regroup(kk)
    g_parts = []
    pre = None
    for s in range(n_seg):
        if s + 1 < n_seg:
            for kk in range(per_seg):
                regroup((s + 1) * per_seg + kk)
        cols = slice(s * S5_SEG, (s + 1) * S5_SEG)
        ys = jnp.concatenate([ys_scr[s * per_seg + kk] for kk in range(per_seg)], axis=1)
        y = u_ref[:, :, cols].reshape(BATCH * S5_TOK, S5_SEG) * d_ref[:, cols] + ys
        g_seg = jax.nn.gelu(y)
        g_parts.append(g_seg)
        part = _dot(g_seg.astype(BF16), gw_ref[cols, :])
        pre = part if pre is None else pre + part
    gate = jax.nn.sigmoid(pre + gb_ref[...])
    g = jnp.concatenate(g_parts, axis=1)
    gated = (g * gate).astype(BF16)
    gated = _dot(perm_ref[...], gated).astype(BF16)
    o_ref[...] = gated.reshape(BATCH, S5_TOK, D_MODEL)


def _s5_out(u, y_g, d_skip, glu_w, glu_b, perm_t):
    ctx_steps = N_CTX_CHUNKS // S5_PAIR
    n_rows = BATCH * S5_TOK
    vec = pl.BlockSpec((1, D_MODEL), lambda p: (0, 0))
    return pl.pallas_call(
        _s5_out_kernel,
        out_shape=jax.ShapeDtypeStruct((BATCH, SEQ, D_MODEL), BF16),
        grid=(S5_STEPS - ctx_steps,),
        in_specs=[pl.BlockSpec((S5_TOK, BATCH, D_MODEL), lambda p: (p + ctx_steps, 0, 0)),
                  pl.BlockSpec((S5_GROUPS, S5_PAIR * BATCH, S5_CW), lambda p: (0, p + ctx_steps, 0)),
                  vec, pl.BlockSpec((D_MODEL, D_MODEL), lambda p: (0, 0)), vec,
                  pl.BlockSpec((n_rows, n_rows), lambda p: (0, 0))],
        out_specs=pl.BlockSpec((BATCH, S5_TOK, D_MODEL), lambda p: (0, p, 0)),
        scratch_shapes=[pltpu.VMEM((D_MODEL // LANES, n_rows, LANES), F32)],
        compiler_params=_params(),
        name="s5_gelu_glu",
    )(u, y_g, d_skip, glu_w, glu_b, perm_t)


def _rope_tables():
    rows_n = SEQ // GRID_W
    row = jnp.repeat(jnp.arange(rows_n, dtype=F32), GRID_W)
    col = jnp.tile(jnp.arange(GRID_W, dtype=F32), rows_n)
    n_freq = HEAD_DIM // 4
    inv = ROPE_BASE ** (-jnp.arange(n_freq, dtype=F32) / n_freq)
    ang = jnp.concatenate([row[:, None] * inv, col[:, None] * inv], axis=-1)
    reps = LANES // (HEAD_DIM // 2)
    cos_t = jnp.tile(jnp.cos(ang), (1, reps))
    sin_t = jnp.tile(jnp.sin(ang), (1, reps))
    sign = jnp.where((jnp.arange(LANES) % HEAD_DIM) < HEAD_DIM // 2, -1.0, 1.0).astype(F32)
    cos_t = jnp.concatenate([jnp.ones((CTX_LEN, LANES), F32), cos_t], axis=0)
    sin_s = jnp.concatenate([jnp.zeros((CTX_LEN, LANES), F32), sin_t * sign], axis=0)
    return cos_t, sin_s


def _fb_cols(x):
    return jnp.transpose(x, (1, 0, 2)).reshape(S5_GROUPS, 2 * S5_STATE, 1)


def _fb_rows(x):
    return jnp.transpose(x, (1, 0, 2)).reshape(S5_GROUPS, 1, 2 * S5_STATE)


def _s5_layout(lam_re, lam_im, log_step, b_re, b_im, c_re, c_im):
    ls = jnp.broadcast_to(log_step[:, :, None], lam_re.shape)
    cols = [_fb_cols(v) for v in (lam_re, lam_im, ls)]
    rows = [_fb_rows(v) for v in (lam_re, lam_im, ls)]

    def bt_of(b):
        return jnp.transpose(b, (1, 3, 0, 2)).reshape(S5_GROUPS, S5_GROUP, 2 * S5_STATE)

    def btile_of(b):
        t = jnp.transpose(b, (1, 0, 2, 3)).reshape(S5_GROUPS, 2 * S5_STATE, S5_GROUP)
        return jnp.tile(t, (1, 1, S5_CHUNK))

    def ctile_of(c):
        t = jnp.transpose(c, (1, 0, 3, 2)).reshape(S5_GROUPS, 2 * S5_STATE, S5_GROUP)
        return jnp.tile(t, (1, 1, S5_CHUNK))

    return cols, rows, [bt_of(b_re), bt_of(b_im)], [btile_of(b_re), btile_of(b_im)], [ctile_of(c_re), ctile_of(c_im)]


def kernel(x, c, ctx, c_ctx, norm1_g, norm2_g, mod_w, mod_b, mlp_w1, mlp_w2, attn_w_in, attn_w_out, a_q_norm, a_k_norm, a_sink, b_q_norm, b_k_norm, b_lq1, b_lk1, b_lq2, b_lk2, b_subln, s5_w_in, s5_lambda_re, s5_lambda_im, s5_log_step, s5_b_re, s5_b_im, s5_c_re, s5_c_im, s5_d, s5_glu_w, s5_glu_b, s5_w_out):
    assert x.shape == (BATCH, SEQ, D_MODEL) and ctx.shape == (BATCH, CTX_LEN, D_MODEL)
    stream = (ctx, x)
    s_rows = jnp.concatenate([c, c_ctx[None], jnp.zeros((16 - BATCH - 1, D_MODEL), F32)], axis=0)
    m_all = _modulation(s_rows, mod_w, mod_b)
    cos_t, sin_s = _rope_tables()
    e_blk = (jnp.kron(jnp.eye(LANES // HEAD_DIM, dtype=F32), jnp.ones((HEAD_DIM, HEAD_DIM), F32))
             / HEAD_DIM).astype(BF16)

    for i in range(DEPTH):
        last = i == DEPTH - 1
        j = i // 2
        m_lat = m_all[i, :BATCH].reshape(BATCH, 6, D_MODEL)
        m_ctx = jnp.broadcast_to(m_all[i, BATCH].reshape(1, 6, D_MODEL), (BATCH, 6, D_MODEL))
        mods = jnp.stack([m_ctx, m_lat])
        g1 = norm1_g[i].reshape(1, D_MODEL)
        g2 = norm2_g[i].reshape(1, D_MODEL)
        if i % 2 == 0:
            lambda_init = 0.8 - 0.6 * math.exp(-0.3 * i)
            gains = jnp.stack([jnp.tile(v[j], LANES // HEAD_DIM) for v in (a_q_norm, a_k_norm, b_q_norm, b_k_norm)])
            qa, k2a, v2a, qb, kb, vb = _attn_in(stream, mods, g1, attn_w_in[j].astype(BF16), gains, cos_t, sin_s, e_blk)
            ya = _win_attn(a_sink[j], qa, k2a, v2a)
            lpar = jnp.stack([b_lq1[j], b_lk1[j], b_lq2[j], b_lk2[j]])
            yb = _diff_attn(lpar, b_subln[j].reshape(1, LANES), qb, kb, vb, lambda_init)
            if last:
                ya, yb = ya[:, CTX_LEN:], yb[:, CTX_LEN:]
            ua, ub, wo = ya, yb, attn_w_out[j]
        else:
            src = np.arange(BATCH * S5_TOK).reshape(BATCH, S5_TOK).T.reshape(-1)
            perm = jnp.asarray(np.eye(BATCH * S5_TOK, dtype=np.float32)[src], BF16)
            h_all = stream[0] if stream[0].shape[1] == TT else jnp.concatenate(stream, axis=1)
            u, u_g = _s5_in(h_all, mods, g1, perm, s5_w_in[j].astype(BF16))
            ops_in = _s5_layout(s5_lambda_re[j], s5_lambda_im[j], s5_log_step[j], s5_b_re[j], s5_b_im[j],
                                s5_c_re[j], s5_c_im[j])
            m_op, q_op, n_op, a_vec = _s5_ops(*ops_in)
            y_g = _s5_scan(u_g, m_op, q_op, n_op, a_vec)
            assert last, "S5 layers before the last one would also need the context rows of the readout"
            gated = _s5_out(u, y_g, s5_d[j].reshape(1, D_MODEL), s5_glu_w[j].astype(BF16),
                            s5_glu_b[j].reshape(1, D_MODEL), perm.T)
            ua, ub, wo = gated, gated, s5_w_out[j]
        h = _mix_mlp(stream, ua, ub, mods, g2, wo.astype(BF16), mlp_w1[i].astype(BF16), mlp_w2[i].astype(BF16),
                     latent_only=last)
        stream = (h, h)
    return h
```

```python
import functools
import math

import jax
import jax.numpy as jnp
import numpy as np
from jax import lax
from jax.experimental import pallas as pl
from jax.experimental.pallas import tpu as pltpu

F32 = jnp.float32
BF16 = jnp.bfloat16

D_MODEL = 1024
BATCH = 8
SEQ = 2048
DEPTH = 2
GRID_W = 64
CTX_LEN = 256
HEAD_DIM = 64
WINDOW = 128
A_Q_HEADS = 8
A_KV_HEADS = 2
B_HEADS = 4
A_Q_W = A_Q_HEADS * HEAD_DIM
A_KV_W = A_KV_HEADS * HEAD_DIM
B_QK_W = B_HEADS * 2 * HEAD_DIM
B_V_W = B_HEADS * 2 * HEAD_DIM
ATTN_IN = A_Q_W + 2 * A_KV_W + 2 * B_QK_W + B_V_W
S5_GROUP = 16
S5_GROUPS = D_MODEL // S5_GROUP
S5_STATE = 64
D_FF = 4 * D_MODEL
ROPE_BASE = 10000.0
EPS = 1e-6
NEG_INF = -1e30
LOG2E = math.log2(math.e)

TT = CTX_LEN + SEQ
TM = 256
N_TILES = TT // TM
LANES = 128
S5_CHUNK = 16
S5_CW = S5_CHUNK * S5_GROUP
N_CHUNKS = TT // S5_CHUNK
N_CTX_CHUNKS = CTX_LEN // S5_CHUNK
VMEM_LIMIT = 56 * 1024 * 1024


def _dot(a, b):
    return jnp.dot(a, b, preferred_element_type=F32)


def _dot_nt(a, b):
    return lax.dot_general(a, b, (((1,), (1,)), ((), ())), preferred_element_type=F32)


def _rms(x):
    return x * lax.rsqrt(jnp.mean(x * x, axis=-1, keepdims=True) + EPS)


def _modnorm(x, g, shift, scale):
    return _rms(x) * g * (1.0 + scale) + shift


def _params(**kw):
    return pltpu.CompilerParams(vmem_limit_bytes=VMEM_LIMIT, **kw)


def _mod_kernel(s_ref, w_ref, b_ref, o_ref):
    s = s_ref[...]
    s = s * jax.nn.sigmoid(s)
    o_ref[...] = _dot(s.astype(BF16), w_ref[...].astype(BF16)) + b_ref[...]


def _modulation(s_rows, mod_w, mod_b):
    return pl.pallas_call(
        _mod_kernel,
        out_shape=jax.ShapeDtypeStruct((DEPTH, 16, 6 * D_MODEL), F32),
        grid=(DEPTH, 6),
        in_specs=[
            pl.BlockSpec((16, D_MODEL), lambda i, j: (0, 0)),
            pl.BlockSpec((None, D_MODEL, D_MODEL), lambda i, j: (i, 0, j)),
            pl.BlockSpec((None, 1, D_MODEL), lambda i, j: (i, 0, j)),
        ],
        out_specs=pl.BlockSpec((None, 16, D_MODEL), lambda i, j: (i, 0, j)),
        compiler_params=_params(),
        name="modulation",
    )(s_rows, mod_w, mod_b.reshape(DEPTH, 1, 6 * D_MODEL))


def _mod_spec():
    return pl.BlockSpec((None, None, 6, D_MODEL), lambda b, t: (jnp.minimum(t, 1), b, 0, 0))


def _stream_specs(stream, h_off):
    unified = stream[0].shape[1] == TT
    first_lat = 1 if unified else 0
    ctx_spec = pl.BlockSpec((None, TM, D_MODEL), lambda b, t: (b, 0, 0))
    lat_spec = pl.BlockSpec((None, TM, D_MODEL),
                            lambda b, t: (b, jnp.maximum(t + h_off - 1 + first_lat, first_lat), 0))
    return [ctx_spec, lat_spec]


def _stream_tile(hc_ref, hl_ref, h_off):
    if h_off > 0:
        return hl_ref[...]
    return jnp.where(pl.program_id(1) == 0, hc_ref[...], hl_ref[...])


def _attn_in_kernel(hc_ref, hl_ref, mod_ref, g_ref, w_ref, gain_ref, cos_ref, sin_ref, e_ref,
                    qa_ref, k2a_ref, v2a_ref, qb_ref, kb_ref, vb_ref):
    mod = mod_ref[...]
    a = _modnorm(_stream_tile(hc_ref, hl_ref, 0), g_ref[...], mod[0:1], mod[1:2]).astype(BF16)
    cos_t = cos_ref[...]
    sin_s = sin_ref[...]
    e = e_ref[...]
    gains = gain_ref[...]
    lane = lax.broadcasted_iota(jnp.int32, (TM, LANES), 1)
    first_half = (lane & (HEAD_DIM - 1)) < HEAD_DIM // 2
    lo = lane < HEAD_DIM

    def norm_rope(c, gain):
        ms = _dot((c * c).astype(BF16), e)
        cn = c * lax.rsqrt(ms + EPS) * gain
        r_fwd = pltpu.roll(cn, HEAD_DIM // 2, 1)
        r_bwd = pltpu.roll(cn, LANES - HEAD_DIM // 2, 1)
        return cn * cos_t + jnp.where(first_half, r_bwd, r_fwd) * sin_s

    def dup_halves(x, ref):
        sw = pltpu.roll(x, HEAD_DIM, 1)
        ref[:, 0:LANES] = jnp.where(lo, x, sw).astype(BF16)
        ref[:, LANES:2 * LANES] = jnp.where(lo, sw, x).astype(BF16)

    q_scale = HEAD_DIM ** -0.5 * LOG2E

    def finish_q(z, ref, gain):
        for c in range(z.shape[1] // LANES):
            ref[:, c * LANES:(c + 1) * LANES] = (
                norm_rope(z[:, c * LANES:(c + 1) * LANES], gain) * q_scale).astype(BF16)

    def finish_kv_a(z):
        dup_halves(norm_rope(z[:, 0:LANES], gains[1:2]), k2a_ref)
        dup_halves(z[:, LANES:2 * LANES], v2a_ref)

    def finish_kb(z):
        for c in range(B_QK_W // LANES):
            kb_ref[:, c * LANES:(c + 1) * LANES] = norm_rope(z[:, c * LANES:(c + 1) * LANES], gains[3:4]).astype(BF16)

    def finish_vb(z):
        for hd in range(B_HEADS):
            vb_ref[:, 2 * hd * LANES:(2 * hd + 1) * LANES] = z[:, hd * LANES:(hd + 1) * LANES].astype(BF16)
            vb_ref[:, (2 * hd + 1) * LANES:(2 * hd + 2) * LANES] = jnp.ones((TM, LANES), BF16)

    segments = [(A_Q_W, lambda z: finish_q(z, qa_ref, gains[0:1])), (2 * A_KV_W, finish_kv_a),
                (B_QK_W, lambda z: finish_q(z, qb_ref, gains[2:3])), (B_QK_W, finish_kb), (B_V_W, finish_vb)]
    off = 0
    pending = None
    for width, finish in segments:
        z = _dot(a, w_ref[:, off:off + width])
        off += width
        if pending is not None:
            pending[1](pending[0])
        pending = (z, finish)
    pending[1](pending[0])


def _attn_in(stream, mods, g, w_in, gains, cos_t, sin_s, e_blk):
    def tok(width):
        return pl.BlockSpec((None, TM, width), lambda b, t: (b, t, 0))

    def full(shape):
        return pl.BlockSpec(shape, lambda b, t: (0,) * len(shape))

    out_shapes = [jax.ShapeDtypeStruct((BATCH, TT, w), BF16)
                  for w in (A_Q_W, 2 * A_KV_W, 2 * A_KV_W, B_QK_W, B_QK_W, 2 * B_V_W)]
    return pl.pallas_call(
        _attn_in_kernel,
        out_shape=out_shapes,
        grid=(BATCH, N_TILES),
        in_specs=[
            *_stream_specs(stream, 0), _mod_spec(), full((1, D_MODEL)), full((D_MODEL, ATTN_IN)), full((4, LANES)),
            pl.BlockSpec((TM, LANES), lambda b, t: (t, 0)), pl.BlockSpec((TM, LANES), lambda b, t: (t, 0)),
            full((LANES, LANES)),
        ],
        out_specs=[tok(A_Q_W), tok(2 * A_KV_W), tok(2 * A_KV_W), tok(B_QK_W), tok(B_QK_W), tok(2 * B_V_W)],
        compiler_params=_params(),
        name="attn_in_proj",
    )(*stream, mods, g, w_in, gains, cos_t, sin_s, e_blk)


QB = 128


def _win_attn_kernel(sink_ref, q_ref, k2_ref, v2_ref, o_ref):
    t = pl.program_id(1)
    lane = lax.broadcasted_iota(jnp.int32, (QB, LANES), 1)
    lo = lane < HEAD_DIM
    rows = 4 * QB
    row = lax.broadcasted_iota(jnp.int32, (rows, 3 * QB), 0)
    col = lax.broadcasted_iota(jnp.int32, (rows, 3 * QB), 1)
    row_head = lax.broadcasted_iota(jnp.int32, (rows, 1), 0) // QB
    zero = jnp.zeros((QB, LANES), BF16)
    blocks = [(qb, g) for qb in range(TM // QB) for g in range(A_KV_HEADS)]

    def window_start(qb):
        n = (t - 1) * (TM // QB) + qb
        ws = jnp.clip((n - 1) * QB, 0, SEQ - 3 * QB)
        return n, ws

    def scores(qb, g, with_window):
        pieces = []
        for p in range(2):
            qp = q_ref[qb * QB:(qb + 1) * QB, g * 2 * LANES + p * LANES: g * 2 * LANES + (p + 1) * LANES]
            pieces.append(jnp.where(lo, qp, zero))
            pieces.append(jnp.where(lo, zero, qp))
        qs = jnp.concatenate(pieces, axis=0)
        s_c = _dot_nt(qs, k2_ref[0:CTX_LEN, g * LANES:(g + 1) * LANES])
        if not with_window:
            return s_c, None
        n, ws = window_start(qb)
        kw = k2_ref[pl.ds(pl.multiple_of(ws + CTX_LEN, QB), 3 * QB), g * LANES:(g + 1) * LANES]
        valid = jnp.abs(n * QB + (row & (QB - 1)) - (ws + col)) <= WINDOW
        return s_c, jnp.where(valid, _dot_nt(qs, kw), NEG_INF)

    def finish(qb, g, s_c, s_w):
        sk = jnp.full((rows, 1), sink_ref[4 * g + 3], F32)
        for hh in range(3):
            sk = jnp.where(row_head == hh, sink_ref[4 * g + hh], sk)
        sk = sk * LOG2E
        m = jnp.maximum(jnp.max(s_c, axis=-1, keepdims=True), sk)
        if s_w is not None:
            m = jnp.maximum(m, jnp.max(s_w, axis=-1, keepdims=True))
        p_c = jnp.exp2(s_c - m)
        denom = jnp.sum(p_c, axis=-1, keepdims=True) + jnp.exp2(sk - m)
        o = _dot(p_c.astype(BF16), v2_ref[0:CTX_LEN, g * LANES:(g + 1) * LANES])
        if s_w is not None:
            _, ws = window_start(qb)
            p_w = jnp.exp2(s_w - m)
            denom = denom + jnp.sum(p_w, axis=-1, keepdims=True)
            vw = v2_ref[pl.ds(pl.multiple_of(ws + CTX_LEN, QB), 3 * QB), g * LANES:(g + 1) * LANES]
            o = o + _dot(p_w.astype(BF16), vw)
        o = o / denom
        for p in range(2):
            o_ref[qb * QB:(qb + 1) * QB, g * 2 * LANES + p * LANES: g * 2 * LANES + (p + 1) * LANES] = jnp.where(
                lo, o[2 * p * QB:(2 * p + 1) * QB], o[(2 * p + 1) * QB:(2 * p + 2) * QB]).astype(BF16)

    def attend(with_window):
        s_next = scores(*blocks[0], with_window)
        for i, blk in enumerate(blocks):
            s_cur = s_next
            if i + 1 < len(blocks):
                s_next = scores(*blocks[i + 1], with_window)
            finish(*blk, *s_cur)

    @pl.when(t == 0)
    def _():
        attend(False)

    @pl.when(t > 0)
    def _():
        attend(True)


def _win_attn(sink, qa, k2a, v2a):
    return pl.pallas_call(
        _win_attn_kernel,
        out_shape=jax.ShapeDtypeStruct((BATCH, TT, A_Q_W), BF16),
        grid=(BATCH, N_TILES),
        in_specs=[
            pl.BlockSpec(memory_space=pltpu.SMEM),
            pl.BlockSpec((None, TM, A_Q_W), lambda b, t: (b, t, 0)),
            pl.BlockSpec((None, TT, 2 * A_KV_W), lambda b, t: (b, 0, 0)),
            pl.BlockSpec((None, TT, 2 * A_KV_W), lambda b, t: (b, 0, 0)),
        ],
        out_specs=pl.BlockSpec((None, TM, A_Q_W), lambda b, t: (b, t, 0)),
        compiler_params=_params(),
        name="window_attention",
    )(sink, qa, k2a, v2a)


DIFF_ROWS = 128


def _diff_attn_kernel(lpar_ref, subln_ref, q_ref, k_ref, v_ref, o_ref, *, lambda_init):
    t = pl.program_id(1)
    lp = lpar_ref[...]
    lam = (jnp.exp(jnp.sum(lp[0:1] * lp[1:2], axis=-1, keepdims=True))
           - jnp.exp(jnp.sum(lp[2:3] * lp[3:4], axis=-1, keepdims=True)) + lambda_init)
    R = DIFF_ROWS
    lane = lax.broadcasted_iota(jnp.int32, (R, LANES), 1)
    lo = lane < HEAD_DIM
    zero = jnp.zeros((R, LANES), BF16)

    def attend(n_keys):
        blocks = [(slice(rb * R, (rb + 1) * R), slice(h * LANES, (h + 1) * LANES))
                  for h in range(B_HEADS) for rb in range(TM // R)]

        def scores(rows, cols):
            q = q_ref[rows, cols]
            qs = jnp.concatenate([jnp.where(lo, q, zero), jnp.where(lo, zero, q)], axis=0)
            return _dot_nt(qs, k_ref[0:n_keys, cols])

        def finish(rows, cols, s):
            p = jnp.exp2(s - jnp.max(s, axis=-1, keepdims=True)).astype(BF16)
            vcols = slice(2 * cols.start, 2 * cols.stop)
            pv = _dot(p, v_ref[0:n_keys, vcols])
            sm = pv[:, 0:LANES] / pv[:, LANES:2 * LANES]
            y = sm[0:R] - lam * sm[R:2 * R]
            o_ref[rows, cols] = (_rms(y) * subln_ref[...] * (1.0 - lambda_init)).astype(BF16)

        s_next = scores(*blocks[0])
        for i, blk in enumerate(blocks):
            s_cur = s_next
            if i + 1 < len(blocks):
                s_next = scores(*blocks[i + 1])
            finish(*blk, s_cur)

    @pl.when(t == 0)
    def _():
        attend(CTX_LEN)

    @pl.when(t > 0)
    def _():
        attend(TT)


def _diff_attn(lpar, subln, qb, kb, vb, lambda_init):
    return pl.pallas_call(
        functools.partial(_diff_attn_kernel, lambda_init=lambda_init),
        out_shape=jax.ShapeDtypeStruct((BATCH, TT, B_V_W), BF16),
        grid=(BATCH, N_TILES),
        in_specs=[
            pl.BlockSpec((4, HEAD_DIM), lambda b, t: (0, 0)),
            pl.BlockSpec((1, LANES), lambda b, t: (0, 0)),
            pl.BlockSpec((None, TM, B_QK_W), lambda b, t: (b, t, 0)),
            pl.BlockSpec((None, TT, B_QK_W), lambda b, t: (b, 0, 0)),
            pl.BlockSpec((None, TT, 2 * B_V_W), lambda b, t: (b, 0, 0)),
        ],
        out_specs=pl.BlockSpec((None, TM, B_V_W), lambda b, t: (b, t, 0)),
        compiler_params=_params(),
        name="diff_attention",
    )(lpar, subln, qb, kb, vb)


FF_CHUNK = 1024


def _mix_mlp_kernel(hc_ref, hl_ref, ua_ref, ub_ref, mod_ref, g_ref, wo_ref, w1_ref, w2_ref, o_ref, *, h_off):
    mod = mod_ref[...]
    half = D_MODEL // 2
    y = _dot(ua_ref[...], wo_ref[0:half, :]) + _dot(ub_ref[...], wo_ref[half:D_MODEL, :])
    h1 = _stream_tile(hc_ref, hl_ref, h_off) + mod[2:3] * y
    f = _modnorm(h1, g_ref[...], mod[3:4], mod[4:5]).astype(BF16)
    acc = jnp.zeros((TM, D_MODEL), F32)
    for c in range(D_FF // FF_CHUNK):
        hid = jnp.maximum(_dot(f, w1_ref[:, c * FF_CHUNK:(c + 1) * FF_CHUNK]), 0.0)
        acc = acc + _dot((hid * hid).astype(BF16), w2_ref[c * FF_CHUNK:(c + 1) * FF_CHUNK, :])
    o_ref[...] = h1 + mod[5:6] * acc


def _mix_mlp(stream, ua, ub, mods, g, wo, w1, w2, *, latent_only):
    n_tiles = SEQ // TM if latent_only else N_TILES
    h_off = N_TILES - n_tiles
    half = D_MODEL // 2
    ub_col = 1 if ub.shape[-1] == D_MODEL else 0

    def full(shape):
        return pl.BlockSpec(shape, lambda b, t: (0,) * len(shape), pipeline_mode=pl.Buffered(1))

    return pl.pallas_call(
        functools.partial(_mix_mlp_kernel, h_off=h_off),
        out_shape=jax.ShapeDtypeStruct((BATCH, n_tiles * TM, D_MODEL), F32),
        grid=(BATCH, n_tiles),
        in_specs=[
            *_stream_specs(stream, h_off),
            pl.BlockSpec((None, TM, half), lambda b, t: (b, t, 0)),
            pl.BlockSpec((None, TM, half), lambda b, t: (b, t, ub_col)),
            pl.BlockSpec((None, None, 6, D_MODEL), lambda b, t: (jnp.minimum(t + h_off, 1), b, 0, 0)),
            pl.BlockSpec((1, D_MODEL), lambda b, t: (0, 0)),
            full((D_MODEL, D_MODEL)), full((D_MODEL, D_FF)), full((D_FF, D_MODEL)),
        ],
        out_specs=pl.BlockSpec((None, TM, D_MODEL), lambda b, t: (b, t, 0)),
        compiler_params=_params(),
        name="mixer_out_mlp",
    )(*stream, ua, ub, mods, g, wo, w1, w2)


S5_PAIR = 2
S5_TOK = S5_PAIR * S5_CHUNK
S5_STEPS = N_CHUNKS // S5_PAIR
GROUPS_PER_VREG = LANES // S5_GROUP
STEPS_PER_VREG = LANES // S5_GROUP
S5_SEG = 2 * LANES


def _lane_block():
    return lax.broadcasted_iota(jnp.int32, (BATCH, LANES), 1) // S5_GROUP


def _s5_in_kernel(h_ref, mod_ref, g_ref, perm_ref, w_ref, u_ref, z_ref, u_scr):
    mod = mod_ref[...]
    x = h_ref[...]
    a = _rms(x) * g_ref[...] * (1.0 + mod[:, 1:2, :]) + mod[:, 0:1, :]
    a = a.reshape(BATCH * S5_TOK, D_MODEL).astype(BF16)
    a = _dot(perm_ref[...], a).astype(BF16)
    lane_blk = _lane_block()

    def regroup(k):
        for r in range(GROUPS_PER_VREG):
            for hh in range(S5_CHUNK // STEPS_PER_VREG):
                halves = []
                for c2 in range(S5_PAIR):
                    acc = None
                    for m in range(STEPS_PER_VREG):
                        tok = c2 * S5_CHUNK + hh * STEPS_PER_VREG + m
                        piece = u_scr[k, tok * BATCH:(tok + 1) * BATCH, :]
                        sh = (S5_GROUP * (m - r)) % LANES
                        if sh:
                            piece = pltpu.roll(piece, sh, 1)
                        acc = piece if acc is None else jnp.where(lane_blk == m, piece, acc)
                    halves.append(acc)
                z_ref[k * GROUPS_PER_VREG + r, :, hh * LANES:(hh + 1) * LANES] = (
                    jnp.concatenate(halves, axis=0).astype(BF16))

    def finish(s, u_seg):
        u_ref[:, :, s * S5_SEG:(s + 1) * S5_SEG] = u_seg.reshape(S5_TOK, BATCH, S5_SEG)
        for kk in range(S5_SEG // LANES):
            k = s * (S5_SEG // LANES) + kk
            u_scr[k] = u_seg[:, kk * LANES:(kk + 1) * LANES]
            regroup(k)

    pending = None
    for s in range(D_MODEL // S5_SEG):
        u_seg = _dot(a, w_ref[:, s * S5_SEG:(s + 1) * S5_SEG])
        if pending is not None:
            finish(*pending)
        pending = (s, u_seg)
    finish(*pending)


def _s5_in(h, mods, g, perm, w_in):
    ctx_steps = N_CTX_CHUNKS // S5_PAIR
    n_rows = BATCH * S5_TOK
    return pl.pallas_call(
        _s5_in_kernel,
        out_shape=[jax.ShapeDtypeStruct((TT, BATCH, D_MODEL), F32),
                   jax.ShapeDtypeStruct((S5_GROUPS, ROWS, S5_CW), BF16)],
        grid=(S5_STEPS,),
        in_specs=[pl.BlockSpec((BATCH, S5_TOK, D_MODEL), lambda p: (0, p, 0)),
                  pl.BlockSpec((None, BATCH, 6, D_MODEL), lambda p: (jnp.minimum(p // ctx_steps, 1), 0, 0, 0)),
                  pl.BlockSpec((1, D_MODEL), lambda p: (0, 0)),
                  pl.BlockSpec((n_rows, n_rows), lambda p: (0, 0)),
                  pl.BlockSpec((D_MODEL, D_MODEL), lambda p: (0, 0))],
        out_specs=[pl.BlockSpec((S5_TOK, BATCH, D_MODEL), lambda p: (p, 0, 0)),
                   pl.BlockSpec((S5_GROUPS, S5_PAIR * BATCH, S5_CW), lambda p: (0, p, 0))],
        scratch_shapes=[pltpu.VMEM((D_MODEL // LANES, n_rows, LANES), F32)],
        compiler_params=_params(),
        name="s5_in_proj",
    )(h, mods, g, perm, w_in)


def _s5_ops_kernel(lr_c_ref, li_c_ref, ls_c_ref, lr_r_ref, li_r_ref, ls_r_ref,
                   btr_ref, bti_ref, btile_r_ref, btile_i_ref, ctile_r_ref, ctile_i_ref,
                   m_ref, q_ref, n_ref, a_ref):
    P2 = 2 * S5_STATE

    def discretize(lr, li, ls):
        dt = jnp.exp(ls)
        mag = jnp.exp(lr * dt)
        ab_re = mag * jnp.cos(li * dt)
        ab_im = mag * jnp.sin(li * dt)
        den = lr * lr + li * li
        nr = ab_re - 1.0
        f_re = (nr * lr + ab_im * li) / den
        f_im = (ab_im * lr - nr * li) / den
        return ab_re, ab_im, f_re, f_im

    def cmul(xr, xi, yr, yi):
        return xr * yr - xi * yi, xr * yi + xi * yr

    ar_c, ai_c, f_re_c, f_im_c = discretize(lr_c_ref[...], li_c_ref[...], ls_c_ref[...])
    lag = lax.broadcasted_iota(jnp.int32, (P2, S5_CW), 1) // S5_GROUP
    is_fwd = lax.broadcasted_iota(jnp.int32, (P2, S5_CW), 0) < S5_STATE
    e_k = jnp.where(is_fwd, lag, (S5_CHUNK - 1) - lag)
    squares = [(ar_c, ai_c)]
    while 2 ** len(squares) < S5_CHUNK:
        squares.append(cmul(*squares[-1], *squares[-1]))

    def a_pow(e):
        pr = jnp.ones((P2, S5_CW), F32)
        pi = jnp.zeros((P2, S5_CW), F32)
        for b, (br, bi) in enumerate(squares):
            bit = ((e >> b) & 1) == 1
            tr, ti = cmul(pr, pi, br, bi)
            pr = jnp.where(bit, tr, pr)
            pi = jnp.where(bit, ti, pi)
        return pr, pi

    c_re, c_im = ctile_r_ref[...], ctile_i_ref[...]
    pk_re, pk_im = a_pow(e_k)
    cp_re, cp_im = cmul(c_re, c_im, pk_re, pk_im)

    ar_r, ai_r, f_re_r, f_im_r = discretize(lr_r_ref[...], li_r_ref[...], ls_r_ref[...])
    bt_re = f_re_r * btr_ref[...] - f_im_r * bti_ref[...]
    bt_im = f_re_r * bti_ref[...] + f_im_r * btr_ref[...]
    lane_fwd = lax.broadcasted_iota(jnp.int32, (S5_GROUP, P2), 1) < S5_STATE

    def lag_kernels(keep):
        br = jnp.where(keep, bt_re, 0.0)
        bi = jnp.where(keep, bt_im, 0.0)
        hi = lax.Precision.HIGHEST
        return (jnp.dot(br, cp_re, precision=hi, preferred_element_type=F32)
                - jnp.dot(bi, cp_im, precision=hi, preferred_element_type=F32))

    kt_f = lag_kernels(lane_fwd)
    kt_b = lag_kernels(jnp.logical_not(lane_fwd))
    lane_w = lax.broadcasted_iota(jnp.int32, (S5_GROUP, S5_CW), 1)
    for s in range(S5_CHUNK):
        f_part = kt_f if s == 0 else jnp.where(lane_w >= S5_GROUP * s, pltpu.roll(kt_f, S5_GROUP * s, 1), 0.0)
        sh = (S5_GROUP * (s + 1)) % S5_CW
        b_roll = kt_b if sh == 0 else pltpu.roll(kt_b, sh, 1)
        b_part = jnp.where(lane_w < S5_GROUP * (s + 1), b_roll, 0.0)
        m_ref[s * S5_GROUP:(s + 1) * S5_GROUP, :] = (f_part + b_part).astype(BF16)

    bb_re = f_re_c * btile_r_ref[...] - f_im_c * btile_i_ref[...]
    bb_im = f_re_c * btile_i_ref[...] + f_im_c * btile_r_ref[...]
    q_re, q_im = cmul(*a_pow((S5_CHUNK - 1) - e_k), bb_re, bb_im)
    q_ref[0:P2, :] = q_re.astype(BF16)
    q_ref[P2:2 * P2, :] = q_im.astype(BF16)

    n_re, n_im = cmul(cp_re, cp_im, ar_c, ai_c)
    n_ref[0:P2, :] = n_re.astype(BF16)
    n_ref[P2:2 * P2, :] = (-n_im).astype(BF16)

    a_re, a_im = ar_r, ai_r
    for _ in range(S5_CHUNK.bit_length() - 1):
        a_re, a_im = cmul(a_re, a_im, a_re, a_im)
    a_ref[0:1, :] = a_re
    a_ref[1:2, :] = a_im


def _s5_ops(col_params, row_params, bt, btile, ctile):
    P2 = 2 * S5_STATE
    col = pl.BlockSpec((None, P2, 1), lambda g: (g, 0, 0))
    row = pl.BlockSpec((None, 1, P2), lambda g: (g, 0, 0))
    btspec = pl.BlockSpec((None, S5_GROUP, P2), lambda g: (g, 0, 0))
    tile = pl.BlockSpec((None, P2, S5_CW), lambda g: (g, 0, 0))
    sq = pl.BlockSpec((None, S5_CW, S5_CW), lambda g: (g, 0, 0))
    return pl.pallas_call(
        _s5_ops_kernel,
        out_shape=[jax.ShapeDtypeStruct((S5_GROUPS, S5_CW, S5_CW), BF16)] * 3
        + [jax.ShapeDtypeStruct((S5_GROUPS, 2, P2), F32)],
        grid=(S5_GROUPS,),
        in_specs=[col, col, col, row, row, row, btspec, btspec, tile, tile, tile, tile],
        out_specs=[sq, sq, sq, pl.BlockSpec((None, 2, P2), lambda g: (g, 0, 0))],
        compiler_params=_params(),
        name="s5_chunk_operators",
    )(*col_params, *row_params, *bt, *btile, *ctile)


ROWS = N_CHUNKS * BATCH


SCAN_G = 4


def _s5_scan_kernel(u_ref, m_ref, q_ref, n_ref, a_ref, y_ref, pu_ref, sp_ref):
    S = S5_STATE
    lane = lax.broadcasted_iota(jnp.int32, (BATCH, 2 * S), 1)
    lo = lane < S
    for j in range(SCAN_G):
        pu_ref[j] = _dot_nt(u_ref[j], q_ref[j])
    for j in range(SCAN_G):
        y_ref[j] = _dot(u_ref[j], m_ref[j])
    decay = [(a_ref[j, 0:1, :], a_ref[j, 1:2, :]) for j in range(SCAN_G)]
    zero = jnp.zeros((BATCH, 2 * S), F32)
    state = [(zero, zero)] * SCAN_G
    for k in range(N_CHUNKS):
        cf = k * BATCH
        cb = (N_CTX_CHUNKS - 1 - k if k < N_CTX_CHUNKS else N_CHUNKS + N_CTX_CHUNKS - 1 - k) * BATCH
        for j in range(SCAN_G):
            s_re, s_im = state[j]
            a_re, a_im = decay[j]
            sp_ref[j, cf:cf + BATCH, 0:S] = s_re[:, 0:S]
            sp_ref[j, cb:cb + BATCH, S:2 * S] = s_re[:, S:2 * S]
            sp_ref[j, cf:cf + BATCH, 2 * S:3 * S] = s_im[:, 0:S]
            sp_ref[j, cb:cb + BATCH, 3 * S:4 * S] = s_im[:, S:2 * S]
            x_re = jnp.where(lo, pu_ref[j, cf:cf + BATCH, 0:2 * S], pu_ref[j, cb:cb + BATCH, 0:2 * S])
            x_im = jnp.where(lo, pu_ref[j, cf:cf + BATCH, 2 * S:4 * S], pu_ref[j, cb:cb + BATCH, 2 * S:4 * S])
            state[j] = (a_re * s_re - a_im * s_im + x_re, a_re * s_im + a_im * s_re + x_im)
    for j in range(SCAN_G):
        y_ref[j] += _dot(sp_ref[j].astype(BF16), n_ref[j])


def _s5_scan(u_g, m_op, q_op, n_op, a_vec):
    sq = pl.BlockSpec((SCAN_G, S5_CW, S5_CW), lambda g: (g, 0, 0))
    rows = pl.BlockSpec((SCAN_G, ROWS, S5_CW), lambda g: (g, 0, 0))
    return pl.pallas_call(
        _s5_scan_kernel,
        out_shape=jax.ShapeDtypeStruct((S5_GROUPS, ROWS, S5_CW), F32),
        grid=(S5_GROUPS // SCAN_G,),
        in_specs=[rows, sq, sq, sq, pl.BlockSpec((SCAN_G, 2, 2 * S5_STATE), lambda g: (g, 0, 0))],
        out_specs=rows,
        scratch_shapes=[pltpu.VMEM((SCAN_G, ROWS, S5_CW), F32), pltpu.VMEM((SCAN_G, ROWS, S5_CW), F32)],
        compiler_params=_params(),
        name="s5_scan",
    )(u_g, m_op, q_op, n_op, a_vec)


def _s5_out_kernel(u_ref, y_ref, d_ref, gw_ref, gb_ref, perm_ref, o_ref, ys_scr):
    lane_blk = _lane_block()

    def regroup(k):
        for c2 in range(S5_PAIR):
            for t in range(S5_CHUNK):
                hh, m = divmod(t, STEPS_PER_VREG)
                acc = None
                for r in range(GROUPS_PER_VREG):
                    piece = y_ref[k * GROUPS_PER_VREG + r, c2 * BATCH:(c2 + 1) * BATCH, hh * LANES:(hh + 1) * LANES]
                    sh = (S5_GROUP * (r - m)) % LANES
                    if sh:
                        piece = pltpu.roll(piece, sh, 1)
                    acc = piece if acc is None else jnp.where(lane_blk == r, piece, acc)
                tok = c2 * S5_CHUNK + t
                ys_scr[k, tok * BATCH:(tok + 1) * BATCH, :] = acc

    n_seg = D_MODEL // S5_SEG
    per_seg = S5_SEG // LANES
    for kk in range(per_seg):
        regroup(kk)
    g_parts = []
    pre = None
    for s in range(n_seg):
        if s + 1 < n_seg:
            for kk in range(per_seg):
                regroup((s + 1) * per_seg + kk)
        cols = slice(s * S5_SEG, (s + 1) * S5_SEG)
        ys = jnp.concatenate([ys_scr[s * per_seg + kk] for kk in range(per_seg)], axis=1)
        y = u_ref[:, :, cols].reshape(BATCH * S5_TOK, S5_SEG) * d_ref[:, cols] + ys
        g_seg = jax.nn.gelu(y)
        g_parts.append(g_seg)
        part = _dot(g_seg.astype(BF16), gw_ref[cols, :])
        pre = part if pre is None else pre + part
    gate = jax.nn.sigmoid(pre + gb_ref[...])
    g = jnp.concatenate(g_parts, axis=1)
    gated = (g * gate).astype(BF16)
    gated = _dot(perm_ref[...], gated).astype(BF16)
    o_ref[...] = gated.reshape(BATCH, S5_TOK, D_MODEL)


def _s5_out(u, y_g, d_skip, glu_w, glu_b, perm_t):
    ctx_steps = N_CTX_CHUNKS // S5_PAIR
    n_rows = BATCH * S5_TOK
    vec = pl.BlockSpec((1, D_MODEL), lambda p: (0, 0))
    return pl.pallas_call(
        _s5_out_kernel,
        out_shape=jax.ShapeDtypeStruct((BATCH, SEQ, D_MODEL), BF16),
        grid=(S5_STEPS - ctx_steps,),
        in_specs=[pl.BlockSpec((S5_TOK, BATCH, D_MODEL), lambda p: (p + ctx_steps, 0, 0)),
                  pl.BlockSpec((S5_GROUPS, S5_PAIR * BATCH, S5_CW), lambda p: (0, p + ctx_steps, 0)),
                  vec, pl.BlockSpec((D_MODEL, D_MODEL), lambda p: (0, 0)), vec,
                  pl.BlockSpec((n_rows, n_rows), lambda p: (0, 0))],
        out_specs=pl.BlockSpec((BATCH, S5_TOK, D_MODEL), lambda p: (0, p, 0)),
        scratch_shapes=[pltpu.VMEM((D_MODEL // LANES, n_rows, LANES), F32)],
        compiler_params=_params(),
        name="s5_gelu_glu",
    )(u, y_g, d_skip, glu_w, glu_b, perm_t)


def _rope_tables():
    rows_n = SEQ // GRID_W
    row = jnp.repeat(jnp.arange(rows_n, dtype=F32), GRID_W)
    col = jnp.tile(jnp.arange(GRID_W, dtype=F32), rows_n)
    n_freq = HEAD_DIM // 4
    inv = ROPE_BASE ** (-jnp.arange(n_freq, dtype=F32) / n_freq)
    ang = jnp.concatenate([row[:, None] * inv, col[:, None] * inv], axis=-1)
    reps = LANES // (HEAD_DIM // 2)
    cos_t = jnp.tile(jnp.cos(ang), (1, reps))
    sin_t = jnp.tile(jnp.sin(ang), (1, reps))
    sign = jnp.where((jnp.arange(LANES) % HEAD_DIM) < HEAD_DIM // 2, -1.0, 1.0).astype(F32)
    cos_t = jnp.concatenate([jnp.ones((CTX_LEN, LANES), F32), cos_t], axis=0)
    sin_s = jnp.concatenate([jnp.zeros((CTX_LEN, LANES), F32), sin_t * sign], axis=0)
    return cos_t, sin_s


def _fb_cols(x):
    return jnp.transpose(x, (1, 0, 2)).reshape(S5_GROUPS, 2 * S5_STATE, 1)


def _fb_rows(x):
    return jnp.transpose(x, (1, 0, 2)).reshape(S5_GROUPS, 1, 2 * S5_STATE)


def _s5_layout(lam_re, lam_im, log_step, b_re, b_im, c_re, c_im):
    ls = jnp.broadcast_to(log_step[:, :, None], lam_re.shape)
    cols = [_fb_cols(v) for v in (lam_re, lam_im, ls)]
    rows = [_fb_rows(v) for v in (lam_re, lam_im, ls)]

    def bt_of(b):
        return jnp.transpose(b, (1, 3, 0, 2)).reshape(S5_GROUPS, S5_GROUP, 2 * S5_STATE)

    def btile_of(b):
        t = jnp.transpose(b, (1, 0, 2, 3)).reshape(S5_GROUPS, 2 * S5_STATE, S5_GROUP)
        return jnp.tile(t, (1, 1, S5_CHUNK))

    def ctile_of(c):
        t = jnp.transpose(c, (1, 0, 3, 2)).reshape(S5_GROUPS, 2 * S5_STATE, S5_GROUP)
        return jnp.tile(t, (1, 1, S5_CHUNK))

    return cols, rows, [bt_of(b_re), bt_of(b_im)], [btile_of(b_re), btile_of(b_im)], [ctile_of(c_re), ctile_of(c_im)]


def kernel(x, c, ctx, c_ctx, norm1_g, norm2_g, mod_w, mod_b, mlp_w1, mlp_w2, attn_w_in, attn_w_out, a_q_norm, a_k_norm, a_sink, b_q_norm, b_k_norm, b_lq1, b_lk1, b_lq2, b_lk2, b_subln, s5_w_in, s5_lambda_re, s5_lambda_im, s5_log_step, s5_b_re, s5_b_im, s5_c_re, s5_c_im, s5_d, s5_glu_w, s5_glu_b, s5_w_out):
    assert x.shape == (BATCH, SEQ, D_MODEL) and ctx.shape == (BATCH, CTX_LEN, D_MODEL)
    stream = (ctx, x)
    s_rows = jnp.concatenate([c, c_ctx[None], jnp.zeros((16 - BATCH - 1, D_MODEL), F32)], axis=0)
    m_all = _modulation(s_rows, mod_w, mod_b)
    cos_t, sin_s = _rope_tables()
    e_blk = (jnp.kron(jnp.eye(LANES // HEAD_DIM, dtype=F32), jnp.ones((HEAD_DIM, HEAD_DIM), F32))
             / HEAD_DIM).astype(BF16)

    for i in range(DEPTH):
        last = i == DEPTH - 1
        j = i // 2
        m_lat = m_all[i, :BATCH].reshape(BATCH, 6, D_MODEL)
        m_ctx = jnp.broadcast_to(m_all[i, BATCH].reshape(1, 6, D_MODEL), (BATCH, 6, D_MODEL))
        mods = jnp.stack([m_ctx, m_lat])
        g1 = norm1_g[i].reshape(1, D_MODEL)
        g2 = norm2_g[i].reshape(1, D_MODEL)
        if i % 2 == 0:
            lambda_init = 0.8 - 0.6 * math.exp(-0.3 * i)
            gains = jnp.stack([jnp.tile(v[j], LANES // HEAD_DIM) for v in (a_q_norm, a_k_norm, b_q_norm, b_k_norm)])
            qa, k2a, v2a, qb, kb, vb = _attn_in(stream, mods, g1, attn_w_in[j].astype(BF16), gains, cos_t, sin_s, e_blk)
            ya = _win_attn(a_sink[j], qa, k2a, v2a)
            lpar = jnp.stack([b_lq1[j], b_lk1[j], b_lq2[j], b_lk2[j]])
            yb = _diff_attn(lpar, b_subln[j].reshape(1, LANES), qb, kb, vb, lambda_init)
            if last:
                ya, yb = ya[:, CTX_LEN:], yb[:, CTX_LEN:]
            ua, ub, wo = ya, yb, attn_w_out[j]
        else:
            src = np.arange(BATCH * S5_TOK).reshape(BATCH, S5_TOK).T.reshape(-1)
            perm = jnp.asarray(np.eye(BATCH * S5_TOK, dtype=np.float32)[src], BF16)
            h_all = stream[0] if stream[0].shape[1] == TT else jnp.concatenate(stream, axis=1)
            u, u_g = _s5_in(h_all, mods, g1, perm, s5_w_in[j].astype(BF16))
            ops_in = _s5_layout(s5_lambda_re[j], s5_lambda_im[j], s5_log_step[j], s5_b_re[j], s5_b_im[j],
                                s5_c_re[j], s5_c_im[j])
            m_op, q_op, n_op, a_vec = _s5_ops(*ops_in)
            y_g = _s5_scan(u_g, m_op, q_op, n_op, a_vec)
            assert last, "S5 layers before the last one would also need the context rows of the readout"
            gated = _s5_out(u, y_g, s5_d[j].reshape(1, D_MODEL), s5_glu_w[j].astype(BF16),
                            s5_glu_b[j].reshape(1, D_MODEL), perm.T)
            ua, ub, wo = gated, gated, s5_w_out[j]
        h = _mix_mlp(stream, ua, ub, mods, g2, wo.astype(BF16), mlp_w1[i].astype(BF16), mlp_w2[i].astype(BF16),
                     latent_only=last)
        stream = (h, h)
    return h
```

```python
import functools
import math

import jax
import jax.numpy as jnp
import numpy as np
from jax import lax
from jax.experimental import pallas as pl
from jax.experimental.pallas import tpu as pltpu

F32 = jnp.float32
BF16 = jnp.bfloat16

D_MODEL = 1024
BATCH = 8
SEQ = 2048
DEPTH = 2
GRID_W = 64
CTX_LEN = 256
HEAD_DIM = 64
WINDOW = 128
A_Q_HEADS = 8
A_KV_HEADS = 2
B_HEADS = 4
A_Q_W = A_Q_HEADS * HEAD_DIM
A_KV_W = A_KV_HEADS * HEAD_DIM
B_QK_W = B_HEADS * 2 * HEAD_DIM
B_V_W = B_HEADS * 2 * HEAD_DIM
ATTN_IN = A_Q_W + 2 * A_KV_W + 2 * B_QK_W + B_V_W
S5_GROUP = 16
S5_GROUPS = D_MODEL // S5_GROUP
S5_STATE = 64
D_FF = 4 * D_MODEL
ROPE_BASE = 10000.0
EPS = 1e-6
NEG_INF = -1e30
LOG2E = math.log2(math.e)

TT = CTX_LEN + SEQ
TM = 256
N_TILES = TT // TM
LANES = 128
S5_CHUNK = 16
S5_CW = S5_CHUNK * S5_GROUP
N_CHUNKS = TT // S5_CHUNK
N_CTX_CHUNKS = CTX_LEN // S5_CHUNK
VMEM_LIMIT = 56 * 1024 * 1024


def _dot(a, b):
    return jnp.dot(a, b, preferred_element_type=F32)


def _dot_nt(a, b):
    return lax.dot_general(a, b, (((1,), (1,)), ((), ())), preferred_element_type=F32)


def _rms(x):
    return x * lax.rsqrt(jnp.mean(x * x, axis=-1, keepdims=True) + EPS)


def _modnorm(x, g, shift, scale):
    return _rms(x) * g * (1.0 + scale) + shift


def _params(**kw):
    return pltpu.CompilerParams(vmem_limit_bytes=VMEM_LIMIT, **kw)


def _mod_kernel(s_ref, w_ref, b_ref, o_ref):
    s = s_ref[...]
    s = s * jax.nn.sigmoid(s)
    o_ref[...] = _dot(s.astype(BF16), w_ref[...].astype(BF16)) + b_ref[...]


def _modulation(s_rows, mod_w, mod_b):
    return pl.pallas_call(
        _mod_kernel,
        out_shape=jax.ShapeDtypeStruct((DEPTH, 16, 6 * D_MODEL), F32),
        grid=(DEPTH, 6),
        in_specs=[
            pl.BlockSpec((16, D_MODEL), lambda i, j: (0, 0)),
            pl.BlockSpec((None, D_MODEL, D_MODEL), lambda i, j: (i, 0, j)),
            pl.BlockSpec((None, 1, D_MODEL), lambda i, j: (i, 0, j)),
        ],
        out_specs=pl.BlockSpec((None, 16, D_MODEL), lambda i, j: (i, 0, j)),
        compiler_params=_params(),
        name="modulation",
    )(s_rows, mod_w, mod_b.reshape(DEPTH, 1, 6 * D_MODEL))


def _mod_spec():
    return pl.BlockSpec((None, None, 6, D_MODEL), lambda b, t: (jnp.minimum(t, 1), b, 0, 0))


def _stream_specs(stream, h_off):
    unified = stream[0].shape[1] == TT
    first_lat = 1 if unified else 0
    ctx_spec = pl.BlockSpec((None, TM, D_MODEL), lambda b, t: (b, 0, 0))
    lat_spec = pl.BlockSpec((None, TM, D_MODEL),
                            lambda b, t: (b, jnp.maximum(t + h_off - 1 + first_lat, first_lat), 0))
    return [ctx_spec, lat_spec]


def _stream_tile(hc_ref, hl_ref, h_off):
    if h_off > 0:
        return hl_ref[...]
    return jnp.where(pl.program_id(1) == 0, hc_ref[...], hl_ref[...])


def _attn_in_kernel(hc_ref, hl_ref, mod_ref, g_ref, w_ref, gain_ref, cos_ref, sin_ref, e_ref,
                    qa_ref, k2a_ref, v2a_ref, qb_ref, kb_ref, vb_ref):
    mod = mod_ref[...]
    a = _modnorm(_stream_tile(hc_ref, hl_ref, 0), g_ref[...], mod[0:1], mod[1:2]).astype(BF16)
    cos_t = cos_ref[...]
    sin_s = sin_ref[...]
    e = e_ref[...]
    gains = gain_ref[...]
    lane = lax.broadcasted_iota(jnp.int32, (TM, LANES), 1)
    first_half = (lane & (HEAD_DIM - 1)) < HEAD_DIM // 2
    lo = lane < HEAD_DIM

    def norm_rope(c, gain):
        ms = _dot((c * c).astype(BF16), e)
        cn = c * lax.rsqrt(ms + EPS) * gain
        r_fwd = pltpu.roll(cn, HEAD_DIM // 2, 1)
        r_bwd = pltpu.roll(cn, LANES - HEAD_DIM // 2, 1)
        return cn * cos_t + jnp.where(first_half, r_bwd, r_fwd) * sin_s

    def dup_halves(x, ref):
        sw = pltpu.roll(x, HEAD_DIM, 1)
        ref[:, 0:LANES] = jnp.where(lo, x, sw).astype(BF16)
        ref[:, LANES:2 * LANES] = jnp.where(lo, sw, x).astype(BF16)


    q_scale = HEAD_DIM ** -0.5 * LOG2E

    def finish_q(z, ref, gain):
        for c in range(z.shape[1] // LANES):
            ref[:, c * LANES:(c + 1) * LANES] = (
                norm_rope(z[:, c * LANES:(c + 1) * LANES], gain) * q_scale).astype(BF16)

    def finish_kv_a(z):
        dup_halves(norm_rope(z[:, 0:LANES], gains[1:2]), k2a_ref)
        v = z[:, LANES:2 * LANES]
        sw = pltpu.roll(v, HEAD_DIM, 1)
        for kvh, dup in enumerate((jnp.where(lo, v, sw), jnp.where(lo, sw, v))):
            v2a_ref[:, 2 * kvh * LANES:(2 * kvh + 1) * LANES] = dup.astype(BF16)
            v2a_ref[:, (2 * kvh + 1) * LANES:(2 * kvh + 2) * LANES] = jnp.ones((TM, LANES), BF16)

    def finish_kb(z):
        for c in range(B_QK_W // LANES):
            kb_ref[:, c * LANES:(c + 1) * LANES] = norm_rope(z[:, c * LANES:(c + 1) * LANES], gains[3:4]).astype(BF16)

    def finish_vb(z):
        for hd in range(B_HEADS):
            vb_ref[:, 2 * hd * LANES:(2 * hd + 1) * LANES] = z[:, hd * LANES:(hd + 1) * LANES].astype(BF16)
            vb_ref[:, (2 * hd + 1) * LANES:(2 * hd + 2) * LANES] = jnp.ones((TM, LANES), BF16)

    segments = [(A_Q_W, lambda z: finish_q(z, qa_ref, gains[0:1])), (2 * A_KV_W, finish_kv_a),
                (B_QK_W, lambda z: finish_q(z, qb_ref, gains[2:3])), (B_QK_W, finish_kb), (B_V_W, finish_vb)]
    off = 0
    pending = None
    for width, finish in segments:
        z = _dot(a, w_ref[:, off:off + width])
        off += width
        if pending is not None:
            pending[1](pending[0])
        pending = (z, finish)
    pending[1](pending[0])


def _attn_in(stream, mods, g, w_in, gains, cos_t, sin_s, e_blk):
    def tok(width):
        return pl.BlockSpec((None, TM, width), lambda b, t: (b, t, 0))

    def full(shape):
        return pl.BlockSpec(shape, lambda b, t: (0,) * len(shape))

    out_shapes = [jax.ShapeDtypeStruct((BATCH, TT, w), BF16)
                  for w in (A_Q_W, 2 * A_KV_W, 4 * A_KV_W, B_QK_W, B_QK_W, 2 * B_V_W)]
    return pl.pallas_call(
        _attn_in_kernel,
        out_shape=out_shapes,
        grid=(BATCH, N_TILES),
        in_specs=[
            *_stream_specs(stream, 0), _mod_spec(), full((1, D_MODEL)), full((D_MODEL, ATTN_IN)), full((4, LANES)),
            pl.BlockSpec((TM, LANES), lambda b, t: (t, 0)), pl.BlockSpec((TM, LANES), lambda b, t: (t, 0)),
            full((LANES, LANES)),
        ],
        out_specs=[tok(A_Q_W), tok(2 * A_KV_W), tok(4 * A_KV_W), tok(B_QK_W), tok(B_QK_W), tok(2 * B_V_W)],
        compiler_params=_params(),
        name="attn_in_proj",
    )(*stream, mods, g, w_in, gains, cos_t, sin_s, e_blk)


QB = 128


def _win_attn_kernel(sink_ref, q_ref, k2_ref, v2_ref, o_ref):
    t = pl.program_id(1)
    lane = lax.broadcasted_iota(jnp.int32, (QB, LANES), 1)
    lo = lane < HEAD_DIM
    rows = 4 * QB
    row = lax.broadcasted_iota(jnp.int32, (rows, 3 * QB), 0)
    col = lax.broadcasted_iota(jnp.int32, (rows, 3 * QB), 1)
    row_head = lax.broadcasted_iota(jnp.int32, (rows, 1), 0) // QB
    zero = jnp.zeros((QB, LANES), BF16)
    blocks = [(qb, g) for qb in range(TM // QB) for g in range(A_KV_HEADS)]

    def window_start(qb):
        n = (t - 1) * (TM // QB) + qb
        ws = jnp.clip((n - 1) * QB, 0, SEQ - 3 * QB)
        return n, ws

    def scores(qb, g, with_window):
        pieces = []
        for p in range(2):
            qp = q_ref[qb * QB:(qb + 1) * QB, g * 2 * LANES + p * LANES: g * 2 * LANES + (p + 1) * LANES]
            pieces.append(jnp.where(lo, qp, zero))
            pieces.append(jnp.where(lo, zero, qp))
        qs = jnp.concatenate(pieces, axis=0)
        s_c = _dot_nt(qs, k2_ref[0:CTX_LEN, g * LANES:(g + 1) * LANES])
        if not with_window:
            return s_c, None
        n, ws = window_start(qb)
        kw = k2_ref[pl.ds(pl.multiple_of(ws + CTX_LEN, QB), 3 * QB), g * LANES:(g + 1) * LANES]
        valid = jnp.abs(n * QB + (row & (QB - 1)) - (ws + col)) <= WINDOW
        return s_c, jnp.where(valid, _dot_nt(qs, kw), NEG_INF)

    def finish(qb, g, s_c, s_w):
        sk = jnp.full((rows, 1), sink_ref[4 * g + 3], F32)
        for hh in range(3):
            sk = jnp.where(row_head == hh, sink_ref[4 * g + hh], sk)
        sk = sk * LOG2E
        m = jnp.maximum(jnp.max(s_c, axis=-1, keepdims=True), sk)
        if s_w is not None:
            m = jnp.maximum(m, jnp.max(s_w, axis=-1, keepdims=True))
        vcols = slice(2 * g * LANES, (2 * g + 2) * LANES)
        pv = _dot(jnp.exp2(s_c - m).astype(BF16), v2_ref[0:CTX_LEN, vcols])
        if s_w is not None:
            _, ws = window_start(qb)
            vw = v2_ref[pl.ds(pl.multiple_of(ws + CTX_LEN, QB), 3 * QB), vcols]
            pv = pv + _dot(jnp.exp2(s_w - m).astype(BF16), vw)
        o = pv[:, 0:LANES] / (pv[:, LANES:2 * LANES] + jnp.exp2(sk - m))
        for p in range(2):
            o_ref[qb * QB:(qb + 1) * QB, g * 2 * LANES + p * LANES: g * 2 * LANES + (p + 1) * LANES] = jnp.where(
                lo, o[2 * p * QB:(2 * p + 1) * QB], o[(2 * p + 1) * QB:(2 * p + 2) * QB]).astype(BF16)

    def attend(with_window):
        s_next = scores(*blocks[0], with_window)
        for i, blk in enumerate(blocks):
            s_cur = s_next
            if i + 1 < len(blocks):
                s_next = scores(*blocks[i + 1], with_window)
            finish(*blk, *s_cur)

    @pl.when(t == 0)
    def _():
        attend(False)

    @pl.when(t > 0)
    def _():
        attend(True)


def _win_attn(sink, qa, k2a, v2a):
    return pl.pallas_call(
        _win_attn_kernel,
        out_shape=jax.ShapeDtypeStruct((BATCH, TT, A_Q_W), BF16),
        grid=(BATCH, N_TILES),
        in_specs=[
            pl.BlockSpec(memory_space=pltpu.SMEM),
            pl.BlockSpec((None, TM, A_Q_W), lambda b, t: (b, t, 0)),
            pl.BlockSpec((None, TT, 2 * A_KV_W), lambda b, t: (b, 0, 0)),
            pl.BlockSpec((None, TT, 4 * A_KV_W), lambda b, t: (b, 0, 0)),
        ],
        out_specs=pl.BlockSpec((None, TM, A_Q_W), lambda b, t: (b, t, 0)),
        compiler_params=_params(),
        name="window_attention",
    )(sink, qa, k2a, v2a)


DIFF_ROWS = 128


def _diff_attn_kernel(lpar_ref, subln_ref, q_ref, k_ref, v_ref, o_ref, *, lambda_init):
    t = pl.program_id(1)
    lp = lpar_ref[...]
    lam = (jnp.exp(jnp.sum(lp[0:1] * lp[1:2], axis=-1, keepdims=True))
           - jnp.exp(jnp.sum(lp[2:3] * lp[3:4], axis=-1, keepdims=True)) + lambda_init)
    R = DIFF_ROWS
    lane = lax.broadcasted_iota(jnp.int32, (R, LANES), 1)
    lo = lane < HEAD_DIM
    zero = jnp.zeros((R, LANES), BF16)

    def attend(n_keys):
        blocks = [(slice(rb * R, (rb + 1) * R), slice(h * LANES, (h + 1) * LANES))
                  for h in range(B_HEADS) for rb in range(TM // R)]

        def scores(rows, cols):
            q = q_ref[rows, cols]
            qs = jnp.concatenate([jnp.where(lo, q, zero), jnp.where(lo, zero, q)], axis=0)
            return _dot_nt(qs, k_ref[0:n_keys, cols])

        def finish(rows, cols, s):
            p = jnp.exp2(s - jnp.max(s, axis=-1, keepdims=True)).astype(BF16)
            vcols = slice(2 * cols.start, 2 * cols.stop)
            pv = _dot(p, v_ref[0:n_keys, vcols])
            sm = pv[:, 0:LANES] / pv[:, LANES:2 * LANES]
            y = sm[0:R] - lam * sm[R:2 * R]
            o_ref[rows, cols] = (_rms(y) * subln_ref[...] * (1.0 - lambda_init)).astype(BF16)

        s_next = scores(*blocks[0])
        for i, blk in enumerate(blocks):
            s_cur = s_next
            if i + 1 < len(blocks):
                s_next = scores(*blocks[i + 1])
            finish(*blk, s_cur)

    @pl.when(t == 0)
    def _():
        attend(CTX_LEN)

    @pl.when(t > 0)
    def _():
        attend(TT)


def _diff_attn(lpar, subln, qb, kb, vb, lambda_init):
    return pl.pallas_call(
        functools.partial(_diff_attn_kernel, lambda_init=lambda_init),
        out_shape=jax.ShapeDtypeStruct((BATCH, TT, B_V_W), BF16),
        grid=(BATCH, N_TILES),
        in_specs=[
            pl.BlockSpec((4, HEAD_DIM), lambda b, t: (0, 0)),
            pl.BlockSpec((1, LANES), lambda b, t: (0, 0)),
            pl.BlockSpec((None, TM, B_QK_W), lambda b, t: (b, t, 0)),
            pl.BlockSpec((None, TT, B_QK_W), lambda b, t: (b, 0, 0)),
            pl.BlockSpec((None, TT, 2 * B_V_W), lambda b, t: (b, 0, 0)),
        ],
        out_specs=pl.BlockSpec((None, TM, B_V_W), lambda b, t: (b, t, 0)),
        compiler_params=_params(),
        name="diff_attention",
    )(lpar, subln, qb, kb, vb)


FF_CHUNK = 1024


def _mix_mlp_kernel(hc_ref, hl_ref, ua_ref, ub_ref, mod_ref, g_ref, wo_ref, w1_ref, w2_ref, o_ref, *, h_off):
    mod = mod_ref[...]
    half = D_MODEL // 2
    y = _dot(ua_ref[...], wo_ref[0:half, :]) + _dot(ub_ref[...], wo_ref[half:D_MODEL, :])
    h1 = _stream_tile(hc_ref, hl_ref, h_off) + mod[2:3] * y
    f = _modnorm(h1, g_ref[...], mod[3:4], mod[4:5]).astype(BF16)
    acc = jnp.zeros((TM, D_MODEL), F32)
    for c in range(D_FF // FF_CHUNK):
        hid = jnp.maximum(_dot(f, w1_ref[:, c * FF_CHUNK:(c + 1) * FF_CHUNK]), 0.0)
        acc = acc + _dot((hid * hid).astype(BF16), w2_ref[c * FF_CHUNK:(c + 1) * FF_CHUNK, :])
    o_ref[...] = h1 + mod[5:6] * acc


def _mix_mlp(stream, ua, ub, mods, g, wo, w1, w2, *, latent_only):
    n_tiles = SEQ // TM if latent_only else N_TILES
    h_off = N_TILES - n_tiles
    half = D_MODEL // 2
    ub_col = 1 if ub.shape[-1] == D_MODEL else 0

    def full(shape):
        return pl.BlockSpec(shape, lambda b, t: (0,) * len(shape), pipeline_mode=pl.Buffered(1))

    return pl.pallas_call(
        functools.partial(_mix_mlp_kernel, h_off=h_off),
        out_shape=jax.ShapeDtypeStruct((BATCH, n_tiles * TM, D_MODEL), F32),
        grid=(BATCH, n_tiles),
        in_specs=[
            *_stream_specs(stream, h_off),
            pl.BlockSpec((None, TM, half), lambda b, t: (b, t, 0)),
            pl.BlockSpec((None, TM, half), lambda b, t: (b, t, ub_col)),
            pl.BlockSpec((None, None, 6, D_MODEL), lambda b, t: (jnp.minimum(t + h_off, 1), b, 0, 0)),
            pl.BlockSpec((1, D_MODEL), lambda b, t: (0, 0)),
            full((D_MODEL, D_MODEL)), full((D_MODEL, D_FF)), full((D_FF, D_MODEL)),
        ],
        out_specs=pl.BlockSpec((None, TM, D_MODEL), lambda b, t: (b, t, 0)),
        compiler_params=_params(),
        name="mixer_out_mlp",
    )(*stream, ua, ub, mods, g, wo, w1, w2)


S5_PAIR = 2
S5_TOK = S5_PAIR * S5_CHUNK
S5_STEPS = N_CHUNKS // S5_PAIR
GROUPS_PER_VREG = LANES // S5_GROUP
STEPS_PER_VREG = LANES // S5_GROUP
S5_SEG = 2 * LANES


def _lane_block():
    return lax.broadcasted_iota(jnp.int32, (BATCH, LANES), 1) // S5_GROUP


def _s5_in_kernel(h_ref, mod_ref, g_ref, perm_ref, w_ref, u_ref, z_ref, u_scr):
    mod = mod_ref[...]
    x = h_ref[...]
    a = _rms(x) * g_ref[...] * (1.0 + mod[:, 1:2, :]) + mod[:, 0:1, :]
    a = a.reshape(BATCH * S5_TOK, D_MODEL).astype(BF16)
    a = _dot(perm_ref[...], a).astype(BF16)
    lane_blk = _lane_block()

    def regroup(k):
        for r in range(GROUPS_PER_VREG):
            for hh in range(S5_CHUNK // STEPS_PER_VREG):
                halves = []
                for c2 in range(S5_PAIR):
                    acc = None
                    for m in range(STEPS_PER_VREG):
                        tok = c2 * S5_CHUNK + hh * STEPS_PER_VREG + m
                        piece = u_scr[k, tok * BATCH:(tok + 1) * BATCH, :]
                        sh = (S5_GROUP * (m - r)) % LANES
                        if sh:
                            piece = pltpu.roll(piece, sh, 1)
                        acc = piece if acc is None else jnp.where(lane_blk == m, piece, acc)
                    halves.append(acc)
                z_ref[k * GROUPS_PER_VREG + r, :, hh * LANES:(hh + 1) * LANES] = (
                    jnp.concatenate(halves, axis=0).astype(BF16))

    def finish(s, u_seg):
        u_ref[:, :, s * S5_SEG:(s + 1) * S5_SEG] = u_seg.reshape(S5_TOK, BATCH, S5_SEG)
        for kk in range(S5_SEG // LANES):
            k = s * (S5_SEG // LANES) + kk
            u_scr[k] = u_seg[:, kk * LANES:(kk + 1) * LANES]
            regroup(k)

    pending = None
    for s in range(D_MODEL // S5_SEG):
        u_seg = _dot(a, w_ref[:, s * S5_SEG:(s + 1) * S5_SEG])
        if pending is not None:
            finish(*pending)
        pending = (s, u_seg)
    finish(*pending)


def _s5_in(h, mods, g, perm, w_in):
    ctx_steps = N_CTX_CHUNKS // S5_PAIR
    n_rows = BATCH * S5_TOK
    return pl.pallas_call(
        _s5_in_kernel,
        out_shape=[jax.ShapeDtypeStruct((TT, BATCH, D_MODEL), F32),
                   jax.ShapeDtypeStruct((S5_GROUPS, ROWS, S5_CW), BF16)],
        grid=(S5_STEPS,),
        in_specs=[pl.BlockSpec((BATCH, S5_TOK, D_MODEL), lambda p: (0, p, 0)),
                  pl.BlockSpec((None, BATCH, 6, D_MODEL), lambda p: (jnp.minimum(p // ctx_steps, 1), 0, 0, 0)),
                  pl.BlockSpec((1, D_MODEL), lambda p: (0, 0)),
                  pl.BlockSpec((n_rows, n_rows), lambda p: (0, 0)),
                  pl.BlockSpec((D_MODEL, D_MODEL), lambda p: (0, 0))],
        out_specs=[pl.BlockSpec((S5_TOK, BATCH, D_MODEL), lambda p: (p, 0, 0)),
                   pl.BlockSpec((S5_GROUPS, S5_PAIR * BATCH, S5_CW), lambda p: (0, p, 0))],
        scratch_shapes=[pltpu.VMEM((D_MODEL // LANES, n_rows, LANES), F32)],
        compiler_params=_params(),
        name="s5_in_proj",
    )(h, mods, g, perm, w_in)


OPS_G = 4


def _s5_ops_kernel(*refs):
    for j in range(OPS_G):
        _s5_group_ops(*(r.at[j] for r in refs))


def _s5_group_ops(lr_ref, li_ref, ls_ref, btr_ref, bti_ref, cr_ref, ci_ref, m_ref, q_ref, n_ref, a_ref):
    P2 = 2 * S5_STATE

    def cmul(xr, xi, yr, yi):
        return xr * yr - xi * yi, xr * yi + xi * yr

    lr, li = lr_ref[...], li_ref[...]
    dt = jnp.exp(ls_ref[...])
    mag = jnp.exp(lr * dt)
    ar = mag * jnp.cos(li * dt)
    ai = mag * jnp.sin(li * dt)
    den = lr * lr + li * li
    nr = ar - 1.0
    f_re = (nr * lr + ai * li) / den
    f_im = (ai * lr - nr * li) / den
    bt_re = f_re * btr_ref[...] - f_im * bti_ref[...]
    bt_im = f_re * bti_ref[...] + f_im * btr_ref[...]

    squares = [(ar, ai)]
    while 2 ** len(squares) < S5_CHUNK:
        squares.append(cmul(*squares[-1], *squares[-1]))
    blk = lax.broadcasted_iota(jnp.int32, (S5_CW, P2), 0) // S5_GROUP
    is_fwd = lax.broadcasted_iota(jnp.int32, (S5_CW, P2), 1) < S5_STATE
    e_k = jnp.where(is_fwd, blk, (S5_CHUNK - 1) - blk)

    def a_pow(e):
        pr = jnp.ones((S5_CW, P2), F32)
        pi = jnp.zeros((S5_CW, P2), F32)
        for b, (br, bi) in enumerate(squares):
            bit = ((e >> b) & 1) == 1
            tr, ti = cmul(pr, pi, br, bi)
            pr = jnp.where(bit, tr, pr)
            pi = jnp.where(bit, ti, pi)
        return pr, pi

    def tile_rows(x):
        return jnp.concatenate([x] * S5_CHUNK, axis=0)

    cp_re, cp_im = cmul(tile_rows(cr_ref[...]), tile_rows(ci_ref[...]), *a_pow(e_k))
    lane_fwd = lax.broadcasted_iota(jnp.int32, (S5_GROUP, P2), 1) < S5_STATE

    def lag_kernels(keep):
        br = jnp.where(keep, bt_re, 0.0)
        bi = jnp.where(keep, bt_im, 0.0)
        dims = (((1,), (1,)), ((), ()))
        hi = lax.Precision.HIGHEST
        return (lax.dot_general(br, cp_re, dims, precision=hi, preferred_element_type=F32)
                - lax.dot_general(bi, cp_im, dims, precision=hi, preferred_element_type=F32))

    kt_f = lag_kernels(lane_fwd)
    kt_b = lag_kernels(jnp.logical_not(lane_fwd))
    lane_w = lax.broadcasted_iota(jnp.int32, (S5_GROUP, S5_CW), 1)
    for s in range(S5_CHUNK):
        f_part = kt_f if s == 0 else jnp.where(lane_w >= S5_GROUP * s, pltpu.roll(kt_f, S5_GROUP * s, 1), 0.0)
        sh = (S5_GROUP * (s + 1)) % S5_CW
        b_roll = kt_b if sh == 0 else pltpu.roll(kt_b, sh, 1)
        b_part = jnp.where(lane_w < S5_GROUP * (s + 1), b_roll, 0.0)
        m_ref[s * S5_GROUP:(s + 1) * S5_GROUP, :] = (f_part + b_part).astype(BF16)

    q_re, q_im = cmul(*a_pow((S5_CHUNK - 1) - e_k), tile_rows(bt_re), tile_rows(bt_im))
    q_ref[:, 0:P2] = q_re.astype(BF16)
    q_ref[:, P2:2 * P2] = q_im.astype(BF16)

    n_re, n_im = cmul(cp_re, cp_im, ar, ai)
    n_ref[:, 0:P2] = n_re.astype(BF16)
    n_ref[:, P2:2 * P2] = (-n_im).astype(BF16)

    a_re, a_im = ar, ai
    for _ in range(S5_CHUNK.bit_length() - 1):
        a_re, a_im = cmul(a_re, a_im, a_re, a_im)
    a_ref[0:1, :] = a_re
    a_ref[1:2, :] = a_im


def _s5_ops(row_params, bt, c_nat):
    P2 = 2 * S5_STATE
    row = pl.BlockSpec((OPS_G, 1, P2), lambda g: (g, 0, 0))
    mat = pl.BlockSpec((OPS_G, S5_GROUP, P2), lambda g: (g, 0, 0))
    sq = pl.BlockSpec((OPS_G, S5_CW, S5_CW), lambda g: (g, 0, 0))
    return pl.pallas_call(
        _s5_ops_kernel,
        out_shape=[jax.ShapeDtypeStruct((S5_GROUPS, S5_CW, S5_CW), BF16)] * 3
        + [jax.ShapeDtypeStruct((S5_GROUPS, 2, P2), F32)],
        grid=(S5_GROUPS // OPS_G,),
        in_specs=[row, row, row, mat, mat, mat, mat],
        out_specs=[sq, sq, sq, pl.BlockSpec((OPS_G, 2, P2), lambda g: (g, 0, 0))],
        compiler_params=_params(),
        name="s5_chunk_operators",
    )(*row_params, *bt, *c_nat)


ROWS = N_CHUNKS * BATCH


SCAN_G = 4


def _s5_scan_kernel(u_ref, m_ref, q_ref, n_ref, a_ref, y_ref, pu_ref, sp_ref):
    S = S5_STATE
    lane = lax.broadcasted_iota(jnp.int32, (BATCH, 2 * S), 1)
    lo = lane < S
    for j in range(SCAN_G):
        pu_ref[j] = _dot(u_ref[j], q_ref[j])
    for j in range(SCAN_G):
        y_ref[j] = _dot(u_ref[j], m_ref[j])
    decay = [(a_ref[j, 0:1, :], a_ref[j, 1:2, :]) for j in range(SCAN_G)]
    zero = jnp.zeros((BATCH, 2 * S), F32)
    state = [(zero, zero)] * SCAN_G
    for k in range(N_CHUNKS):
        cf = k * BATCH
        cb = (N_CTX_CHUNKS - 1 - k if k < N_CTX_CHUNKS else N_CHUNKS + N_CTX_CHUNKS - 1 - k) * BATCH
        for j in range(SCAN_G):
            s_re, s_im = state[j]
            a_re, a_im = decay[j]
            sp_ref[j, cf:cf + BATCH, 0:S] = s_re[:, 0:S]
            sp_ref[j, cb:cb + BATCH, S:2 * S] = s_re[:, S:2 * S]
            sp_ref[j, cf:cf + BATCH, 2 * S:3 * S] = s_im[:, 0:S]
            sp_ref[j, cb:cb + BATCH, 3 * S:4 * S] = s_im[:, S:2 * S]
            x_re = jnp.where(lo, pu_ref[j, cf:cf + BATCH, 0:2 * S], pu_ref[j, cb:cb + BATCH, 0:2 * S])
            x_im = jnp.where(lo, pu_ref[j, cf:cf + BATCH, 2 * S:4 * S], pu_ref[j, cb:cb + BATCH, 2 * S:4 * S])
            state[j] = (a_re * s_re - a_im * s_im + x_re, a_re * s_im + a_im * s_re + x_im)
    for j in range(SCAN_G):
        y_ref[j] += _dot_nt(sp_ref[j].astype(BF16), n_ref[j])


def _s5_scan(u_g, m_op, q_op, n_op, a_vec):
    sq = pl.BlockSpec((SCAN_G, S5_CW, S5_CW), lambda g: (g, 0, 0))
    rows = pl.BlockSpec((SCAN_G, ROWS, S5_CW), lambda g: (g, 0, 0))
    return pl.pallas_call(
        _s5_scan_kernel,
        out_shape=jax.ShapeDtypeStruct((S5_GROUPS, ROWS, S5_CW), F32),
        grid=(S5_GROUPS // SCAN_G,),
        in_specs=[rows, sq, sq, sq, pl.BlockSpec((SCAN_G, 2, 2 * S5_STATE), lambda g: (g, 0, 0))],
        out_specs=rows,
        scratch_shapes=[pltpu.VMEM((SCAN_G, ROWS, S5_CW), F32), pltpu.VMEM((SCAN_G, ROWS, S5_CW), F32)],
        compiler_params=_params(),
        name="s5_scan",
    )(u_g, m_op, q_op, n_op, a_vec)


def _s5_out_kernel(u_ref, y_ref, d_ref, gw_ref, gb_ref, perm_ref, o_ref, ys_scr):
    lane_blk = _lane_block()

    def regroup(k):
        for c2 in range(S5_PAIR):
            for t in range(S5_CHUNK):
                hh, m = divmod(t, STEPS_PER_VREG)
                acc = None
                for r in range(GROUPS_PER_VREG):
                    piece = y_ref[k * GROUPS_PER_VREG + r, c2 * BATCH:(c2 + 1) * BATCH, hh * LANES:(hh + 1) * LANES]
                    sh = (S5_GROUP * (r - m)) % LANES
                    if sh:
                        piece = pltpu.roll(piece, sh, 1)
                    acc = piece if acc is None else jnp.where(lane_blk == r, piece, acc)
                tok = c2 * S5_CHUNK + t
                ys_scr[k, tok * BATCH:(tok + 1) * BATCH, :] = acc

    n_seg = D_MODEL // S5_SEG
    per_seg = S5_SEG // LANES
    for kk in range(per_seg):
        regroup(kk)
    g_parts = []
    pre = None
    for s in range(n_seg):
        if s + 1 < n_seg:
            for kk in range(per_seg):
                regroup((s + 1) * per_seg + kk)
        cols = slice(s * S5_SEG, (s + 1) * S5_SEG)
        ys = jnp.concatenate([ys_scr[s * per_seg + kk] for kk in range(per_seg)], axis=1)
        y = u_ref[:, :, cols].reshape(BATCH * S5_TOK, S5_SEG) * d_ref[:, cols] + ys
        g_seg = jax.nn.gelu(y)
        g_parts.append(g_seg)
        part = _dot(g_seg.astype(BF16), gw_ref[cols, :])
        pre = part if pre is None else pre + part
    gate = jax.nn.sigmoid(pre + gb_ref[...])
    g = jnp.concatenate(g_parts, axis=1)
    gated = (g * gate).astype(BF16)
    gated = _dot(perm_ref[...], gated).astype(BF16)
    o_ref[...] = gated.reshape(BATCH, S5_TOK, D_MODEL)


def _s5_out(u, y_g, d_skip, glu_w, glu_b, perm_t):
    ctx_steps = N_CTX_CHUNKS // S5_PAIR
    n_rows = BATCH * S5_TOK
    vec = pl.BlockSpec((1, D_MODEL), lambda p: (0, 0))
    return pl.pallas_call(
        _s5_out_kernel,
        out_shape=jax.ShapeDtypeStruct((BATCH, SEQ, D_MODEL), BF16),
        grid=(S5_STEPS - ctx_steps,),
        in_specs=[pl.BlockSpec((S5_TOK, BATCH, D_MODEL), lambda p: (p + ctx_steps, 0, 0)),
                  pl.BlockSpec((S5_GROUPS, S5_PAIR * BATCH, S5_CW), lambda p: (0, p + ctx_steps, 0)),
                  vec, pl.BlockSpec((D_MODEL, D_MODEL), lambda p: (0, 0)), vec,
                  pl.BlockSpec((n_rows, n_rows), lambda p: (0, 0))],
        out_specs=pl.BlockSpec((BATCH, S5_TOK, D_MODEL), lambda p: (0, p, 0)),
        scratch_shapes=[pltpu.VMEM((D_MODEL // LANES, n_rows, LANES), F32)],
        compiler_params=_params(),
        name="s5_gelu_glu",
    )(u, y_g, d_skip, glu_w, glu_b, perm_t)


def _rope_tables():
    rows_n = SEQ // GRID_W
    row = np.repeat(np.arange(rows_n, dtype=np.float64), GRID_W)
    col = np.tile(np.arange(GRID_W, dtype=np.float64), rows_n)
    n_freq = HEAD_DIM // 4
    inv = ROPE_BASE ** (-np.arange(n_freq, dtype=np.float64) / n_freq)
    ang = np.concatenate([row[:, None] * inv, col[:, None] * inv], axis=-1)
    reps = LANES // (HEAD_DIM // 2)
    cos_t = np.tile(np.cos(ang), (1, reps))
    sin_t = np.tile(np.sin(ang), (1, reps))
    sign = np.where((np.arange(LANES) % HEAD_DIM) < HEAD_DIM // 2, -1.0, 1.0)
    cos_t = np.concatenate([np.ones((CTX_LEN, LANES)), cos_t], axis=0)
    sin_s = np.concatenate([np.zeros((CTX_LEN, LANES)), sin_t * sign], axis=0)
    return jnp.asarray(cos_t, F32), jnp.asarray(sin_s, F32)


def _fb_rows(x):
    return jnp.transpose(x, (1, 0, 2)).reshape(S5_GROUPS, 1, 2 * S5_STATE)


def _s5_layout(lam_re, lam_im, log_step, b_re, b_im, c_re, c_im):
    ls = jnp.broadcast_to(log_step[:, :, None], lam_re.shape)
    rows = [_fb_rows(v) for v in (lam_re, lam_im, ls)]

    def bt_of(b):
        return jnp.transpose(b, (1, 3, 0, 2)).reshape(S5_GROUPS, S5_GROUP, 2 * S5_STATE)

    def c_of(c):
        return jnp.transpose(c, (1, 2, 0, 3)).reshape(S5_GROUPS, S5_GROUP, 2 * S5_STATE)

    return rows, [bt_of(b_re), bt_of(b_im)], [c_of(c_re), c_of(c_im)]


def kernel(x, c, ctx, c_ctx, norm1_g, norm2_g, mod_w, mod_b, mlp_w1, mlp_w2, attn_w_in, attn_w_out, a_q_norm, a_k_norm, a_sink, b_q_norm, b_k_norm, b_lq1, b_lk1, b_lq2, b_lk2, b_subln, s5_w_in, s5_lambda_re, s5_lambda_im, s5_log_step, s5_b_re, s5_b_im, s5_c_re, s5_c_im, s5_d, s5_glu_w, s5_glu_b, s5_w_out):
    assert x.shape == (BATCH, SEQ, D_MODEL) and ctx.shape == (BATCH, CTX_LEN, D_MODEL)
    stream = (ctx, x)
    s_rows = jnp.concatenate([c, c_ctx[None], jnp.zeros((16 - BATCH - 1, D_MODEL), F32)], axis=0)
    m_all = _modulation(s_rows, mod_w, mod_b)
    cos_t, sin_s = _rope_tables()
    e_blk = jnp.asarray(np.kron(np.eye(LANES // HEAD_DIM), np.ones((HEAD_DIM, HEAD_DIM))) / HEAD_DIM, BF16)

    for i in range(DEPTH):
        last = i == DEPTH - 1
        j = i // 2
        m_lat = m_all[i, :BATCH].reshape(BATCH, 6, D_MODEL)
        m_ctx = jnp.broadcast_to(m_all[i, BATCH].reshape(1, 6, D_MODEL), (BATCH, 6, D_MODEL))
        mods = jnp.stack([m_ctx, m_lat])
        g1 = norm1_g[i].reshape(1, D_MODEL)
        g2 = norm2_g[i].reshape(1, D_MODEL)
        if i % 2 == 0:
            lambda_init = 0.8 - 0.6 * math.exp(-0.3 * i)
            gains = jnp.stack([jnp.tile(v[j], LANES // HEAD_DIM) for v in (a_q_norm, a_k_norm, b_q_norm, b_k_norm)])
            qa, k2a, v2a, qb, kb, vb = _attn_in(stream, mods, g1, attn_w_in[j].astype(BF16), gains, cos_t, sin_s, e_blk)
            ya = _win_attn(a_sink[j], qa, k2a, v2a)
            lpar = jnp.stack([b_lq1[j], b_lk1[j], b_lq2[j], b_lk2[j]])
            yb = _diff_attn(lpar, b_subln[j].reshape(1, LANES), qb, kb, vb, lambda_init)
            if last:
                ya, yb = ya[:, CTX_LEN:], yb[:, CTX_LEN:]
            ua, ub, wo = ya, yb, attn_w_out[j]
        else:
            src = np.arange(BATCH * S5_TOK).reshape(BATCH, S5_TOK).T.reshape(-1)
            perm = jnp.asarray(np.eye(BATCH * S5_TOK, dtype=np.float32)[src], BF16)
            h_all = stream[0] if stream[0].shape[1] == TT else jnp.concatenate(stream, axis=1)
            u, u_g = _s5_in(h_all, mods, g1, perm, s5_w_in[j].astype(BF16))
            ops_in = _s5_layout(s5_lambda_re[j], s5_lambda_im[j], s5_log_step[j], s5_b_re[j], s5_b_im[j],
                                s5_c_re[j], s5_c_im[j])
            m_op, q_op, n_op, a_vec = _s5_ops(*ops_in)
            y_g = _s5_scan(u_g, m_op, q_op, n_op, a_vec)
            assert last, "S5 layers before the last one would also need the context rows of the readout"
            gated = _s5_out(u, y_g, s5_d[j].reshape(1, D_MODEL), s5_glu_w[j].astype(BF16),
                            s5_glu_b[j].reshape(1, D_MODEL), perm.T)
            ua, ub, wo = gated, gated, s5_w_out[j]
        h = _mix_mlp(stream, ua, ub, mods, g2, wo.astype(BF16), mlp_w1[i].astype(BF16), mlp_w2[i].astype(BF16),
                     latent_only=last)
        stream = (h, h)
    return h
```

```python
import functools
import math

import jax
import jax.numpy as jnp
import numpy as np
from jax import lax
from jax.experimental import pallas as pl
from jax.experimental.pallas import tpu as pltpu

F32 = jnp.float32
BF16 = jnp.bfloat16

D_MODEL = 1024
BATCH = 8
SEQ = 2048
DEPTH = 2
GRID_W = 64
CTX_LEN = 256
HEAD_DIM = 64
WINDOW = 128
A_Q_HEADS = 8
A_KV_HEADS = 2
B_HEADS = 4
A_Q_W = A_Q_HEADS * HEAD_DIM
A_KV_W = A_KV_HEADS * HEAD_DIM
B_QK_W = B_HEADS * 2 * HEAD_DIM
B_V_W = B_HEADS * 2 * HEAD_DIM
ATTN_IN = A_Q_W + 2 * A_KV_W + 2 * B_QK_W + B_V_W
S5_GROUP = 16
S5_GROUPS = D_MODEL // S5_GROUP
S5_STATE = 64
D_FF = 4 * D_MODEL
ROPE_BASE = 10000.0
EPS = 1e-6
NEG_INF = -1e30
LOG2E = math.log2(math.e)

TT = CTX_LEN + SEQ
TM = 256
N_TILES = TT // TM
LANES = 128
S5_CHUNK = 16
S5_CW = S5_CHUNK * S5_GROUP
N_CHUNKS = TT // S5_CHUNK
N_CTX_CHUNKS = CTX_LEN // S5_CHUNK
VMEM_LIMIT = 56 * 1024 * 1024


def _dot(a, b):
    return jnp.dot(a, b, preferred_element_type=F32)


def _dot_nt(a, b):
    return lax.dot_general(a, b, (((1,), (1,)), ((), ())), preferred_element_type=F32)


def _rms(x):
    return x * lax.rsqrt(jnp.mean(x * x, axis=-1, keepdims=True) + EPS)


def _modnorm(x, g, shift, scale):
    return _rms(x) * g * (1.0 + scale) + shift


def _params(**kw):
    return pltpu.CompilerParams(vmem_limit_bytes=VMEM_LIMIT, **kw)


def _mod_kernel(s_ref, w_ref, b_ref, o_ref):
    s = s_ref[...]
    s = s * jax.nn.sigmoid(s)
    o_ref[...] = _dot(s.astype(BF16), w_ref[...].astype(BF16)) + b_ref[...]


def _modulation(s_rows, mod_w, mod_b):
    return pl.pallas_call(
        _mod_kernel,
        out_shape=jax.ShapeDtypeStruct((DEPTH, 16, 6 * D_MODEL), F32),
        grid=(DEPTH, 6),
        in_specs=[
            pl.BlockSpec((16, D_MODEL), lambda i, j: (0, 0)),
            pl.BlockSpec((None, D_MODEL, D_MODEL), lambda i, j: (i, 0, j)),
            pl.BlockSpec((None, 1, D_MODEL), lambda i, j: (i, 0, j)),
        ],
        out_specs=pl.BlockSpec((None, 16, D_MODEL), lambda i, j: (i, 0, j)),
        compiler_params=_params(),
        name="modulation",
    )(s_rows, mod_w, mod_b.reshape(DEPTH, 1, 6 * D_MODEL))


def _mod_spec():
    return pl.BlockSpec((None, None, 6, D_MODEL), lambda b, t: (jnp.minimum(t, 1), b, 0, 0))


def _stream_specs(stream, h_off):
    unified = stream[0].shape[1] == TT
    first_lat = 1 if unified else 0
    ctx_spec = pl.BlockSpec((None, TM, D_MODEL), lambda b, t: (b, 0, 0))
    lat_spec = pl.BlockSpec((None, TM, D_MODEL),
                            lambda b, t: (b, jnp.maximum(t + h_off - 1 + first_lat, first_lat), 0))
    return [ctx_spec, lat_spec]


def _stream_tile(hc_ref, hl_ref, h_off):
    if h_off > 0:
        return hl_ref[...]
    return jnp.where(pl.program_id(1) == 0, hc_ref[...], hl_ref[...])


def _attn_in_kernel(hc_ref, hl_ref, mod_ref, g_ref, w_ref, gain_ref, cos_ref, sin_ref, e_ref,
                    qa_ref, k2a_ref, v2a_ref, qb_ref, kb_ref, vb_ref):
    mod = mod_ref[...]
    a = _modnorm(_stream_tile(hc_ref, hl_ref, 0), g_ref[...], mod[0:1], mod[1:2]).astype(BF16)
    cos_t = cos_ref[...]
    sin_s = sin_ref[...]
    e = e_ref[...]
    gains = gain_ref[...]
    lane = lax.broadcasted_iota(jnp.int32, (TM, LANES), 1)
    first_half = (lane & (HEAD_DIM - 1)) < HEAD_DIM // 2
    lo = lane < HEAD_DIM

    def norm_rope(c, gain):
        ms = _dot((c * c).astype(BF16), e)
        cn = c * lax.rsqrt(ms + EPS) * gain
        r_fwd = pltpu.roll(cn, HEAD_DIM // 2, 1)
        r_bwd = pltpu.roll(cn, LANES - HEAD_DIM // 2, 1)
        return cn * cos_t + jnp.where(first_half, r_bwd, r_fwd) * sin_s

    def dup_halves(x, ref):
        sw = pltpu.roll(x, HEAD_DIM, 1)
        ref[:, 0:LANES] = jnp.where(lo, x, sw).astype(BF16)
        ref[:, LANES:2 * LANES] = jnp.where(lo, sw, x).astype(BF16)


    q_scale = HEAD_DIM ** -0.5 * LOG2E

    def finish_q(z, ref, gain):
        for c in range(z.shape[1] // LANES):
            ref[:, c * LANES:(c + 1) * LANES] = (
                norm_rope(z[:, c * LANES:(c + 1) * LANES], gain) * q_scale).astype(BF16)

    def finish_kv_a(z):
        dup_halves(norm_rope(z[:, 0:LANES], gains[1:2]), k2a_ref)
        v = z[:, LANES:2 * LANES]
        sw = pltpu.roll(v, HEAD_DIM, 1)
        for kvh, dup in enumerate((jnp.where(lo, v, sw), jnp.where(lo, sw, v))):
            v2a_ref[:, 2 * kvh * LANES:(2 * kvh + 1) * LANES] = dup.astype(BF16)
            v2a_ref[:, (2 * kvh + 1) * LANES:(2 * kvh + 2) * LANES] = jnp.ones((TM, LANES), BF16)

    def finish_kb(z):
        for c in range(B_QK_W // LANES):
            kb_ref[:, c * LANES:(c + 1) * LANES] = norm_rope(z[:, c * LANES:(c + 1) * LANES], gains[3:4]).astype(BF16)

    def finish_vb(z):
        for hd in range(B_HEADS):
            vb_ref[:, 2 * hd * LANES:(2 * hd + 1) * LANES] = z[:, hd * LANES:(hd + 1) * LANES].astype(BF16)
            vb_ref[:, (2 * hd + 1) * LANES:(2 * hd + 2) * LANES] = jnp.ones((TM, LANES), BF16)

    segments = [(A_Q_W, lambda z: finish_q(z, qa_ref, gains[0:1])), (2 * A_KV_W, finish_kv_a),
                (B_QK_W, lambda z: finish_q(z, qb_ref, gains[2:3])), (B_QK_W, finish_kb), (B_V_W, finish_vb)]
    off = 0
    pending = None
    for width, finish in segments:
        z = _dot(a, w_ref[:, off:off + width])
        off += width
        if pending is not None:
            pending[1](pending[0])
        pending = (z, finish)
    pending[1](pending[0])


def _attn_in(stream, mods, g, w_in, gains, cos_t, sin_s, e_blk):
    def tok(width):
        return pl.BlockSpec((None, TM, width), lambda b, t: (b, t, 0))

    def full(shape):
        return pl.BlockSpec(shape, lambda b, t: (0,) * len(shape))

    out_shapes = [jax.ShapeDtypeStruct((BATCH, TT, w), BF16)
                  for w in (A_Q_W, 2 * A_KV_W, 4 * A_KV_W, B_QK_W, B_QK_W, 2 * B_V_W)]
    return pl.pallas_call(
        _attn_in_kernel,
        out_shape=out_shapes,
        grid=(BATCH, N_TILES),
        in_specs=[
            *_stream_specs(stream, 0), _mod_spec(), full((1, D_MODEL)), full((D_MODEL, ATTN_IN)), full((4, LANES)),
            pl.BlockSpec((TM, LANES), lambda b, t: (t, 0)), pl.BlockSpec((TM, LANES), lambda b, t: (t, 0)),
            full((LANES, LANES)),
        ],
        out_specs=[tok(A_Q_W), tok(2 * A_KV_W), tok(4 * A_KV_W), tok(B_QK_W), tok(B_QK_W), tok(2 * B_V_W)],
        compiler_params=_params(),
        name="attn_in_proj",
    )(*stream, mods, g, w_in, gains, cos_t, sin_s, e_blk)


QB = 128


def _win_attn_kernel(sink_ref, q_ref, k2_ref, v2_ref, o_ref):
    t = pl.program_id(1)
    lane = lax.broadcasted_iota(jnp.int32, (QB, LANES), 1)
    lo = lane < HEAD_DIM
    rows = 4 * QB
    row = lax.broadcasted_iota(jnp.int32, (rows, 3 * QB), 0)
    col = lax.broadcasted_iota(jnp.int32, (rows, 3 * QB), 1)
    row_head = lax.broadcasted_iota(jnp.int32, (rows, 1), 0) // QB
    zero = jnp.zeros((QB, LANES), BF16)
    blocks = [(qb, g) for qb in range(TM // QB) for g in range(A_KV_HEADS)]

    def window_start(qb):
        n = (t - 1) * (TM // QB) + qb
        ws = jnp.clip((n - 1) * QB, 0, SEQ - 3 * QB)
        return n, ws

    def scores(qb, g, with_window):
        pieces = []
        for p in range(2):
            qp = q_ref[qb * QB:(qb + 1) * QB, g * 2 * LANES + p * LANES: g * 2 * LANES + (p + 1) * LANES]
            pieces.append(jnp.where(lo, qp, zero))
            pieces.append(jnp.where(lo, zero, qp))
        qs = jnp.concatenate(pieces, axis=0)
        s_c = _dot_nt(qs, k2_ref[0:CTX_LEN, g * LANES:(g + 1) * LANES])
        if not with_window:
            return s_c, None
        n, ws = window_start(qb)
        kw = k2_ref[pl.ds(pl.multiple_of(ws + CTX_LEN, QB), 3 * QB), g * LANES:(g + 1) * LANES]
        valid = jnp.abs(n * QB + (row & (QB - 1)) - (ws + col)) <= WINDOW
        return s_c, jnp.where(valid, _dot_nt(qs, kw), NEG_INF)

    def finish(qb, g, s_c, s_w):
        sk = jnp.full((rows, 1), sink_ref[4 * g + 3], F32)
        for hh in range(3):
            sk = jnp.where(row_head == hh, sink_ref[4 * g + hh], sk)
        sk = sk * LOG2E
        m = jnp.maximum(jnp.max(s_c, axis=-1, keepdims=True), sk)
        if s_w is not None:
            m = jnp.maximum(m, jnp.max(s_w, axis=-1, keepdims=True))
        vcols = slice(2 * g * LANES, (2 * g + 2) * LANES)
        pv = _dot(jnp.exp2(s_c - m).astype(BF16), v2_ref[0:CTX_LEN, vcols])
        if s_w is not None:
            _, ws = window_start(qb)
            vw = v2_ref[pl.ds(pl.multiple_of(ws + CTX_LEN, QB), 3 * QB), vcols]
            pv = pv + _dot(jnp.exp2(s_w - m).astype(BF16), vw)
        o = pv[:, 0:LANES] / (pv[:, LANES:2 * LANES] + jnp.exp2(sk - m))
        for p in range(2):
            o_ref[qb * QB:(qb + 1) * QB, g * 2 * LANES + p * LANES: g * 2 * LANES + (p + 1) * LANES] = jnp.where(
                lo, o[2 * p * QB:(2 * p + 1) * QB], o[(2 * p + 1) * QB:(2 * p + 2) * QB]).astype(BF16)

    def attend(with_window):
        s_next = scores(*blocks[0], with_window)
        for i, blk in enumerate(blocks):
            s_cur = s_next
            if i + 1 < len(blocks):
                s_next = scores(*blocks[i + 1], with_window)
            finish(*blk, *s_cur)

    @pl.when(t == 0)
    def _():
        attend(False)

    @pl.when(t > 0)
    def _():
        attend(True)


def _win_attn(sink, qa, k2a, v2a):
    return pl.pallas_call(
        _win_attn_kernel,
        out_shape=jax.ShapeDtypeStruct((BATCH, TT, A_Q_W), BF16),
        grid=(BATCH, N_TILES),
        in_specs=[
            pl.BlockSpec(memory_space=pltpu.SMEM),
            pl.BlockSpec((None, TM, A_Q_W), lambda b, t: (b, t, 0)),
            pl.BlockSpec((None, TT, 2 * A_KV_W), lambda b, t: (b, 0, 0)),
            pl.BlockSpec((None, TT, 4 * A_KV_W), lambda b, t: (b, 0, 0)),
        ],
        out_specs=pl.BlockSpec((None, TM, A_Q_W), lambda b, t: (b, t, 0)),
        compiler_params=_params(),
        name="window_attention",
    )(sink, qa, k2a, v2a)


DIFF_ROWS = 128


def _diff_attn_kernel(lpar_ref, subln_ref, q_ref, k_ref, v_ref, o_ref, *, lambda_init):
    t = pl.program_id(1)
    lp = lpar_ref[...]
    lam = (jnp.exp(jnp.sum(lp[0:1] * lp[1:2], axis=-1, keepdims=True))
           - jnp.exp(jnp.sum(lp[2:3] * lp[3:4], axis=-1, keepdims=True)) + lambda_init)
    R = DIFF_ROWS
    lane = lax.broadcasted_iota(jnp.int32, (R, LANES), 1)
    lo = lane < HEAD_DIM
    zero = jnp.zeros((R, LANES), BF16)

    def attend(n_keys):
        blocks = [(slice(rb * R, (rb + 1) * R), slice(h * LANES, (h + 1) * LANES))
                  for h in range(B_HEADS) for rb in range(TM // R)]

        def scores(rows, cols):
            q = q_ref[rows, cols]
            qs = jnp.concatenate([jnp.where(lo, q, zero), jnp.where(lo, zero, q)], axis=0)
            return _dot_nt(qs, k_ref[0:n_keys, cols])

        def finish(rows, cols, s):
            p = jnp.exp2(s - jnp.max(s, axis=-1, keepdims=True)).astype(BF16)
            vcols = slice(2 * cols.start, 2 * cols.stop)
            pv = _dot(p, v_ref[0:n_keys, vcols])
            sm = pv[:, 0:LANES] / pv[:, LANES:2 * LANES]
            y = sm[0:R] - lam * sm[R:2 * R]
            o_ref[rows, cols] = (_rms(y) * subln_ref[...] * (1.0 - lambda_init)).astype(BF16)

        s_next = scores(*blocks[0])
        for i, blk in enumerate(blocks):
            s_cur = s_next
            if i + 1 < len(blocks):
                s_next = scores(*blocks[i + 1])
            finish(*blk, s_cur)

    @pl.when(t == 0)
    def _():
        attend(CTX_LEN)

    @pl.when(t > 0)
    def _():
        attend(TT)


def _diff_attn(lpar, subln, qb, kb, vb, lambda_init):
    return pl.pallas_call(
        functools.partial(_diff_attn_kernel, lambda_init=lambda_init),
        out_shape=jax.ShapeDtypeStruct((BATCH, TT, B_V_W), BF16),
        grid=(BATCH, N_TILES),
        in_specs=[
            pl.BlockSpec((4, HEAD_DIM), lambda b, t: (0, 0)),
            pl.BlockSpec((1, LANES), lambda b, t: (0, 0)),
            pl.BlockSpec((None, TM, B_QK_W), lambda b, t: (b, t, 0)),
            pl.BlockSpec((None, TT, B_QK_W), lambda b, t: (b, 0, 0)),
            pl.BlockSpec((None, TT, 2 * B_V_W), lambda b, t: (b, 0, 0)),
        ],
        out_specs=pl.BlockSpec((None, TM, B_V_W), lambda b, t: (b, t, 0)),
        compiler_params=_params(),
        name="diff_attention",
    )(lpar, subln, qb, kb, vb)


FF_CHUNK = 1024


def _mix_mlp_kernel(hc_ref, hl_ref, ua_ref, ub_ref, mod_ref, g_ref, wo_ref, w1_ref, w2_ref, o_ref, *, h_off):
    mod = mod_ref[...]
    half = D_MODEL // 2
    y = _dot(ua_ref[...], wo_ref[0:half, :]) + _dot(ub_ref[...], wo_ref[half:D_MODEL, :])
    h1 = _stream_tile(hc_ref, hl_ref, h_off) + mod[2:3] * y
    f = _modnorm(h1, g_ref[...], mod[3:4], mod[4:5]).astype(BF16)
    acc = jnp.zeros((TM, D_MODEL), F32)
    for c in range(D_FF // FF_CHUNK):
        hid = jnp.maximum(_dot(f, w1_ref[:, c * FF_CHUNK:(c + 1) * FF_CHUNK]), 0.0)
        acc = acc + _dot((hid * hid).astype(BF16), w2_ref[c * FF_CHUNK:(c + 1) * FF_CHUNK, :])
    o_ref[...] = h1 + mod[5:6] * acc


def _mix_mlp(stream, ua, ub, mods, g, wo, w1, w2, *, latent_only):
    n_tiles = SEQ // TM if latent_only else N_TILES
    h_off = N_TILES - n_tiles
    half = D_MODEL // 2
    ub_col = 1 if ub.shape[-1] == D_MODEL else 0

    def full(shape):
        return pl.BlockSpec(shape, lambda b, t: (0,) * len(shape), pipeline_mode=pl.Buffered(1))

    return pl.pallas_call(
        functools.partial(_mix_mlp_kernel, h_off=h_off),
        out_shape=jax.ShapeDtypeStruct((BATCH, n_tiles * TM, D_MODEL), F32),
        grid=(BATCH, n_tiles),
        in_specs=[
            *_stream_specs(stream, h_off),
            pl.BlockSpec((None, TM, half), lambda b, t: (b, t, 0)),
            pl.BlockSpec((None, TM, half), lambda b, t: (b, t, ub_col)),
            pl.BlockSpec((None, None, 6, D_MODEL), lambda b, t: (jnp.minimum(t + h_off, 1), b, 0, 0)),
            pl.BlockSpec((1, D_MODEL), lambda b, t: (0, 0)),
            full((D_MODEL, D_MODEL)), full((D_MODEL, D_FF)), full((D_FF, D_MODEL)),
        ],
        out_specs=pl.BlockSpec((None, TM, D_MODEL), lambda b, t: (b, t, 0)),
        compiler_params=_params(),
        name="mixer_out_mlp",
    )(*stream, ua, ub, mods, g, wo, w1, w2)


S5_PAIR = 2
S5_TOK = S5_PAIR * S5_CHUNK
S5_STEPS = N_CHUNKS // S5_PAIR
GROUPS_PER_VREG = LANES // S5_GROUP
STEPS_PER_VREG = LANES // S5_GROUP
S5_SEG = 2 * LANES


def _lane_block():
    return lax.broadcasted_iota(jnp.int32, (BATCH, LANES), 1) // S5_GROUP


def _block_transpose(xs, lane_blk):
    xs = list(xs)
    n = len(xs)
    d = n // 2
    while d:
        low = (lane_blk & d) == 0
        for i in range(n):
            if i & d:
                continue
            a, b = xs[i], xs[i + d]
            xs[i] = jnp.where(low, a, pltpu.roll(b, S5_GROUP * d, 1))
            xs[i + d] = jnp.where(low, pltpu.roll(a, LANES - S5_GROUP * d, 1), b)
        d //= 2
    return xs


def _s5_in_kernel(h_ref, mod_ref, g_ref, perm_ref, w_ref, u_ref, z_ref, u_scr):
    mod = mod_ref[...]
    x = h_ref[...]
    a = _rms(x) * g_ref[...] * (1.0 + mod[:, 1:2, :]) + mod[:, 0:1, :]
    a = a.reshape(BATCH * S5_TOK, D_MODEL).astype(BF16)
    a = _dot(perm_ref[...], a).astype(BF16)
    lane_blk = _lane_block()

    def regroup(k):
        for hh in range(S5_CHUNK // STEPS_PER_VREG):
            halves = []
            for c2 in range(S5_PAIR):
                tok0 = c2 * S5_CHUNK + hh * STEPS_PER_VREG
                steps = [u_scr[k, (tok0 + m) * BATCH:(tok0 + m + 1) * BATCH, :] for m in range(STEPS_PER_VREG)]
                halves.append(_block_transpose(steps, lane_blk))
            for r in range(GROUPS_PER_VREG):
                z_ref[k * GROUPS_PER_VREG + r, :, hh * LANES:(hh + 1) * LANES] = (
                    jnp.concatenate([h[r] for h in halves], axis=0).astype(BF16))

    def finish(s, u_seg):
        u_ref[:, :, s * S5_SEG:(s + 1) * S5_SEG] = u_seg.reshape(S5_TOK, BATCH, S5_SEG)
        for kk in range(S5_SEG // LANES):
            k = s * (S5_SEG // LANES) + kk
            u_scr[k] = u_seg[:, kk * LANES:(kk + 1) * LANES]
            regroup(k)

    pending = None
    for s in range(D_MODEL // S5_SEG):
        u_seg = _dot(a, w_ref[:, s * S5_SEG:(s + 1) * S5_SEG])
        if pending is not None:
            finish(*pending)
        pending = (s, u_seg)
    finish(*pending)


def _s5_in(h, mods, g, perm, w_in):
    ctx_steps = N_CTX_CHUNKS // S5_PAIR
    n_rows = BATCH * S5_TOK
    return pl.pallas_call(
        _s5_in_kernel,
        out_shape=[jax.ShapeDtypeStruct((TT, BATCH, D_MODEL), F32),
                   jax.ShapeDtypeStruct((S5_GROUPS, ROWS, S5_CW), BF16)],
        grid=(S5_STEPS,),
        in_specs=[pl.BlockSpec((BATCH, S5_TOK, D_MODEL), lambda p: (0, p, 0)),
                  pl.BlockSpec((None, BATCH, 6, D_MODEL), lambda p: (jnp.minimum(p // ctx_steps, 1), 0, 0, 0)),
                  pl.BlockSpec((1, D_MODEL), lambda p: (0, 0)),
                  pl.BlockSpec((n_rows, n_rows), lambda p: (0, 0)),
                  pl.BlockSpec((D_MODEL, D_MODEL), lambda p: (0, 0))],
        out_specs=[pl.BlockSpec((S5_TOK, BATCH, D_MODEL), lambda p: (p, 0, 0)),
                   pl.BlockSpec((S5_GROUPS, S5_PAIR * BATCH, S5_CW), lambda p: (0, p, 0))],
        scratch_shapes=[pltpu.VMEM((D_MODEL // LANES, n_rows, LANES), F32)],
        compiler_params=_params(),
        name="s5_in_proj",
    )(h, mods, g, perm, w_in)


OPS_G = 4


def _s5_ops_kernel(*refs):
    for j in range(OPS_G):
        _s5_group_ops(*(r.at[j] for r in refs))


def _s5_group_ops(lr_ref, li_ref, ls_ref, btr_ref, bti_ref, cr_ref, ci_ref, m_ref, q_ref, n_ref, a_ref):
    P2 = 2 * S5_STATE

    def cmul(xr, xi, yr, yi):
        return xr * yr - xi * yi, xr * yi + xi * yr

    lr, li = lr_ref[...], li_ref[...]
    dt = jnp.exp(ls_ref[...])
    mag = jnp.exp(lr * dt)
    ar = mag * jnp.cos(li * dt)
    ai = mag * jnp.sin(li * dt)
    den = lr * lr + li * li
    nr = ar - 1.0
    f_re = (nr * lr + ai * li) / den
    f_im = (ai * lr - nr * li) / den
    bt_re = f_re * btr_ref[...] - f_im * bti_ref[...]
    bt_im = f_re * bti_ref[...] + f_im * btr_ref[...]

    squares = [(ar, ai)]
    while 2 ** len(squares) < S5_CHUNK:
        squares.append(cmul(*squares[-1], *squares[-1]))
    blk = lax.broadcasted_iota(jnp.int32, (S5_CW, P2), 0) // S5_GROUP
    is_fwd = lax.broadcasted_iota(jnp.int32, (S5_CW, P2), 1) < S5_STATE
    e_k = jnp.where(is_fwd, blk, (S5_CHUNK - 1) - blk)

    def a_pow(e):
        pr = jnp.ones((S5_CW, P2), F32)
        pi = jnp.zeros((S5_CW, P2), F32)
        for b, (br, bi) in enumerate(squares):
            bit = ((e >> b) & 1) == 1
            tr, ti = cmul(pr, pi, br, bi)
            pr = jnp.where(bit, tr, pr)
            pi = jnp.where(bit, ti, pi)
        return pr, pi

    def tile_rows(x):
        return jnp.concatenate([x] * S5_CHUNK, axis=0)

    cp_re, cp_im = cmul(tile_rows(cr_ref[...]), tile_rows(ci_ref[...]), *a_pow(e_k))
    lane_fwd = lax.broadcasted_iota(jnp.int32, (S5_GROUP, P2), 1) < S5_STATE

    def lag_kernels(keep):
        br = jnp.where(keep, bt_re, 0.0)
        bi = jnp.where(keep, bt_im, 0.0)
        dims = (((1,), (1,)), ((), ()))
        hi = lax.Precision.HIGHEST
        return (lax.dot_general(br, cp_re, dims, precision=hi, preferred_element_type=F32)
                - lax.dot_general(bi, cp_im, dims, precision=hi, preferred_element_type=F32))

    kt_f = lag_kernels(lane_fwd)
    kt_b = lag_kernels(jnp.logical_not(lane_fwd))
    lane_w = lax.broadcasted_iota(jnp.int32, (S5_GROUP, S5_CW), 1)
    for s in range(S5_CHUNK):
        f_part = kt_f if s == 0 else jnp.where(lane_w >= S5_GROUP * s, pltpu.roll(kt_f, S5_GROUP * s, 1), 0.0)
        sh = (S5_GROUP * (s + 1)) % S5_CW
        b_roll = kt_b if sh == 0 else pltpu.roll(kt_b, sh, 1)
        b_part = jnp.where(lane_w < S5_GROUP * (s + 1), b_roll, 0.0)
        m_ref[s * S5_GROUP:(s + 1) * S5_GROUP, :] = (f_part + b_part).astype(BF16)

    q_re, q_im = cmul(*a_pow((S5_CHUNK - 1) - e_k), tile_rows(bt_re), tile_rows(bt_im))
    q_ref[:, 0:P2] = q_re.astype(BF16)
    q_ref[:, P2:2 * P2] = q_im.astype(BF16)

    n_re, n_im = cmul(cp_re, cp_im, ar, ai)
    n_ref[:, 0:P2] = n_re.astype(BF16)
    n_ref[:, P2:2 * P2] = (-n_im).astype(BF16)

    a_re, a_im = ar, ai
    for _ in range(S5_CHUNK.bit_length() - 1):
        a_re, a_im = cmul(a_re, a_im, a_re, a_im)
    a_ref[0:1, :] = a_re
    a_ref[1:2, :] = a_im


def _s5_ops(row_params, bt, c_nat):
    P2 = 2 * S5_STATE
    row = pl.BlockSpec((OPS_G, 1, P2), lambda g: (g, 0, 0))
    mat = pl.BlockSpec((OPS_G, S5_GROUP, P2), lambda g: (g, 0, 0))
    sq = pl.BlockSpec((OPS_G, S5_CW, S5_CW), lambda g: (g, 0, 0))
    return pl.pallas_call(
        _s5_ops_kernel,
        out_shape=[jax.ShapeDtypeStruct((S5_GROUPS, S5_CW, S5_CW), BF16)] * 3
        + [jax.ShapeDtypeStruct((S5_GROUPS, 2, P2), F32)],
        grid=(S5_GROUPS // OPS_G,),
        in_specs=[row, row, row, mat, mat, mat, mat],
        out_specs=[sq, sq, sq, pl.BlockSpec((OPS_G, 2, P2), lambda g: (g, 0, 0))],
        compiler_params=_params(),
        name="s5_chunk_operators",
    )(*row_params, *bt, *c_nat)


ROWS = N_CHUNKS * BATCH


SCAN_G = 4


def _s5_scan_kernel(u_ref, m_ref, q_ref, n_ref, a_ref, y_ref, pu_ref, sp_ref):
    S = S5_STATE
    lane = lax.broadcasted_iota(jnp.int32, (BATCH, 2 * S), 1)
    lo = lane < S
    for j in range(SCAN_G):
        pu_ref[j] = _dot(u_ref[j], q_ref[j])
    for j in range(SCAN_G):
        y_ref[j] = _dot(u_ref[j], m_ref[j])
    decay = [(a_ref[j, 0:1, :], a_ref[j, 1:2, :]) for j in range(SCAN_G)]
    zero = jnp.zeros((BATCH, 2 * S), F32)
    state = [(zero, zero)] * SCAN_G
    for k in range(N_CHUNKS):
        cf = k * BATCH
        cb = (N_CTX_CHUNKS - 1 - k if k < N_CTX_CHUNKS else N_CHUNKS + N_CTX_CHUNKS - 1 - k) * BATCH
        for j in range(SCAN_G):
            s_re, s_im = state[j]
            a_re, a_im = decay[j]
            sp_ref[j, cf:cf + BATCH, 0:S] = s_re[:, 0:S]
            sp_ref[j, cb:cb + BATCH, S:2 * S] = s_re[:, S:2 * S]
            sp_ref[j, cf:cf + BATCH, 2 * S:3 * S] = s_im[:, 0:S]
            sp_ref[j, cb:cb + BATCH, 3 * S:4 * S] = s_im[:, S:2 * S]
            x_re = jnp.where(lo, pu_ref[j, cf:cf + BATCH, 0:2 * S], pu_ref[j, cb:cb + BATCH, 0:2 * S])
            x_im = jnp.where(lo, pu_ref[j, cf:cf + BATCH, 2 * S:4 * S], pu_ref[j, cb:cb + BATCH, 2 * S:4 * S])
            state[j] = (a_re * s_re - a_im * s_im + x_re, a_re * s_im + a_im * s_re + x_im)
    for j in range(SCAN_G):
        y_ref[j] += _dot_nt(sp_ref[j].astype(BF16), n_ref[j])


def _s5_scan(u_g, m_op, q_op, n_op, a_vec):
    sq = pl.BlockSpec((SCAN_G, S5_CW, S5_CW), lambda g: (g, 0, 0))
    rows = pl.BlockSpec((SCAN_G, ROWS, S5_CW), lambda g: (g, 0, 0))
    return pl.pallas_call(
        _s5_scan_kernel,
        out_shape=jax.ShapeDtypeStruct((S5_GROUPS, ROWS, S5_CW), F32),
        grid=(S5_GROUPS // SCAN_G,),
        in_specs=[rows, sq, sq, sq, pl.BlockSpec((SCAN_G, 2, 2 * S5_STATE), lambda g: (g, 0, 0))],
        out_specs=rows,
        scratch_shapes=[pltpu.VMEM((SCAN_G, ROWS, S5_CW), F32), pltpu.VMEM((SCAN_G, ROWS, S5_CW), F32)],
        compiler_params=_params(),
        name="s5_scan",
    )(u_g, m_op, q_op, n_op, a_vec)


def _s5_out_kernel(u_ref, y_ref, d_ref, gw_ref, gb_ref, perm_ref, o_ref, ys_scr):
    lane_blk = _lane_block()

    def regroup(k):
        for c2 in range(S5_PAIR):
            for hh in range(S5_CHUNK // STEPS_PER_VREG):
                groups = [y_ref[k * GROUPS_PER_VREG + r, c2 * BATCH:(c2 + 1) * BATCH, hh * LANES:(hh + 1) * LANES]
                          for r in range(GROUPS_PER_VREG)]
                steps = _block_transpose(groups, lane_blk)
                for m in range(STEPS_PER_VREG):
                    tok = c2 * S5_CHUNK + hh * STEPS_PER_VREG + m
                    ys_scr[k, tok * BATCH:(tok + 1) * BATCH, :] = steps[m]

    n_seg = D_MODEL // S5_SEG
    per_seg = S5_SEG // LANES
    for kk in range(per_seg):
        regroup(kk)
    g_parts = []
    pre = None
    for s in range(n_seg):
        if s + 1 < n_seg:
            for kk in range(per_seg):
                regroup((s + 1) * per_seg + kk)
        cols = slice(s * S5_SEG, (s + 1) * S5_SEG)
        ys = jnp.concatenate([ys_scr[s * per_seg + kk] for kk in range(per_seg)], axis=1)
        y = u_ref[:, :, cols].reshape(BATCH * S5_TOK, S5_SEG) * d_ref[:, cols] + ys
        g_seg = jax.nn.gelu(y)
        g_parts.append(g_seg)
        part = _dot(g_seg.astype(BF16), gw_ref[cols, :])
        pre = part if pre is None else pre + part
    gate = jax.nn.sigmoid(pre + gb_ref[...])
    g = jnp.concatenate(g_parts, axis=1)
    gated = (g * gate).astype(BF16)
    gated = _dot(perm_ref[...], gated).astype(BF16)
    o_ref[...] = gated.reshape(BATCH, S5_TOK, D_MODEL)


def _s5_out(u, y_g, d_skip, glu_w, glu_b, perm_t):
    ctx_steps = N_CTX_CHUNKS // S5_PAIR
    n_rows = BATCH * S5_TOK
    vec = pl.BlockSpec((1, D_MODEL), lambda p: (0, 0))
    return pl.pallas_call(
        _s5_out_kernel,
        out_shape=jax.ShapeDtypeStruct((BATCH, SEQ, D_MODEL), BF16),
        grid=(S5_STEPS - ctx_steps,),
        in_specs=[pl.BlockSpec((S5_TOK, BATCH, D_MODEL), lambda p: (p + ctx_steps, 0, 0)),
                  pl.BlockSpec((S5_GROUPS, S5_PAIR * BATCH, S5_CW), lambda p: (0, p + ctx_steps, 0)),
                  vec, pl.BlockSpec((D_MODEL, D_MODEL), lambda p: (0, 0)), vec,
                  pl.BlockSpec((n_rows, n_rows), lambda p: (0, 0))],
        out_specs=pl.BlockSpec((BATCH, S5_TOK, D_MODEL), lambda p: (0, p, 0)),
        scratch_shapes=[pltpu.VMEM((D_MODEL // LANES, n_rows, LANES), F32)],
        compiler_params=_params(),
        name="s5_gelu_glu",
    )(u, y_g, d_skip, glu_w, glu_b, perm_t)


def _rope_tables():
    rows_n = SEQ // GRID_W
    row = np.repeat(np.arange(rows_n, dtype=np.float64), GRID_W)
    col = np.tile(np.arange(GRID_W, dtype=np.float64), rows_n)
    n_freq = HEAD_DIM // 4
    inv = ROPE_BASE ** (-np.arange(n_freq, dtype=np.float64) / n_freq)
    ang = np.concatenate([row[:, None] * inv, col[:, None] * inv], axis=-1)
    reps = LANES // (HEAD_DIM // 2)
    cos_t = np.tile(np.cos(ang), (1, reps))
    sin_t = np.tile(np.sin(ang), (1, reps))
    sign = np.where((np.arange(LANES) % HEAD_DIM) < HEAD_DIM // 2, -1.0, 1.0)
    cos_t = np.concatenate([np.ones((CTX_LEN, LANES)), cos_t], axis=0)
    sin_s = np.concatenate([np.zeros((CTX_LEN, LANES)), sin_t * sign], axis=0)
    return jnp.asarray(cos_t, F32), jnp.asarray(sin_s, F32)


def _fb_rows(x):
    return jnp.transpose(x, (1, 0, 2)).reshape(S5_GROUPS, 1, 2 * S5_STATE)


def _s5_layout(lam_re, lam_im, log_step, b_re, b_im, c_re, c_im):
    ls = jnp.broadcast_to(log_step[:, :, None], lam_re.shape)
    rows = [_fb_rows(v) for v in (lam_re, lam_im, ls)]

    def bt_of(b):
        return jnp.transpose(b, (1, 3, 0, 2)).reshape(S5_GROUPS, S5_GROUP, 2 * S5_STATE)

    def c_of(c):
        return jnp.transpose(c, (1, 2, 0, 3)).reshape(S5_GROUPS, S5_GROUP, 2 * S5_STATE)

    return rows, [bt_of(b_re), bt_of(b_im)], [c_of(c_re), c_of(c_im)]


def kernel(x, c, ctx, c_ctx, norm1_g, norm2_g, mod_w, mod_b, mlp_w1, mlp_w2, attn_w_in, attn_w_out, a_q_norm, a_k_norm, a_sink, b_q_norm, b_k_norm, b_lq1, b_lk1, b_lq2, b_lk2, b_subln, s5_w_in, s5_lambda_re, s5_lambda_im, s5_log_step, s5_b_re, s5_b_im, s5_c_re, s5_c_im, s5_d, s5_glu_w, s5_glu_b, s5_w_out):
    assert x.shape == (BATCH, SEQ, D_MODEL) and ctx.shape == (BATCH, CTX_LEN, D_MODEL)
    stream = (ctx, x)
    s_rows = jnp.concatenate([c, c_ctx[None], jnp.zeros((16 - BATCH - 1, D_MODEL), F32)], axis=0)
    m_all = _modulation(s_rows, mod_w, mod_b)
    cos_t, sin_s = _rope_tables()
    e_blk = jnp.asarray(np.kron(np.eye(LANES // HEAD_DIM), np.ones((HEAD_DIM, HEAD_DIM))) / HEAD_DIM, BF16)

    for i in range(DEPTH):
        last = i == DEPTH - 1
        j = i // 2
        m_lat = m_all[i, :BATCH].reshape(BATCH, 6, D_MODEL)
        m_ctx = jnp.broadcast_to(m_all[i, BATCH].reshape(1, 6, D_MODEL), (BATCH, 6, D_MODEL))
        mods = jnp.stack([m_ctx, m_lat])
        g1 = norm1_g[i].reshape(1, D_MODEL)
        g2 = norm2_g[i].reshape(1, D_MODEL)
        if i % 2 == 0:
            lambda_init = 0.8 - 0.6 * math.exp(-0.3 * i)
            gains = jnp.stack([jnp.tile(v[j], LANES // HEAD_DIM) for v in (a_q_norm, a_k_norm, b_q_norm, b_k_norm)])
            qa, k2a, v2a, qb, kb, vb = _attn_in(stream, mods, g1, attn_w_in[j].astype(BF16), gains, cos_t, sin_s, e_blk)
            ya = _win_attn(a_sink[j], qa, k2a, v2a)
            lpar = jnp.stack([b_lq1[j], b_lk1[j], b_lq2[j], b_lk2[j]])
            yb = _diff_attn(lpar, b_subln[j].reshape(1, LANES), qb, kb, vb, lambda_init)
            if last:
                ya, yb = ya[:, CTX_LEN:], yb[:, CTX_LEN:]
            ua, ub, wo = ya, yb, attn_w_out[j]
        else:
            src = np.arange(BATCH * S5_TOK).reshape(BATCH, S5_TOK).T.reshape(-1)
            perm = jnp.asarray(np.eye(BATCH * S5_TOK, dtype=np.float32)[src], BF16)
            h_all = stream[0] if stream[0].shape[1] == TT else jnp.concatenate(stream, axis=1)
            u, u_g = _s5_in(h_all, mods, g1, perm, s5_w_in[j].astype(BF16))
            ops_in = _s5_layout(s5_lambda_re[j], s5_lambda_im[j], s5_log_step[j], s5_b_re[j], s5_b_im[j],
                                s5_c_re[j], s5_c_im[j])
            m_op, q_op, n_op, a_vec = _s5_ops(*ops_in)
            y_g = _s5_scan(u_g, m_op, q_op, n_op, a_vec)
            assert last, "S5 layers before the last one would also need the context rows of the readout"
            gated = _s5_out(u, y_g, s5_d[j].reshape(1, D_MODEL), s5_glu_w[j].astype(BF16),
                            s5_glu_b[j].reshape(1, D_MODEL), perm.T)
            ua, ub, wo = gated, gated, s5_w_out[j]
        h = _mix_mlp(stream, ua, ub, mods, g2, wo.astype(BF16), mlp_w1[i].astype(BF16), mlp_w2[i].astype(BF16),
                     latent_only=last)
        stream = (h, h)
    return h
```

```python
import functools
import math

import jax
import jax.numpy as jnp
import numpy as np
from jax import lax
from jax.experimental import pallas as pl
from jax.experimental.pallas import tpu as pltpu

F32 = jnp.float32
BF16 = jnp.bfloat16

D_MODEL = 1024
BATCH = 8
SEQ = 2048
DEPTH = 2
GRID_W = 64
CTX_LEN = 256
HEAD_DIM = 64
WINDOW = 128
A_Q_HEADS = 8
A_KV_HEADS = 2
B_HEADS = 4
A_Q_W = A_Q_HEADS * HEAD_DIM
A_KV_W = A_KV_HEADS * HEAD_DIM
B_QK_W = B_HEADS * 2 * HEAD_DIM
B_V_W = B_HEADS * 2 * HEAD_DIM
ATTN_IN = A_Q_W + 2 * A_KV_W + 2 * B_QK_W + B_V_W
S5_GROUP = 16
S5_GROUPS = D_MODEL // S5_GROUP
S5_STATE = 64
D_FF = 4 * D_MODEL
ROPE_BASE = 10000.0
EPS = 1e-6
NEG_INF = -1e30
LOG2E = math.log2(math.e)

TT = CTX_LEN + SEQ
TM = 256
N_TILES = TT // TM
LANES = 128
S5_CHUNK = 16
S5_CW = S5_CHUNK * S5_GROUP
N_CHUNKS = TT // S5_CHUNK
N_CTX_CHUNKS = CTX_LEN // S5_CHUNK
VMEM_LIMIT = 56 * 1024 * 1024


def _dot(a, b):
    return jnp.dot(a, b, preferred_element_type=F32)


def _dot_nt(a, b):
    return lax.dot_general(a, b, (((1,), (1,)), ((), ())), preferred_element_type=F32)


def _rms(x):
    return x * lax.rsqrt(jnp.mean(x * x, axis=-1, keepdims=True) + EPS)


def _modnorm(x, g, shift, scale):
    return _rms(x) * g * (1.0 + scale) + shift


def _params(**kw):
    return pltpu.CompilerParams(vmem_limit_bytes=VMEM_LIMIT, **kw)


def _mod_kernel(s_ref, w_ref, b_ref, o_ref):
    s = s_ref[...]
    s = s * jax.nn.sigmoid(s)
    o_ref[...] = _dot(s.astype(BF16), w_ref[...].astype(BF16)) + b_ref[...]


def _modulation(s_rows, mod_w, mod_b):
    return pl.pallas_call(
        _mod_kernel,
        out_shape=jax.ShapeDtypeStruct((DEPTH, 16, 6 * D_MODEL), F32),
        grid=(DEPTH, 6),
        in_specs=[
            pl.BlockSpec((16, D_MODEL), lambda i, j: (0, 0)),
            pl.BlockSpec((None, D_MODEL, D_MODEL), lambda i, j: (i, 0, j)),
            pl.BlockSpec((None, 1, D_MODEL), lambda i, j: (i, 0, j)),
        ],
        out_specs=pl.BlockSpec((None, 16, D_MODEL), lambda i, j: (i, 0, j)),
        compiler_params=_params(),
        name="modulation",
    )(s_rows, mod_w, mod_b.reshape(DEPTH, 1, 6 * D_MODEL))


def _mod_spec():
    return pl.BlockSpec((None, None, 6, D_MODEL), lambda b, t: (jnp.minimum(t, 1), b, 0, 0))


def _stream_specs(stream, h_off):
    unified = stream[0].shape[1] == TT
    first_lat = 1 if unified else 0
    ctx_spec = pl.BlockSpec((None, TM, D_MODEL), lambda b, t: (b, 0, 0))
    lat_spec = pl.BlockSpec((None, TM, D_MODEL),
                            lambda b, t: (b, jnp.maximum(t + h_off - 1 + first_lat, first_lat), 0))
    return [ctx_spec, lat_spec]


def _stream_tile(hc_ref, hl_ref, h_off):
    if h_off > 0:
        return hl_ref[...]
    return jnp.where(pl.program_id(1) == 0, hc_ref[...], hl_ref[...])


def _attn_in_kernel(hc_ref, hl_ref, mod_ref, g_ref, w_ref, gain_ref, cos_ref, sin_ref, e_ref,
                    qa_ref, k2a_ref, v2a_ref, qb_ref, kb_ref, vb_ref):
    mod = mod_ref[...]
    a = _modnorm(_stream_tile(hc_ref, hl_ref, 0), g_ref[...], mod[0:1], mod[1:2]).astype(BF16)
    cos_t = cos_ref[...]
    sin_s = sin_ref[...]
    e = e_ref[...]
    gains = gain_ref[...]
    lane = lax.broadcasted_iota(jnp.int32, (TM, LANES), 1)
    first_half = (lane & (HEAD_DIM - 1)) < HEAD_DIM // 2
    lo = lane < HEAD_DIM

    def norm_rope(c, gain):
        ms = _dot((c * c).astype(BF16), e)
        cn = c * lax.rsqrt(ms + EPS) * gain
        r_fwd = pltpu.roll(cn, HEAD_DIM // 2, 1)
        r_bwd = pltpu.roll(cn, LANES - HEAD_DIM // 2, 1)
        return cn * cos_t + jnp.where(first_half, r_bwd, r_fwd) * sin_s

    def dup_halves(x, ref):
        sw = pltpu.roll(x, HEAD_DIM, 1)
        ref[:, 0:LANES] = jnp.where(lo, x, sw).astype(BF16)
        ref[:, LANES:2 * LANES] = jnp.where(lo, sw, x).astype(BF16)


    q_scale = HEAD_DIM ** -0.5 * LOG2E

    def finish_q(z, ref, gain):
        for c in range(z.shape[1] // LANES):
            ref[:, c * LANES:(c + 1) * LANES] = (
                norm_rope(z[:, c * LANES:(c + 1) * LANES], gain) * q_scale).astype(BF16)

    def finish_kv_a(z):
        dup_halves(norm_rope(z[:, 0:LANES], gains[1:2]), k2a_ref)
        v = z[:, LANES:2 * LANES]
        sw = pltpu.roll(v, HEAD_DIM, 1)
        for kvh, dup in enumerate((jnp.where(lo, v, sw), jnp.where(lo, sw, v))):
            v2a_ref[:, 2 * kvh * LANES:(2 * kvh + 1) * LANES] = dup.astype(BF16)
            v2a_ref[:, (2 * kvh + 1) * LANES:(2 * kvh + 2) * LANES] = jnp.ones((TM, LANES), BF16)

    def finish_kb(z):
        for c in range(B_QK_W // LANES):
            kb_ref[:, c * LANES:(c + 1) * LANES] = norm_rope(z[:, c * LANES:(c + 1) * LANES], gains[3:4]).astype(BF16)

    def finish_vb(z):
        for hd in range(B_HEADS):
            vb_ref[:, 2 * hd * LANES:(2 * hd + 1) * LANES] = z[:, hd * LANES:(hd + 1) * LANES].astype(BF16)
            vb_ref[:, (2 * hd + 1) * LANES:(2 * hd + 2) * LANES] = jnp.ones((TM, LANES), BF16)

    segments = [(A_Q_W, lambda z: finish_q(z, qa_ref, gains[0:1])), (2 * A_KV_W, finish_kv_a),
                (B_QK_W, lambda z: finish_q(z, qb_ref, gains[2:3])), (B_QK_W, finish_kb), (B_V_W, finish_vb)]
    off = 0
    pending = None
    for width, finish in segments:
        z = _dot(a, w_ref[:, off:off + width])
        off += width
        if pending is not None:
            pending[1](pending[0])
        pending = (z, finish)
    pending[1](pending[0])


def _attn_in(stream, mods, g, w_in, gains, cos_t, sin_s, e_blk):
    def tok(width):
        return pl.BlockSpec((None, TM, width), lambda b, t: (b, t, 0))

    def full(shape):
        return pl.BlockSpec(shape, lambda b, t: (0,) * len(shape))

    out_shapes = [jax.ShapeDtypeStruct((BATCH, TT, w), BF16)
                  for w in (A_Q_W, 2 * A_KV_W, 4 * A_KV_W, B_QK_W, B_QK_W, 2 * B_V_W)]
    return pl.pallas_call(
        _attn_in_kernel,
        out_shape=out_shapes,
        grid=(BATCH, N_TILES),
        in_specs=[
            *_stream_specs(stream, 0), _mod_spec(), full((1, D_MODEL)), full((D_MODEL, ATTN_IN)), full((4, LANES)),
            pl.BlockSpec((TM, LANES), lambda b, t: (t, 0)), pl.BlockSpec((TM, LANES), lambda b, t: (t, 0)),
            full((LANES, LANES)),
        ],
        out_specs=[tok(A_Q_W), tok(2 * A_KV_W), tok(4 * A_KV_W), tok(B_QK_W), tok(B_QK_W), tok(2 * B_V_W)],
        compiler_params=_params(),
        name="attn_in_proj",
    )(*stream, mods, g, w_in, gains, cos_t, sin_s, e_blk)


QB = 128


def _win_attn_kernel(sink_ref, q_ref, k2_ref, v2_ref, o_ref):
    t = pl.program_id(1)
    lane = lax.broadcasted_iota(jnp.int32, (QB, LANES), 1)
    lo = lane < HEAD_DIM
    rows = 4 * QB
    row = lax.broadcasted_iota(jnp.int32, (rows, 3 * QB), 0)
    col = lax.broadcasted_iota(jnp.int32, (rows, 3 * QB), 1)
    row_head = lax.broadcasted_iota(jnp.int32, (rows, 1), 0) // QB
    zero = jnp.zeros((QB, LANES), BF16)
    blocks = [(qb, g) for qb in range(TM // QB) for g in range(A_KV_HEADS)]

    def window_start(qb):
        n = (t - 1) * (TM // QB) + qb
        ws = jnp.clip((n - 1) * QB, 0, SEQ - 3 * QB)
        return n, ws

    def scores(qb, g, with_window):
        pieces = []
        for p in range(2):
            qp = q_ref[qb * QB:(qb + 1) * QB, g * 2 * LANES + p * LANES: g * 2 * LANES + (p + 1) * LANES]
            pieces.append(jnp.where(lo, qp, zero))
            pieces.append(jnp.where(lo, zero, qp))
        qs = jnp.concatenate(pieces, axis=0)
        s_c = _dot_nt(qs, k2_ref[0:CTX_LEN, g * LANES:(g + 1) * LANES])
        if not with_window:
            return s_c, None
        n, ws = window_start(qb)
        kw = k2_ref[pl.ds(pl.multiple_of(ws + CTX_LEN, QB), 3 * QB), g * LANES:(g + 1) * LANES]
        valid = jnp.abs(n * QB + (row & (QB - 1)) - (ws + col)) <= WINDOW
        return s_c, jnp.where(valid, _dot_nt(qs, kw), NEG_INF)

    def finish(qb, g, s_c, s_w):
        sk = jnp.full((rows, 1), sink_ref[4 * g + 3], F32)
        for hh in range(3):
            sk = jnp.where(row_head == hh, sink_ref[4 * g + hh], sk)
        sk = sk * LOG2E
        m = jnp.maximum(jnp.max(s_c, axis=-1, keepdims=True), sk)
        if s_w is not None:
            m = jnp.maximum(m, jnp.max(s_w, axis=-1, keepdims=True))
        vcols = slice(2 * g * LANES, (2 * g + 2) * LANES)
        pv = _dot(jnp.exp2(s_c - m).astype(BF16), v2_ref[0:CTX_LEN, vcols])
        if s_w is not None:
            _, ws = window_start(qb)
            vw = v2_ref[pl.ds(pl.multiple_of(ws + CTX_LEN, QB), 3 * QB), vcols]
            pv = pv + _dot(jnp.exp2(s_w - m).astype(BF16), vw)
        o = pv[:, 0:LANES] / (pv[:, LANES:2 * LANES] + jnp.exp2(sk - m))
        for p in range(2):
            o_ref[qb * QB:(qb + 1) * QB, g * 2 * LANES + p * LANES: g * 2 * LANES + (p + 1) * LANES] = jnp.where(
                lo, o[2 * p * QB:(2 * p + 1) * QB], o[(2 * p + 1) * QB:(2 * p + 2) * QB]).astype(BF16)

    def attend(with_window):
        s_next = scores(*blocks[0], with_window)
        for i, blk in enumerate(blocks):
            s_cur = s_next
            if i + 1 < len(blocks):
                s_next = scores(*blocks[i + 1], with_window)
            finish(*blk, *s_cur)

    @pl.when(t == 0)
    def _():
        attend(False)

    @pl.when(t > 0)
    def _():
        attend(True)


def _win_attn(sink, qa, k2a, v2a):
    return pl.pallas_call(
        _win_attn_kernel,
        out_shape=jax.ShapeDtypeStruct((BATCH, TT, A_Q_W), BF16),
        grid=(BATCH, N_TILES),
        in_specs=[
            pl.BlockSpec(memory_space=pltpu.SMEM),
            pl.BlockSpec((None, TM, A_Q_W), lambda b, t: (b, t, 0)),
            pl.BlockSpec((None, TT, 2 * A_KV_W), lambda b, t: (b, 0, 0)),
            pl.BlockSpec((None, TT, 4 * A_KV_W), lambda b, t: (b, 0, 0)),
        ],
        out_specs=pl.BlockSpec((None, TM, A_Q_W), lambda b, t: (b, t, 0)),
        compiler_params=_params(),
        name="window_attention",
    )(sink, qa, k2a, v2a)


DIFF_ROWS = 128


def _diff_attn_kernel(lpar_ref, subln_ref, q_ref, k_ref, v_ref, o_ref, *, lambda_init):
    t = pl.program_id(1)
    lp = lpar_ref[...]
    lam = (jnp.exp(jnp.sum(lp[0:1] * lp[1:2], axis=-1, keepdims=True))
           - jnp.exp(jnp.sum(lp[2:3] * lp[3:4], axis=-1, keepdims=True)) + lambda_init)
    R = DIFF_ROWS
    lane = lax.broadcasted_iota(jnp.int32, (R, LANES), 1)
    lo = lane < HEAD_DIM
    zero = jnp.zeros((R, LANES), BF16)

    def attend(n_keys):
        blocks = [(slice(rb * R, (rb + 1) * R), slice(h * LANES, (h + 1) * LANES))
                  for h in range(B_HEADS) for rb in range(TM // R)]

        def scores(rows, cols):
            q = q_ref[rows, cols]
            qs = jnp.concatenate([jnp.where(lo, q, zero), jnp.where(lo, zero, q)], axis=0)
            return _dot_nt(qs, k_ref[0:n_keys, cols])

        def finish(rows, cols, s):
            p = jnp.exp2(s - jnp.max(s, axis=-1, keepdims=True)).astype(BF16)
            vcols = slice(2 * cols.start, 2 * cols.stop)
            pv = _dot(p, v_ref[0:n_keys, vcols])
            sm = pv[:, 0:LANES] / pv[:, LANES:2 * LANES]
            y = sm[0:R] - lam * sm[R:2 * R]
            o_ref[rows, cols] = (_rms(y) * subln_ref[...] * (1.0 - lambda_init)).astype(BF16)

        s_next = scores(*blocks[0])
        for i, blk in enumerate(blocks):
            s_cur = s_next
            if i + 1 < len(blocks):
                s_next = scores(*blocks[i + 1])
            finish(*blk, s_cur)

    @pl.when(t == 0)
    def _():
        attend(CTX_LEN)

    @pl.when(t > 0)
    def _():
        attend(TT)


def _diff_attn(lpar, subln, qb, kb, vb, lambda_init):
    return pl.pallas_call(
        functools.partial(_diff_attn_kernel, lambda_init=lambda_init),
        out_shape=jax.ShapeDtypeStruct((BATCH, TT, B_V_W), BF16),
        grid=(BATCH, N_TILES),
        in_specs=[
            pl.BlockSpec((4, HEAD_DIM), lambda b, t: (0, 0)),
            pl.BlockSpec((1, LANES), lambda b, t: (0, 0)),
            pl.BlockSpec((None, TM, B_QK_W), lambda b, t: (b, t, 0)),
            pl.BlockSpec((None, TT, B_QK_W), lambda b, t: (b, 0, 0)),
            pl.BlockSpec((None, TT, 2 * B_V_W), lambda b, t: (b, 0, 0)),
        ],
        out_specs=pl.BlockSpec((None, TM, B_V_W), lambda b, t: (b, t, 0)),
        compiler_params=_params(),
        name="diff_attention",
    )(lpar, subln, qb, kb, vb)


FF_CHUNK = 1024


def _mix_mlp_kernel(*refs, h_off, n_sub):
    hc_ref = refs[0]
    hl_refs, ua_refs, ub_refs, mod_refs = (refs[1 + i * n_sub:1 + (i + 1) * n_sub] for i in range(4))
    g_ref, wo_ref, w1_ref, w2_ref, o_ref = refs[1 + 4 * n_sub:]
    half = D_MODEL // 2

    def prologue(k):
        mod = mod_refs[k][...]
        y = _dot(ua_refs[k][...], wo_ref[0:half, :]) + _dot(ub_refs[k][...], wo_ref[half:D_MODEL, :])
        x = hl_refs[k][...]
        if h_off == 0 and k == 0:
            x = jnp.where(pl.program_id(1) == 0, hc_ref[...], x)
        h1 = x + mod[2:3] * y
        f = _modnorm(h1, g_ref[...], mod[3:4], mod[4:5]).astype(BF16)
        return h1, f, mod[5:6]

    def mlp(k, h1, f, gate):
        acc = jnp.zeros((TM, D_MODEL), F32)
        for c in range(D_FF // FF_CHUNK):
            hid = jnp.maximum(_dot(f, w1_ref[:, c * FF_CHUNK:(c + 1) * FF_CHUNK]), 0.0)
            acc = acc + _dot((hid * hid).astype(BF16), w2_ref[c * FF_CHUNK:(c + 1) * FF_CHUNK, :])
        o_ref[k * TM:(k + 1) * TM, :] = h1 + gate * acc

    nxt = prologue(0)
    for k in range(n_sub):
        cur = nxt
        if k + 1 < n_sub:
            nxt = prologue(k + 1)
        mlp(k, *cur)


def _mix_mlp(stream, ua, ub, mods, g, wo, w1, w2, *, latent_only):
    n_tiles = SEQ // TM if latent_only else N_TILES
    h_off = N_TILES - n_tiles
    n_sub = 4 if n_tiles % 4 == 0 else 3
    half = D_MODEL // 2
    ub_col = 1 if ub.shape[-1] == D_MODEL else 0
    first_lat = 1 if stream[0].shape[1] == TT else 0

    def full(shape):
        return pl.BlockSpec(shape, lambda b, t: (0,) * len(shape), pipeline_mode=pl.Buffered(1))

    def sub_specs(make):
        return [make(k) for k in range(n_sub)]

    def tile(t, k):
        return t * n_sub + k

    return pl.pallas_call(
        functools.partial(_mix_mlp_kernel, h_off=h_off, n_sub=n_sub),
        out_shape=jax.ShapeDtypeStruct((BATCH, n_tiles * TM, D_MODEL), F32),
        grid=(BATCH, n_tiles // n_sub),
        in_specs=[
            pl.BlockSpec((None, TM, D_MODEL), lambda b, t: (b, 0, 0)),
            *sub_specs(lambda k: pl.BlockSpec(
                (None, TM, D_MODEL),
                lambda b, t: (b, jnp.maximum(tile(t, k) + h_off - 1 + first_lat, first_lat), 0))),
            *sub_specs(lambda k: pl.BlockSpec((None, TM, half), lambda b, t: (b, tile(t, k), 0))),
            *sub_specs(lambda k: pl.BlockSpec((None, TM, half), lambda b, t: (b, tile(t, k), ub_col))),
            *sub_specs(lambda k: pl.BlockSpec(
                (None, None, 6, D_MODEL), lambda b, t: (jnp.minimum(tile(t, k) + h_off, 1), b, 0, 0))),
            pl.BlockSpec((1, D_MODEL), lambda b, t: (0, 0)),
            full((D_MODEL, D_MODEL)), full((D_MODEL, D_FF)), full((D_FF, D_MODEL)),
        ],
        out_specs=pl.BlockSpec((None, n_sub * TM, D_MODEL), lambda b, t: (b, t, 0)),
        compiler_params=_params(),
        name="mixer_out_mlp",
    )(stream[0], *[stream[1]] * n_sub, *[ua] * n_sub, *[ub] * n_sub, *[mods] * n_sub, g, wo, w1, w2)


S5_PAIR = 2
S5_TOK = S5_PAIR * S5_CHUNK
S5_STEPS = N_CHUNKS // S5_PAIR
GROUPS_PER_VREG = LANES // S5_GROUP
STEPS_PER_VREG = LANES // S5_GROUP
S5_SEG = 2 * LANES


def _lane_block():
    return lax.broadcasted_iota(jnp.int32, (BATCH, LANES), 1) // S5_GROUP


def _block_transpose(xs, lane_blk):
    xs = list(xs)
    n = len(xs)
    d = n // 2
    while d:
        low = (lane_blk & d) == 0
        for i in range(n):
            if i & d:
                continue
            a, b = xs[i], xs[i + d]
            xs[i] = jnp.where(low, a, pltpu.roll(b, S5_GROUP * d, 1))
            xs[i + d] = jnp.where(low, pltpu.roll(a, LANES - S5_GROUP * d, 1), b)
        d //= 2
    return xs


def _s5_in_kernel(h_ref, mod_ref, g_ref, perm_ref, w_ref, u_ref, z_ref, u_scr):
    mod = mod_ref[...]
    x = h_ref[...]
    a = _rms(x) * g_ref[...] * (1.0 + mod[:, 1:2, :]) + mod[:, 0:1, :]
    a = a.reshape(BATCH * S5_TOK, D_MODEL).astype(BF16)
    a = _dot(perm_ref[...], a).astype(BF16)
    lane_blk = _lane_block()

    def regroup(k):
        for hh in range(S5_CHUNK // STEPS_PER_VREG):
            halves = []
            for c2 in range(S5_PAIR):
                tok0 = c2 * S5_CHUNK + hh * STEPS_PER_VREG
                steps = [u_scr[k, (tok0 + m) * BATCH:(tok0 + m + 1) * BATCH, :] for m in range(STEPS_PER_VREG)]
                halves.append(_block_transpose(steps, lane_blk))
            for r in range(GROUPS_PER_VREG):
                z_ref[k * GROUPS_PER_VREG + r, :, hh * LANES:(hh + 1) * LANES] = (
                    jnp.concatenate([h[r] for h in halves], axis=0).astype(BF16))

    def finish(s, u_seg):
        u_ref[:, :, s * S5_SEG:(s + 1) * S5_SEG] = u_seg.reshape(S5_TOK, BATCH, S5_SEG)
        for kk in range(S5_SEG // LANES):
            k = s * (S5_SEG // LANES) + kk
            u_scr[k] = u_seg[:, kk * LANES:(kk + 1) * LANES]
            regroup(k)

    pending = None
    for s in range(D_MODEL // S5_SEG):
        u_seg = _dot(a, w_ref[:, s * S5_SEG:(s + 1) * S5_SEG])
        if pending is not None:
            finish(*pending)
        pending = (s, u_seg)
    finish(*pending)


def _s5_in(h, mods, g, perm, w_in):
    ctx_steps = N_CTX_CHUNKS // S5_PAIR
    n_rows = BATCH * S5_TOK
    return pl.pallas_call(
        _s5_in_kernel,
        out_shape=[jax.ShapeDtypeStruct((TT, BATCH, D_MODEL), F32),
                   jax.ShapeDtypeStruct((S5_GROUPS, ROWS, S5_CW), BF16)],
        grid=(S5_STEPS,),
        in_specs=[pl.BlockSpec((BATCH, S5_TOK, D_MODEL), lambda p: (0, p, 0)),
                  pl.BlockSpec((None, BATCH, 6, D_MODEL), lambda p: (jnp.minimum(p // ctx_steps, 1), 0, 0, 0)),
                  pl.BlockSpec((1, D_MODEL), lambda p: (0, 0)),
                  pl.BlockSpec((n_rows, n_rows), lambda p: (0, 0)),
                  pl.BlockSpec((D_MODEL, D_MODEL), lambda p: (0, 0))],
        out_specs=[pl.BlockSpec((S5_TOK, BATCH, D_MODEL), lambda p: (p, 0, 0)),
                   pl.BlockSpec((S5_GROUPS, S5_PAIR * BATCH, S5_CW), lambda p: (0, p, 0))],
        scratch_shapes=[pltpu.VMEM((D_MODEL // LANES, n_rows, LANES), F32)],
        compiler_params=_params(),
        name="s5_in_proj",
    )(h, mods, g, perm, w_in)


OPS_G = 4


def _s5_ops_kernel(*refs):
    for j in range(OPS_G):
        _s5_group_ops(*(r.at[j] for r in refs))


def _s5_group_ops(lr_ref, li_ref, ls_ref, btr_ref, bti_ref, cr_ref, ci_ref, m_ref, q_ref, n_ref, a_ref):
    P2 = 2 * S5_STATE

    def cmul(xr, xi, yr, yi):
        return xr * yr - xi * yi, xr * yi + xi * yr

    lr, li = lr_ref[...], li_ref[...]
    dt = jnp.exp(ls_ref[...])
    mag = jnp.exp(lr * dt)
    ar = mag * jnp.cos(li * dt)
    ai = mag * jnp.sin(li * dt)
    den = lr * lr + li * li
    nr = ar - 1.0
    f_re = (nr * lr + ai * li) / den
    f_im = (ai * lr - nr * li) / den
    bt_re = f_re * btr_ref[...] - f_im * bti_ref[...]
    bt_im = f_re * bti_ref[...] + f_im * btr_ref[...]

    squares = [(ar, ai)]
    while 2 ** len(squares) < S5_CHUNK:
        squares.append(cmul(*squares[-1], *squares[-1]))
    blk = lax.broadcasted_iota(jnp.int32, (S5_CW, P2), 0) // S5_GROUP
    is_fwd = lax.broadcasted_iota(jnp.int32, (S5_CW, P2), 1) < S5_STATE
    e_k = jnp.where(is_fwd, blk, (S5_CHUNK - 1) - blk)

    def a_pow(e):
        pr = jnp.ones((S5_CW, P2), F32)
        pi = jnp.zeros((S5_CW, P2), F32)
        for b, (br, bi) in enumerate(squares):
            bit = ((e >> b) & 1) == 1
            tr, ti = cmul(pr, pi, br, bi)
            pr = jnp.where(bit, tr, pr)
            pi = jnp.where(bit, ti, pi)
        return pr, pi

    def tile_rows(x):
        return jnp.concatenate([x] * S5_CHUNK, axis=0)

    cp_re, cp_im = cmul(tile_rows(cr_ref[...]), tile_rows(ci_ref[...]), *a_pow(e_k))
    lane_fwd = lax.broadcasted_iota(jnp.int32, (S5_GROUP, P2), 1) < S5_STATE

    def lag_kernels(keep):
        br = jnp.where(keep, bt_re, 0.0)
        bi = jnp.where(keep, bt_im, 0.0)
        dims = (((1,), (1,)), ((), ()))
        hi = lax.Precision.HIGHEST
        return (lax.dot_general(br, cp_re, dims, precision=hi, preferred_element_type=F32)
                - lax.dot_general(bi, cp_im, dims, precision=hi, preferred_element_type=F32))

    kt_f = lag_kernels(lane_fwd)
    kt_b = lag_kernels(jnp.logical_not(lane_fwd))
    lane_w = lax.broadcasted_iota(jnp.int32, (S5_GROUP, S5_CW), 1)
    for s in range(S5_CHUNK):
        f_part = kt_f if s == 0 else jnp.where(lane_w >= S5_GROUP * s, pltpu.roll(kt_f, S5_GROUP * s, 1), 0.0)
        sh = (S5_GROUP * (s + 1)) % S5_CW
        b_roll = kt_b if sh == 0 else pltpu.roll(kt_b, sh, 1)
        b_part = jnp.where(lane_w < S5_GROUP * (s + 1), b_roll, 0.0)
        m_ref[s * S5_GROUP:(s + 1) * S5_GROUP, :] = (f_part + b_part).astype(BF16)

    q_re, q_im = cmul(*a_pow((S5_CHUNK - 1) - e_k), tile_rows(bt_re), tile_rows(bt_im))
    q_ref[:, 0:P2] = q_re.astype(BF16)
    q_ref[:, P2:2 * P2] = q_im.astype(BF16)

    n_re, n_im = cmul(cp_re, cp_im, ar, ai)
    n_ref[:, 0:P2] = n_re.astype(BF16)
    n_ref[:, P2:2 * P2] = (-n_im).astype(BF16)

    a_re, a_im = ar, ai
    for _ in range(S5_CHUNK.bit_length() - 1):
        a_re, a_im = cmul(a_re, a_im, a_re, a_im)
    a_ref[0:1, :] = a_re
    a_ref[1:2, :] = a_im


def _s5_ops(row_params, bt, c_nat):
    P2 = 2 * S5_STATE
    row = pl.BlockSpec((OPS_G, 1, P2), lambda g: (g, 0, 0))
    mat = pl.BlockSpec((OPS_G, S5_GROUP, P2), lambda g: (g, 0, 0))
    sq = pl.BlockSpec((OPS_G, S5_CW, S5_CW), lambda g: (g, 0, 0))
    return pl.pallas_call(
        _s5_ops_kernel,
        out_shape=[jax.ShapeDtypeStruct((S5_GROUPS, S5_CW, S5_CW), BF16)] * 3
        + [jax.ShapeDtypeStruct((S5_GROUPS, 2, P2), F32)],
        grid=(S5_GROUPS // OPS_G,),
        in_specs=[row, row, row, mat, mat, mat, mat],
        out_specs=[sq, sq, sq, pl.BlockSpec((OPS_G, 2, P2), lambda g: (g, 0, 0))],
        compiler_params=_params(),
        name="s5_chunk_operators",
    )(*row_params, *bt, *c_nat)


ROWS = N_CHUNKS * BATCH


SCAN_G = 4


def _s5_scan_kernel(u_ref, m_ref, q_ref, n_ref, a_ref, y_ref, pu_ref, sp_ref):
    S = S5_STATE
    lane = lax.broadcasted_iota(jnp.int32, (BATCH, 2 * S), 1)
    lo = lane < S
    for j in range(SCAN_G):
        pu_ref[j] = _dot(u_ref[j], q_ref[j])
    for j in range(SCAN_G):
        y_ref[j] = _dot(u_ref[j], m_ref[j])
    decay = [(a_ref[j, 0:1, :], a_ref[j, 1:2, :]) for j in range(SCAN_G)]
    zero = jnp.zeros((BATCH, 2 * S), F32)
    state = [(zero, zero)] * SCAN_G
    for k in range(N_CHUNKS):
        cf = k * BATCH
        cb = (N_CTX_CHUNKS - 1 - k if k < N_CTX_CHUNKS else N_CHUNKS + N_CTX_CHUNKS - 1 - k) * BATCH
        for j in range(SCAN_G):
            s_re, s_im = state[j]
            a_re, a_im = decay[j]
            sp_ref[j, cf:cf + BATCH, 0:S] = s_re[:, 0:S]
            sp_ref[j, cb:cb + BATCH, S:2 * S] = s_re[:, S:2 * S]
            sp_ref[j, cf:cf + BATCH, 2 * S:3 * S] = s_im[:, 0:S]
            sp_ref[j, cb:cb + BATCH, 3 * S:4 * S] = s_im[:, S:2 * S]
            x_re = jnp.where(lo, pu_ref[j, cf:cf + BATCH, 0:2 * S], pu_ref[j, cb:cb + BATCH, 0:2 * S])
            x_im = jnp.where(lo, pu_ref[j, cf:cf + BATCH, 2 * S:4 * S], pu_ref[j, cb:cb + BATCH, 2 * S:4 * S])
            state[j] = (a_re * s_re - a_im * s_im + x_re, a_re * s_im + a_im * s_re + x_im)
    for j in range(SCAN_G):
        y_ref[j] += _dot_nt(sp_ref[j].astype(BF16), n_ref[j])


def _s5_scan(u_g, m_op, q_op, n_op, a_vec):
    sq = pl.BlockSpec((SCAN_G, S5_CW, S5_CW), lambda g: (g, 0, 0))
    rows = pl.BlockSpec((SCAN_G, ROWS, S5_CW), lambda g: (g, 0, 0))
    return pl.pallas_call(
        _s5_scan_kernel,
        out_shape=jax.ShapeDtypeStruct((S5_GROUPS, ROWS, S5_CW), F32),
        grid=(S5_GROUPS // SCAN_G,),
        in_specs=[rows, sq, sq, sq, pl.BlockSpec((SCAN_G, 2, 2 * S5_STATE), lambda g: (g, 0, 0))],
        out_specs=rows,
        scratch_shapes=[pltpu.VMEM((SCAN_G, ROWS, S5_CW), F32), pltpu.VMEM((SCAN_G, ROWS, S5_CW), F32)],
        compiler_params=_params(),
        name="s5_scan",
    )(u_g, m_op, q_op, n_op, a_vec)


def _s5_out_kernel(u_ref, y_ref, d_ref, gw_ref, gb_ref, perm_ref, o_ref, ys_scr):
    lane_blk = _lane_block()

    def regroup(k):
        for c2 in range(S5_PAIR):
            for hh in range(S5_CHUNK // STEPS_PER_VREG):
                groups = [y_ref[k * GROUPS_PER_VREG + r, c2 * BATCH:(c2 + 1) * BATCH, hh * LANES:(hh + 1) * LANES]
                          for r in range(GROUPS_PER_VREG)]
                steps = _block_transpose(groups, lane_blk)
                for m in range(STEPS_PER_VREG):
                    tok = c2 * S5_CHUNK + hh * STEPS_PER_VREG + m
                    ys_scr[k, tok * BATCH:(tok + 1) * BATCH, :] = steps[m]

    n_seg = D_MODEL // S5_SEG
    per_seg = S5_SEG // LANES
    for kk in range(per_seg):
        regroup(kk)
    g_parts = []
    pre = None
    for s in range(n_seg):
        if s + 1 < n_seg:
            for kk in range(per_seg):
                regroup((s + 1) * per_seg + kk)
        cols = slice(s * S5_SEG, (s + 1) * S5_SEG)
        ys = jnp.concatenate([ys_scr[s * per_seg + kk] for kk in range(per_seg)], axis=1)
        y = u_ref[:, :, cols].reshape(BATCH * S5_TOK, S5_SEG) * d_ref[:, cols] + ys
        g_seg = jax.nn.gelu(y)
        g_parts.append(g_seg)
        part = _dot(g_seg.astype(BF16), gw_ref[cols, :])
        pre = part if pre is None else pre + part
    gate = jax.nn.sigmoid(pre + gb_ref[...])
    g = jnp.concatenate(g_parts, axis=1)
    gated = (g * gate).astype(BF16)
    gated = _dot(perm_ref[...], gated).astype(BF16)
    o_ref[...] = gated.reshape(BATCH, S5_TOK, D_MODEL)


def _s5_out(u, y_g, d_skip, glu_w, glu_b, perm_t):
    ctx_steps = N_CTX_CHUNKS // S5_PAIR
    n_rows = BATCH * S5_TOK
    vec = pl.BlockSpec((1, D_MODEL), lambda p: (0, 0))
    return pl.pallas_call(
        _s5_out_kernel,
        out_shape=jax.ShapeDtypeStruct((BATCH, SEQ, D_MODEL), BF16),
        grid=(S5_STEPS - ctx_steps,),
        in_specs=[pl.BlockSpec((S5_TOK, BATCH, D_MODEL), lambda p: (p + ctx_steps, 0, 0)),
                  pl.BlockSpec((S5_GROUPS, S5_PAIR * BATCH, S5_CW), lambda p: (0, p + ctx_steps, 0)),
                  vec, pl.BlockSpec((D_MODEL, D_MODEL), lambda p: (0, 0)), vec,
                  pl.BlockSpec((n_rows, n_rows), lambda p: (0, 0))],
        out_specs=pl.BlockSpec((BATCH, S5_TOK, D_MODEL), lambda p: (0, p, 0)),
        scratch_shapes=[pltpu.VMEM((D_MODEL // LANES, n_rows, LANES), F32)],
        compiler_params=_params(),
        name="s5_gelu_glu",
    )(u, y_g, d_skip, glu_w, glu_b, perm_t)


def _rope_tables():
    rows_n = SEQ // GRID_W
    row = np.repeat(np.arange(rows_n, dtype=np.float64), GRID_W)
    col = np.tile(np.arange(GRID_W, dtype=np.float64), rows_n)
    n_freq = HEAD_DIM // 4
    inv = ROPE_BASE ** (-np.arange(n_freq, dtype=np.float64) / n_freq)
    ang = np.concatenate([row[:, None] * inv, col[:, None] * inv], axis=-1)
    reps = LANES // (HEAD_DIM // 2)
    cos_t = np.tile(np.cos(ang), (1, reps))
    sin_t = np.tile(np.sin(ang), (1, reps))
    sign = np.where((np.arange(LANES) % HEAD_DIM) < HEAD_DIM // 2, -1.0, 1.0)
    cos_t = np.concatenate([np.ones((CTX_LEN, LANES)), cos_t], axis=0)
    sin_s = np.concatenate([np.zeros((CTX_LEN, LANES)), sin_t * sign], axis=0)
    return jnp.asarray(cos_t, F32), jnp.asarray(sin_s, F32)


def _fb_rows(x):
    return jnp.transpose(x, (1, 0, 2)).reshape(S5_GROUPS, 1, 2 * S5_STATE)


def _s5_layout(lam_re, lam_im, log_step, b_re, b_im, c_re, c_im):
    ls = jnp.broadcast_to(log_step[:, :, None], lam_re.shape)
    rows = [_fb_rows(v) for v in (lam_re, lam_im, ls)]

    def bt_of(b):
        return jnp.transpose(b, (1, 3, 0, 2)).reshape(S5_GROUPS, S5_GROUP, 2 * S5_STATE)

    def c_of(c):
        return jnp.transpose(c, (1, 2, 0, 3)).reshape(S5_GROUPS, S5_GROUP, 2 * S5_STATE)

    return rows, [bt_of(b_re), bt_of(b_im)], [c_of(c_re), c_of(c_im)]


def kernel(x, c, ctx, c_ctx, norm1_g, norm2_g, mod_w, mod_b, mlp_w1, mlp_w2, attn_w_in, attn_w_out, a_q_norm, a_k_norm, a_sink, b_q_norm, b_k_norm, b_lq1, b_lk1, b_lq2, b_lk2, b_subln, s5_w_in, s5_lambda_re, s5_lambda_im, s5_log_step, s5_b_re, s5_b_im, s5_c_re, s5_c_im, s5_d, s5_glu_w, s5_glu_b, s5_w_out):
    assert x.shape == (BATCH, SEQ, D_MODEL) and ctx.shape == (BATCH, CTX_LEN, D_MODEL)
    stream = (ctx, x)
    s_rows = jnp.concatenate([c, c_ctx[None], jnp.zeros((16 - BATCH - 1, D_MODEL), F32)], axis=0)
    m_all = _modulation(s_rows, mod_w, mod_b)
    cos_t, sin_s = _rope_tables()
    e_blk = jnp.asarray(np.kron(np.eye(LANES // HEAD_DIM), np.ones((HEAD_DIM, HEAD_DIM))) / HEAD_DIM, BF16)

    for i in range(DEPTH):
        last = i == DEPTH - 1
        j = i // 2
        m_lat = m_all[i, :BATCH].reshape(BATCH, 6, D_MODEL)
        m_ctx = jnp.broadcast_to(m_all[i, BATCH].reshape(1, 6, D_MODEL), (BATCH, 6, D_MODEL))
        mods = jnp.stack([m_ctx, m_lat])
        g1 = norm1_g[i].reshape(1, D_MODEL)
        g2 = norm2_g[i].reshape(1, D_MODEL)
        if i % 2 == 0:
            lambda_init = 0.8 - 0.6 * math.exp(-0.3 * i)
            gains = jnp.stack([jnp.tile(v[j], LANES // HEAD_DIM) for v in (a_q_norm, a_k_norm, b_q_norm, b_k_norm)])
            qa, k2a, v2a, qb, kb, vb = _attn_in(stream, mods, g1, attn_w_in[j].astype(BF16), gains, cos_t, sin_s, e_blk)
            ya = _win_attn(a_sink[j], qa, k2a, v2a)
            lpar = jnp.stack([b_lq1[j], b_lk1[j], b_lq2[j], b_lk2[j]])
            yb = _diff_attn(lpar, b_subln[j].reshape(1, LANES), qb, kb, vb, lambda_init)
            if last:
                ya, yb = ya[:, CTX_LEN:], yb[:, CTX_LEN:]
            ua, ub, wo = ya, yb, attn_w_out[j]
        else:
            src = np.arange(BATCH * S5_TOK).reshape(BATCH, S5_TOK).T.reshape(-1)
            perm = jnp.asarray(np.eye(BATCH * S5_TOK, dtype=np.float32)[src], BF16)
            h_all = stream[0] if stream[0].shape[1] == TT else jnp.concatenate(stream, axis=1)
            u, u_g = _s5_in(h_all, mods, g1, perm, s5_w_in[j].astype(BF16))
            ops_in = _s5_layout(s5_lambda_re[j], s5_lambda_im[j], s5_log_step[j], s5_b_re[j], s5_b_im[j],
                                s5_c_re[j], s5_c_im[j])
            m_op, q_op, n_op, a_vec = _s5_ops(*ops_in)
            y_g = _s5_scan(u_g, m_op, q_op, n_op, a_vec)
            assert last, "S5 layers before the last one would also need the context rows of the readout"
            gated = _s5_out(u, y_g, s5_d[j].reshape(1, D_MODEL), s5_glu_w[j].astype(BF16),
                            s5_glu_b[j].reshape(1, D_MODEL), perm.T)
            ua, ub, wo = gated, gated, s5_w_out[j]
        h = _mix_mlp(stream, ua, ub, mods, g2, wo.astype(BF16), mlp_w1[i].astype(BF16), mlp_w2[i].astype(BF16),
                     latent_only=last)
        stream = (h, h)
    return h
```

```python
import functools
import math

import jax
import jax.numpy as jnp
import numpy as np
from jax import lax
from jax.experimental import pallas as pl
from jax.experimental.pallas import tpu as pltpu

F32 = jnp.float32
BF16 = jnp.bfloat16

D_MODEL = 1024
BATCH = 8
SEQ = 2048
DEPTH = 2
GRID_W = 64
CTX_LEN = 256
HEAD_DIM = 64
WINDOW = 128
A_Q_HEADS = 8
A_KV_HEADS = 2
B_HEADS = 4
A_Q_W = A_Q_HEADS * HEAD_DIM
A_KV_W = A_KV_HEADS * HEAD_DIM
B_QK_W = B_HEADS * 2 * HEAD_DIM
B_V_W = B_HEADS * 2 * HEAD_DIM
ATTN_IN = A_Q_W + 2 * A_KV_W + 2 * B_QK_W + B_V_W
S5_GROUP = 16
S5_GROUPS = D_MODEL // S5_GROUP
S5_STATE = 64
D_FF = 4 * D_MODEL
ROPE_BASE = 10000.0
EPS = 1e-6
NEG_INF = -1e30
LOG2E = math.log2(math.e)

TT = CTX_LEN + SEQ
TM = 256
N_TILES = TT // TM
LANES = 128
S5_CHUNK = 16
S5_CW = S5_CHUNK * S5_GROUP
N_CHUNKS = TT // S5_CHUNK
N_CTX_CHUNKS = CTX_LEN // S5_CHUNK
VMEM_LIMIT = 56 * 1024 * 1024


def _dot(a, b):
    return jnp.dot(a, b, preferred_element_type=F32)


def _dot_nt(a, b):
    return lax.dot_general(a, b, (((1,), (1,)), ((), ())), preferred_element_type=F32)


def _rms(x):
    return x * lax.rsqrt(jnp.mean(x * x, axis=-1, keepdims=True) + EPS)


def _modnorm(x, g, shift, scale):
    return _rms(x) * g * (1.0 + scale) + shift


def _params(**kw):
    return pltpu.CompilerParams(vmem_limit_bytes=VMEM_LIMIT, **kw)


def _mod_kernel(s_ref, w_ref, b_ref, o_ref):
    s = s_ref[...]
    s = s * jax.nn.sigmoid(s)
    o_ref[...] = _dot(s.astype(BF16), w_ref[...].astype(BF16)) + b_ref[...]


def _modulation(s_rows, mod_w, mod_b):
    return pl.pallas_call(
        _mod_kernel,
        out_shape=jax.ShapeDtypeStruct((DEPTH, 16, 6 * D_MODEL), F32),
        grid=(DEPTH, 6),
        in_specs=[
            pl.BlockSpec((16, D_MODEL), lambda i, j: (0, 0)),
            pl.BlockSpec((None, D_MODEL, D_MODEL), lambda i, j: (i, 0, j)),
            pl.BlockSpec((None, 1, D_MODEL), lambda i, j: (i, 0, j)),
        ],
        out_specs=pl.BlockSpec((None, 16, D_MODEL), lambda i, j: (i, 0, j)),
        compiler_params=_params(),
        name="modulation",
    )(s_rows, mod_w, mod_b.reshape(DEPTH, 1, 6 * D_MODEL))


def _stream_specs(stream, n_sub, h_off):
    first_lat = 1 if stream[0].shape[1] == TT else 0

    def tile(t, k):
        return t * n_sub + k + h_off

    ctx_spec = pl.BlockSpec((None, TM, D_MODEL), lambda b, t: (b, 0, 0))
    lat_specs = [pl.BlockSpec((None, TM, D_MODEL),
                              lambda b, t, k=k: (b, jnp.maximum(tile(t, k) - 1 + first_lat, first_lat), 0))
                 for k in range(n_sub)]
    mod_specs = [pl.BlockSpec((None, None, 6, D_MODEL), lambda b, t, k=k: (jnp.minimum(tile(t, k), 1), b, 0, 0))
                 for k in range(n_sub)]
    return ctx_spec, lat_specs, mod_specs


ATTN_IN_SUB = 3


def _attn_in_kernel(*refs):
    n_sub = ATTN_IN_SUB
    hc_ref = refs[0]
    hl_refs = refs[1:1 + n_sub]
    mod_refs = refs[1 + n_sub:1 + 2 * n_sub]
    (g_ref, w_ref, gain_ref, cos_ref, sin_ref, e_ref,
     qa_ref, k2a_ref, v2a_ref, qb_ref, kb_ref, vb_ref) = refs[1 + 2 * n_sub:]
    e = e_ref[...]
    gains = gain_ref[...]
    lane = lax.broadcasted_iota(jnp.int32, (TM, LANES), 1)
    first_half = (lane & (HEAD_DIM - 1)) < HEAD_DIM // 2
    lo = lane < HEAD_DIM
    q_scale = HEAD_DIM ** -0.5 * LOG2E
    ones = jnp.ones((TM, LANES), BF16)

    def tile_segments(k):
        rows = slice(k * TM, (k + 1) * TM)
        cos_t = cos_ref[rows, :]
        sin_s = sin_ref[rows, :]

        def norm_rope(c, gain):
            ms = _dot((c * c).astype(BF16), e)
            cn = c * lax.rsqrt(ms + EPS) * gain
            r_fwd = pltpu.roll(cn, HEAD_DIM // 2, 1)
            r_bwd = pltpu.roll(cn, LANES - HEAD_DIM // 2, 1)
            return cn * cos_t + jnp.where(first_half, r_bwd, r_fwd) * sin_s

        def dup_halves(x):
            sw = pltpu.roll(x, HEAD_DIM, 1)
            return jnp.where(lo, x, sw).astype(BF16), jnp.where(lo, sw, x).astype(BF16)

        def finish_q(z, ref, gain):
            for c in range(z.shape[1] // LANES):
                ref[rows, c * LANES:(c + 1) * LANES] = (
                    norm_rope(z[:, c * LANES:(c + 1) * LANES], gain) * q_scale).astype(BF16)

        def finish_kv_a(z):
            for kvh, dup in enumerate(dup_halves(norm_rope(z[:, 0:LANES], gains[1:2]))):
                k2a_ref[rows, kvh * LANES:(kvh + 1) * LANES] = dup
            for kvh, dup in enumerate(dup_halves(z[:, LANES:2 * LANES])):
                v2a_ref[rows, 2 * kvh * LANES:(2 * kvh + 1) * LANES] = dup
                v2a_ref[rows, (2 * kvh + 1) * LANES:(2 * kvh + 2) * LANES] = ones

        def finish_kb(z):
            for c in range(B_QK_W // LANES):
                kb_ref[rows, c * LANES:(c + 1) * LANES] = (
                    norm_rope(z[:, c * LANES:(c + 1) * LANES], gains[3:4]).astype(BF16))

        def finish_vb(z):
            for hd in range(B_HEADS):
                vb_ref[rows, 2 * hd * LANES:(2 * hd + 1) * LANES] = z[:, hd * LANES:(hd + 1) * LANES].astype(BF16)
                vb_ref[rows, (2 * hd + 1) * LANES:(2 * hd + 2) * LANES] = ones

        return [(A_Q_W, lambda z: finish_q(z, qa_ref, gains[0:1])), (2 * A_KV_W, finish_kv_a),
                (B_QK_W, lambda z: finish_q(z, qb_ref, gains[2:3])), (B_QK_W, finish_kb), (B_V_W, finish_vb)]

    def normed(k):
        mod = mod_refs[k][...]
        x = hl_refs[k][...]
        if k == 0:
            x = jnp.where(pl.program_id(1) == 0, hc_ref[...], x)
        return _modnorm(x, g_ref[...], mod[0:1], mod[1:2]).astype(BF16)

    pending = None
    for k in range(n_sub):
        a = normed(k)
        off = 0
        for width, finish in tile_segments(k):
            z = _dot(a, w_ref[:, off:off + width])
            off += width
            if pending is not None:
                pending[1](pending[0])
            pending = (z, finish)
    pending[1](pending[0])


def _attn_in(stream, mods, g, w_in, gains, cos_t, sin_s, e_blk):
    n_sub = ATTN_IN_SUB
    ctx_spec, lat_specs, mod_specs = _stream_specs(stream, n_sub, 0)

    def tok(width):
        return pl.BlockSpec((None, n_sub * TM, width), lambda b, t: (b, t, 0))

    def full(shape):
        return pl.BlockSpec(shape, lambda b, t: (0,) * len(shape))

    out_shapes = [jax.ShapeDtypeStruct((BATCH, TT, w), BF16)
                  for w in (A_Q_W, 2 * A_KV_W, 4 * A_KV_W, B_QK_W, B_QK_W, 2 * B_V_W)]
    return pl.pallas_call(
        _attn_in_kernel,
        out_shape=out_shapes,
        grid=(BATCH, N_TILES // n_sub),
        in_specs=[
            ctx_spec, *lat_specs, *mod_specs,
            full((1, D_MODEL)), full((D_MODEL, ATTN_IN)), full((4, LANES)),
            pl.BlockSpec((n_sub * TM, LANES), lambda b, t: (t, 0)),
            pl.BlockSpec((n_sub * TM, LANES), lambda b, t: (t, 0)),
            full((LANES, LANES)),
        ],
        out_specs=[tok(A_Q_W), tok(2 * A_KV_W), tok(4 * A_KV_W), tok(B_QK_W), tok(B_QK_W), tok(2 * B_V_W)],
        compiler_params=_params(),
        name="attn_in_proj",
    )(stream[0], *[stream[1]] * n_sub, *[mods] * n_sub, g, w_in, gains, cos_t, sin_s, e_blk)


QB = 128


def _win_attn_kernel(sink_ref, q_ref, k2_ref, v2_ref, o_ref):
    t = pl.program_id(1)
    lane = lax.broadcasted_iota(jnp.int32, (QB, LANES), 1)
    lo = lane < HEAD_DIM
    rows = 4 * QB
    row = lax.broadcasted_iota(jnp.int32, (rows, 3 * QB), 0)
    col = lax.broadcasted_iota(jnp.int32, (rows, 3 * QB), 1)
    row_head = lax.broadcasted_iota(jnp.int32, (rows, 1), 0) // QB
    zero = jnp.zeros((QB, LANES), BF16)
    blocks = [(qb, g) for qb in range(TM // QB) for g in range(A_KV_HEADS)]

    def window_start(qb):
        n = (t - 1) * (TM // QB) + qb
        ws = jnp.clip((n - 1) * QB, 0, SEQ - 3 * QB)
        return n, ws

    def scores(qb, g, with_window):
        pieces = []
        for p in range(2):
            qp = q_ref[qb * QB:(qb + 1) * QB, g * 2 * LANES + p * LANES: g * 2 * LANES + (p + 1) * LANES]
            pieces.append(jnp.where(lo, qp, zero))
            pieces.append(jnp.where(lo, zero, qp))
        qs = jnp.concatenate(pieces, axis=0)
        s_c = _dot_nt(qs, k2_ref[0:CTX_LEN, g * LANES:(g + 1) * LANES])
        if not with_window:
            return s_c, None
        n, ws = window_start(qb)
        kw = k2_ref[pl.ds(pl.multiple_of(ws + CTX_LEN, QB), 3 * QB), g * LANES:(g + 1) * LANES]
        valid = jnp.abs(n * QB + (row & (QB - 1)) - (ws + col)) <= WINDOW
        return s_c, jnp.where(valid, _dot_nt(qs, kw), NEG_INF)

    def finish(qb, g, s_c, s_w):
        sk = jnp.full((rows, 1), sink_ref[4 * g + 3], F32)
        for hh in range(3):
            sk = jnp.where(row_head == hh, sink_ref[4 * g + hh], sk)
        sk = sk * LOG2E
        m = jnp.maximum(jnp.max(s_c, axis=-1, keepdims=True), sk)
        if s_w is not None:
            m = jnp.maximum(m, jnp.max(s_w, axis=-1, keepdims=True))
        vcols = slice(2 * g * LANES, (2 * g + 2) * LANES)
        pv = _dot(jnp.exp2(s_c - m).astype(BF16), v2_ref[0:CTX_LEN, vcols])
        if s_w is not None:
            _, ws = window_start(qb)
            vw = v2_ref[pl.ds(pl.multiple_of(ws + CTX_LEN, QB), 3 * QB), vcols]
            pv = pv + _dot(jnp.exp2(s_w - m).astype(BF16), vw)
        o = pv[:, 0:LANES] / (pv[:, LANES:2 * LANES] + jnp.exp2(sk - m))
        for p in range(2):
            o_ref[qb * QB:(qb + 1) * QB, g * 2 * LANES + p * LANES: g * 2 * LANES + (p + 1) * LANES] = jnp.where(
                lo, o[2 * p * QB:(2 * p + 1) * QB], o[(2 * p + 1) * QB:(2 * p + 2) * QB]).astype(BF16)

    def attend(with_window):
        s_next = scores(*blocks[0], with_window)
        for i, blk in enumerate(blocks):
            s_cur = s_next
            if i + 1 < len(blocks):
                s_next = scores(*blocks[i + 1], with_window)
            finish(*blk, *s_cur)

    @pl.when(t == 0)
    def _():
        attend(False)

    @pl.when(t > 0)
    def _():
        attend(True)


def _win_attn(sink, qa, k2a, v2a):
    return pl.pallas_call(
        _win_attn_kernel,
        out_shape=jax.ShapeDtypeStruct((BATCH, TT, A_Q_W), BF16),
        grid=(BATCH, N_TILES),
        in_specs=[
            pl.BlockSpec(memory_space=pltpu.SMEM),
            pl.BlockSpec((None, TM, A_Q_W), lambda b, t: (b, t, 0)),
            pl.BlockSpec((None, TT, 2 * A_KV_W), lambda b, t: (b, 0, 0)),
            pl.BlockSpec((None, TT, 4 * A_KV_W), lambda b, t: (b, 0, 0)),
        ],
        out_specs=pl.BlockSpec((None, TM, A_Q_W), lambda b, t: (b, t, 0)),
        compiler_params=_params(),
        name="window_attention",
    )(sink, qa, k2a, v2a)


DIFF_ROWS = 128


def _diff_attn_kernel(lpar_ref, subln_ref, q_ref, k_ref, v_ref, o_ref, *, lambda_init):
    t = pl.program_id(1)
    lp = lpar_ref[...]
    lam = (jnp.exp(jnp.sum(lp[0:1] * lp[1:2], axis=-1, keepdims=True))
           - jnp.exp(jnp.sum(lp[2:3] * lp[3:4], axis=-1, keepdims=True)) + lambda_init)
    R = DIFF_ROWS
    lane = lax.broadcasted_iota(jnp.int32, (R, LANES), 1)
    lo = lane < HEAD_DIM
    zero = jnp.zeros((R, LANES), BF16)

    def attend(n_keys):
        blocks = [(slice(rb * R, (rb + 1) * R), slice(h * LANES, (h + 1) * LANES))
                  for h in range(B_HEADS) for rb in range(TM // R)]

        def scores(rows, cols):
            q = q_ref[rows, cols]
            qs = jnp.concatenate([jnp.where(lo, q, zero), jnp.where(lo, zero, q)], axis=0)
            return _dot_nt(qs, k_ref[0:n_keys, cols])

        def finish(rows, cols, s):
            p = jnp.exp2(s - jnp.max(s, axis=-1, keepdims=True)).astype(BF16)
            vcols = slice(2 * cols.start, 2 * cols.stop)
            pv = _dot(p, v_ref[0:n_keys, vcols])
            sm = pv[:, 0:LANES] / pv[:, LANES:2 * LANES]
            y = sm[0:R] - lam * sm[R:2 * R]
            o_ref[rows, cols] = (_rms(y) * subln_ref[...] * (1.0 - lambda_init)).astype(BF16)

        s_next = scores(*blocks[0])
        for i, blk in enumerate(blocks):
            s_cur = s_next
            if i + 1 < len(blocks):
                s_next = scores(*blocks[i + 1])
            finish(*blk, s_cur)

    @pl.when(t == 0)
    def _():
        attend(CTX_LEN)

    @pl.when(t > 0)
    def _():
        attend(TT)


def _diff_attn(lpar, subln, qb, kb, vb, lambda_init):
    return pl.pallas_call(
        functools.partial(_diff_attn_kernel, lambda_init=lambda_init),
        out_shape=jax.ShapeDtypeStruct((BATCH, TT, B_V_W), BF16),
        grid=(BATCH, N_TILES),
        in_specs=[
            pl.BlockSpec((4, HEAD_DIM), lambda b, t: (0, 0)),
            pl.BlockSpec((1, LANES), lambda b, t: (0, 0)),
            pl.BlockSpec((None, TM, B_QK_W), lambda b, t: (b, t, 0)),
            pl.BlockSpec((None, TT, B_QK_W), lambda b, t: (b, 0, 0)),
            pl.BlockSpec((None, TT, 2 * B_V_W), lambda b, t: (b, 0, 0)),
        ],
        out_specs=pl.BlockSpec((None, TM, B_V_W), lambda b, t: (b, t, 0)),
        compiler_params=_params(),
        name="diff_attention",
    )(lpar, subln, qb, kb, vb)


FF_CHUNK = 1024


def _mix_mlp_kernel(*refs, h_off, n_sub):
    hc_ref = refs[0]
    hl_refs, ua_refs, ub_refs, mod_refs = (refs[1 + i * n_sub:1 + (i + 1) * n_sub] for i in range(4))
    g_ref, wo_ref, w1_ref, w2_ref, o_ref = refs[1 + 4 * n_sub:]
    half = D_MODEL // 2

    def prologue(k):
        mod = mod_refs[k][...]
        y = _dot(ua_refs[k][...], wo_ref[0:half, :]) + _dot(ub_refs[k][...], wo_ref[half:D_MODEL, :])
        x = hl_refs[k][...]
        if h_off == 0 and k == 0:
            x = jnp.where(pl.program_id(1) == 0, hc_ref[...], x)
        h1 = x + mod[2:3] * y
        f = _modnorm(h1, g_ref[...], mod[3:4], mod[4:5]).astype(BF16)
        return h1, f, mod[5:6]

    def mlp(k, h1, f, gate):
        acc = jnp.zeros((TM, D_MODEL), F32)
        for c in range(D_FF // FF_CHUNK):
            hid = jnp.maximum(_dot(f, w1_ref[:, c * FF_CHUNK:(c + 1) * FF_CHUNK]), 0.0)
            acc = acc + _dot((hid * hid).astype(BF16), w2_ref[c * FF_CHUNK:(c + 1) * FF_CHUNK, :])
        o_ref[k * TM:(k + 1) * TM, :] = h1 + gate * acc

    nxt = prologue(0)
    for k in range(n_sub):
        cur = nxt
        if k + 1 < n_sub:
            nxt = prologue(k + 1)
        mlp(k, *cur)


def _mix_mlp(stream, ua, ub, mods, g, wo, w1, w2, *, latent_only):
    n_tiles = SEQ // TM if latent_only else N_TILES
    h_off = N_TILES - n_tiles
    n_sub = 4 if n_tiles % 4 == 0 else 3
    half = D_MODEL // 2
    ub_col = 1 if ub.shape[-1] == D_MODEL else 0
    ctx_spec, lat_specs, mod_specs = _stream_specs(stream, n_sub, h_off)

    def full(shape):
        return pl.BlockSpec(shape, lambda b, t: (0,) * len(shape), pipeline_mode=pl.Buffered(1))

    def mixer_specs(col):
        return [pl.BlockSpec((None, TM, half), lambda b, t, k=k: (b, t * n_sub + k, col)) for k in range(n_sub)]

    return pl.pallas_call(
        functools.partial(_mix_mlp_kernel, h_off=h_off, n_sub=n_sub),
        out_shape=jax.ShapeDtypeStruct((BATCH, n_tiles * TM, D_MODEL), F32),
        grid=(BATCH, n_tiles // n_sub),
        in_specs=[
            ctx_spec, *lat_specs, *mixer_specs(0), *mixer_specs(ub_col), *mod_specs,
            pl.BlockSpec((1, D_MODEL), lambda b, t: (0, 0)),
            full((D_MODEL, D_MODEL)), full((D_MODEL, D_FF)), full((D_FF, D_MODEL)),
        ],
        out_specs=pl.BlockSpec((None, n_sub * TM, D_MODEL), lambda b, t: (b, t, 0)),
        compiler_params=_params(),
        name="mixer_out_mlp",
    )(stream[0], *[stream[1]] * n_sub, *[ua] * n_sub, *[ub] * n_sub, *[mods] * n_sub, g, wo, w1, w2)


S5_PAIR = 2
S5_TOK = S5_PAIR * S5_CHUNK
S5_STEPS = N_CHUNKS // S5_PAIR
GROUPS_PER_VREG = LANES // S5_GROUP
STEPS_PER_VREG = LANES // S5_GROUP
S5_SEG = 2 * LANES


def _lane_block():
    return lax.broadcasted_iota(jnp.int32, (BATCH, LANES), 1) // S5_GROUP


def _block_transpose(xs, lane_blk):
    xs = list(xs)
    n = len(xs)
    d = n // 2
    while d:
        low = (lane_blk & d) == 0
        for i in range(n):
            if i & d:
                continue
            a, b = xs[i], xs[i + d]
            xs[i] = jnp.where(low, a, pltpu.roll(b, S5_GROUP * d, 1))
            xs[i + d] = jnp.where(low, pltpu.roll(a, LANES - S5_GROUP * d, 1), b)
        d //= 2
    return xs


S5_SUB = 2


def _s5_in_kernel(*refs):
    h_refs = refs[:S5_SUB]
    mod_ref, g_ref, perm_ref, w_ref, u_ref, z_ref = refs[S5_SUB:]
    mod = mod_ref[...]
    lane_blk = _lane_block()

    def normed(j):
        x = h_refs[j][...]
        a = _rms(x) * g_ref[...] * (1.0 + mod[:, 1:2, :]) + mod[:, 0:1, :]
        a = a.reshape(BATCH * S5_TOK, D_MODEL).astype(BF16)
        return _dot(perm_ref[...], a).astype(BF16)

    def finish(j, s, u_seg):
        u_ref[j * S5_TOK:(j + 1) * S5_TOK, :, s * S5_SEG:(s + 1) * S5_SEG] = u_seg.reshape(S5_TOK, BATCH, S5_SEG)
        for kk in range(S5_SEG // LANES):
            k = s * (S5_SEG // LANES) + kk
            u_col = u_seg[:, kk * LANES:(kk + 1) * LANES]
            for hh in range(S5_CHUNK // STEPS_PER_VREG):
                halves = []
                for c2 in range(S5_PAIR):
                    tok0 = c2 * S5_CHUNK + hh * STEPS_PER_VREG
                    steps = [u_col[(tok0 + m) * BATCH:(tok0 + m + 1) * BATCH, :] for m in range(STEPS_PER_VREG)]
                    halves.append(_block_transpose(steps, lane_blk))
                rows = slice(j * S5_PAIR * BATCH, (j + 1) * S5_PAIR * BATCH)
                for r in range(GROUPS_PER_VREG):
                    z_ref[k * GROUPS_PER_VREG + r, rows, hh * LANES:(hh + 1) * LANES] = (
                        jnp.concatenate([h[r] for h in halves], axis=0).astype(BF16))

    pending = None
    for j in range(S5_SUB):
        a = normed(j)
        for s in range(D_MODEL // S5_SEG):
            u_seg = _dot(a, w_ref[:, s * S5_SEG:(s + 1) * S5_SEG])
            if pending is not None:
                finish(*pending)
            pending = (j, s, u_seg)
    finish(*pending)


def _s5_in(h, mods, g, perm, w_in):
    ctx_steps = N_CTX_CHUNKS // (S5_PAIR * S5_SUB)
    n_rows = BATCH * S5_TOK
    return pl.pallas_call(
        _s5_in_kernel,
        out_shape=[jax.ShapeDtypeStruct((TT, BATCH, D_MODEL), F32),
                   jax.ShapeDtypeStruct((S5_GROUPS, ROWS, S5_CW), BF16)],
        grid=(S5_STEPS // S5_SUB,),
        in_specs=[*[pl.BlockSpec((BATCH, S5_TOK, D_MODEL), lambda p, j=j: (0, p * S5_SUB + j, 0))
                    for j in range(S5_SUB)],
                  pl.BlockSpec((None, BATCH, 6, D_MODEL), lambda p: (jnp.minimum(p // ctx_steps, 1), 0, 0, 0)),
                  pl.BlockSpec((1, D_MODEL), lambda p: (0, 0)),
                  pl.BlockSpec((n_rows, n_rows), lambda p: (0, 0)),
                  pl.BlockSpec((D_MODEL, D_MODEL), lambda p: (0, 0))],
        out_specs=[pl.BlockSpec((S5_SUB * S5_TOK, BATCH, D_MODEL), lambda p: (p, 0, 0)),
                   pl.BlockSpec((S5_GROUPS, S5_SUB * S5_PAIR * BATCH, S5_CW), lambda p: (0, p, 0))],
        compiler_params=_params(),
        name="s5_in_proj",
    )(*[h] * S5_SUB, mods, g, perm, w_in)


OPS_G = 4


def _s5_ops_kernel(*refs):
    for j in range(OPS_G):
        _s5_group_ops(*(r.at[j] for r in refs))


def _s5_group_ops(lr_ref, li_ref, ls_ref, btr_ref, bti_ref, cr_ref, ci_ref, m_ref, q_ref, n_ref, a_ref):
    P2 = 2 * S5_STATE

    def cmul(xr, xi, yr, yi):
        return xr * yr - xi * yi, xr * yi + xi * yr

    lr, li = lr_ref[...], li_ref[...]
    dt = jnp.exp(ls_ref[...])
    mag = jnp.exp(lr * dt)
    ar = mag * jnp.cos(li * dt)
    ai = mag * jnp.sin(li * dt)
    den = lr * lr + li * li
    nr = ar - 1.0
    f_re = (nr * lr + ai * li) / den
    f_im = (ai * lr - nr * li) / den
    bt_re = f_re * btr_ref[...] - f_im * bti_ref[...]
    bt_im = f_re * bti_ref[...] + f_im * btr_ref[...]

    squares = [(ar, ai)]
    while 2 ** len(squares) < S5_CHUNK:
        squares.append(cmul(*squares[-1], *squares[-1]))
    blk = lax.broadcasted_iota(jnp.int32, (S5_CW, P2), 0) // S5_GROUP
    is_fwd = lax.broadcasted_iota(jnp.int32, (S5_CW, P2), 1) < S5_STATE
    e_k = jnp.where(is_fwd, blk, (S5_CHUNK - 1) - blk)

    def a_pow(e):
        pr = jnp.ones((S5_CW, P2), F32)
        pi = jnp.zeros((S5_CW, P2), F32)
        for b, (br, bi) in enumerate(squares):
            bit = ((e >> b) & 1) == 1
            tr, ti = cmul(pr, pi, br, bi)
            pr = jnp.where(bit, tr, pr)
            pi = jnp.where(bit, ti, pi)
        return pr, pi

    def tile_rows(x):
        return jnp.concatenate([x] * S5_CHUNK, axis=0)

    cp_re, cp_im = cmul(tile_rows(cr_ref[...]), tile_rows(ci_ref[...]), *a_pow(e_k))
    lane_fwd = lax.broadcasted_iota(jnp.int32, (S5_GROUP, P2), 1) < S5_STATE

    def lag_kernels(keep):
        br = jnp.where(keep, bt_re, 0.0)
        bi = jnp.where(keep, bt_im, 0.0)
        dims = (((1,), (1,)), ((), ()))
        hi = lax.Precision.HIGHEST
        return (lax.dot_general(br, cp_re, dims, precision=hi, preferred_element_type=F32)
                - lax.dot_general(bi, cp_im, dims, precision=hi, preferred_element_type=F32))

    kt_f = lag_kernels(lane_fwd)
    kt_b = lag_kernels(jnp.logical_not(lane_fwd))
    lane_w = lax.broadcasted_iota(jnp.int32, (S5_GROUP, S5_CW), 1)
    for s in range(S5_CHUNK):
        f_part = kt_f if s == 0 else jnp.where(lane_w >= S5_GROUP * s, pltpu.roll(kt_f, S5_GROUP * s, 1), 0.0)
        sh = (S5_GROUP * (s + 1)) % S5_CW
        b_roll = kt_b if sh == 0 else pltpu.roll(kt_b, sh, 1)
        b_part = jnp.where(lane_w < S5_GROUP * (s + 1), b_roll, 0.0)
        m_ref[s * S5_GROUP:(s + 1) * S5_GROUP, :] = (f_part + b_part).astype(BF16)

    q_re, q_im = cmul(*a_pow((S5_CHUNK - 1) - e_k), tile_rows(bt_re), tile_rows(bt_im))
    q_ref[:, 0:P2] = q_re.astype(BF16)
    q_ref[:, P2:2 * P2] = q_im.astype(BF16)

    n_re, n_im = cmul(cp_re, cp_im, ar, ai)
    n_ref[:, 0:P2] = n_re.astype(BF16)
    n_ref[:, P2:2 * P2] = (-n_im).astype(BF16)

    a_re, a_im = ar, ai
    for _ in range(S5_CHUNK.bit_length() - 1):
        a_re, a_im = cmul(a_re, a_im, a_re, a_im)
    a_ref[0:1, :] = a_re
    a_ref[1:2, :] = a_im


def _s5_ops(row_params, bt, c_nat):
    P2 = 2 * S5_STATE
    row = pl.BlockSpec((OPS_G, 1, P2), lambda g: (g, 0, 0))
    mat = pl.BlockSpec((OPS_G, S5_GROUP, P2), lambda g: (g, 0, 0))
    sq = pl.BlockSpec((OPS_G, S5_CW, S5_CW), lambda g: (g, 0, 0))
    return pl.pallas_call(
        _s5_ops_kernel,
        out_shape=[jax.ShapeDtypeStruct((S5_GROUPS, S5_CW, S5_CW), BF16)] * 3
        + [jax.ShapeDtypeStruct((S5_GROUPS, 2, P2), F32)],
        grid=(S5_GROUPS // OPS_G,),
        in_specs=[row, row, row, mat, mat, mat, mat],
        out_specs=[sq, sq, sq, pl.BlockSpec((OPS_G, 2, P2), lambda g: (g, 0, 0))],
        compiler_params=_params(),
        name="s5_chunk_operators",
    )(*row_params, *bt, *c_nat)


ROWS = N_CHUNKS * BATCH


SCAN_G = 4


def _s5_scan_kernel(u_ref, m_ref, q_ref, n_ref, a_ref, y_ref, pu_ref, sp_ref):
    S = S5_STATE
    lane = lax.broadcasted_iota(jnp.int32, (BATCH, 2 * S), 1)
    lo = lane < S
    for j in range(SCAN_G):
        pu_ref[j] = _dot(u_ref[j], q_ref[j])
    for j in range(SCAN_G):
        y_ref[j] = _dot(u_ref[j], m_ref[j])
    decay = [(a_ref[j, 0:1, :], a_ref[j, 1:2, :]) for j in range(SCAN_G)]
    zero = jnp.zeros((BATCH, 2 * S), F32)
    state = [(zero, zero)] * SCAN_G
    for k in range(N_CHUNKS):
        cf = k * BATCH
        cb = (N_CTX_CHUNKS - 1 - k if k < N_CTX_CHUNKS else N_CHUNKS + N_CTX_CHUNKS - 1 - k) * BATCH
        for j in range(SCAN_G):
            s_re, s_im = state[j]
            a_re, a_im = decay[j]
            sp_ref[j, cf:cf + BATCH, 0:S] = s_re[:, 0:S]
            sp_ref[j, cb:cb + BATCH, S:2 * S] = s_re[:, S:2 * S]
            sp_ref[j, cf:cf + BATCH, 2 * S:3 * S] = s_im[:, 0:S]
            sp_ref[j, cb:cb + BATCH, 3 * S:4 * S] = s_im[:, S:2 * S]
            x_re = jnp.where(lo, pu_ref[j, cf:cf + BATCH, 0:2 * S], pu_ref[j, cb:cb + BATCH, 0:2 * S])
            x_im = jnp.where(lo, pu_ref[j, cf:cf + BATCH, 2 * S:4 * S], pu_ref[j, cb:cb + BATCH, 2 * S:4 * S])
            state[j] = (a_re * s_re - a_im * s_im + x_re, a_re * s_im + a_im * s_re + x_im)
    for j in range(SCAN_G):
        y_ref[j] += _dot_nt(sp_ref[j].astype(BF16), n_ref[j])


def _s5_scan(u_g, m_op, q_op, n_op, a_vec):
    sq = pl.BlockSpec((SCAN_G, S5_CW, S5_CW), lambda g: (g, 0, 0))
    rows = pl.BlockSpec((SCAN_G, ROWS, S5_CW), lambda g: (g, 0, 0))
    return pl.pallas_call(
        _s5_scan_kernel,
        out_shape=jax.ShapeDtypeStruct((S5_GROUPS, ROWS, S5_CW), F32),
        grid=(S5_GROUPS // SCAN_G,),
        in_specs=[rows, sq, sq, sq, pl.BlockSpec((SCAN_G, 2, 2 * S5_STATE), lambda g: (g, 0, 0))],
        out_specs=rows,
        scratch_shapes=[pltpu.VMEM((SCAN_G, ROWS, S5_CW), F32), pltpu.VMEM((SCAN_G, ROWS, S5_CW), F32)],
        compiler_params=_params(),
        name="s5_scan",
    )(u_g, m_op, q_op, n_op, a_vec)


def _s5_out_kernel(u_ref, y_ref, d_ref, gw_ref, gb_ref, perm_ref, o_ref):
    lane_blk = _lane_block()
    n_seg = D_MODEL // S5_SEG
    per_seg = S5_SEG // LANES

    def regroup(j, s):
        cols = []
        for k in range(s * per_seg, (s + 1) * per_seg):
            by_tok = [None] * S5_TOK
            for c2 in range(S5_PAIR):
                rows = slice((j * S5_PAIR + c2) * BATCH, (j * S5_PAIR + c2 + 1) * BATCH)
                for hh in range(S5_CHUNK // STEPS_PER_VREG):
                    groups = [y_ref[k * GROUPS_PER_VREG + r, rows, hh * LANES:(hh + 1) * LANES]
                              for r in range(GROUPS_PER_VREG)]
                    steps = _block_transpose(groups, lane_blk)
                    for m in range(STEPS_PER_VREG):
                        by_tok[c2 * S5_CHUNK + hh * STEPS_PER_VREG + m] = steps[m]
            cols.append(jnp.concatenate(by_tok, axis=0))
        return jnp.concatenate(cols, axis=1)

    state = {}

    def compute(j, s, ys):
        cols = slice(s * S5_SEG, (s + 1) * S5_SEG)
        u = u_ref[j * S5_TOK:(j + 1) * S5_TOK, :, cols].reshape(BATCH * S5_TOK, S5_SEG)
        g_seg = jax.nn.gelu(u * d_ref[:, cols] + ys)
        part = _dot(g_seg.astype(BF16), gw_ref[cols, :])
        g_parts, pre = state.get(j, ([], None))
        state[j] = (g_parts + [g_seg], part if pre is None else pre + part)
        if s == n_seg - 1:
            g_parts, pre = state[j]
            gate = jax.nn.sigmoid(pre + gb_ref[...])
            gated = (jnp.concatenate(g_parts, axis=1) * gate).astype(BF16)
            gated = _dot(perm_ref[...], gated).astype(BF16)
            o_ref[:, j * S5_TOK:(j + 1) * S5_TOK, :] = gated.reshape(BATCH, S5_TOK, D_MODEL)

    items = [(j, s) for j in range(S5_SUB) for s in range(n_seg)]
    ys_next = regroup(*items[0])
    for i, item in enumerate(items):
        ys = ys_next
        if i + 1 < len(items):
            ys_next = regroup(*items[i + 1])
        compute(*item, ys)


def _s5_out(u, y_g, d_skip, glu_w, glu_b, perm_t):
    ctx_steps = N_CTX_CHUNKS // (S5_PAIR * S5_SUB)
    n_rows = BATCH * S5_TOK
    vec = pl.BlockSpec((1, D_MODEL), lambda p: (0, 0))
    return pl.pallas_call(
        _s5_out_kernel,
        out_shape=jax.ShapeDtypeStruct((BATCH, SEQ, D_MODEL), BF16),
        grid=(S5_STEPS // S5_SUB - ctx_steps,),
        in_specs=[pl.BlockSpec((S5_SUB * S5_TOK, BATCH, D_MODEL), lambda p: (p + ctx_steps, 0, 0)),
                  pl.BlockSpec((S5_GROUPS, S5_SUB * S5_PAIR * BATCH, S5_CW), lambda p: (0, p + ctx_steps, 0)),
                  vec, pl.BlockSpec((D_MODEL, D_MODEL), lambda p: (0, 0)), vec,
                  pl.BlockSpec((n_rows, n_rows), lambda p: (0, 0))],
        out_specs=pl.BlockSpec((BATCH, S5_SUB * S5_TOK, D_MODEL), lambda p: (0, p, 0)),
        compiler_params=_params(),
        name="s5_gelu_glu",
    )(u, y_g, d_skip, glu_w, glu_b, perm_t)


def _rope_tables():
    rows_n = SEQ // GRID_W
    row = np.repeat(np.arange(rows_n, dtype=np.float64), GRID_W)
    col = np.tile(np.arange(GRID_W, dtype=np.float64), rows_n)
    n_freq = HEAD_DIM // 4
    inv = ROPE_BASE ** (-np.arange(n_freq, dtype=np.float64) / n_freq)
    ang = np.concatenate([row[:, None] * inv, col[:, None] * inv], axis=-1)
    reps = LANES // (HEAD_DIM // 2)
    cos_t = np.tile(np.cos(ang), (1, reps))
    sin_t = np.tile(np.sin(ang), (1, reps))
    sign = np.where((np.arange(LANES) % HEAD_DIM) < HEAD_DIM // 2, -1.0, 1.0)
    cos_t = np.concatenate([np.ones((CTX_LEN, LANES)), cos_t], axis=0)
    sin_s = np.concatenate([np.zeros((CTX_LEN, LANES)), sin_t * sign], axis=0)
    return jnp.asarray(cos_t, F32), jnp.asarray(sin_s, F32)


def _fb_rows(x):
    return jnp.transpose(x, (1, 0, 2)).reshape(S5_GROUPS, 1, 2 * S5_STATE)


def _s5_layout(lam_re, lam_im, log_step, b_re, b_im, c_re, c_im):
    ls = jnp.broadcast_to(log_step[:, :, None], lam_re.shape)
    rows = [_fb_rows(v) for v in (lam_re, lam_im, ls)]

    def bt_of(b):
        return jnp.transpose(b, (1, 3, 0, 2)).reshape(S5_GROUPS, S5_GROUP, 2 * S5_STATE)

    def c_of(c):
        return jnp.transpose(c, (1, 2, 0, 3)).reshape(S5_GROUPS, S5_GROUP, 2 * S5_STATE)

    return rows, [bt_of(b_re), bt_of(b_im)], [c_of(c_re), c_of(c_im)]


def kernel(x, c, ctx, c_ctx, norm1_g, norm2_g, mod_w, mod_b, mlp_w1, mlp_w2, attn_w_in, attn_w_out, a_q_norm, a_k_norm, a_sink, b_q_norm, b_k_norm, b_lq1, b_lk1, b_lq2, b_lk2, b_subln, s5_w_in, s5_lambda_re, s5_lambda_im, s5_log_step, s5_b_re, s5_b_im, s5_c_re, s5_c_im, s5_d, s5_glu_w, s5_glu_b, s5_w_out):
    assert x.shape == (BATCH, SEQ, D_MODEL) and ctx.shape == (BATCH, CTX_LEN, D_MODEL)
    stream = (ctx, x)
    s_rows = jnp.concatenate([c, c_ctx[None], jnp.zeros((16 - BATCH - 1, D_MODEL), F32)], axis=0)
    m_all = _modulation(s_rows, mod_w, mod_b)
    cos_t, sin_s = _rope_tables()
    e_blk = jnp.asarray(np.kron(np.eye(LANES // HEAD_DIM), np.ones((HEAD_DIM, HEAD_DIM))) / HEAD_DIM, BF16)

    for i in range(DEPTH):
        last = i == DEPTH - 1
        j = i // 2
        m_lat = m_all[i, :BATCH].reshape(BATCH, 6, D_MODEL)
        m_ctx = jnp.broadcast_to(m_all[i, BATCH].reshape(1, 6, D_MODEL), (BATCH, 6, D_MODEL))
        mods = jnp.stack([m_ctx, m_lat])
        g1 = norm1_g[i].reshape(1, D_MODEL)
        g2 = norm2_g[i].reshape(1, D_MODEL)
        if i % 2 == 0:
            lambda_init = 0.8 - 0.6 * math.exp(-0.3 * i)
            gains = jnp.stack([jnp.tile(v[j], LANES // HEAD_DIM) for v in (a_q_norm, a_k_norm, b_q_norm, b_k_norm)])
            qa, k2a, v2a, qb, kb, vb = _attn_in(stream, mods, g1, attn_w_in[j].astype(BF16), gains, cos_t, sin_s, e_blk)
            ya = _win_attn(a_sink[j], qa, k2a, v2a)
            lpar = jnp.stack([b_lq1[j], b_lk1[j], b_lq2[j], b_lk2[j]])
            yb = _diff_attn(lpar, b_subln[j].reshape(1, LANES), qb, kb, vb, lambda_init)
            if last:
                ya, yb = ya[:, CTX_LEN:], yb[:, CTX_LEN:]
            ua, ub, wo = ya, yb, attn_w_out[j]
        else:
            src = np.arange(BATCH * S5_TOK).reshape(BATCH, S5_TOK).T.reshape(-1)
            perm = jnp.asarray(np.eye(BATCH * S5_TOK, dtype=np.float32)[src], BF16)
            h_all = stream[0] if stream[0].shape[1] == TT else jnp.concatenate(stream, axis=1)
            u, u_g = _s5_in(h_all, mods, g1, perm, s5_w_in[j].astype(BF16))
            ops_in = _s5_layout(s5_lambda_re[j], s5_lambda_im[j], s5_log_step[j], s5_b_re[j], s5_b_im[j],
                                s5_c_re[j], s5_c_im[j])
            m_op, q_op, n_op, a_vec = _s5_ops(*ops_in)
            y_g = _s5_scan(u_g, m_op, q_op, n_op, a_vec)
            assert last, "S5 layers before the last one would also need the context rows of the readout"
            gated = _s5_out(u, y_g, s5_d[j].reshape(1, D_MODEL), s5_glu_w[j].astype(BF16),
                            s5_glu_b[j].reshape(1, D_MODEL), perm.T)
            ua, ub, wo = gated, gated, s5_w_out[j]
        h = _mix_mlp(stream, ua, ub, mods, g2, wo.astype(BF16), mlp_w1[i].astype(BF16), mlp_w2[i].astype(BF16),
                     latent_only=last)
        stream = (h, h)
    return h
```

```python
import functools
import math

import jax
import jax.numpy as jnp
import numpy as np
from jax import lax
from jax.experimental import pallas as pl
from jax.experimental.pallas import tpu as pltpu

F32 = jnp.float32
BF16 = jnp.bfloat16

D_MODEL = 1024
BATCH = 8
SEQ = 2048
DEPTH = 2
GRID_W = 64
CTX_LEN = 256
HEAD_DIM = 64
WINDOW = 128
A_Q_HEADS = 8
A_KV_HEADS = 2
B_HEADS = 4
A_Q_W = A_Q_HEADS * HEAD_DIM
A_KV_W = A_KV_HEADS * HEAD_DIM
B_QK_W = B_HEADS * 2 * HEAD_DIM
B_V_W = B_HEADS * 2 * HEAD_DIM
ATTN_IN = A_Q_W + 2 * A_KV_W + 2 * B_QK_W + B_V_W
S5_GROUP = 16
S5_GROUPS = D_MODEL // S5_GROUP
S5_STATE = 64
D_FF = 4 * D_MODEL
ROPE_BASE = 10000.0
EPS = 1e-6
NEG_INF = -1e30
LOG2E = math.log2(math.e)

TT = CTX_LEN + SEQ
TM = 256
N_TILES = TT // TM
LANES = 128
S5_CHUNK = 16
S5_CW = S5_CHUNK * S5_GROUP
N_CHUNKS = TT // S5_CHUNK
N_CTX_CHUNKS = CTX_LEN // S5_CHUNK
VMEM_LIMIT = 56 * 1024 * 1024


def _dot(a, b):
    return jnp.dot(a, b, preferred_element_type=F32)


def _dot_nt(a, b):
    return lax.dot_general(a, b, (((1,), (1,)), ((), ())), preferred_element_type=F32)


def _rms(x):
    return x * lax.rsqrt(jnp.mean(x * x, axis=-1, keepdims=True) + EPS)


def _modnorm(x, g, shift, scale):
    return _rms(x) * g * (1.0 + scale) + shift


def _params(**kw):
    return pltpu.CompilerParams(vmem_limit_bytes=VMEM_LIMIT, **kw)


def _mod_kernel(s_ref, w_ref, b_ref, o_ref):
    s = s_ref[...]
    s = s * jax.nn.sigmoid(s)
    o_ref[...] = _dot(s.astype(BF16), w_ref[...].astype(BF16)) + b_ref[...]


def _modulation(s_rows, mod_w, mod_b):
    return pl.pallas_call(
        _mod_kernel,
        out_shape=jax.ShapeDtypeStruct((DEPTH, 16, 6 * D_MODEL), F32),
        grid=(DEPTH, 6),
        in_specs=[
            pl.BlockSpec((16, D_MODEL), lambda i, j: (0, 0)),
            pl.BlockSpec((None, D_MODEL, D_MODEL), lambda i, j: (i, 0, j)),
            pl.BlockSpec((None, 1, D_MODEL), lambda i, j: (i, 0, j)),
        ],
        out_specs=pl.BlockSpec((None, 16, D_MODEL), lambda i, j: (i, 0, j)),
        compiler_params=_params(),
        name="modulation",
    )(s_rows, mod_w, mod_b.reshape(DEPTH, 1, 6 * D_MODEL))


def _stream_specs(stream, n_sub, h_off):
    first_lat = 1 if stream[0].shape[1] == TT else 0

    def tile(t, k):
        return t * n_sub + k + h_off

    ctx_spec = pl.BlockSpec((None, TM, D_MODEL), lambda b, t: (b, 0, 0))
    lat_specs = [pl.BlockSpec((None, TM, D_MODEL),
                              lambda b, t, k=k: (b, jnp.maximum(tile(t, k) - 1 + first_lat, first_lat), 0))
                 for k in range(n_sub)]
    mod_specs = [pl.BlockSpec((None, None, 6, D_MODEL), lambda b, t, k=k: (jnp.minimum(tile(t, k), 1), b, 0, 0))
                 for k in range(n_sub)]
    return ctx_spec, lat_specs, mod_specs


ATTN_IN_SUB = 3


def _attn_in_kernel(*refs):
    n_sub = ATTN_IN_SUB
    hc_ref = refs[0]
    hl_refs = refs[1:1 + n_sub]
    mod_refs = refs[1 + n_sub:1 + 2 * n_sub]
    (g_ref, w_ref, gain_ref, cos_ref, sin_ref, e_ref,
     qa_ref, k2a_ref, v2a_ref, qb_ref, kb_ref, vb_ref) = refs[1 + 2 * n_sub:]
    e = e_ref[...]
    gains = gain_ref[...]
    lane = lax.broadcasted_iota(jnp.int32, (TM, LANES), 1)
    first_half = (lane & (HEAD_DIM - 1)) < HEAD_DIM // 2
    lo = lane < HEAD_DIM
    q_scale = HEAD_DIM ** -0.5 * LOG2E
    ones = jnp.ones((TM, LANES), BF16)

    def tile_segments(k):
        rows = slice(k * TM, (k + 1) * TM)
        cos_t = cos_ref[rows, :]
        sin_s = sin_ref[rows, :]

        def norm_rope(c, gain):
            ms = _dot((c * c).astype(BF16), e)
            cn = c * lax.rsqrt(ms + EPS) * gain
            r_fwd = pltpu.roll(cn, HEAD_DIM // 2, 1)
            r_bwd = pltpu.roll(cn, LANES - HEAD_DIM // 2, 1)
            return cn * cos_t + jnp.where(first_half, r_bwd, r_fwd) * sin_s

        def dup_halves(x):
            sw = pltpu.roll(x, HEAD_DIM, 1)
            return jnp.where(lo, x, sw).astype(BF16), jnp.where(lo, sw, x).astype(BF16)

        def finish_q(z, ref, gain):
            for c in range(z.shape[1] // LANES):
                ref[rows, c * LANES:(c + 1) * LANES] = (
                    norm_rope(z[:, c * LANES:(c + 1) * LANES], gain) * q_scale).astype(BF16)

        def finish_kv_a(z):
            for kvh, dup in enumerate(dup_halves(norm_rope(z[:, 0:LANES], gains[1:2]))):
                k2a_ref[rows, kvh * LANES:(kvh + 1) * LANES] = dup
            for kvh, dup in enumerate(dup_halves(z[:, LANES:2 * LANES])):
                v2a_ref[rows, 2 * kvh * LANES:(2 * kvh + 1) * LANES] = dup
                v2a_ref[rows, (2 * kvh + 1) * LANES:(2 * kvh + 2) * LANES] = ones

        def finish_kb(z):
            for c in range(B_QK_W // LANES):
                kb_ref[rows, c * LANES:(c + 1) * LANES] = (
                    norm_rope(z[:, c * LANES:(c + 1) * LANES], gains[3:4]).astype(BF16))

        def finish_vb(z):
            for hd in range(B_HEADS):
                vb_ref[rows, 2 * hd * LANES:(2 * hd + 1) * LANES] = z[:, hd * LANES:(hd + 1) * LANES].astype(BF16)
                vb_ref[rows, (2 * hd + 1) * LANES:(2 * hd + 2) * LANES] = ones

        return [(A_Q_W, lambda z: finish_q(z, qa_ref, gains[0:1])), (2 * A_KV_W, finish_kv_a),
                (B_QK_W, lambda z: finish_q(z, qb_ref, gains[2:3])), (B_QK_W, finish_kb), (B_V_W, finish_vb)]

    def normed(k):
        mod = mod_refs[k][...]
        x = hl_refs[k][...]
        if k == 0:
            x = jnp.where(pl.program_id(1) == 0, hc_ref[...], x)
        return _modnorm(x, g_ref[...], mod[0:1], mod[1:2]).astype(BF16)

    pending = None
    for k in range(n_sub):
        a = normed(k)
        off = 0
        for width, finish in tile_segments(k):
            z = _dot(a, w_ref[:, off:off + width])
            off += width
            if pending is not None:
                pending[1](pending[0])
            pending = (z, finish)
    pending[1](pending[0])


def _attn_in(stream, mods, g, w_in, gains, cos_t, sin_s, e_blk):
    n_sub = ATTN_IN_SUB
    ctx_spec, lat_specs, mod_specs = _stream_specs(stream, n_sub, 0)

    def tok(width):
        return pl.BlockSpec((None, n_sub * TM, width), lambda b, t: (b, t, 0))

    def full(shape):
        return pl.BlockSpec(shape, lambda b, t: (0,) * len(shape))

    out_shapes = [jax.ShapeDtypeStruct((BATCH, TT, w), BF16)
                  for w in (A_Q_W, 2 * A_KV_W, 4 * A_KV_W, B_QK_W, B_QK_W, 2 * B_V_W)]
    return pl.pallas_call(
        _attn_in_kernel,
        out_shape=out_shapes,
        grid=(BATCH, N_TILES // n_sub),
        in_specs=[
            ctx_spec, *lat_specs, *mod_specs,
            full((1, D_MODEL)), full((D_MODEL, ATTN_IN)), full((4, LANES)),
            pl.BlockSpec((n_sub * TM, LANES), lambda b, t: (t, 0)),
            pl.BlockSpec((n_sub * TM, LANES), lambda b, t: (t, 0)),
            full((LANES, LANES)),
        ],
        out_specs=[tok(A_Q_W), tok(2 * A_KV_W), tok(4 * A_KV_W), tok(B_QK_W), tok(B_QK_W), tok(2 * B_V_W)],
        compiler_params=_params(),
        name="attn_in_proj",
    )(stream[0], *[stream[1]] * n_sub, *[mods] * n_sub, g, w_in, gains, cos_t, sin_s, e_blk)


QB = 128


def _win_attn_kernel(sink_ref, q_ref, k2_ref, v2_ref, o_ref):
    t = pl.program_id(1)
    lane = lax.broadcasted_iota(jnp.int32, (QB, LANES), 1)
    lo = lane < HEAD_DIM
    rows = 4 * QB
    row = lax.broadcasted_iota(jnp.int32, (rows, 3 * QB), 0)
    col = lax.broadcasted_iota(jnp.int32, (rows, 3 * QB), 1)
    row_head = lax.broadcasted_iota(jnp.int32, (rows, 1), 0) // QB
    zero = jnp.zeros((QB, LANES), BF16)
    blocks = [(qb, g) for qb in range(TM // QB) for g in range(A_KV_HEADS)]

    def window_start(qb):
        n = (t - 1) * (TM // QB) + qb
        ws = jnp.clip((n - 1) * QB, 0, SEQ - 3 * QB)
        return n, ws

    def scores(qb, g, with_window):
        pieces = []
        for p in range(2):
            qp = q_ref[qb * QB:(qb + 1) * QB, g * 2 * LANES + p * LANES: g * 2 * LANES + (p + 1) * LANES]
            pieces.append(jnp.where(lo, qp, zero))
            pieces.append(jnp.where(lo, zero, qp))
        qs = jnp.concatenate(pieces, axis=0)
        s_c = _dot_nt(qs, k2_ref[0:CTX_LEN, g * LANES:(g + 1) * LANES])
        if not with_window:
            return s_c, None
        n, ws = window_start(qb)
        kw = k2_ref[pl.ds(pl.multiple_of(ws + CTX_LEN, QB), 3 * QB), g * LANES:(g + 1) * LANES]
        valid = jnp.abs(n * QB + (row & (QB - 1)) - (ws + col)) <= WINDOW
        return s_c, jnp.where(valid, _dot_nt(qs, kw), NEG_INF)

    def finish(qb, g, s_c, s_w):
        sk = jnp.full((rows, 1), sink_ref[4 * g + 3], F32)
        for hh in range(3):
            sk = jnp.where(row_head == hh, sink_ref[4 * g + hh], sk)
        sk = sk * LOG2E
        m = jnp.maximum(jnp.max(s_c, axis=-1, keepdims=True), sk)
        if s_w is not None:
            m = jnp.maximum(m, jnp.max(s_w, axis=-1, keepdims=True))
        vcols = slice(2 * g * LANES, (2 * g + 2) * LANES)
        pv = _dot(jnp.exp2(s_c - m).astype(BF16), v2_ref[0:CTX_LEN, vcols])
        if s_w is not None:
            _, ws = window_start(qb)
            vw = v2_ref[pl.ds(pl.multiple_of(ws + CTX_LEN, QB), 3 * QB), vcols]
            pv = pv + _dot(jnp.exp2(s_w - m).astype(BF16), vw)
        o = pv[:, 0:LANES] / (pv[:, LANES:2 * LANES] + jnp.exp2(sk - m))
        for p in range(2):
            o_ref[qb * QB:(qb + 1) * QB, g * 2 * LANES + p * LANES: g * 2 * LANES + (p + 1) * LANES] = jnp.where(
                lo, o[2 * p * QB:(2 * p + 1) * QB], o[(2 * p + 1) * QB:(2 * p + 2) * QB]).astype(BF16)

    def attend(with_window):
        s_next = scores(*blocks[0], with_window)
        for i, blk in enumerate(blocks):
            s_cur = s_next
            if i + 1 < len(blocks):
                s_next = scores(*blocks[i + 1], with_window)
            finish(*blk, *s_cur)

    @pl.when(t == 0)
    def _():
        attend(False)

    @pl.when(t > 0)
    def _():
        attend(True)


def _win_attn(sink, qa, k2a, v2a):
    return pl.pallas_call(
        _win_attn_kernel,
        out_shape=jax.ShapeDtypeStruct((BATCH, TT, A_Q_W), BF16),
        grid=(BATCH, N_TILES),
        in_specs=[
            pl.BlockSpec(memory_space=pltpu.SMEM),
            pl.BlockSpec((None, TM, A_Q_W), lambda b, t: (b, t, 0)),
            pl.BlockSpec((None, TT, 2 * A_KV_W), lambda b, t: (b, 0, 0)),
            pl.BlockSpec((None, TT, 4 * A_KV_W), lambda b, t: (b, 0, 0)),
        ],
        out_specs=pl.BlockSpec((None, TM, A_Q_W), lambda b, t: (b, t, 0)),
        compiler_params=_params(),
        name="window_attention",
    )(sink, qa, k2a, v2a)


DIFF_ROWS = 256


def _diff_attn_kernel(lpar_ref, subln_ref, q_ref, k_ref, v_ref, o_ref, *, lambda_init):
    t = pl.program_id(1)
    lp = lpar_ref[...]
    lam = (jnp.exp(jnp.sum(lp[0:1] * lp[1:2], axis=-1, keepdims=True))
           - jnp.exp(jnp.sum(lp[2:3] * lp[3:4], axis=-1, keepdims=True)) + lambda_init)
    R = DIFF_ROWS
    lane = lax.broadcasted_iota(jnp.int32, (R, LANES), 1)
    lo = lane < HEAD_DIM
    zero = jnp.zeros((R, LANES), BF16)

    def attend(n_keys):
        blocks = [(slice(rb * R, (rb + 1) * R), slice(h * LANES, (h + 1) * LANES))
                  for h in range(B_HEADS) for rb in range(TM // R)]

        def scores(rows, cols):
            q = q_ref[rows, cols]
            qs = jnp.concatenate([jnp.where(lo, q, zero), jnp.where(lo, zero, q)], axis=0)
            return _dot_nt(qs, k_ref[0:n_keys, cols])

        def finish(rows, cols, s):
            p = jnp.exp2(s - jnp.max(s, axis=-1, keepdims=True)).astype(BF16)
            vcols = slice(2 * cols.start, 2 * cols.stop)
            pv = _dot(p, v_ref[0:n_keys, vcols])
            sm = pv[:, 0:LANES] / pv[:, LANES:2 * LANES]
            y = sm[0:R] - lam * sm[R:2 * R]
            o_ref[rows, cols] = (_rms(y) * subln_ref[...] * (1.0 - lambda_init)).astype(BF16)

        s_next = scores(*blocks[0])
        for i, blk in enumerate(blocks):
            s_cur = s_next
            if i + 1 < len(blocks):
                s_next = scores(*blocks[i + 1])
            finish(*blk, s_cur)

    @pl.when(t == 0)
    def _():
        attend(CTX_LEN)

    @pl.when(t > 0)
    def _():
        attend(TT)


def _diff_attn(lpar, subln, qb, kb, vb, lambda_init):
    return pl.pallas_call(
        functools.partial(_diff_attn_kernel, lambda_init=lambda_init),
        out_shape=jax.ShapeDtypeStruct((BATCH, TT, B_V_W), BF16),
        grid=(BATCH, N_TILES),
        in_specs=[
            pl.BlockSpec((4, HEAD_DIM), lambda b, t: (0, 0)),
            pl.BlockSpec((1, LANES), lambda b, t: (0, 0)),
            pl.BlockSpec((None, TM, B_QK_W), lambda b, t: (b, t, 0)),
            pl.BlockSpec((None, TT, B_QK_W), lambda b, t: (b, 0, 0)),
            pl.BlockSpec((None, TT, 2 * B_V_W), lambda b, t: (b, 0, 0)),
        ],
        out_specs=pl.BlockSpec((None, TM, B_V_W), lambda b, t: (b, t, 0)),
        compiler_params=_params(),
        name="diff_attention",
    )(lpar, subln, qb, kb, vb)


FF_CHUNK = 1024


def _mix_mlp_kernel(*refs, h_off, n_sub):
    hc_ref = refs[0]
    hl_refs, ua_refs, ub_refs, mod_refs = (refs[1 + i * n_sub:1 + (i + 1) * n_sub] for i in range(4))
    g_ref, wo_ref, w1_ref, w2_ref, o_ref = refs[1 + 4 * n_sub:]
    half = D_MODEL // 2

    def prologue(k):
        mod = mod_refs[k][...]
        y = _dot(ua_refs[k][...], wo_ref[0:half, :]) + _dot(ub_refs[k][...], wo_ref[half:D_MODEL, :])
        x = hl_refs[k][...]
        if h_off == 0 and k == 0:
            x = jnp.where(pl.program_id(1) == 0, hc_ref[...], x)
        h1 = x + mod[2:3] * y
        f = _modnorm(h1, g_ref[...], mod[3:4], mod[4:5]).astype(BF16)
        return h1, f, mod[5:6]

    def mlp(k, h1, f, gate):
        acc = jnp.zeros((TM, D_MODEL), F32)
        for c in range(D_FF // FF_CHUNK):
            hid = jnp.maximum(_dot(f, w1_ref[:, c * FF_CHUNK:(c + 1) * FF_CHUNK]), 0.0)
            acc = acc + _dot((hid * hid).astype(BF16), w2_ref[c * FF_CHUNK:(c + 1) * FF_CHUNK, :])
        o_ref[k * TM:(k + 1) * TM, :] = h1 + gate * acc

    nxt = prologue(0)
    for k in range(n_sub):
        cur = nxt
        if k + 1 < n_sub:
            nxt = prologue(k + 1)
        mlp(k, *cur)


def _mix_mlp(stream, ua, ub, mods, g, wo, w1_all, w2_all, layer, *, latent_only):
    n_tiles = SEQ // TM if latent_only else N_TILES
    h_off = N_TILES - n_tiles
    n_sub = 4 if n_tiles % 4 == 0 else 3
    half = D_MODEL // 2
    ub_col = 1 if ub.shape[-1] == D_MODEL else 0
    ctx_spec, lat_specs, mod_specs = _stream_specs(stream, n_sub, h_off)

    def full(shape):
        return pl.BlockSpec(shape, lambda b, t: (0,) * len(shape), pipeline_mode=pl.Buffered(1))

    def mixer_specs(col):
        return [pl.BlockSpec((None, TM, half), lambda b, t, k=k: (b, t * n_sub + k, col)) for k in range(n_sub)]

    return pl.pallas_call(
        functools.partial(_mix_mlp_kernel, h_off=h_off, n_sub=n_sub),
        out_shape=jax.ShapeDtypeStruct((BATCH, n_tiles * TM, D_MODEL), F32),
        grid=(BATCH, n_tiles // n_sub),
        in_specs=[
            ctx_spec, *lat_specs, *mixer_specs(0), *mixer_specs(ub_col), *mod_specs,
            pl.BlockSpec((1, D_MODEL), lambda b, t: (0, 0)),
            full((D_MODEL, D_MODEL)),
            pl.BlockSpec((None, D_MODEL, D_FF), lambda b, t: (layer, 0, 0), pipeline_mode=pl.Buffered(1)),
            pl.BlockSpec((None, D_FF, D_MODEL), lambda b, t: (layer, 0, 0), pipeline_mode=pl.Buffered(1)),
        ],
        out_specs=pl.BlockSpec((None, n_sub * TM, D_MODEL), lambda b, t: (b, t, 0)),
        compiler_params=_params(),
        name="mixer_out_mlp",
    )(stream[0], *[stream[1]] * n_sub, *[ua] * n_sub, *[ub] * n_sub, *[mods] * n_sub, g, wo, w1_all, w2_all)


S5_PAIR = 2
S5_TOK = S5_PAIR * S5_CHUNK
S5_STEPS = N_CHUNKS // S5_PAIR
GROUPS_PER_VREG = LANES // S5_GROUP
STEPS_PER_VREG = LANES // S5_GROUP
S5_SEG = 2 * LANES


def _lane_block():
    return lax.broadcasted_iota(jnp.int32, (BATCH, LANES), 1) // S5_GROUP


def _block_transpose(xs, lane_blk):
    xs = list(xs)
    n = len(xs)
    d = n // 2
    while d:
        low = (lane_blk & d) == 0
        for i in range(n):
            if i & d:
                continue
            a, b = xs[i], xs[i + d]
            xs[i] = jnp.where(low, a, pltpu.roll(b, S5_GROUP * d, 1))
            xs[i + d] = jnp.where(low, pltpu.roll(a, LANES - S5_GROUP * d, 1), b)
        d //= 2
    return xs


S5_SUB = 4


def _s5_in_kernel(*refs):
    h_refs = refs[:S5_SUB]
    mod_ref, g_ref, perm_ref, w_ref, u_ref, z_ref = refs[S5_SUB:]
    mod = mod_ref[...]
    lane_blk = _lane_block()

    def normed(j):
        x = h_refs[j][...]
        a = _rms(x) * g_ref[...] * (1.0 + mod[:, 1:2, :]) + mod[:, 0:1, :]
        a = a.reshape(BATCH * S5_TOK, D_MODEL).astype(BF16)
        return _dot(perm_ref[...], a).astype(BF16)

    def finish(j, s, u_seg):
        u_ref[j * S5_TOK:(j + 1) * S5_TOK, :, s * S5_SEG:(s + 1) * S5_SEG] = u_seg.reshape(S5_TOK, BATCH, S5_SEG)
        for kk in range(S5_SEG // LANES):
            k = s * (S5_SEG // LANES) + kk
            u_col = u_seg[:, kk * LANES:(kk + 1) * LANES]
            for hh in range(S5_CHUNK // STEPS_PER_VREG):
                halves = []
                for c2 in range(S5_PAIR):
                    tok0 = c2 * S5_CHUNK + hh * STEPS_PER_VREG
                    steps = [u_col[(tok0 + m) * BATCH:(tok0 + m + 1) * BATCH, :] for m in range(STEPS_PER_VREG)]
                    halves.append(_block_transpose(steps, lane_blk))
                rows = slice(j * S5_PAIR * BATCH, (j + 1) * S5_PAIR * BATCH)
                for r in range(GROUPS_PER_VREG):
                    z_ref[k * GROUPS_PER_VREG + r, rows, hh * LANES:(hh + 1) * LANES] = (
                        jnp.concatenate([h[r] for h in halves], axis=0).astype(BF16))

    pending = None
    for j in range(S5_SUB):
        a = normed(j)
        for s in range(D_MODEL // S5_SEG):
            u_seg = _dot(a, w_ref[:, s * S5_SEG:(s + 1) * S5_SEG])
            if pending is not None:
                finish(*pending)
            pending = (j, s, u_seg)
    finish(*pending)


def _s5_in(h, mods, g, perm, w_in):
    ctx_steps = N_CTX_CHUNKS // (S5_PAIR * S5_SUB)
    n_rows = BATCH * S5_TOK
    return pl.pallas_call(
        _s5_in_kernel,
        out_shape=[jax.ShapeDtypeStruct((TT, BATCH, D_MODEL), F32),
                   jax.ShapeDtypeStruct((S5_GROUPS, ROWS, S5_CW), BF16)],
        grid=(S5_STEPS // S5_SUB,),
        in_specs=[*[pl.BlockSpec((BATCH, S5_TOK, D_MODEL), lambda p, j=j: (0, p * S5_SUB + j, 0))
                    for j in range(S5_SUB)],
                  pl.BlockSpec((None, BATCH, 6, D_MODEL), lambda p: (jnp.minimum(p // ctx_steps, 1), 0, 0, 0)),
                  pl.BlockSpec((1, D_MODEL), lambda p: (0, 0)),
                  pl.BlockSpec((n_rows, n_rows), lambda p: (0, 0)),
                  pl.BlockSpec((D_MODEL, D_MODEL), lambda p: (0, 0))],
        out_specs=[pl.BlockSpec((S5_SUB * S5_TOK, BATCH, D_MODEL), lambda p: (p, 0, 0)),
                   pl.BlockSpec((S5_GROUPS, S5_SUB * S5_PAIR * BATCH, S5_CW), lambda p: (0, p, 0))],
        compiler_params=_params(),
        name="s5_in_proj",
    )(*[h] * S5_SUB, mods, g, perm, w_in)


OPS_G = 4


def _s5_ops_kernel(*refs):
    for j in range(OPS_G):
        _s5_group_ops(*(r.at[j] for r in refs))


def _s5_group_ops(lr_ref, li_ref, ls_ref, btr_ref, bti_ref, cr_ref, ci_ref, m_ref, q_ref, n_ref, a_ref):
    P2 = 2 * S5_STATE

    def cmul(xr, xi, yr, yi):
        return xr * yr - xi * yi, xr * yi + xi * yr

    lr, li = lr_ref[...], li_ref[...]
    dt = jnp.exp(ls_ref[...])
    mag = jnp.exp(lr * dt)
    ar = mag * jnp.cos(li * dt)
    ai = mag * jnp.sin(li * dt)
    den = lr * lr + li * li
    nr = ar - 1.0
    f_re = (nr * lr + ai * li) / den
    f_im = (ai * lr - nr * li) / den
    bt_re = f_re * btr_ref[...] - f_im * bti_ref[...]
    bt_im = f_re * bti_ref[...] + f_im * btr_ref[...]

    powers = [(jnp.ones_like(ar), jnp.zeros_like(ai))]
    for _ in range(S5_CHUNK):
        powers.append(cmul(*powers[-1], ar, ai))
    row_fwd = lax.broadcasted_iota(jnp.int32, (1, P2), 1) < S5_STATE

    def pow_rows(exp_fwd, exp_bwd):
        return (jnp.where(row_fwd, powers[exp_fwd][0], powers[exp_bwd][0]),
                jnp.where(row_fwd, powers[exp_fwd][1], powers[exp_bwd][1]))

    c_re, c_im = cr_ref[...], ci_ref[...]
    last = S5_CHUNK - 1
    cp_blocks = [cmul(c_re, c_im, *pow_rows(blk, last - blk)) for blk in range(S5_CHUNK)]
    cp_re = jnp.concatenate([b[0] for b in cp_blocks], axis=0)
    cp_im = jnp.concatenate([b[1] for b in cp_blocks], axis=0)
    lane_fwd = lax.broadcasted_iota(jnp.int32, (S5_GROUP, P2), 1) < S5_STATE

    def lag_kernels(keep):
        br = jnp.where(keep, bt_re, 0.0)
        bi = jnp.where(keep, bt_im, 0.0)
        dims = (((1,), (1,)), ((), ()))
        hi = lax.Precision.HIGHEST
        return (lax.dot_general(br, cp_re, dims, precision=hi, preferred_element_type=F32)
                - lax.dot_general(bi, cp_im, dims, precision=hi, preferred_element_type=F32))

    kt_f = lag_kernels(lane_fwd)
    kt_b = lag_kernels(jnp.logical_not(lane_fwd))
    lane_w = lax.broadcasted_iota(jnp.int32, (S5_GROUP, S5_CW), 1)
    for s in range(S5_CHUNK):
        f_part = kt_f if s == 0 else jnp.where(lane_w >= S5_GROUP * s, pltpu.roll(kt_f, S5_GROUP * s, 1), 0.0)
        sh = (S5_GROUP * (s + 1)) % S5_CW
        b_roll = kt_b if sh == 0 else pltpu.roll(kt_b, sh, 1)
        b_part = jnp.where(lane_w < S5_GROUP * (s + 1), b_roll, 0.0)
        m_ref[s * S5_GROUP:(s + 1) * S5_GROUP, :] = (f_part + b_part).astype(BF16)

    for blk in range(S5_CHUNK):
        rows = slice(blk * S5_GROUP, (blk + 1) * S5_GROUP)
        q_re, q_im = cmul(bt_re, bt_im, *pow_rows(last - blk, blk))
        q_ref[rows, 0:P2] = q_re.astype(BF16)
        q_ref[rows, P2:2 * P2] = q_im.astype(BF16)

    for blk in range(S5_CHUNK):
        rows = slice(blk * S5_GROUP, (blk + 1) * S5_GROUP)
        n_re, n_im = cmul(c_re, c_im, *pow_rows(blk + 1, S5_CHUNK - blk))
        n_ref[rows, 0:P2] = n_re.astype(BF16)
        n_ref[rows, P2:2 * P2] = (-n_im).astype(BF16)

    a_ref[0:1, :] = powers[S5_CHUNK][0]
    a_ref[1:2, :] = powers[S5_CHUNK][1]


def _s5_ops(row_params, bt, c_nat):
    P2 = 2 * S5_STATE
    row = pl.BlockSpec((OPS_G, 1, P2), lambda g: (g, 0, 0))
    mat = pl.BlockSpec((OPS_G, S5_GROUP, P2), lambda g: (g, 0, 0))
    sq = pl.BlockSpec((OPS_G, S5_CW, S5_CW), lambda g: (g, 0, 0))
    return pl.pallas_call(
        _s5_ops_kernel,
        out_shape=[jax.ShapeDtypeStruct((S5_GROUPS, S5_CW, S5_CW), BF16)] * 3
        + [jax.ShapeDtypeStruct((S5_GROUPS, 2, P2), F32)],
        grid=(S5_GROUPS // OPS_G,),
        in_specs=[row, row, row, mat, mat, mat, mat],
        out_specs=[sq, sq, sq, pl.BlockSpec((OPS_G, 2, P2), lambda g: (g, 0, 0))],
        compiler_params=_params(),
        name="s5_chunk_operators",
    )(*row_params, *bt, *c_nat)


ROWS = N_CHUNKS * BATCH


SCAN_G = 4


def _s5_scan_kernel(u_ref, m_ref, q_ref, n_ref, a_ref, y_ref, pu_ref, sp_ref):
    S = S5_STATE
    lane = lax.broadcasted_iota(jnp.int32, (BATCH, 2 * S), 1)
    lo = lane < S
    for j in range(SCAN_G):
        pu_ref[j] = _dot(u_ref[j], q_ref[j])
    for j in range(SCAN_G):
        y_ref[j] = _dot(u_ref[j], m_ref[j])
    decay = [(a_ref[j, 0:1, :], a_ref[j, 1:2, :]) for j in range(SCAN_G)]
    zero = jnp.zeros((BATCH, 2 * S), F32)
    state = [(zero, zero)] * SCAN_G
    for k in range(N_CHUNKS):
        cf = k * BATCH
        cb = (N_CTX_CHUNKS - 1 - k if k < N_CTX_CHUNKS else N_CHUNKS + N_CTX_CHUNKS - 1 - k) * BATCH
        for j in range(SCAN_G):
            s_re, s_im = state[j]
            a_re, a_im = decay[j]
            sp_ref[j, cf:cf + BATCH, 0:S] = s_re[:, 0:S]
            sp_ref[j, cb:cb + BATCH, S:2 * S] = s_re[:, S:2 * S]
            sp_ref[j, cf:cf + BATCH, 2 * S:3 * S] = s_im[:, 0:S]
            sp_ref[j, cb:cb + BATCH, 3 * S:4 * S] = s_im[:, S:2 * S]
            x_re = jnp.where(lo, pu_ref[j, cf:cf + BATCH, 0:2 * S], pu_ref[j, cb:cb + BATCH, 0:2 * S])
            x_im = jnp.where(lo, pu_ref[j, cf:cf + BATCH, 2 * S:4 * S], pu_ref[j, cb:cb + BATCH, 2 * S:4 * S])
            state[j] = (a_re * s_re - a_im * s_im + x_re, a_re * s_im + a_im * s_re + x_im)
    for j in range(SCAN_G):
        y_ref[j] += _dot_nt(sp_ref[j].astype(BF16), n_ref[j])


def _s5_scan(u_g, m_op, q_op, n_op, a_vec):
    sq = pl.BlockSpec((SCAN_G, S5_CW, S5_CW), lambda g: (g, 0, 0))
    rows = pl.BlockSpec((SCAN_G, ROWS, S5_CW), lambda g: (g, 0, 0))
    return pl.pallas_call(
        _s5_scan_kernel,
        out_shape=jax.ShapeDtypeStruct((S5_GROUPS, ROWS, S5_CW), F32),
        grid=(S5_GROUPS // SCAN_G,),
        in_specs=[rows, sq, sq, sq, pl.BlockSpec((SCAN_G, 2, 2 * S5_STATE), lambda g: (g, 0, 0))],
        out_specs=rows,
        scratch_shapes=[pltpu.VMEM((SCAN_G, ROWS, S5_CW), F32), pltpu.VMEM((SCAN_G, ROWS, S5_CW), F32)],
        compiler_params=_params(),
        name="s5_scan",
    )(u_g, m_op, q_op, n_op, a_vec)


def _s5_out_kernel(u_ref, y_ref, d_ref, gw_ref, gb_ref, perm_ref, o_ref):
    lane_blk = _lane_block()
    n_seg = D_MODEL // S5_SEG
    per_seg = S5_SEG // LANES

    def regroup(j, s):
        cols = []
        for k in range(s * per_seg, (s + 1) * per_seg):
            by_tok = [None] * S5_TOK
            for c2 in range(S5_PAIR):
                rows = slice((j * S5_PAIR + c2) * BATCH, (j * S5_PAIR + c2 + 1) * BATCH)
                for hh in range(S5_CHUNK // STEPS_PER_VREG):
                    groups = [y_ref[k * GROUPS_PER_VREG + r, rows, hh * LANES:(hh + 1) * LANES]
                              for r in range(GROUPS_PER_VREG)]
                    steps = _block_transpose(groups, lane_blk)
                    for m in range(STEPS_PER_VREG):
                        by_tok[c2 * S5_CHUNK + hh * STEPS_PER_VREG + m] = steps[m]
            cols.append(jnp.concatenate(by_tok, axis=0))
        return jnp.concatenate(cols, axis=1)

    state = {}

    def compute(j, s, ys):
        cols = slice(s * S5_SEG, (s + 1) * S5_SEG)
        u = u_ref[j * S5_TOK:(j + 1) * S5_TOK, :, cols].reshape(BATCH * S5_TOK, S5_SEG)
        g_seg = jax.nn.gelu(u * d_ref[:, cols] + ys)
        part = _dot(g_seg.astype(BF16), gw_ref[cols, :])
        g_parts, pre = state.get(j, ([], None))
        state[j] = (g_parts + [g_seg], part if pre is None else pre + part)
        if s == n_seg - 1:
            g_parts, pre = state[j]
            gate = jax.nn.sigmoid(pre + gb_ref[...])
            gated = (jnp.concatenate(g_parts, axis=1) * gate).astype(BF16)
            gated = _dot(perm_ref[...], gated).astype(BF16)
            o_ref[:, j * S5_TOK:(j + 1) * S5_TOK, :] = gated.reshape(BATCH, S5_TOK, D_MODEL)

    items = [(j, s) for j in range(S5_SUB) for s in range(n_seg)]
    ys_next = regroup(*items[0])
    for i, item in enumerate(items):
        ys = ys_next
        if i + 1 < len(items):
            ys_next = regroup(*items[i + 1])
        compute(*item, ys)


def _s5_out(u, y_g, d_skip, glu_w, glu_b, perm_t):
    ctx_steps = N_CTX_CHUNKS // (S5_PAIR * S5_SUB)
    n_rows = BATCH * S5_TOK
    vec = pl.BlockSpec((1, D_MODEL), lambda p: (0, 0))
    return pl.pallas_call(
        _s5_out_kernel,
        out_shape=jax.ShapeDtypeStruct((BATCH, SEQ, D_MODEL), BF16),
        grid=(S5_STEPS // S5_SUB - ctx_steps,),
        in_specs=[pl.BlockSpec((S5_SUB * S5_TOK, BATCH, D_MODEL), lambda p: (p + ctx_steps, 0, 0)),
                  pl.BlockSpec((S5_GROUPS, S5_SUB * S5_PAIR * BATCH, S5_CW), lambda p: (0, p + ctx_steps, 0)),
                  vec, pl.BlockSpec((D_MODEL, D_MODEL), lambda p: (0, 0)), vec,
                  pl.BlockSpec((n_rows, n_rows), lambda p: (0, 0))],
        out_specs=pl.BlockSpec((BATCH, S5_SUB * S5_TOK, D_MODEL), lambda p: (0, p, 0)),
        compiler_params=_params(),
        name="s5_gelu_glu",
    )(u, y_g, d_skip, glu_w, glu_b, perm_t)


def _rope_tables():
    rows_n = SEQ // GRID_W
    row = np.repeat(np.arange(rows_n, dtype=np.float64), GRID_W)
    col = np.tile(np.arange(GRID_W, dtype=np.float64), rows_n)
    n_freq = HEAD_DIM // 4
    inv = ROPE_BASE ** (-np.arange(n_freq, dtype=np.float64) / n_freq)
    ang = np.concatenate([row[:, None] * inv, col[:, None] * inv], axis=-1)
    reps = LANES // (HEAD_DIM // 2)
    cos_t = np.tile(np.cos(ang), (1, reps))
    sin_t = np.tile(np.sin(ang), (1, reps))
    sign = np.where((np.arange(LANES) % HEAD_DIM) < HEAD_DIM // 2, -1.0, 1.0)
    cos_t = np.concatenate([np.ones((CTX_LEN, LANES)), cos_t], axis=0)
    sin_s = np.concatenate([np.zeros((CTX_LEN, LANES)), sin_t * sign], axis=0)
    return jnp.asarray(cos_t, F32), jnp.asarray(sin_s, F32)


def _fb_rows(x):
    return jnp.transpose(x, (1, 0, 2)).reshape(S5_GROUPS, 1, 2 * S5_STATE)


def _s5_layout(lam_re, lam_im, log_step, b_re, b_im, c_re, c_im):
    ls = jnp.broadcast_to(log_step[:, :, None], lam_re.shape)
    rows = [_fb_rows(v) for v in (lam_re, lam_im, ls)]

    def bt_of(b):
        return jnp.transpose(b, (1, 3, 0, 2)).reshape(S5_GROUPS, S5_GROUP, 2 * S5_STATE)

    def c_of(c):
        return jnp.transpose(c, (1, 2, 0, 3)).reshape(S5_GROUPS, S5_GROUP, 2 * S5_STATE)

    return rows, [bt_of(b_re), bt_of(b_im)], [c_of(c_re), c_of(c_im)]


def kernel(x, c, ctx, c_ctx, norm1_g, norm2_g, mod_w, mod_b, mlp_w1, mlp_w2, attn_w_in, attn_w_out, a_q_norm, a_k_norm, a_sink, b_q_norm, b_k_norm, b_lq1, b_lk1, b_lq2, b_lk2, b_subln, s5_w_in, s5_lambda_re, s5_lambda_im, s5_log_step, s5_b_re, s5_b_im, s5_c_re, s5_c_im, s5_d, s5_glu_w, s5_glu_b, s5_w_out):
    assert x.shape == (BATCH, SEQ, D_MODEL) and ctx.shape == (BATCH, CTX_LEN, D_MODEL)
    stream = (ctx, x)
    s_rows = jnp.concatenate([c, c_ctx[None], jnp.zeros((16 - BATCH - 1, D_MODEL), F32)], axis=0)
    m_all = _modulation(s_rows, mod_w, mod_b)
    cos_t, sin_s = _rope_tables()
    w1_all, w2_all = mlp_w1.astype(BF16), mlp_w2.astype(BF16)
    e_blk = jnp.asarray(np.kron(np.eye(LANES // HEAD_DIM), np.ones((HEAD_DIM, HEAD_DIM))) / HEAD_DIM, BF16)

    for i in range(DEPTH):
        last = i == DEPTH - 1
        j = i // 2
        m_lat = m_all[i, :BATCH].reshape(BATCH, 6, D_MODEL)
        m_ctx = jnp.broadcast_to(m_all[i, BATCH].reshape(1, 6, D_MODEL), (BATCH, 6, D_MODEL))
        mods = jnp.stack([m_ctx, m_lat])
        g1 = norm1_g[i].reshape(1, D_MODEL)
        g2 = norm2_g[i].reshape(1, D_MODEL)
        if i % 2 == 0:
            lambda_init = 0.8 - 0.6 * math.exp(-0.3 * i)
            gains = jnp.stack([jnp.tile(v[j], LANES // HEAD_DIM) for v in (a_q_norm, a_k_norm, b_q_norm, b_k_norm)])
            qa, k2a, v2a, qb, kb, vb = _attn_in(stream, mods, g1, attn_w_in[j].astype(BF16), gains, cos_t, sin_s, e_blk)
            ya = _win_attn(a_sink[j], qa, k2a, v2a)
            lpar = jnp.stack([b_lq1[j], b_lk1[j], b_lq2[j], b_lk2[j]])
            yb = _diff_attn(lpar, b_subln[j].reshape(1, LANES), qb, kb, vb, lambda_init)
            if last:
                ya, yb = ya[:, CTX_LEN:], yb[:, CTX_LEN:]
            ua, ub, wo = ya, yb, attn_w_out[j]
        else:
            src = np.arange(BATCH * S5_TOK).reshape(BATCH, S5_TOK).T.reshape(-1)
            perm = jnp.asarray(np.eye(BATCH * S5_TOK, dtype=np.float32)[src], BF16)
            h_all = stream[0] if stream[0].shape[1] == TT else jnp.concatenate(stream, axis=1)
            u, u_g = _s5_in(h_all, mods, g1, perm, s5_w_in[j].astype(BF16))
            ops_in = _s5_layout(s5_lambda_re[j], s5_lambda_im[j], s5_log_step[j], s5_b_re[j], s5_b_im[j],
                                s5_c_re[j], s5_c_im[j])
            m_op, q_op, n_op, a_vec = _s5_ops(*ops_in)
            y_g = _s5_scan(u_g, m_op, q_op, n_op, a_vec)
            assert last, "S5 layers before the last one would also need the context rows of the readout"
            gated = _s5_out(u, y_g, s5_d[j].reshape(1, D_MODEL), s5_glu_w[j].astype(BF16),
                            s5_glu_b[j].reshape(1, D_MODEL), perm.T)
            ua, ub, wo = gated, gated, s5_w_out[j]
        h = _mix_mlp(stream, ua, ub, mods, g2, wo.astype(BF16), w1_all, w2_all, i, latent_only=last)
        stream = (h, h)
    return h
```

```python
import functools
import math

import jax
import jax.numpy as jnp
import numpy as np
from jax import lax
from jax.experimental import pallas as pl
from jax.experimental.pallas import tpu as pltpu

F32 = jnp.float32
BF16 = jnp.bfloat16

D_MODEL = 1024
BATCH = 8
SEQ = 2048
DEPTH = 2
GRID_W = 64
CTX_LEN = 256
HEAD_DIM = 64
WINDOW = 128
A_Q_HEADS = 8
A_KV_HEADS = 2
B_HEADS = 4
A_Q_W = A_Q_HEADS * HEAD_DIM
A_KV_W = A_KV_HEADS * HEAD_DIM
B_QK_W = B_HEADS * 2 * HEAD_DIM
B_V_W = B_HEADS * 2 * HEAD_DIM
ATTN_IN = A_Q_W + 2 * A_KV_W + 2 * B_QK_W + B_V_W
S5_GROUP = 16
S5_GROUPS = D_MODEL // S5_GROUP
S5_STATE = 64
D_FF = 4 * D_MODEL
ROPE_BASE = 10000.0
EPS = 1e-6
NEG_INF = -1e30
LOG2E = math.log2(math.e)

TT = CTX_LEN + SEQ
TM = 256
N_TILES = TT // TM
LANES = 128
S5_CHUNK = 16
S5_CW = S5_CHUNK * S5_GROUP
N_CHUNKS = TT // S5_CHUNK
N_CTX_CHUNKS = CTX_LEN // S5_CHUNK
VMEM_LIMIT = 56 * 1024 * 1024
N_MOD = 6
MOD_ROWS = 16
GQA = A_Q_HEADS // A_KV_HEADS


def _dot(a, b):
    return jnp.dot(a, b, preferred_element_type=F32)


def _dot_nt(a, b):
    return lax.dot_general(a, b, (((1,), (1,)), ((), ())), preferred_element_type=F32)


def _rms(x):
    return x * lax.rsqrt(jnp.mean(x * x, axis=-1, keepdims=True) + EPS)


def _modnorm(x, g, shift, scale):
    return _rms(x) * g * (1.0 + scale) + shift


def _params(**kw):
    return pltpu.CompilerParams(vmem_limit_bytes=VMEM_LIMIT, **kw)


def _mod_kernel(s_ref, w_ref, b_ref, o_ref):
    s = s_ref[...]
    s = s * jax.nn.sigmoid(s)
    o_ref[...] = _dot(s.astype(BF16), w_ref[...].astype(BF16)) + b_ref[...]


def _modulation(s_rows, mod_w, mod_b):
    return pl.pallas_call(
        _mod_kernel,
        out_shape=jax.ShapeDtypeStruct((DEPTH, MOD_ROWS, N_MOD * D_MODEL), F32),
        grid=(DEPTH, N_MOD),
        in_specs=[
            pl.BlockSpec((MOD_ROWS, D_MODEL), lambda i, j: (0, 0)),
            pl.BlockSpec((None, D_MODEL, D_MODEL), lambda i, j: (i, 0, j)),
            pl.BlockSpec((None, 1, D_MODEL), lambda i, j: (i, 0, j)),
        ],
        out_specs=pl.BlockSpec((None, MOD_ROWS, D_MODEL), lambda i, j: (i, 0, j)),
        compiler_params=_params(),
        name="modulation",
    )(s_rows, mod_w, mod_b.reshape(DEPTH, 1, N_MOD * D_MODEL))


def _stream_specs(stream, n_sub, h_off):
    first_lat = 1 if stream[0].shape[1] == TT else 0

    def tile(t, k):
        return t * n_sub + k + h_off

    ctx_spec = pl.BlockSpec((None, TM, D_MODEL), lambda b, t: (b, 0, 0))
    lat_specs = [pl.BlockSpec((None, TM, D_MODEL),
                              lambda b, t, k=k: (b, jnp.maximum(tile(t, k) - 1 + first_lat, first_lat), 0))
                 for k in range(n_sub)]
    mod_specs = [pl.BlockSpec((None, None, N_MOD, D_MODEL), lambda b, t, k=k: (jnp.minimum(tile(t, k), 1), b, 0, 0))
                 for k in range(n_sub)]
    return ctx_spec, lat_specs, mod_specs


ATTN_IN_SUB = 3


def _attn_in_kernel(*refs):
    n_sub = ATTN_IN_SUB
    hc_ref = refs[0]
    hl_refs = refs[1:1 + n_sub]
    mod_refs = refs[1 + n_sub:1 + 2 * n_sub]
    (g_ref, w_ref, gain_ref, cos_ref, sin_ref, e_ref,
     qa_ref, k2a_ref, v2a_ref, qb_ref, kb_ref, vb_ref) = refs[1 + 2 * n_sub:]
    e = e_ref[...]
    gains = gain_ref[...]
    lane = lax.broadcasted_iota(jnp.int32, (TM, LANES), 1)
    first_half = (lane & (HEAD_DIM - 1)) < HEAD_DIM // 2
    lo = lane < HEAD_DIM
    q_scale = HEAD_DIM ** -0.5 * LOG2E
    ones = jnp.ones((TM, LANES), BF16)

    def tile_segments(k):
        rows = slice(k * TM, (k + 1) * TM)
        cos_t = cos_ref[rows, :]
        sin_s = sin_ref[rows, :]

        def head_mean_sq(z):
            sq = (z * z).astype(BF16)
            width = z.shape[1]
            if width < 2 * LANES:
                return _dot(sq, e[0:width, 0:width])
            return jnp.concatenate([_dot(sq[:, c:c + 2 * LANES], e) for c in range(0, width, 2 * LANES)], axis=1)

        def norm_rope_chunks(z, gain):
            ms = head_mean_sq(z)
            for c in range(z.shape[1] // LANES):
                cols = slice(c * LANES, (c + 1) * LANES)
                cn = z[:, cols] * lax.rsqrt(ms[:, cols] + EPS) * gain
                r_fwd = pltpu.roll(cn, HEAD_DIM // 2, 1)
                r_bwd = pltpu.roll(cn, LANES - HEAD_DIM // 2, 1)
                yield cn * cos_t + jnp.where(first_half, r_bwd, r_fwd) * sin_s

        def dup_halves(x):
            sw = pltpu.roll(x, HEAD_DIM, 1)
            return jnp.where(lo, x, sw).astype(BF16), jnp.where(lo, sw, x).astype(BF16)

        def finish_q(z, ref, gain):
            for c, chunk in enumerate(norm_rope_chunks(z, gain)):
                ref[rows, c * LANES:(c + 1) * LANES] = (chunk * q_scale).astype(BF16)

        def finish_kv_a(z):
            (k_roped,) = norm_rope_chunks(z[:, 0:LANES], gains[1:2])
            for kvh, dup in enumerate(dup_halves(k_roped)):
                k2a_ref[rows, kvh * LANES:(kvh + 1) * LANES] = dup
            for kvh, dup in enumerate(dup_halves(z[:, LANES:2 * LANES])):
                v2a_ref[rows, 2 * kvh * LANES:(2 * kvh + 1) * LANES] = dup
                v2a_ref[rows, (2 * kvh + 1) * LANES:(2 * kvh + 2) * LANES] = ones

        def finish_kb(z):
            for c, chunk in enumerate(norm_rope_chunks(z, gains[3:4])):
                kb_ref[rows, c * LANES:(c + 1) * LANES] = chunk.astype(BF16)

        def finish_vb(z):
            for hd in range(B_HEADS):
                vb_ref[rows, 2 * hd * LANES:(2 * hd + 1) * LANES] = z[:, hd * LANES:(hd + 1) * LANES].astype(BF16)
                vb_ref[rows, (2 * hd + 1) * LANES:(2 * hd + 2) * LANES] = ones

        return [(A_Q_W, lambda z: finish_q(z, qa_ref, gains[0:1])), (2 * A_KV_W, finish_kv_a),
                (B_QK_W, lambda z: finish_q(z, qb_ref, gains[2:3])), (B_QK_W, finish_kb), (B_V_W, finish_vb)]

    def normed(k):
        mod = mod_refs[k][...]
        x = hl_refs[k][...]
        if k == 0:
            x = jnp.where(pl.program_id(1) == 0, hc_ref[...], x)
        return _modnorm(x, g_ref[...], mod[0:1], mod[1:2]).astype(BF16)

    pending = None
    for k in range(n_sub):
        a = normed(k)
        off = 0
        for width, finish in tile_segments(k):
            z = _dot(a, w_ref[:, off:off + width])
            off += width
            if pending is not None:
                pending[1](pending[0])
            pending = (z, finish)
    pending[1](pending[0])


def _attn_in(stream, mods, g, w_in, gains, cos_t, sin_s, e_blk):
    n_sub = ATTN_IN_SUB
    ctx_spec, lat_specs, mod_specs = _stream_specs(stream, n_sub, 0)

    def tok(width):
        return pl.BlockSpec((None, n_sub * TM, width), lambda b, t: (b, t, 0))

    def full(shape):
        return pl.BlockSpec(shape, lambda b, t: (0,) * len(shape))

    out_shapes = [jax.ShapeDtypeStruct((BATCH, TT, w), BF16)
                  for w in (A_Q_W, 2 * A_KV_W, 4 * A_KV_W, B_QK_W, B_QK_W, 2 * B_V_W)]
    return pl.pallas_call(
        _attn_in_kernel,
        out_shape=out_shapes,
        grid=(BATCH, N_TILES // n_sub),
        in_specs=[
            ctx_spec, *lat_specs, *mod_specs,
            full((1, D_MODEL)), full((D_MODEL, ATTN_IN)), full((4, LANES)),
            pl.BlockSpec((n_sub * TM, LANES), lambda b, t: (t, 0)),
            pl.BlockSpec((n_sub * TM, LANES), lambda b, t: (t, 0)),
            full((2 * LANES, 2 * LANES)),
        ],
        out_specs=[tok(A_Q_W), tok(2 * A_KV_W), tok(4 * A_KV_W), tok(B_QK_W), tok(B_QK_W), tok(2 * B_V_W)],
        compiler_params=_params(),
        name="attn_in_proj",
    )(stream[0], *[stream[1]] * n_sub, *[mods] * n_sub, g, w_in, gains, cos_t, sin_s, e_blk)


QB = 128


def _win_attn_kernel(sink_ref, q_ref, k2_ref, v2_ref, o_ref):
    t = pl.program_id(1)
    lane = lax.broadcasted_iota(jnp.int32, (QB, LANES), 1)
    lo = lane < HEAD_DIM
    rows = GQA * QB
    row = lax.broadcasted_iota(jnp.int32, (rows, 3 * QB), 0)
    col = lax.broadcasted_iota(jnp.int32, (rows, 3 * QB), 1)
    row_head = lax.broadcasted_iota(jnp.int32, (rows, 1), 0) // QB
    zero = jnp.zeros((QB, LANES), BF16)
    blocks = [(qb, g) for qb in range(TM // QB) for g in range(A_KV_HEADS)]

    def window_start(qb):
        n = (t - 1) * (TM // QB) + qb
        ws = jnp.clip((n - 1) * QB, 0, SEQ - 3 * QB)
        return n, ws

    def scores(qb, g, with_window):
        pieces = []
        for p in range(2):
            qp = q_ref[qb * QB:(qb + 1) * QB, g * 2 * LANES + p * LANES: g * 2 * LANES + (p + 1) * LANES]
            pieces.append(jnp.where(lo, qp, zero))
            pieces.append(jnp.where(lo, zero, qp))
        qs = jnp.concatenate(pieces, axis=0)
        s_c = _dot_nt(qs, k2_ref[0:CTX_LEN, g * LANES:(g + 1) * LANES])
        if not with_window:
            return s_c, None
        n, ws = window_start(qb)
        kw = k2_ref[pl.ds(pl.multiple_of(ws + CTX_LEN, QB), 3 * QB), g * LANES:(g + 1) * LANES]
        valid = jnp.abs(n * QB + (row & (QB - 1)) - (ws + col)) <= WINDOW
        return s_c, jnp.where(valid, _dot_nt(qs, kw), NEG_INF)

    def finish(qb, g, s_c, s_w):
        sk = jnp.full((rows, 1), sink_ref[GQA * g + GQA - 1], F32)
        for hh in range(GQA - 1):
            sk = jnp.where(row_head == hh, sink_ref[GQA * g + hh], sk)
        sk = sk * LOG2E
        m = jnp.maximum(jnp.max(s_c, axis=-1, keepdims=True), sk)
        if s_w is not None:
            m = jnp.maximum(m, jnp.max(s_w, axis=-1, keepdims=True))
        vcols = slice(2 * g * LANES, (2 * g + 2) * LANES)
        pv = _dot(jnp.exp2(s_c - m).astype(BF16), v2_ref[0:CTX_LEN, vcols])
        if s_w is not None:
            _, ws = window_start(qb)
            vw = v2_ref[pl.ds(pl.multiple_of(ws + CTX_LEN, QB), 3 * QB), vcols]
            pv = pv + _dot(jnp.exp2(s_w - m).astype(BF16), vw)
        o = pv[:, 0:LANES] / (pv[:, LANES:2 * LANES] + jnp.exp2(sk - m))
        for p in range(2):
            o_ref[qb * QB:(qb + 1) * QB, g * 2 * LANES + p * LANES: g * 2 * LANES + (p + 1) * LANES] = jnp.where(
                lo, o[2 * p * QB:(2 * p + 1) * QB], o[(2 * p + 1) * QB:(2 * p + 2) * QB]).astype(BF16)

    def attend(with_window):
        s_next = scores(*blocks[0], with_window)
        for i, blk in enumerate(blocks):
            s_cur = s_next
            if i + 1 < len(blocks):
                s_next = scores(*blocks[i + 1], with_window)
            finish(*blk, *s_cur)

    @pl.when(t == 0)
    def _():
        attend(False)

    @pl.when(t > 0)
    def _():
        attend(True)


def _win_attn(sink, qa, k2a, v2a):
    return pl.pallas_call(
        _win_attn_kernel,
        out_shape=jax.ShapeDtypeStruct((BATCH, TT, A_Q_W), BF16),
        grid=(BATCH, N_TILES),
        in_specs=[
            pl.BlockSpec(memory_space=pltpu.SMEM),
            pl.BlockSpec((None, TM, A_Q_W), lambda b, t: (b, t, 0)),
            pl.BlockSpec((None, TT, 2 * A_KV_W), lambda b, t: (b, 0, 0)),
            pl.BlockSpec((None, TT, 4 * A_KV_W), lambda b, t: (b, 0, 0)),
        ],
        out_specs=pl.BlockSpec((None, TM, A_Q_W), lambda b, t: (b, t, 0)),
        compiler_params=_params(),
        name="window_attention",
    )(sink, qa, k2a, v2a)


DIFF_ROWS = 256


def _diff_attn_kernel(lpar_ref, subln_ref, q_ref, k_ref, v_ref, o_ref, *, lambda_init):
    t = pl.program_id(1)
    lp = lpar_ref[...]
    lam = (jnp.exp(jnp.sum(lp[0:1] * lp[1:2], axis=-1, keepdims=True))
           - jnp.exp(jnp.sum(lp[2:3] * lp[3:4], axis=-1, keepdims=True)) + lambda_init)
    R = DIFF_ROWS
    lane = lax.broadcasted_iota(jnp.int32, (R, LANES), 1)
    lo = lane < HEAD_DIM
    zero = jnp.zeros((R, LANES), BF16)

    def attend(n_keys):
        blocks = [(slice(rb * R, (rb + 1) * R), slice(h * LANES, (h + 1) * LANES))
                  for h in range(B_HEADS) for rb in range(TM // R)]

        def scores(rows, cols):
            q = q_ref[rows, cols]
            qs = jnp.concatenate([jnp.where(lo, q, zero), jnp.where(lo, zero, q)], axis=0)
            return _dot_nt(qs, k_ref[0:n_keys, cols])

        def finish(rows, cols, s):
            p = jnp.exp2(s - jnp.max(s, axis=-1, keepdims=True)).astype(BF16)
            vcols = slice(2 * cols.start, 2 * cols.stop)
            pv = _dot(p, v_ref[0:n_keys, vcols])
            sm = pv[:, 0:LANES] / pv[:, LANES:2 * LANES]
            y = sm[0:R] - lam * sm[R:2 * R]
            o_ref[rows, cols] = (_rms(y) * subln_ref[...] * (1.0 - lambda_init)).astype(BF16)

        s_next = scores(*blocks[0])
        for i, blk in enumerate(blocks):
            s_cur = s_next
            if i + 1 < len(blocks):
                s_next = scores(*blocks[i + 1])
            finish(*blk, s_cur)

    @pl.when(t == 0)
    def _():
        attend(CTX_LEN)

    @pl.when(t > 0)
    def _():
        attend(TT)


def _diff_attn(lpar, subln, qb, kb, vb, lambda_init):
    return pl.pallas_call(
        functools.partial(_diff_attn_kernel, lambda_init=lambda_init),
        out_shape=jax.ShapeDtypeStruct((BATCH, TT, B_V_W), BF16),
        grid=(BATCH, N_TILES),
        in_specs=[
            pl.BlockSpec((4, HEAD_DIM), lambda b, t: (0, 0)),
            pl.BlockSpec((1, LANES), lambda b, t: (0, 0)),
            pl.BlockSpec((None, TM, B_QK_W), lambda b, t: (b, t, 0)),
            pl.BlockSpec((None, TT, B_QK_W), lambda b, t: (b, 0, 0)),
            pl.BlockSpec((None, TT, 2 * B_V_W), lambda b, t: (b, 0, 0)),
        ],
        out_specs=pl.BlockSpec((None, TM, B_V_W), lambda b, t: (b, t, 0)),
        compiler_params=_params(),
        name="diff_attention",
    )(lpar, subln, qb, kb, vb)


FF_CHUNK = 1024


def _mix_mlp_kernel(*refs, h_off, n_sub):
    hc_ref = refs[0]
    hl_refs, ua_refs, ub_refs, mod_refs = (refs[1 + i * n_sub:1 + (i + 1) * n_sub] for i in range(4))
    g_ref, wo_ref, w1_ref, w2_ref, o_ref = refs[1 + 4 * n_sub:]
    half = D_MODEL // 2

    def prologue(k):
        mod = mod_refs[k][...]
        y = _dot(ua_refs[k][...], wo_ref[0:half, :]) + _dot(ub_refs[k][...], wo_ref[half:D_MODEL, :])
        x = hl_refs[k][...]
        if h_off == 0 and k == 0:
            x = jnp.where(pl.program_id(1) == 0, hc_ref[...], x)
        h1 = x + mod[2:3] * y
        f = _modnorm(h1, g_ref[...], mod[3:4], mod[4:5]).astype(BF16)
        return h1, f, mod[5:6]

    def mlp(k, h1, f, gate):
        acc = jnp.zeros((TM, D_MODEL), F32)
        for c in range(D_FF // FF_CHUNK):
            hid = jnp.maximum(_dot(f, w1_ref[:, c * FF_CHUNK:(c + 1) * FF_CHUNK]), 0.0)
            acc = acc + _dot((hid * hid).astype(BF16), w2_ref[c * FF_CHUNK:(c + 1) * FF_CHUNK, :])
        o_ref[k * TM:(k + 1) * TM, :] = h1 + gate * acc

    nxt = prologue(0)
    for k in range(n_sub):
        cur = nxt
        if k + 1 < n_sub:
            nxt = prologue(k + 1)
        mlp(k, *cur)


def _mix_mlp(stream, ua, ub, mods, g, wo, w1_all, w2_all, layer, *, latent_only):
    n_tiles = SEQ // TM if latent_only else N_TILES
    h_off = N_TILES - n_tiles
    n_sub = 4 if n_tiles % 4 == 0 else 3
    half = D_MODEL // 2
    ub_col = 1 if ub.shape[-1] == D_MODEL else 0
    ctx_spec, lat_specs, mod_specs = _stream_specs(stream, n_sub, h_off)

    def full(shape):
        return pl.BlockSpec(shape, lambda b, t: (0,) * len(shape), pipeline_mode=pl.Buffered(1))

    def mixer_specs(col):
        return [pl.BlockSpec((None, TM, half), lambda b, t, k=k: (b, t * n_sub + k, col)) for k in range(n_sub)]

    return pl.pallas_call(
        functools.partial(_mix_mlp_kernel, h_off=h_off, n_sub=n_sub),
        out_shape=jax.ShapeDtypeStruct((BATCH, n_tiles * TM, D_MODEL), F32),
        grid=(BATCH, n_tiles // n_sub),
        in_specs=[
            ctx_spec, *lat_specs, *mixer_specs(0), *mixer_specs(ub_col), *mod_specs,
            pl.BlockSpec((1, D_MODEL), lambda b, t: (0, 0)),
            full((D_MODEL, D_MODEL)),
            pl.BlockSpec((None, D_MODEL, D_FF), lambda b, t: (layer, 0, 0), pipeline_mode=pl.Buffered(1)),
            pl.BlockSpec((None, D_FF, D_MODEL), lambda b, t: (layer, 0, 0), pipeline_mode=pl.Buffered(1)),
        ],
        out_specs=pl.BlockSpec((None, n_sub * TM, D_MODEL), lambda b, t: (b, t, 0)),
        compiler_params=_params(),
        name="mixer_out_mlp",
    )(stream[0], *[stream[1]] * n_sub, *[ua] * n_sub, *[ub] * n_sub, *[mods] * n_sub, g, wo, w1_all, w2_all)


S5_PAIR = 2
S5_TOK = S5_PAIR * S5_CHUNK
S5_STEPS = N_CHUNKS // S5_PAIR
GROUPS_PER_VREG = LANES // S5_GROUP
STEPS_PER_VREG = LANES // S5_GROUP
S5_SEG = 2 * LANES


def _lane_block():
    return lax.broadcasted_iota(jnp.int32, (BATCH, LANES), 1) // S5_GROUP


def _block_transpose(xs, lane_blk):
    xs = list(xs)
    n = len(xs)
    d = n // 2
    while d:
        low = (lane_blk & d) == 0
        for i in range(n):
            if i & d:
                continue
            a, b = xs[i], xs[i + d]
            xs[i] = jnp.where(low, a, pltpu.roll(b, S5_GROUP * d, 1))
            xs[i + d] = jnp.where(low, pltpu.roll(a, LANES - S5_GROUP * d, 1), b)
        d //= 2
    return xs


S5_SUB = 4


def _s5_in_kernel(*refs):
    h_refs = refs[:S5_SUB]
    mod_ref, g_ref, perm_ref, w_ref, u_ref, z_ref = refs[S5_SUB:]
    mod = mod_ref[...]
    lane_blk = _lane_block()

    def normed(j):
        x = h_refs[j][...]
        a = _rms(x) * g_ref[...] * (1.0 + mod[:, 1:2, :]) + mod[:, 0:1, :]
        a = a.reshape(BATCH * S5_TOK, D_MODEL).astype(BF16)
        return _dot(perm_ref[...], a).astype(BF16)

    def finish(j, s, u_seg):
        u_ref[j * S5_TOK:(j + 1) * S5_TOK, :, s * S5_SEG:(s + 1) * S5_SEG] = u_seg.reshape(S5_TOK, BATCH, S5_SEG)
        for kk in range(S5_SEG // LANES):
            k = s * (S5_SEG // LANES) + kk
            u_col = u_seg[:, kk * LANES:(kk + 1) * LANES]
            for hh in range(S5_CHUNK // STEPS_PER_VREG):
                halves = []
                for c2 in range(S5_PAIR):
                    tok0 = c2 * S5_CHUNK + hh * STEPS_PER_VREG
                    steps = [u_col[(tok0 + m) * BATCH:(tok0 + m + 1) * BATCH, :] for m in range(STEPS_PER_VREG)]
                    halves.append(_block_transpose(steps, lane_blk))
                rows = slice(j * S5_PAIR * BATCH, (j + 1) * S5_PAIR * BATCH)
                for r in range(GROUPS_PER_VREG):
                    z_ref[k * GROUPS_PER_VREG + r, rows, hh * LANES:(hh + 1) * LANES] = (
                        jnp.concatenate([h[r] for h in halves], axis=0).astype(BF16))

    pending = None
    for j in range(S5_SUB):
        a = normed(j)
        for s in range(D_MODEL // S5_SEG):
            u_seg = _dot(a, w_ref[:, s * S5_SEG:(s + 1) * S5_SEG])
            if pending is not None:
                finish(*pending)
            pending = (j, s, u_seg)
    finish(*pending)


def _s5_in(h, mods, g, perm, w_in):
    ctx_steps = N_CTX_CHUNKS // (S5_PAIR * S5_SUB)
    n_rows = BATCH * S5_TOK
    return pl.pallas_call(
        _s5_in_kernel,
        out_shape=[jax.ShapeDtypeStruct((TT, BATCH, D_MODEL), F32),
                   jax.ShapeDtypeStruct((S5_GROUPS, ROWS, S5_CW), BF16)],
        grid=(S5_STEPS // S5_SUB,),
        in_specs=[*[pl.BlockSpec((BATCH, S5_TOK, D_MODEL), lambda p, j=j: (0, p * S5_SUB + j, 0))
                    for j in range(S5_SUB)],
                  pl.BlockSpec((None, BATCH, N_MOD, D_MODEL), lambda p: (jnp.minimum(p // ctx_steps, 1), 0, 0, 0)),
                  pl.BlockSpec((1, D_MODEL), lambda p: (0, 0)),
                  pl.BlockSpec((n_rows, n_rows), lambda p: (0, 0)),
                  pl.BlockSpec((D_MODEL, D_MODEL), lambda p: (0, 0))],
        out_specs=[pl.BlockSpec((S5_SUB * S5_TOK, BATCH, D_MODEL), lambda p: (p, 0, 0)),
                   pl.BlockSpec((S5_GROUPS, S5_SUB * S5_PAIR * BATCH, S5_CW), lambda p: (0, p, 0))],
        compiler_params=_params(),
        name="s5_in_proj",
    )(*[h] * S5_SUB, mods, g, perm, w_in)


OPS_G = 4


def _s5_ops_kernel(*refs):
    for j in range(OPS_G):
        _s5_group_ops(*(r.at[j] for r in refs))


def _s5_group_ops(lr_ref, li_ref, ls_ref, btr_ref, bti_ref, cr_ref, ci_ref, m_ref, q_ref, n_ref, a_ref):
    P2 = 2 * S5_STATE

    def cmul(xr, xi, yr, yi):
        return xr * yr - xi * yi, xr * yi + xi * yr

    lr, li = lr_ref[...], li_ref[...]
    dt = jnp.exp(ls_ref[...])
    mag = jnp.exp(lr * dt)
    ar = mag * jnp.cos(li * dt)
    ai = mag * jnp.sin(li * dt)
    den = lr * lr + li * li
    nr = ar - 1.0
    f_re = (nr * lr + ai * li) / den
    f_im = (ai * lr - nr * li) / den
    bt_re = f_re * btr_ref[...] - f_im * bti_ref[...]
    bt_im = f_re * bti_ref[...] + f_im * btr_ref[...]

    powers = [(jnp.ones_like(ar), jnp.zeros_like(ai))]
    for _ in range(S5_CHUNK):
        powers.append(cmul(*powers[-1], ar, ai))
    row_fwd = lax.broadcasted_iota(jnp.int32, (1, P2), 1) < S5_STATE

    def pow_rows(exp_fwd, exp_bwd):
        return (jnp.where(row_fwd, powers[exp_fwd][0], powers[exp_bwd][0]),
                jnp.where(row_fwd, powers[exp_fwd][1], powers[exp_bwd][1]))

    c_re, c_im = cr_ref[...], ci_ref[...]
    last = S5_CHUNK - 1
    cp_blocks = [cmul(c_re, c_im, *pow_rows(blk, last - blk)) for blk in range(S5_CHUNK)]
    cp_re = jnp.concatenate([b[0] for b in cp_blocks], axis=0)
    cp_im = jnp.concatenate([b[1] for b in cp_blocks], axis=0)
    lane_fwd = lax.broadcasted_iota(jnp.int32, (S5_GROUP, P2), 1) < S5_STATE

    def lag_kernels(keep):
        br = jnp.where(keep, bt_re, 0.0)
        bi = jnp.where(keep, bt_im, 0.0)
        dims = (((1,), (1,)), ((), ()))
        hi = lax.Precision.HIGHEST
        return (lax.dot_general(br, cp_re, dims, precision=hi, preferred_element_type=F32)
                - lax.dot_general(bi, cp_im, dims, precision=hi, preferred_element_type=F32))

    kt_f = lag_kernels(lane_fwd)
    kt_b = lag_kernels(jnp.logical_not(lane_fwd))
    lane_w = lax.broadcasted_iota(jnp.int32, (S5_GROUP, S5_CW), 1)
    for s in range(S5_CHUNK):
        f_part = kt_f if s == 0 else jnp.where(lane_w >= S5_GROUP * s, pltpu.roll(kt_f, S5_GROUP * s, 1), 0.0)
        sh = (S5_GROUP * (s + 1)) % S5_CW
        b_roll = kt_b if sh == 0 else pltpu.roll(kt_b, sh, 1)
        b_part = jnp.where(lane_w < S5_GROUP * (s + 1), b_roll, 0.0)
        m_ref[s * S5_GROUP:(s + 1) * S5_GROUP, :] = (f_part + b_part).astype(BF16)

    for blk in range(S5_CHUNK):
        rows = slice(blk * S5_GROUP, (blk + 1) * S5_GROUP)
        q_re, q_im = cmul(bt_re, bt_im, *pow_rows(last - blk, blk))
        q_ref[rows, 0:P2] = q_re.astype(BF16)
        q_ref[rows, P2:2 * P2] = q_im.astype(BF16)

    for blk in range(S5_CHUNK):
        rows = slice(blk * S5_GROUP, (blk + 1) * S5_GROUP)
        n_re, n_im = cmul(c_re, c_im, *pow_rows(blk + 1, S5_CHUNK - blk))
        n_ref[rows, 0:P2] = n_re.astype(BF16)
        n_ref[rows, P2:2 * P2] = (-n_im).astype(BF16)

    a_ref[0:1, :] = powers[S5_CHUNK][0]
    a_ref[1:2, :] = powers[S5_CHUNK][1]


def _s5_ops(row_params, bt, c_nat):
    P2 = 2 * S5_STATE
    row = pl.BlockSpec((OPS_G, 1, P2), lambda g: (g, 0, 0))
    mat = pl.BlockSpec((OPS_G, S5_GROUP, P2), lambda g: (g, 0, 0))
    sq = pl.BlockSpec((OPS_G, S5_CW, S5_CW), lambda g: (g, 0, 0))
    return pl.pallas_call(
        _s5_ops_kernel,
        out_shape=[jax.ShapeDtypeStruct((S5_GROUPS, S5_CW, S5_CW), BF16)] * 3
        + [jax.ShapeDtypeStruct((S5_GROUPS, 2, P2), F32)],
        grid=(S5_GROUPS // OPS_G,),
        in_specs=[row, row, row, mat, mat, mat, mat],
        out_specs=[sq, sq, sq, pl.BlockSpec((OPS_G, 2, P2), lambda g: (g, 0, 0))],
        compiler_params=_params(),
        name="s5_chunk_operators",
    )(*row_params, *bt, *c_nat)


ROWS = N_CHUNKS * BATCH


SCAN_G = 4


def _s5_scan_kernel(u_ref, m_ref, q_ref, n_ref, a_ref, y_ref, pu_ref, sp_ref):
    S = S5_STATE
    lane = lax.broadcasted_iota(jnp.int32, (BATCH, 2 * S), 1)
    lo = lane < S
    for j in range(SCAN_G):
        pu_ref[j] = _dot(u_ref[j], q_ref[j])
    for j in range(SCAN_G):
        y_ref[j] = _dot(u_ref[j], m_ref[j])
    decay = [(a_ref[j, 0:1, :], a_ref[j, 1:2, :]) for j in range(SCAN_G)]
    zero = jnp.zeros((BATCH, 2 * S), F32)
    state = [(zero, zero)] * SCAN_G
    for k in range(N_CHUNKS):
        cf = k * BATCH
        cb = (N_CTX_CHUNKS - 1 - k if k < N_CTX_CHUNKS else N_CHUNKS + N_CTX_CHUNKS - 1 - k) * BATCH
        for j in range(SCAN_G):
            s_re, s_im = state[j]
            a_re, a_im = decay[j]
            sp_ref[j, cf:cf + BATCH, 0:S] = s_re[:, 0:S]
            sp_ref[j, cb:cb + BATCH, S:2 * S] = s_re[:, S:2 * S]
            sp_ref[j, cf:cf + BATCH, 2 * S:3 * S] = s_im[:, 0:S]
            sp_ref[j, cb:cb + BATCH, 3 * S:4 * S] = s_im[:, S:2 * S]
            x_re = jnp.where(lo, pu_ref[j, cf:cf + BATCH, 0:2 * S], pu_ref[j, cb:cb + BATCH, 0:2 * S])
            x_im = jnp.where(lo, pu_ref[j, cf:cf + BATCH, 2 * S:4 * S], pu_ref[j, cb:cb + BATCH, 2 * S:4 * S])
            state[j] = (a_re * s_re - a_im * s_im + x_re, a_re * s_im + a_im * s_re + x_im)
    for j in range(SCAN_G):
        y_ref[j] += _dot_nt(sp_ref[j].astype(BF16), n_ref[j])


def _s5_scan(u_g, m_op, q_op, n_op, a_vec):
    sq = pl.BlockSpec((SCAN_G, S5_CW, S5_CW), lambda g: (g, 0, 0))
    rows = pl.BlockSpec((SCAN_G, ROWS, S5_CW), lambda g: (g, 0, 0))
    return pl.pallas_call(
        _s5_scan_kernel,
        out_shape=jax.ShapeDtypeStruct((S5_GROUPS, ROWS, S5_CW), F32),
        grid=(S5_GROUPS // SCAN_G,),
        in_specs=[rows, sq, sq, sq, pl.BlockSpec((SCAN_G, 2, 2 * S5_STATE), lambda g: (g, 0, 0))],
        out_specs=rows,
        scratch_shapes=[pltpu.VMEM((SCAN_G, ROWS, S5_CW), F32), pltpu.VMEM((SCAN_G, ROWS, S5_CW), F32)],
        compiler_params=_params(),
        name="s5_scan",
    )(u_g, m_op, q_op, n_op, a_vec)


def _s5_out_kernel(u_ref, y_ref, d_ref, gw_ref, gb_ref, perm_ref, o_ref):
    lane_blk = _lane_block()
    n_seg = D_MODEL // S5_SEG
    per_seg = S5_SEG // LANES

    def regroup(j, s):
        cols = []
        for k in range(s * per_seg, (s + 1) * per_seg):
            by_tok = [None] * S5_TOK
            for c2 in range(S5_PAIR):
                rows = slice((j * S5_PAIR + c2) * BATCH, (j * S5_PAIR + c2 + 1) * BATCH)
                for hh in range(S5_CHUNK // STEPS_PER_VREG):
                    groups = [y_ref[k * GROUPS_PER_VREG + r, rows, hh * LANES:(hh + 1) * LANES]
                              for r in range(GROUPS_PER_VREG)]
                    steps = _block_transpose(groups, lane_blk)
                    for m in range(STEPS_PER_VREG):
                        by_tok[c2 * S5_CHUNK + hh * STEPS_PER_VREG + m] = steps[m]
            cols.append(jnp.concatenate(by_tok, axis=0))
        return jnp.concatenate(cols, axis=1)

    state = {}

    def compute(j, s, ys):
        cols = slice(s * S5_SEG, (s + 1) * S5_SEG)
        u = u_ref[j * S5_TOK:(j + 1) * S5_TOK, :, cols].reshape(BATCH * S5_TOK, S5_SEG)
        g_seg = jax.nn.gelu(u * d_ref[:, cols] + ys)
        part = _dot(g_seg.astype(BF16), gw_ref[cols, :])
        g_parts, pre = state.get(j, ([], None))
        state[j] = (g_parts + [g_seg], part if pre is None else pre + part)
        if s == n_seg - 1:
            g_parts, pre = state[j]
            gate = jax.nn.sigmoid(pre + gb_ref[...])
            gated = (jnp.concatenate(g_parts, axis=1) * gate).astype(BF16)
            gated = _dot(perm_ref[...], gated).astype(BF16)
            o_ref[:, j * S5_TOK:(j + 1) * S5_TOK, :] = gated.reshape(BATCH, S5_TOK, D_MODEL)

    items = [(j, s) for j in range(S5_SUB) for s in range(n_seg)]
    ys_next = regroup(*items[0])
    for i, item in enumerate(items):
        ys = ys_next
        if i + 1 < len(items):
            ys_next = regroup(*items[i + 1])
        compute(*item, ys)


def _s5_out(u, y_g, d_skip, glu_w, glu_b, perm_t):
    ctx_steps = N_CTX_CHUNKS // (S5_PAIR * S5_SUB)
    n_rows = BATCH * S5_TOK
    vec = pl.BlockSpec((1, D_MODEL), lambda p: (0, 0))
    return pl.pallas_call(
        _s5_out_kernel,
        out_shape=jax.ShapeDtypeStruct((BATCH, SEQ, D_MODEL), BF16),
        grid=(S5_STEPS // S5_SUB - ctx_steps,),
        in_specs=[pl.BlockSpec((S5_SUB * S5_TOK, BATCH, D_MODEL), lambda p: (p + ctx_steps, 0, 0)),
                  pl.BlockSpec((S5_GROUPS, S5_SUB * S5_PAIR * BATCH, S5_CW), lambda p: (0, p + ctx_steps, 0)),
                  vec, pl.BlockSpec((D_MODEL, D_MODEL), lambda p: (0, 0)), vec,
                  pl.BlockSpec((n_rows, n_rows), lambda p: (0, 0))],
        out_specs=pl.BlockSpec((BATCH, S5_SUB * S5_TOK, D_MODEL), lambda p: (0, p, 0)),
        compiler_params=_params(),
        name="s5_gelu_glu",
    )(u, y_g, d_skip, glu_w, glu_b, perm_t)


def _rope_tables():
    rows_n = SEQ // GRID_W
    row = np.repeat(np.arange(rows_n, dtype=np.float64), GRID_W)
    col = np.tile(np.arange(GRID_W, dtype=np.float64), rows_n)
    n_freq = HEAD_DIM // 4
    inv = ROPE_BASE ** (-np.arange(n_freq, dtype=np.float64) / n_freq)
    ang = np.concatenate([row[:, None] * inv, col[:, None] * inv], axis=-1)
    reps = LANES // (HEAD_DIM // 2)
    cos_t = np.tile(np.cos(ang), (1, reps))
    sin_t = np.tile(np.sin(ang), (1, reps))
    sign = np.where((np.arange(LANES) % HEAD_DIM) < HEAD_DIM // 2, -1.0, 1.0)
    cos_t = np.concatenate([np.ones((CTX_LEN, LANES)), cos_t], axis=0)
    sin_s = np.concatenate([np.zeros((CTX_LEN, LANES)), sin_t * sign], axis=0)
    return jnp.asarray(cos_t, F32), jnp.asarray(sin_s, F32)


def _fb_rows(x):
    return jnp.transpose(x, (1, 0, 2)).reshape(S5_GROUPS, 1, 2 * S5_STATE)


def _s5_layout(lam_re, lam_im, log_step, b_re, b_im, c_re, c_im):
    ls = jnp.broadcast_to(log_step[:, :, None], lam_re.shape)
    rows = [_fb_rows(v) for v in (lam_re, lam_im, ls)]

    def bt_of(b):
        return jnp.transpose(b, (1, 3, 0, 2)).reshape(S5_GROUPS, S5_GROUP, 2 * S5_STATE)

    def c_of(c):
        return jnp.transpose(c, (1, 2, 0, 3)).reshape(S5_GROUPS, S5_GROUP, 2 * S5_STATE)

    return rows, [bt_of(b_re), bt_of(b_im)], [c_of(c_re), c_of(c_im)]


def kernel(x, c, ctx, c_ctx, norm1_g, norm2_g, mod_w, mod_b, mlp_w1, mlp_w2, attn_w_in, attn_w_out, a_q_norm, a_k_norm, a_sink, b_q_norm, b_k_norm, b_lq1, b_lk1, b_lq2, b_lk2, b_subln, s5_w_in, s5_lambda_re, s5_lambda_im, s5_log_step, s5_b_re, s5_b_im, s5_c_re, s5_c_im, s5_d, s5_glu_w, s5_glu_b, s5_w_out):
    assert x.shape == (BATCH, SEQ, D_MODEL) and ctx.shape == (BATCH, CTX_LEN, D_MODEL)
    stream = (ctx, x)
    s_rows = jnp.concatenate([c, c_ctx[None], jnp.zeros((MOD_ROWS - BATCH - 1, D_MODEL), F32)], axis=0)
    m_all = _modulation(s_rows, mod_w, mod_b)
    cos_t, sin_s = _rope_tables()
    w1_all, w2_all = mlp_w1.astype(BF16), mlp_w2.astype(BF16)
    e_blk = jnp.asarray(np.kron(np.eye(2 * LANES // HEAD_DIM), np.ones((HEAD_DIM, HEAD_DIM))) / HEAD_DIM, BF16)

    for i in range(DEPTH):
        last = i == DEPTH - 1
        j = i // 2
        m_lat = m_all[i, :BATCH].reshape(BATCH, N_MOD, D_MODEL)
        m_ctx = jnp.broadcast_to(m_all[i, BATCH].reshape(1, N_MOD, D_MODEL), (BATCH, N_MOD, D_MODEL))
        mods = jnp.stack([m_ctx, m_lat])
        g1 = norm1_g[i].reshape(1, D_MODEL)
        g2 = norm2_g[i].reshape(1, D_MODEL)
        if i % 2 == 0:
            lambda_init = 0.8 - 0.6 * math.exp(-0.3 * i)
            gains = jnp.stack([jnp.tile(v[j], LANES // HEAD_DIM) for v in (a_q_norm, a_k_norm, b_q_norm, b_k_norm)])
            qa, k2a, v2a, qb, kb, vb = _attn_in(stream, mods, g1, attn_w_in[j].astype(BF16), gains, cos_t, sin_s, e_blk)
            ya = _win_attn(a_sink[j], qa, k2a, v2a)
            lpar = jnp.stack([b_lq1[j], b_lk1[j], b_lq2[j], b_lk2[j]])
            yb = _diff_attn(lpar, b_subln[j].reshape(1, LANES), qb, kb, vb, lambda_init)
            if last:
                ya, yb = ya[:, CTX_LEN:], yb[:, CTX_LEN:]
            ua, ub, wo = ya, yb, attn_w_out[j]
        else:
            src = np.arange(BATCH * S5_TOK).reshape(BATCH, S5_TOK).T.reshape(-1)
            perm = jnp.asarray(np.eye(BATCH * S5_TOK, dtype=np.float32)[src], BF16)
            h_all = stream[0] if stream[0].shape[1] == TT else jnp.concatenate(stream, axis=1)
            u, u_g = _s5_in(h_all, mods, g1, perm, s5_w_in[j].astype(BF16))
            ops_in = _s5_layout(s5_lambda_re[j], s5_lambda_im[j], s5_log_step[j], s5_b_re[j], s5_b_im[j],
                                s5_c_re[j], s5_c_im[j])
            m_op, q_op, n_op, a_vec = _s5_ops(*ops_in)
            y_g = _s5_scan(u_g, m_op, q_op, n_op, a_vec)
            assert last, "S5 layers before the last one would also need the context rows of the readout"
            gated = _s5_out(u, y_g, s5_d[j].reshape(1, D_MODEL), s5_glu_w[j].astype(BF16),
                            s5_glu_b[j].reshape(1, D_MODEL), perm.T)
            ua, ub, wo = gated, gated, s5_w_out[j]
        h = _mix_mlp(stream, ua, ub, mods, g2, wo.astype(BF16), w1_all, w2_all, i, latent_only=last)
        stream = (h, h)
    return h
```

```python
import functools
import math

import jax
import jax.numpy as jnp
import numpy as np
from jax import lax
from jax.experimental import pallas as pl
from jax.experimental.pallas import tpu as pltpu

F32 = jnp.float32
BF16 = jnp.bfloat16

D_MODEL = 1024
BATCH = 8
SEQ = 2048
DEPTH = 2
GRID_W = 64
CTX_LEN = 256
HEAD_DIM = 64
WINDOW = 128
A_Q_HEADS = 8
A_KV_HEADS = 2
B_HEADS = 4
A_Q_W = A_Q_HEADS * HEAD_DIM
A_KV_W = A_KV_HEADS * HEAD_DIM
B_QK_W = B_HEADS * 2 * HEAD_DIM
B_V_W = B_HEADS * 2 * HEAD_DIM
ATTN_IN = A_Q_W + 2 * A_KV_W + 2 * B_QK_W + B_V_W
S5_GROUP = 16
S5_GROUPS = D_MODEL // S5_GROUP
S5_STATE = 64
D_FF = 4 * D_MODEL
ROPE_BASE = 10000.0
EPS = 1e-6
NEG_INF = -1e30
LOG2E = math.log2(math.e)

TT = CTX_LEN + SEQ
TM = 256
N_TILES = TT // TM
LANES = 128
S5_CHUNK = 16
S5_CW = S5_CHUNK * S5_GROUP
N_CHUNKS = TT // S5_CHUNK
N_CTX_CHUNKS = CTX_LEN // S5_CHUNK
VMEM_LIMIT = 56 * 1024 * 1024
N_MOD = 6
MOD_ROWS = 16
GQA = A_Q_HEADS // A_KV_HEADS


def _dot(a, b):
    return jnp.dot(a, b, preferred_element_type=F32)


def _dot_nt(a, b):
    return lax.dot_general(a, b, (((1,), (1,)), ((), ())), preferred_element_type=F32)


def _rms(x):
    return x * lax.rsqrt(jnp.mean(x * x, axis=-1, keepdims=True) + EPS)


def _modnorm(x, g, shift, scale):
    return _rms(x) * g * (1.0 + scale) + shift


def _params(**kw):
    return pltpu.CompilerParams(vmem_limit_bytes=VMEM_LIMIT, **kw)


def _mod_kernel(s_ref, w_ref, b_ref, o_ref):
    s = s_ref[...]
    s = s * jax.nn.sigmoid(s)
    o_ref[...] = _dot(s.astype(BF16), w_ref[...].astype(BF16)) + b_ref[...]


def _modulation(s_rows, mod_w, mod_b):
    return pl.pallas_call(
        _mod_kernel,
        out_shape=jax.ShapeDtypeStruct((DEPTH, MOD_ROWS, N_MOD * D_MODEL), F32),
        grid=(DEPTH, N_MOD),
        in_specs=[
            pl.BlockSpec((MOD_ROWS, D_MODEL), lambda i, j: (0, 0)),
            pl.BlockSpec((None, D_MODEL, D_MODEL), lambda i, j: (i, 0, j)),
            pl.BlockSpec((None, 1, D_MODEL), lambda i, j: (i, 0, j)),
        ],
        out_specs=pl.BlockSpec((None, MOD_ROWS, D_MODEL), lambda i, j: (i, 0, j)),
        compiler_params=_params(),
        name="modulation",
    )(s_rows, mod_w, mod_b.reshape(DEPTH, 1, N_MOD * D_MODEL))


def _stream_specs(stream, n_sub, h_off):
    first_lat = 1 if stream[0].shape[1] == TT else 0

    def tile(t, k):
        return t * n_sub + k + h_off

    ctx_spec = pl.BlockSpec((None, TM, D_MODEL), lambda b, t: (b, 0, 0))
    lat_specs = [pl.BlockSpec((None, TM, D_MODEL),
                              lambda b, t, k=k: (b, jnp.maximum(tile(t, k) - 1 + first_lat, first_lat), 0))
                 for k in range(n_sub)]
    mod_specs = [pl.BlockSpec((None, None, N_MOD, D_MODEL), lambda b, t, k=k: (jnp.minimum(tile(t, k), 1), b, 0, 0))
                 for k in range(n_sub)]
    return ctx_spec, lat_specs, mod_specs


ATTN_IN_SUB = 3


def _attn_in_kernel(*refs):
    n_sub = ATTN_IN_SUB
    hc_ref = refs[0]
    hl_refs = refs[1:1 + n_sub]
    mod_refs = refs[1 + n_sub:1 + 2 * n_sub]
    (g_ref, w_ref, gain_ref, cos_ref, sin_ref, e_ref,
     qa_ref, k2a_ref, v2a_ref, qb_ref, kb_ref, vb_ref) = refs[1 + 2 * n_sub:]
    e = e_ref[...]
    gains = gain_ref[...]
    lane = lax.broadcasted_iota(jnp.int32, (TM, LANES), 1)
    first_half = (lane & (HEAD_DIM - 1)) < HEAD_DIM // 2
    lo = lane < HEAD_DIM
    q_scale = HEAD_DIM ** -0.5 * LOG2E
    ones = jnp.ones((TM, LANES), BF16)

    def tile_segments(k):
        rows = slice(k * TM, (k + 1) * TM)
        cos_t = cos_ref[rows, :]
        sin_s = sin_ref[rows, :]

        def head_mean_sq(z):
            sq = (z * z).astype(BF16)
            width = z.shape[1]
            if width < 2 * LANES:
                return _dot(sq, e[0:width, 0:width])
            return jnp.concatenate([_dot(sq[:, c:c + 2 * LANES], e) for c in range(0, width, 2 * LANES)], axis=1)

        def norm_rope_chunks(z, gain):
            ms = head_mean_sq(z)
            for c in range(z.shape[1] // LANES):
                cols = slice(c * LANES, (c + 1) * LANES)
                cn = z[:, cols] * lax.rsqrt(ms[:, cols] + EPS) * gain
                r_fwd = pltpu.roll(cn, HEAD_DIM // 2, 1)
                r_bwd = pltpu.roll(cn, LANES - HEAD_DIM // 2, 1)
                yield cn * cos_t + jnp.where(first_half, r_bwd, r_fwd) * sin_s

        def dup_halves(x):
            sw = pltpu.roll(x, HEAD_DIM, 1)
            return jnp.where(lo, x, sw).astype(BF16), jnp.where(lo, sw, x).astype(BF16)

        def finish_q(z, ref, gain):
            for c, chunk in enumerate(norm_rope_chunks(z, gain)):
                ref[rows, c * LANES:(c + 1) * LANES] = (chunk * q_scale).astype(BF16)

        def finish_kv_a(z):
            (k_roped,) = norm_rope_chunks(z[:, 0:LANES], gains[1:2])
            for kvh, dup in enumerate(dup_halves(k_roped)):
                k2a_ref[rows, kvh * LANES:(kvh + 1) * LANES] = dup
            for kvh, dup in enumerate(dup_halves(z[:, LANES:2 * LANES])):
                v2a_ref[rows, 2 * kvh * LANES:(2 * kvh + 1) * LANES] = dup
                v2a_ref[rows, (2 * kvh + 1) * LANES:(2 * kvh + 2) * LANES] = ones

        def finish_kb(z):
            for c, chunk in enumerate(norm_rope_chunks(z, gains[3:4])):
                kb_ref[rows, c * LANES:(c + 1) * LANES] = chunk.astype(BF16)

        def finish_vb(z):
            for hd in range(B_HEADS):
                vb_ref[rows, 2 * hd * LANES:(2 * hd + 1) * LANES] = z[:, hd * LANES:(hd + 1) * LANES].astype(BF16)
                vb_ref[rows, (2 * hd + 1) * LANES:(2 * hd + 2) * LANES] = ones

        return [(A_Q_W, lambda z: finish_q(z, qa_ref, gains[0:1])), (2 * A_KV_W, finish_kv_a),
                (B_QK_W, lambda z: finish_q(z, qb_ref, gains[2:3])), (B_QK_W, finish_kb), (B_V_W, finish_vb)]

    def normed(k):
        mod = mod_refs[k][...]
        x = hl_refs[k][...]
        if k == 0:
            x = jnp.where(pl.program_id(1) == 0, hc_ref[...], x)
        return _modnorm(x, g_ref[...], mod[0:1], mod[1:2]).astype(BF16)

    pending = None
    for k in range(n_sub):
        a = normed(k)
        off = 0
        for width, finish in tile_segments(k):
            z = _dot(a, w_ref[:, off:off + width])
            off += width
            if pending is not None:
                pending[1](pending[0])
            pending = (z, finish)
    pending[1](pending[0])


def _attn_in(stream, mods, g, w_in, gains, cos_t, sin_s, e_blk):
    n_sub = ATTN_IN_SUB
    ctx_spec, lat_specs, mod_specs = _stream_specs(stream, n_sub, 0)

    def tok(width):
        return pl.BlockSpec((None, n_sub * TM, width), lambda b, t: (b, t, 0))

    def full(shape):
        return pl.BlockSpec(shape, lambda b, t: (0,) * len(shape))

    out_shapes = [jax.ShapeDtypeStruct((BATCH, TT, w), BF16)
                  for w in (A_Q_W, 2 * A_KV_W, 4 * A_KV_W, B_QK_W, B_QK_W, 2 * B_V_W)]
    return pl.pallas_call(
        _attn_in_kernel,
        out_shape=out_shapes,
        grid=(BATCH, N_TILES // n_sub),
        in_specs=[
            ctx_spec, *lat_specs, *mod_specs,
            full((1, D_MODEL)), full((D_MODEL, ATTN_IN)), full((4, LANES)),
            pl.BlockSpec((n_sub * TM, LANES), lambda b, t: (t, 0)),
            pl.BlockSpec((n_sub * TM, LANES), lambda b, t: (t, 0)),
            full((2 * LANES, 2 * LANES)),
        ],
        out_specs=[tok(A_Q_W), tok(2 * A_KV_W), tok(4 * A_KV_W), tok(B_QK_W), tok(B_QK_W), tok(2 * B_V_W)],
        compiler_params=_params(),
        name="attn_in_proj",
    )(stream[0], *[stream[1]] * n_sub, *[mods] * n_sub, g, w_in, gains, cos_t, sin_s, e_blk)


QB = 128


def _run_pipelined(items):
    s_next = items[0][0]()
    for i, (_, finish) in enumerate(items):
        s_cur = s_next
        if i + 1 < len(items):
            s_next = items[i + 1][0]()
        finish(s_cur)


def _win_attn_items(t, sink_ref, q_ref, k2_ref, v2_ref, o_ref, with_window):
    lane = lax.broadcasted_iota(jnp.int32, (QB, LANES), 1)
    lo = lane < HEAD_DIM
    rows = GQA * QB
    row = lax.broadcasted_iota(jnp.int32, (rows, 3 * QB), 0)
    col = lax.broadcasted_iota(jnp.int32, (rows, 3 * QB), 1)
    row_head = lax.broadcasted_iota(jnp.int32, (rows, 1), 0) // QB
    zero = jnp.zeros((QB, LANES), BF16)
    blocks = [(qb, g) for qb in range(TM // QB) for g in range(A_KV_HEADS)]

    def window_start(qb):
        n = (t - 1) * (TM // QB) + qb
        ws = jnp.clip((n - 1) * QB, 0, SEQ - 3 * QB)
        return n, ws

    def scores(qb, g):
        pieces = []
        for p in range(2):
            qp = q_ref[qb * QB:(qb + 1) * QB, g * 2 * LANES + p * LANES: g * 2 * LANES + (p + 1) * LANES]
            pieces.append(jnp.where(lo, qp, zero))
            pieces.append(jnp.where(lo, zero, qp))
        qs = jnp.concatenate(pieces, axis=0)
        s_c = _dot_nt(qs, k2_ref[0:CTX_LEN, g * LANES:(g + 1) * LANES])
        if not with_window:
            return s_c, None
        n, ws = window_start(qb)
        kw = k2_ref[pl.ds(pl.multiple_of(ws + CTX_LEN, QB), 3 * QB), g * LANES:(g + 1) * LANES]
        valid = jnp.abs(n * QB + (row & (QB - 1)) - (ws + col)) <= WINDOW
        return s_c, jnp.where(valid, _dot_nt(qs, kw), NEG_INF)

    def finish(qb, g, s):
        s_c, s_w = s
        sk = jnp.full((rows, 1), sink_ref[GQA * g + GQA - 1], F32)
        for hh in range(GQA - 1):
            sk = jnp.where(row_head == hh, sink_ref[GQA * g + hh], sk)
        sk = sk * LOG2E
        m = jnp.maximum(jnp.max(s_c, axis=-1, keepdims=True), sk)
        if s_w is not None:
            m = jnp.maximum(m, jnp.max(s_w, axis=-1, keepdims=True))
        vcols = slice(2 * g * LANES, (2 * g + 2) * LANES)
        pv = _dot(jnp.exp2(s_c - m).astype(BF16), v2_ref[0:CTX_LEN, vcols])
        if s_w is not None:
            _, ws = window_start(qb)
            vw = v2_ref[pl.ds(pl.multiple_of(ws + CTX_LEN, QB), 3 * QB), vcols]
            pv = pv + _dot(jnp.exp2(s_w - m).astype(BF16), vw)
        o = pv[:, 0:LANES] / (pv[:, LANES:2 * LANES] + jnp.exp2(sk - m))
        for p in range(2):
            o_ref[qb * QB:(qb + 1) * QB, g * 2 * LANES + p * LANES: g * 2 * LANES + (p + 1) * LANES] = jnp.where(
                lo, o[2 * p * QB:(2 * p + 1) * QB], o[(2 * p + 1) * QB:(2 * p + 2) * QB]).astype(BF16)

    return [(functools.partial(scores, qb, g), functools.partial(finish, qb, g)) for qb, g in blocks]


DIFF_ROWS = 256


def _diff_attn_items(lam, subln_ref, q_ref, k_ref, v_ref, o_ref, n_keys, lambda_init):
    R = DIFF_ROWS
    lane = lax.broadcasted_iota(jnp.int32, (R, LANES), 1)
    lo = lane < HEAD_DIM
    zero = jnp.zeros((R, LANES), BF16)
    blocks = [(slice(rb * R, (rb + 1) * R), slice(h * LANES, (h + 1) * LANES))
              for h in range(B_HEADS) for rb in range(TM // R)]

    def scores(rows, cols):
        q = q_ref[rows, cols]
        qs = jnp.concatenate([jnp.where(lo, q, zero), jnp.where(lo, zero, q)], axis=0)
        return _dot_nt(qs, k_ref[0:n_keys, cols])

    def finish(rows, cols, s):
        p = jnp.exp2(s - jnp.max(s, axis=-1, keepdims=True)).astype(BF16)
        vcols = slice(2 * cols.start, 2 * cols.stop)
        pv = _dot(p, v_ref[0:n_keys, vcols])
        sm = pv[:, 0:LANES] / pv[:, LANES:2 * LANES]
        y = sm[0:R] - lam * sm[R:2 * R]
        o_ref[rows, cols] = (_rms(y) * subln_ref[...] * (1.0 - lambda_init)).astype(BF16)

    return [(functools.partial(scores, *blk), functools.partial(finish, *blk)) for blk in blocks]


def _attention_kernel(sink_ref, lpar_ref, subln_ref, qa_ref, k2a_ref, v2a_ref, qb_ref, kb_ref, vb_ref,
                      ya_ref, yb_ref, *, lambda_init):
    t = pl.program_id(1)
    lp = lpar_ref[...]
    lam = (jnp.exp(jnp.sum(lp[0:1] * lp[1:2], axis=-1, keepdims=True))
           - jnp.exp(jnp.sum(lp[2:3] * lp[3:4], axis=-1, keepdims=True)) + lambda_init)

    def attend(is_ctx):
        win = _win_attn_items(t, sink_ref, qa_ref, k2a_ref, v2a_ref, ya_ref, not is_ctx)
        dif = _diff_attn_items(lam, subln_ref, qb_ref, kb_ref, vb_ref, yb_ref, CTX_LEN if is_ctx else TT,
                               lambda_init)
        assert len(win) == len(dif)
        _run_pipelined([item for pair in zip(dif, win) for item in pair])

    @pl.when(t == 0)
    def _():
        attend(True)

    @pl.when(t > 0)
    def _():
        attend(False)


def _attention(sink, lpar, subln, qa, k2a, v2a, qb, kb, vb, lambda_init):
    def tile(width):
        return pl.BlockSpec((None, TM, width), lambda b, t: (b, t, 0))

    def keys(width):
        return pl.BlockSpec((None, TT, width), lambda b, t: (b, 0, 0))

    return pl.pallas_call(
        functools.partial(_attention_kernel, lambda_init=lambda_init),
        out_shape=[jax.ShapeDtypeStruct((BATCH, TT, A_Q_W), BF16), jax.ShapeDtypeStruct((BATCH, TT, B_V_W), BF16)],
        grid=(BATCH, N_TILES),
        in_specs=[
            pl.BlockSpec(memory_space=pltpu.SMEM),
            pl.BlockSpec((4, HEAD_DIM), lambda b, t: (0, 0)),
            pl.BlockSpec((1, LANES), lambda b, t: (0, 0)),
            tile(A_Q_W), keys(2 * A_KV_W), keys(4 * A_KV_W),
            tile(B_QK_W), keys(B_QK_W), keys(2 * B_V_W),
        ],
        out_specs=[tile(A_Q_W), tile(B_V_W)],
        compiler_params=_params(),
        name="attention",
    )(sink, lpar, subln, qa, k2a, v2a, qb, kb, vb)


FF_CHUNK = 1024


def _mix_mlp_kernel(*refs, h_off, n_sub):
    hc_ref = refs[0]
    hl_refs, ua_refs, ub_refs, mod_refs = (refs[1 + i * n_sub:1 + (i + 1) * n_sub] for i in range(4))
    g_ref, wo_ref, w1_ref, w2_ref, o_ref = refs[1 + 4 * n_sub:]
    half = D_MODEL // 2

    def prologue(k):
        mod = mod_refs[k][...]
        y = _dot(ua_refs[k][...], wo_ref[0:half, :]) + _dot(ub_refs[k][...], wo_ref[half:D_MODEL, :])
        x = hl_refs[k][...]
        if h_off == 0 and k == 0:
            x = jnp.where(pl.program_id(1) == 0, hc_ref[...], x)
        h1 = x + mod[2:3] * y
        f = _modnorm(h1, g_ref[...], mod[3:4], mod[4:5]).astype(BF16)
        return h1, f, mod[5:6]

    def mlp(k, h1, f, gate):
        acc = jnp.zeros((TM, D_MODEL), F32)
        for c in range(D_FF // FF_CHUNK):
            hid = jnp.maximum(_dot(f, w1_ref[:, c * FF_CHUNK:(c + 1) * FF_CHUNK]), 0.0)
            acc = acc + _dot((hid * hid).astype(BF16), w2_ref[c * FF_CHUNK:(c + 1) * FF_CHUNK, :])
        o_ref[k * TM:(k + 1) * TM, :] = h1 + gate * acc

    nxt = prologue(0)
    for k in range(n_sub):
        cur = nxt
        if k + 1 < n_sub:
            nxt = prologue(k + 1)
        mlp(k, *cur)


def _mix_mlp(stream, ua, ub, mods, g, wo, w1_all, w2_all, layer, *, latent_only):
    n_tiles = SEQ // TM if latent_only else N_TILES
    h_off = N_TILES - n_tiles
    n_sub = 4 if n_tiles % 4 == 0 else 3
    half = D_MODEL // 2
    ub_col = 1 if ub.shape[-1] == D_MODEL else 0
    ctx_spec, lat_specs, mod_specs = _stream_specs(stream, n_sub, h_off)

    def full(shape):
        return pl.BlockSpec(shape, lambda b, t: (0,) * len(shape), pipeline_mode=pl.Buffered(1))

    def mixer_specs(col):
        return [pl.BlockSpec((None, TM, half), lambda b, t, k=k: (b, t * n_sub + k, col)) for k in range(n_sub)]

    return pl.pallas_call(
        functools.partial(_mix_mlp_kernel, h_off=h_off, n_sub=n_sub),
        out_shape=jax.ShapeDtypeStruct((BATCH, n_tiles * TM, D_MODEL), F32),
        grid=(BATCH, n_tiles // n_sub),
        in_specs=[
            ctx_spec, *lat_specs, *mixer_specs(0), *mixer_specs(ub_col), *mod_specs,
            pl.BlockSpec((1, D_MODEL), lambda b, t: (0, 0)),
            full((D_MODEL, D_MODEL)),
            pl.BlockSpec((None, D_MODEL, D_FF), lambda b, t: (layer, 0, 0), pipeline_mode=pl.Buffered(1)),
            pl.BlockSpec((None, D_FF, D_MODEL), lambda b, t: (layer, 0, 0), pipeline_mode=pl.Buffered(1)),
        ],
        out_specs=pl.BlockSpec((None, n_sub * TM, D_MODEL), lambda b, t: (b, t, 0)),
        compiler_params=_params(),
        name="mixer_out_mlp",
    )(stream[0], *[stream[1]] * n_sub, *[ua] * n_sub, *[ub] * n_sub, *[mods] * n_sub, g, wo, w1_all, w2_all)


S5_PAIR = 2
S5_TOK = S5_PAIR * S5_CHUNK
S5_STEPS = N_CHUNKS // S5_PAIR
GROUPS_PER_VREG = LANES // S5_GROUP
STEPS_PER_VREG = LANES // S5_GROUP
S5_SEG = 2 * LANES


def _lane_block():
    return lax.broadcasted_iota(jnp.int32, (BATCH, LANES), 1) // S5_GROUP


def _block_transpose(xs, lane_blk):
    xs = list(xs)
    n = len(xs)
    d = n // 2
    while d:
        low = (lane_blk & d) == 0
        for i in range(n):
            if i & d:
                continue
            a, b = xs[i], xs[i + d]
            xs[i] = jnp.where(low, a, pltpu.roll(b, S5_GROUP * d, 1))
            xs[i + d] = jnp.where(low, pltpu.roll(a, LANES - S5_GROUP * d, 1), b)
        d //= 2
    return xs


S5_SUB = 4


def _s5_in_kernel(*refs):
    h_refs = refs[:S5_SUB]
    mod_ref, g_ref, perm_ref, w_ref, u_ref, z_ref = refs[S5_SUB:]
    mod = mod_ref[...]
    lane_blk = _lane_block()

    def normed(j):
        x = h_refs[j][...]
        a = _rms(x) * g_ref[...] * (1.0 + mod[:, 1:2, :]) + mod[:, 0:1, :]
        a = a.reshape(BATCH * S5_TOK, D_MODEL).astype(BF16)
        return _dot(perm_ref[...], a).astype(BF16)

    def finish(j, s, u_seg):
        u_ref[j * S5_TOK:(j + 1) * S5_TOK, :, s * S5_SEG:(s + 1) * S5_SEG] = u_seg.reshape(S5_TOK, BATCH, S5_SEG)
        for kk in range(S5_SEG // LANES):
            k = s * (S5_SEG // LANES) + kk
            u_col = u_seg[:, kk * LANES:(kk + 1) * LANES]
            for hh in range(S5_CHUNK // STEPS_PER_VREG):
                halves = []
                for c2 in range(S5_PAIR):
                    tok0 = c2 * S5_CHUNK + hh * STEPS_PER_VREG
                    steps = [u_col[(tok0 + m) * BATCH:(tok0 + m + 1) * BATCH, :] for m in range(STEPS_PER_VREG)]
                    halves.append(_block_transpose(steps, lane_blk))
                rows = slice(j * S5_PAIR * BATCH, (j + 1) * S5_PAIR * BATCH)
                for r in range(GROUPS_PER_VREG):
                    z_ref[k * GROUPS_PER_VREG + r, rows, hh * LANES:(hh + 1) * LANES] = (
                        jnp.concatenate([h[r] for h in halves], axis=0).astype(BF16))

    pending = None
    for j in range(S5_SUB):
        a = normed(j)
        for s in range(D_MODEL // S5_SEG):
            u_seg = _dot(a, w_ref[:, s * S5_SEG:(s + 1) * S5_SEG])
            if pending is not None:
                finish(*pending)
            pending = (j, s, u_seg)
    finish(*pending)


def _s5_in(h, mods, g, perm, w_in):
    ctx_steps = N_CTX_CHUNKS // (S5_PAIR * S5_SUB)
    n_rows = BATCH * S5_TOK
    return pl.pallas_call(
        _s5_in_kernel,
        out_shape=[jax.ShapeDtypeStruct((TT, BATCH, D_MODEL), F32),
                   jax.ShapeDtypeStruct((S5_GROUPS, ROWS, S5_CW), BF16)],
        grid=(S5_STEPS // S5_SUB,),
        in_specs=[*[pl.BlockSpec((BATCH, S5_TOK, D_MODEL), lambda p, j=j: (0, p * S5_SUB + j, 0))
                    for j in range(S5_SUB)],
                  pl.BlockSpec((None, BATCH, N_MOD, D_MODEL), lambda p: (jnp.minimum(p // ctx_steps, 1), 0, 0, 0)),
                  pl.BlockSpec((1, D_MODEL), lambda p: (0, 0)),
                  pl.BlockSpec((n_rows, n_rows), lambda p: (0, 0)),
                  pl.BlockSpec((D_MODEL, D_MODEL), lambda p: (0, 0))],
        out_specs=[pl.BlockSpec((S5_SUB * S5_TOK, BATCH, D_MODEL), lambda p: (p, 0, 0)),
                   pl.BlockSpec((S5_GROUPS, S5_SUB * S5_PAIR * BATCH, S5_CW), lambda p: (0, p, 0))],
        compiler_params=_params(),
        name="s5_in_proj",
    )(*[h] * S5_SUB, mods, g, perm, w_in)


OPS_G = 4


def _s5_ops_kernel(*refs):
    for j in range(OPS_G):
        _s5_group_ops(*(r.at[j] for r in refs))


def _s5_group_ops(lr_ref, li_ref, ls_ref, btr_ref, bti_ref, cr_ref, ci_ref, m_ref, q_ref, n_ref, a_ref):
    P2 = 2 * S5_STATE

    def cmul(xr, xi, yr, yi):
        return xr * yr - xi * yi, xr * yi + xi * yr

    lr, li = lr_ref[...], li_ref[...]
    dt = jnp.exp(ls_ref[...])
    mag = jnp.exp(lr * dt)
    ar = mag * jnp.cos(li * dt)
    ai = mag * jnp.sin(li * dt)
    den = lr * lr + li * li
    nr = ar - 1.0
    f_re = (nr * lr + ai * li) / den
    f_im = (ai * lr - nr * li) / den
    bt_re = f_re * btr_ref[...] - f_im * bti_ref[...]
    bt_im = f_re * bti_ref[...] + f_im * btr_ref[...]

    powers = [(jnp.ones_like(ar), jnp.zeros_like(ai))]
    for _ in range(S5_CHUNK):
        powers.append(cmul(*powers[-1], ar, ai))
    row_fwd = lax.broadcasted_iota(jnp.int32, (1, P2), 1) < S5_STATE

    def pow_rows(exp_fwd, exp_bwd):
        return (jnp.where(row_fwd, powers[exp_fwd][0], powers[exp_bwd][0]),
                jnp.where(row_fwd, powers[exp_fwd][1], powers[exp_bwd][1]))

    c_re, c_im = cr_ref[...], ci_ref[...]
    last = S5_CHUNK - 1
    cp_blocks = [cmul(c_re, c_im, *pow_rows(blk, last - blk)) for blk in range(S5_CHUNK)]
    cp_re = jnp.concatenate([b[0] for b in cp_blocks], axis=0)
    cp_im = jnp.concatenate([b[1] for b in cp_blocks], axis=0)
    lane_fwd = lax.broadcasted_iota(jnp.int32, (S5_GROUP, P2), 1) < S5_STATE

    def lag_kernels(keep):
        br = jnp.where(keep, bt_re, 0.0)
        bi = jnp.where(keep, bt_im, 0.0)
        dims = (((1,), (1,)), ((), ()))
        hi = lax.Precision.HIGHEST
        return (lax.dot_general(br, cp_re, dims, precision=hi, preferred_element_type=F32)
                - lax.dot_general(bi, cp_im, dims, precision=hi, preferred_element_type=F32))

    kt_f = lag_kernels(lane_fwd)
    kt_b = lag_kernels(jnp.logical_not(lane_fwd))
    lane_w = lax.broadcasted_iota(jnp.int32, (S5_GROUP, S5_CW), 1)
    for s in range(S5_CHUNK):
        f_part = kt_f if s == 0 else jnp.where(lane_w >= S5_GROUP * s, pltpu.roll(kt_f, S5_GROUP * s, 1), 0.0)
        sh = (S5_GROUP * (s + 1)) % S5_CW
        b_roll = kt_b if sh == 0 else pltpu.roll(kt_b, sh, 1)
        b_part = jnp.where(lane_w < S5_GROUP * (s + 1), b_roll, 0.0)
        m_ref[s * S5_GROUP:(s + 1) * S5_GROUP, :] = (f_part + b_part).astype(BF16)

    for blk in range(S5_CHUNK):
        rows = slice(blk * S5_GROUP, (blk + 1) * S5_GROUP)
        q_re, q_im = cmul(bt_re, bt_im, *pow_rows(last - blk, blk))
        q_ref[rows, 0:P2] = q_re.astype(BF16)
        q_ref[rows, P2:2 * P2] = q_im.astype(BF16)

    for blk in range(S5_CHUNK):
        rows = slice(blk * S5_GROUP, (blk + 1) * S5_GROUP)
        n_re, n_im = cmul(c_re, c_im, *pow_rows(blk + 1, S5_CHUNK - blk))
        n_ref[rows, 0:P2] = n_re.astype(BF16)
        n_ref[rows, P2:2 * P2] = (-n_im).astype(BF16)

    a_ref[0:1, :] = powers[S5_CHUNK][0]
    a_ref[1:2, :] = powers[S5_CHUNK][1]


def _s5_ops(row_params, bt, c_nat):
    P2 = 2 * S5_STATE
    row = pl.BlockSpec((OPS_G, 1, P2), lambda g: (g, 0, 0))
    mat = pl.BlockSpec((OPS_G, S5_GROUP, P2), lambda g: (g, 0, 0))
    sq = pl.BlockSpec((OPS_G, S5_CW, S5_CW), lambda g: (g, 0, 0))
    return pl.pallas_call(
        _s5_ops_kernel,
        out_shape=[jax.ShapeDtypeStruct((S5_GROUPS, S5_CW, S5_CW), BF16)] * 3
        + [jax.ShapeDtypeStruct((S5_GROUPS, 2, P2), F32)],
        grid=(S5_GROUPS // OPS_G,),
        in_specs=[row, row, row, mat, mat, mat, mat],
        out_specs=[sq, sq, sq, pl.BlockSpec((OPS_G, 2, P2), lambda g: (g, 0, 0))],
        compiler_params=_params(),
        name="s5_chunk_operators",
    )(*row_params, *bt, *c_nat)


ROWS = N_CHUNKS * BATCH


SCAN_G = 4


def _s5_scan_kernel(u_ref, m_ref, q_ref, n_ref, a_ref, y_ref, pu_ref, sp_ref):
    S = S5_STATE
    lane = lax.broadcasted_iota(jnp.int32, (BATCH, 2 * S), 1)
    lo = lane < S
    for j in range(SCAN_G):
        pu_ref[j] = _dot(u_ref[j], q_ref[j])
    for j in range(SCAN_G):
        y_ref[j] = _dot(u_ref[j], m_ref[j])
    decay = [(a_ref[j, 0:1, :], a_ref[j, 1:2, :]) for j in range(SCAN_G)]
    zero = jnp.zeros((BATCH, 2 * S), F32)
    state = [(zero, zero)] * SCAN_G
    for k in range(N_CHUNKS):
        cf = k * BATCH
        cb = (N_CTX_CHUNKS - 1 - k if k < N_CTX_CHUNKS else N_CHUNKS + N_CTX_CHUNKS - 1 - k) * BATCH
        for j in range(SCAN_G):
            s_re, s_im = state[j]
            a_re, a_im = decay[j]
            sp_ref[j, cf:cf + BATCH, 0:S] = s_re[:, 0:S]
            sp_ref[j, cb:cb + BATCH, S:2 * S] = s_re[:, S:2 * S]
            sp_ref[j, cf:cf + BATCH, 2 * S:3 * S] = s_im[:, 0:S]
            sp_ref[j, cb:cb + BATCH, 3 * S:4 * S] = s_im[:, S:2 * S]
            x_re = jnp.where(lo, pu_ref[j, cf:cf + BATCH, 0:2 * S], pu_ref[j, cb:cb + BATCH, 0:2 * S])
            x_im = jnp.where(lo, pu_ref[j, cf:cf + BATCH, 2 * S:4 * S], pu_ref[j, cb:cb + BATCH, 2 * S:4 * S])
            state[j] = (a_re * s_re - a_im * s_im + x_re, a_re * s_im + a_im * s_re + x_im)
    for j in range(SCAN_G):
        y_ref[j] += _dot_nt(sp_ref[j].astype(BF16), n_ref[j])


def _s5_scan(u_g, m_op, q_op, n_op, a_vec):
    sq = pl.BlockSpec((SCAN_G, S5_CW, S5_CW), lambda g: (g, 0, 0))
    rows = pl.BlockSpec((SCAN_G, ROWS, S5_CW), lambda g: (g, 0, 0))
    return pl.pallas_call(
        _s5_scan_kernel,
        out_shape=jax.ShapeDtypeStruct((S5_GROUPS, ROWS, S5_CW), F32),
        grid=(S5_GROUPS // SCAN_G,),
        in_specs=[rows, sq, sq, sq, pl.BlockSpec((SCAN_G, 2, 2 * S5_STATE), lambda g: (g, 0, 0))],
        out_specs=rows,
        scratch_shapes=[pltpu.VMEM((SCAN_G, ROWS, S5_CW), F32), pltpu.VMEM((SCAN_G, ROWS, S5_CW), F32)],
        compiler_params=_params(),
        name="s5_scan",
    )(u_g, m_op, q_op, n_op, a_vec)


def _s5_out_kernel(u_ref, y_ref, d_ref, gw_ref, gb_ref, perm_ref, o_ref):
    lane_blk = _lane_block()
    n_seg = D_MODEL // S5_SEG
    per_seg = S5_SEG // LANES

    def regroup(j, s):
        cols = []
        for k in range(s * per_seg, (s + 1) * per_seg):
            by_tok = [None] * S5_TOK
            for c2 in range(S5_PAIR):
                rows = slice((j * S5_PAIR + c2) * BATCH, (j * S5_PAIR + c2 + 1) * BATCH)
                for hh in range(S5_CHUNK // STEPS_PER_VREG):
                    groups = [y_ref[k * GROUPS_PER_VREG + r, rows, hh * LANES:(hh + 1) * LANES]
                              for r in range(GROUPS_PER_VREG)]
                    steps = _block_transpose(groups, lane_blk)
                    for m in range(STEPS_PER_VREG):
                        by_tok[c2 * S5_CHUNK + hh * STEPS_PER_VREG + m] = steps[m]
            cols.append(jnp.concatenate(by_tok, axis=0))
        return jnp.concatenate(cols, axis=1)

    state = {}

    def compute(j, s, ys):
        cols = slice(s * S5_SEG, (s + 1) * S5_SEG)
        u = u_ref[j * S5_TOK:(j + 1) * S5_TOK, :, cols].reshape(BATCH * S5_TOK, S5_SEG)
        g_seg = jax.nn.gelu(u * d_ref[:, cols] + ys)
        part = _dot(g_seg.astype(BF16), gw_ref[cols, :])
        g_parts, pre = state.get(j, ([], None))
        state[j] = (g_parts + [g_seg], part if pre is None else pre + part)
        if s == n_seg - 1:
            g_parts, pre = state[j]
            gate = jax.nn.sigmoid(pre + gb_ref[...])
            gated = (jnp.concatenate(g_parts, axis=1) * gate).astype(BF16)
            gated = _dot(perm_ref[...], gated).astype(BF16)
            o_ref[:, j * S5_TOK:(j + 1) * S5_TOK, :] = gated.reshape(BATCH, S5_TOK, D_MODEL)

    items = [(j, s) for j in range(S5_SUB) for s in range(n_seg)]
    ys_next = regroup(*items[0])
    for i, item in enumerate(items):
        ys = ys_next
        if i + 1 < len(items):
            ys_next = regroup(*items[i + 1])
        compute(*item, ys)


def _s5_out(u, y_g, d_skip, glu_w, glu_b, perm_t):
    ctx_steps = N_CTX_CHUNKS // (S5_PAIR * S5_SUB)
    n_rows = BATCH * S5_TOK
    vec = pl.BlockSpec((1, D_MODEL), lambda p: (0, 0))
    return pl.pallas_call(
        _s5_out_kernel,
        out_shape=jax.ShapeDtypeStruct((BATCH, SEQ, D_MODEL), BF16),
        grid=(S5_STEPS // S5_SUB - ctx_steps,),
        in_specs=[pl.BlockSpec((S5_SUB * S5_TOK, BATCH, D_MODEL), lambda p: (p + ctx_steps, 0, 0)),
                  pl.BlockSpec((S5_GROUPS, S5_SUB * S5_PAIR * BATCH, S5_CW), lambda p: (0, p + ctx_steps, 0)),
                  vec, pl.BlockSpec((D_MODEL, D_MODEL), lambda p: (0, 0)), vec,
                  pl.BlockSpec((n_rows, n_rows), lambda p: (0, 0))],
        out_specs=pl.BlockSpec((BATCH, S5_SUB * S5_TOK, D_MODEL), lambda p: (0, p, 0)),
        compiler_params=_params(),
        name="s5_gelu_glu",
    )(u, y_g, d_skip, glu_w, glu_b, perm_t)


def _rope_tables():
    rows_n = SEQ // GRID_W
    row = np.repeat(np.arange(rows_n, dtype=np.float64), GRID_W)
    col = np.tile(np.arange(GRID_W, dtype=np.float64), rows_n)
    n_freq = HEAD_DIM // 4
    inv = ROPE_BASE ** (-np.arange(n_freq, dtype=np.float64) / n_freq)
    ang = np.concatenate([row[:, None] * inv, col[:, None] * inv], axis=-1)
    reps = LANES // (HEAD_DIM // 2)
    cos_t = np.tile(np.cos(ang), (1, reps))
    sin_t = np.tile(np.sin(ang), (1, reps))
    sign = np.where((np.arange(LANES) % HEAD_DIM) < HEAD_DIM // 2, -1.0, 1.0)
    cos_t = np.concatenate([np.ones((CTX_LEN, LANES)), cos_t], axis=0)
    sin_s = np.concatenate([np.zeros((CTX_LEN, LANES)), sin_t * sign], axis=0)
    return jnp.asarray(cos_t, F32), jnp.asarray(sin_s, F32)


def _fb_rows(x):
    return jnp.transpose(x, (1, 0, 2)).reshape(S5_GROUPS, 1, 2 * S5_STATE)


def _s5_layout(lam_re, lam_im, log_step, b_re, b_im, c_re, c_im):
    ls = jnp.broadcast_to(log_step[:, :, None], lam_re.shape)
    rows = [_fb_rows(v) for v in (lam_re, lam_im, ls)]

    def bt_of(b):
        return jnp.transpose(b, (1, 3, 0, 2)).reshape(S5_GROUPS, S5_GROUP, 2 * S5_STATE)

    def c_of(c):
        return jnp.transpose(c, (1, 2, 0, 3)).reshape(S5_GROUPS, S5_GROUP, 2 * S5_STATE)

    return rows, [bt_of(b_re), bt_of(b_im)], [c_of(c_re), c_of(c_im)]


def kernel(x, c, ctx, c_ctx, norm1_g, norm2_g, mod_w, mod_b, mlp_w1, mlp_w2, attn_w_in, attn_w_out, a_q_norm, a_k_norm, a_sink, b_q_norm, b_k_norm, b_lq1, b_lk1, b_lq2, b_lk2, b_subln, s5_w_in, s5_lambda_re, s5_lambda_im, s5_log_step, s5_b_re, s5_b_im, s5_c_re, s5_c_im, s5_d, s5_glu_w, s5_glu_b, s5_w_out):
    assert x.shape == (BATCH, SEQ, D_MODEL) and ctx.shape == (BATCH, CTX_LEN, D_MODEL)
    stream = (ctx, x)
    s_rows = jnp.concatenate([c, c_ctx[None], jnp.zeros((MOD_ROWS - BATCH - 1, D_MODEL), F32)], axis=0)
    m_all = _modulation(s_rows, mod_w, mod_b)
    cos_t, sin_s = _rope_tables()
    w1_all, w2_all = mlp_w1.astype(BF16), mlp_w2.astype(BF16)
    e_blk = jnp.asarray(np.kron(np.eye(2 * LANES // HEAD_DIM), np.ones((HEAD_DIM, HEAD_DIM))) / HEAD_DIM, BF16)

    for i in range(DEPTH):
        last = i == DEPTH - 1
        j = i // 2
        m_lat = m_all[i, :BATCH].reshape(BATCH, N_MOD, D_MODEL)
        m_ctx = jnp.broadcast_to(m_all[i, BATCH].reshape(1, N_MOD, D_MODEL), (BATCH, N_MOD, D_MODEL))
        mods = jnp.stack([m_ctx, m_lat])
        g1 = norm1_g[i].reshape(1, D_MODEL)
        g2 = norm2_g[i].reshape(1, D_MODEL)
        if i % 2 == 0:
            lambda_init = 0.8 - 0.6 * math.exp(-0.3 * i)
            gains = jnp.stack([jnp.tile(v[j], LANES // HEAD_DIM) for v in (a_q_norm, a_k_norm, b_q_norm, b_k_norm)])
            qa, k2a, v2a, qb, kb, vb = _attn_in(stream, mods, g1, attn_w_in[j].astype(BF16), gains, cos_t, sin_s, e_blk)
            lpar = jnp.stack([b_lq1[j], b_lk1[j], b_lq2[j], b_lk2[j]])
            ya, yb = _attention(a_sink[j], lpar, b_subln[j].reshape(1, LANES), qa, k2a, v2a, qb, kb, vb, lambda_init)
            if last:
                ya, yb = ya[:, CTX_LEN:], yb[:, CTX_LEN:]
            ua, ub, wo = ya, yb, attn_w_out[j]
        else:
            src = np.arange(BATCH * S5_TOK).reshape(BATCH, S5_TOK).T.reshape(-1)
            perm = jnp.asarray(np.eye(BATCH * S5_TOK, dtype=np.float32)[src], BF16)
            h_all = stream[0] if stream[0].shape[1] == TT else jnp.concatenate(stream, axis=1)
            u, u_g = _s5_in(h_all, mods, g1, perm, s5_w_in[j].astype(BF16))
            ops_in = _s5_layout(s5_lambda_re[j], s5_lambda_im[j], s5_log_step[j], s5_b_re[j], s5_b_im[j],
                                s5_c_re[j], s5_c_im[j])
            m_op, q_op, n_op, a_vec = _s5_ops(*ops_in)
            y_g = _s5_scan(u_g, m_op, q_op, n_op, a_vec)
            assert last, "S5 layers before the last one would also need the context rows of the readout"
            gated = _s5_out(u, y_g, s5_d[j].reshape(1, D_MODEL), s5_glu_w[j].astype(BF16),
                            s5_glu_b[j].reshape(1, D_MODEL), perm.T)
            ua, ub, wo = gated, gated, s5_w_out[j]
        h = _mix_mlp(stream, ua, ub, mods, g2, wo.astype(BF16), w1_all, w2_all, i, latent_only=last)
        stream = (h, h)
    return h
```

```python
import functools
import math

import jax
import jax.numpy as jnp
import numpy as np
from jax import lax
from jax.experimental import pallas as pl
from jax.experimental.pallas import tpu as pltpu

F32 = jnp.float32
BF16 = jnp.bfloat16

D_MODEL = 1024
BATCH = 8
SEQ = 2048
DEPTH = 2
GRID_W = 64
CTX_LEN = 256
HEAD_DIM = 64
WINDOW = 128
A_Q_HEADS = 8
A_KV_HEADS = 2
B_HEADS = 4
A_Q_W = A_Q_HEADS * HEAD_DIM
A_KV_W = A_KV_HEADS * HEAD_DIM
B_QK_W = B_HEADS * 2 * HEAD_DIM
B_V_W = B_HEADS * 2 * HEAD_DIM
ATTN_IN = A_Q_W + 2 * A_KV_W + 2 * B_QK_W + B_V_W
S5_GROUP = 16
S5_GROUPS = D_MODEL // S5_GROUP
S5_STATE = 64
D_FF = 4 * D_MODEL
ROPE_BASE = 10000.0
EPS = 1e-6
NEG_INF = -1e30
LOG2E = math.log2(math.e)

TT = CTX_LEN + SEQ
TM = 256
N_TILES = TT // TM
LANES = 128
S5_CHUNK = 16
S5_CW = S5_CHUNK * S5_GROUP
N_CHUNKS = TT // S5_CHUNK
N_CTX_CHUNKS = CTX_LEN // S5_CHUNK
VMEM_LIMIT = 56 * 1024 * 1024
N_MOD = 6
MOD_ROWS = 16
GQA = A_Q_HEADS // A_KV_HEADS


def _dot(a, b):
    return jnp.dot(a, b, preferred_element_type=F32)


def _dot_nt(a, b):
    return lax.dot_general(a, b, (((1,), (1,)), ((), ())), preferred_element_type=F32)


def _rms(x):
    return x * lax.rsqrt(jnp.mean(x * x, axis=-1, keepdims=True) + EPS)


def _modnorm(x, g, shift, scale):
    return _rms(x) * g * (1.0 + scale) + shift


def _params(**kw):
    return pltpu.CompilerParams(vmem_limit_bytes=VMEM_LIMIT, **kw)


def _mod_kernel(s_ref, w_ref, b_ref, o_ref):
    s = s_ref[...]
    s = s * jax.nn.sigmoid(s)
    o_ref[...] = _dot(s.astype(BF16), w_ref[...].astype(BF16)) + b_ref[...]


def _modulation(s_rows, mod_w, mod_b):
    return pl.pallas_call(
        _mod_kernel,
        out_shape=jax.ShapeDtypeStruct((DEPTH, MOD_ROWS, N_MOD * D_MODEL), F32),
        grid=(DEPTH, N_MOD),
        in_specs=[
            pl.BlockSpec((MOD_ROWS, D_MODEL), lambda i, j: (0, 0)),
            pl.BlockSpec((None, D_MODEL, D_MODEL), lambda i, j: (i, 0, j)),
            pl.BlockSpec((None, 1, D_MODEL), lambda i, j: (i, 0, j)),
        ],
        out_specs=pl.BlockSpec((None, MOD_ROWS, D_MODEL), lambda i, j: (i, 0, j)),
        compiler_params=_params(),
        name="modulation",
    )(s_rows, mod_w, mod_b.reshape(DEPTH, 1, N_MOD * D_MODEL))


def _stream_specs(stream, n_sub, h_off):
    first_lat = 1 if stream[0].shape[1] == TT else 0

    def tile(t, k):
        return t * n_sub + k + h_off

    ctx_spec = pl.BlockSpec((None, TM, D_MODEL), lambda b, t: (b, 0, 0))
    lat_specs = [pl.BlockSpec((None, TM, D_MODEL),
                              lambda b, t, k=k: (b, jnp.maximum(tile(t, k) - 1 + first_lat, first_lat), 0))
                 for k in range(n_sub)]
    mod_specs = [pl.BlockSpec((None, None, N_MOD, D_MODEL), lambda b, t, k=k: (jnp.minimum(tile(t, k), 1), b, 0, 0))
                 for k in range(n_sub)]
    return ctx_spec, lat_specs, mod_specs


ATTN_IN_SUB = 3


def _attn_in_kernel(*refs):
    n_sub = ATTN_IN_SUB
    hc_ref = refs[0]
    hl_refs = refs[1:1 + n_sub]
    mod_refs = refs[1 + n_sub:1 + 2 * n_sub]
    (g_ref, w_ref, gain_ref, cos_ref, sin_ref, e_ref,
     qa_ref, k2a_ref, v2a_ref, qb_ref, kb_ref, vb_ref) = refs[1 + 2 * n_sub:]
    e = e_ref[...]
    gains = gain_ref[...]
    lane = lax.broadcasted_iota(jnp.int32, (TM, LANES), 1)
    first_half = (lane & (HEAD_DIM - 1)) < HEAD_DIM // 2
    lo = lane < HEAD_DIM
    q_scale = HEAD_DIM ** -0.5 * LOG2E
    ones = jnp.ones((TM, LANES), BF16)

    def tile_segments(k):
        rows = slice(k * TM, (k + 1) * TM)
        cos_t = cos_ref[rows, :]
        sin_s = sin_ref[rows, :]

        def head_mean_sq(z):
            sq = (z * z).astype(BF16)
            width = z.shape[1]
            if width < 2 * LANES:
                return _dot(sq, e[0:width, 0:width])
            return jnp.concatenate([_dot(sq[:, c:c + 2 * LANES], e) for c in range(0, width, 2 * LANES)], axis=1)

        def norm_rope_chunks(z, gain):
            ms = head_mean_sq(z)
            for c in range(z.shape[1] // LANES):
                cols = slice(c * LANES, (c + 1) * LANES)
                cn = z[:, cols] * lax.rsqrt(ms[:, cols] + EPS) * gain
                r_fwd = pltpu.roll(cn, HEAD_DIM // 2, 1)
                r_bwd = pltpu.roll(cn, LANES - HEAD_DIM // 2, 1)
                yield cn * cos_t + jnp.where(first_half, r_bwd, r_fwd) * sin_s

        def dup_halves(x):
            sw = pltpu.roll(x, HEAD_DIM, 1)
            return jnp.where(lo, x, sw).astype(BF16), jnp.where(lo, sw, x).astype(BF16)

        def finish_q(z, ref, gain):
            for c, chunk in enumerate(norm_rope_chunks(z, gain)):
                ref[rows, c * LANES:(c + 1) * LANES] = (chunk * q_scale).astype(BF16)

        def finish_kv_a(z):
            (k_roped,) = norm_rope_chunks(z[:, 0:LANES], gains[1:2])
            for kvh, dup in enumerate(dup_halves(k_roped)):
                k2a_ref[rows, kvh * LANES:(kvh + 1) * LANES] = dup
            for kvh, dup in enumerate(dup_halves(z[:, LANES:2 * LANES])):
                v2a_ref[rows, 2 * kvh * LANES:(2 * kvh + 1) * LANES] = dup
                v2a_ref[rows, (2 * kvh + 1) * LANES:(2 * kvh + 2) * LANES] = ones

        def finish_kb(z):
            for c, chunk in enumerate(norm_rope_chunks(z, gains[3:4])):
                kb_ref[rows, c * LANES:(c + 1) * LANES] = chunk.astype(BF16)

        def finish_vb(z):
            for hd in range(B_HEADS):
                vb_ref[rows, 2 * hd * LANES:(2 * hd + 1) * LANES] = z[:, hd * LANES:(hd + 1) * LANES].astype(BF16)
                vb_ref[rows, (2 * hd + 1) * LANES:(2 * hd + 2) * LANES] = ones

        return [(A_Q_W, lambda z: finish_q(z, qa_ref, gains[0:1])), (2 * A_KV_W, finish_kv_a),
                (B_QK_W, lambda z: finish_q(z, qb_ref, gains[2:3])), (B_QK_W, finish_kb), (B_V_W, finish_vb)]

    def normed(k):
        mod = mod_refs[k][...]
        x = hl_refs[k][...]
        if k == 0:
            x = jnp.where(pl.program_id(1) == 0, hc_ref[...], x)
        return _modnorm(x, g_ref[...], mod[0:1], mod[1:2]).astype(BF16)

    pending = None
    for k in range(n_sub):
        a = normed(k)
        off = 0
        for width, finish in tile_segments(k):
            z = _dot(a, w_ref[:, off:off + width])
            off += width
            if pending is not None:
                pending[1](pending[0])
            pending = (z, finish)
    pending[1](pending[0])


def _attn_in(stream, mods, g, w_in, gains, cos_t, sin_s, e_blk):
    n_sub = ATTN_IN_SUB
    ctx_spec, lat_specs, mod_specs = _stream_specs(stream, n_sub, 0)

    def tok(width):
        return pl.BlockSpec((None, n_sub * TM, width), lambda b, t: (b, t, 0))

    def full(shape):
        return pl.BlockSpec(shape, lambda b, t: (0,) * len(shape))

    out_shapes = [jax.ShapeDtypeStruct((BATCH, TT, w), BF16)
                  for w in (A_Q_W, 2 * A_KV_W, 4 * A_KV_W, B_QK_W, B_QK_W, 2 * B_V_W)]
    return pl.pallas_call(
        _attn_in_kernel,
        out_shape=out_shapes,
        grid=(BATCH, N_TILES // n_sub),
        in_specs=[
            ctx_spec, *lat_specs, *mod_specs,
            full((1, D_MODEL)), full((D_MODEL, ATTN_IN)), full((4, LANES)),
            pl.BlockSpec((n_sub * TM, LANES), lambda b, t: (t, 0)),
            pl.BlockSpec((n_sub * TM, LANES), lambda b, t: (t, 0)),
            full((2 * LANES, 2 * LANES)),
        ],
        out_specs=[tok(A_Q_W), tok(2 * A_KV_W), tok(4 * A_KV_W), tok(B_QK_W), tok(B_QK_W), tok(2 * B_V_W)],
        compiler_params=_params(),
        name="attn_in_proj",
    )(stream[0], *[stream[1]] * n_sub, *[mods] * n_sub, g, w_in, gains, cos_t, sin_s, e_blk)


QB = 128


def _run_pipelined(items):
    s_next = items[0][0]()
    for i, (_, finish) in enumerate(items):
        s_cur = s_next
        if i + 1 < len(items):
            s_next = items[i + 1][0]()
        finish(s_cur)


def _win_attn_items(t, sink_ref, q_ref, k2_ref, v2_ref, o_ref, with_window):
    lane = lax.broadcasted_iota(jnp.int32, (QB, LANES), 1)
    lo = lane < HEAD_DIM
    rows = GQA * QB
    row = lax.broadcasted_iota(jnp.int32, (rows, 3 * QB), 0)
    col = lax.broadcasted_iota(jnp.int32, (rows, 3 * QB), 1)
    row_head = lax.broadcasted_iota(jnp.int32, (rows, 1), 0) // QB
    zero = jnp.zeros((QB, LANES), BF16)
    blocks = [(qb, g) for qb in range(TM // QB) for g in range(A_KV_HEADS)]

    def window_start(qb):
        n = (t - 1) * (TM // QB) + qb
        ws = jnp.clip((n - 1) * QB, 0, SEQ - 3 * QB)
        return n, ws

    def scores(qb, g):
        pieces = []
        for p in range(2):
            qp = q_ref[qb * QB:(qb + 1) * QB, g * 2 * LANES + p * LANES: g * 2 * LANES + (p + 1) * LANES]
            pieces.append(jnp.where(lo, qp, zero))
            pieces.append(jnp.where(lo, zero, qp))
        qs = jnp.concatenate(pieces, axis=0)
        s_c = _dot_nt(qs, k2_ref[0:CTX_LEN, g * LANES:(g + 1) * LANES])
        if not with_window:
            return s_c, None
        n, ws = window_start(qb)
        kw = k2_ref[pl.ds(pl.multiple_of(ws + CTX_LEN, QB), 3 * QB), g * LANES:(g + 1) * LANES]
        valid = jnp.abs(n * QB + (row & (QB - 1)) - (ws + col)) <= WINDOW
        return s_c, jnp.where(valid, _dot_nt(qs, kw), NEG_INF)

    def finish(qb, g, s):
        s_c, s_w = s
        sk = jnp.full((rows, 1), sink_ref[GQA * g + GQA - 1], F32)
        for hh in range(GQA - 1):
            sk = jnp.where(row_head == hh, sink_ref[GQA * g + hh], sk)
        sk = sk * LOG2E
        m = jnp.maximum(jnp.max(s_c, axis=-1, keepdims=True), sk)
        if s_w is not None:
            m = jnp.maximum(m, jnp.max(s_w, axis=-1, keepdims=True))
        vcols = slice(2 * g * LANES, (2 * g + 2) * LANES)
        pv = _dot(jnp.exp2(s_c - m).astype(BF16), v2_ref[0:CTX_LEN, vcols])
        if s_w is not None:
            _, ws = window_start(qb)
            vw = v2_ref[pl.ds(pl.multiple_of(ws + CTX_LEN, QB), 3 * QB), vcols]
            pv = pv + _dot(jnp.exp2(s_w - m).astype(BF16), vw)
        o = pv[:, 0:LANES] / (pv[:, LANES:2 * LANES] + jnp.exp2(sk - m))
        for p in range(2):
            o_ref[qb * QB:(qb + 1) * QB, g * 2 * LANES + p * LANES: g * 2 * LANES + (p + 1) * LANES] = jnp.where(
                lo, o[2 * p * QB:(2 * p + 1) * QB], o[(2 * p + 1) * QB:(2 * p + 2) * QB]).astype(BF16)

    return [(functools.partial(scores, qb, g), functools.partial(finish, qb, g)) for qb, g in blocks]


DIFF_ROWS = 256


def _diff_attn_items(lam, subln_ref, q_ref, k_ref, v_ref, o_ref, n_keys, lambda_init):
    R = DIFF_ROWS
    lane = lax.broadcasted_iota(jnp.int32, (R, LANES), 1)
    lo = lane < HEAD_DIM
    zero = jnp.zeros((R, LANES), BF16)
    blocks = [(slice(rb * R, (rb + 1) * R), slice(h * LANES, (h + 1) * LANES))
              for h in range(B_HEADS) for rb in range(TM // R)]

    def scores(rows, cols):
        q = q_ref[rows, cols]
        qs = jnp.concatenate([jnp.where(lo, q, zero), jnp.where(lo, zero, q)], axis=0)
        return _dot_nt(qs, k_ref[0:n_keys, cols])

    def finish(rows, cols, s):
        p = jnp.exp2(s - jnp.max(s, axis=-1, keepdims=True)).astype(BF16)
        vcols = slice(2 * cols.start, 2 * cols.stop)
        pv = _dot(p, v_ref[0:n_keys, vcols])
        sm = pv[:, 0:LANES] / pv[:, LANES:2 * LANES]
        y = sm[0:R] - lam * sm[R:2 * R]
        o_ref[rows, cols] = (_rms(y) * subln_ref[...] * (1.0 - lambda_init)).astype(BF16)

    return [(functools.partial(scores, *blk), functools.partial(finish, *blk)) for blk in blocks]


def _attention_kernel(sink_ref, lpar_ref, subln_ref, qa_ref, k2a_ref, v2a_ref, qb_ref, kb_ref, vb_ref,
                      ya_ref, yb_ref, *, lambda_init):
    t = pl.program_id(1)
    lp = lpar_ref[...]
    lam = (jnp.exp(jnp.sum(lp[0:1] * lp[1:2], axis=-1, keepdims=True))
           - jnp.exp(jnp.sum(lp[2:3] * lp[3:4], axis=-1, keepdims=True)) + lambda_init)

    def attend(is_ctx):
        win = _win_attn_items(t, sink_ref, qa_ref, k2a_ref, v2a_ref, ya_ref, not is_ctx)
        dif = _diff_attn_items(lam, subln_ref, qb_ref, kb_ref, vb_ref, yb_ref, CTX_LEN if is_ctx else TT,
                               lambda_init)
        assert len(win) == len(dif)
        _run_pipelined([item for pair in zip(dif, win) for item in pair])

    @pl.when(t == 0)
    def _():
        attend(True)

    @pl.when(t > 0)
    def _():
        attend(False)


def _attention(sink, lpar, subln, qa, k2a, v2a, qb, kb, vb, lambda_init):
    def tile(width):
        return pl.BlockSpec((None, TM, width), lambda b, t: (b, t, 0))

    def keys(width):
        return pl.BlockSpec((None, TT, width), lambda b, t: (b, 0, 0))

    return pl.pallas_call(
        functools.partial(_attention_kernel, lambda_init=lambda_init),
        out_shape=[jax.ShapeDtypeStruct((BATCH, TT, A_Q_W), BF16), jax.ShapeDtypeStruct((BATCH, TT, B_V_W), BF16)],
        grid=(BATCH, N_TILES),
        in_specs=[
            pl.BlockSpec(memory_space=pltpu.SMEM),
            pl.BlockSpec((4, HEAD_DIM), lambda b, t: (0, 0)),
            pl.BlockSpec((1, LANES), lambda b, t: (0, 0)),
            tile(A_Q_W), keys(2 * A_KV_W), keys(4 * A_KV_W),
            tile(B_QK_W), keys(B_QK_W), keys(2 * B_V_W),
        ],
        out_specs=[tile(A_Q_W), tile(B_V_W)],
        compiler_params=_params(),
        name="attention",
    )(sink, lpar, subln, qa, k2a, v2a, qb, kb, vb)


FF_CHUNK = 1024


def _mix_mlp_kernel(*refs, h_off, n_sub):
    hc_ref = refs[0]
    hl_refs, ua_refs, ub_refs, mod_refs = (refs[1 + i * n_sub:1 + (i + 1) * n_sub] for i in range(4))
    g_ref, wo_ref, w1_ref, w2_ref, o_ref = refs[1 + 4 * n_sub:]
    half = D_MODEL // 2

    def prologue(k):
        mod = mod_refs[k][...]
        y = _dot(ua_refs[k][...], wo_ref[0:half, :]) + _dot(ub_refs[k][...], wo_ref[half:D_MODEL, :])
        x = hl_refs[k][...]
        if h_off == 0 and k == 0:
            x = jnp.where(pl.program_id(1) == 0, hc_ref[...], x)
        h1 = x + mod[2:3] * y
        f = _modnorm(h1, g_ref[...], mod[3:4], mod[4:5]).astype(BF16)
        return h1, f, mod[5:6]

    def mlp(k, h1, f, gate):
        acc = jnp.zeros((TM, D_MODEL), F32)
        for c in range(D_FF // FF_CHUNK):
            hid = jnp.maximum(_dot(f, w1_ref[:, c * FF_CHUNK:(c + 1) * FF_CHUNK]), 0.0)
            acc = acc + _dot((hid * hid).astype(BF16), w2_ref[c * FF_CHUNK:(c + 1) * FF_CHUNK, :])
        o_ref[k * TM:(k + 1) * TM, :] = h1 + gate * acc

    nxt = prologue(0)
    for k in range(n_sub):
        cur = nxt
        if k + 1 < n_sub:
            nxt = prologue(k + 1)
        mlp(k, *cur)


def _mix_mlp(stream, ua, ub, mods, g, wo, w1_all, w2_all, layer, *, latent_only):
    n_tiles = SEQ // TM if latent_only else N_TILES
    h_off = N_TILES - n_tiles
    n_sub = 4 if n_tiles % 4 == 0 else 3
    half = D_MODEL // 2
    ub_col = 1 if ub.shape[-1] == D_MODEL else 0
    ctx_spec, lat_specs, mod_specs = _stream_specs(stream, n_sub, h_off)

    def full(shape):
        return pl.BlockSpec(shape, lambda b, t: (0,) * len(shape), pipeline_mode=pl.Buffered(1))

    def mixer_specs(col):
        return [pl.BlockSpec((None, TM, half), lambda b, t, k=k: (b, t * n_sub + k, col)) for k in range(n_sub)]

    return pl.pallas_call(
        functools.partial(_mix_mlp_kernel, h_off=h_off, n_sub=n_sub),
        out_shape=jax.ShapeDtypeStruct((BATCH, n_tiles * TM, D_MODEL), F32),
        grid=(BATCH, n_tiles // n_sub),
        in_specs=[
            ctx_spec, *lat_specs, *mixer_specs(0), *mixer_specs(ub_col), *mod_specs,
            pl.BlockSpec((1, D_MODEL), lambda b, t: (0, 0)),
            full((D_MODEL, D_MODEL)),
            pl.BlockSpec((None, D_MODEL, D_FF), lambda b, t: (layer, 0, 0), pipeline_mode=pl.Buffered(1)),
            pl.BlockSpec((None, D_FF, D_MODEL), lambda b, t: (layer, 0, 0), pipeline_mode=pl.Buffered(1)),
        ],
        out_specs=pl.BlockSpec((None, n_sub * TM, D_MODEL), lambda b, t: (b, t, 0)),
        compiler_params=_params(),
        name="mixer_out_mlp",
    )(stream[0], *[stream[1]] * n_sub, *[ua] * n_sub, *[ub] * n_sub, *[mods] * n_sub, g, wo, w1_all, w2_all)


S5_PAIR = 2
S5_TOK = S5_PAIR * S5_CHUNK
S5_STEPS = N_CHUNKS // S5_PAIR
GROUPS_PER_VREG = LANES // S5_GROUP
STEPS_PER_VREG = LANES // S5_GROUP
S5_SEG = 2 * LANES


def _lane_block():
    return lax.broadcasted_iota(jnp.int32, (BATCH, LANES), 1) // S5_GROUP


def _block_transpose(xs, lane_blk):
    xs = list(xs)
    n = len(xs)
    d = n // 2
    while d:
        low = (lane_blk & d) == 0
        for i in range(n):
            if i & d:
                continue
            a, b = xs[i], xs[i + d]
            xs[i] = jnp.where(low, a, pltpu.roll(b, S5_GROUP * d, 1))
            xs[i + d] = jnp.where(low, pltpu.roll(a, LANES - S5_GROUP * d, 1), b)
        d //= 2
    return xs


S5_SUB = 4


def _s5_in_kernel(*refs):
    h_refs = refs[:S5_SUB]
    mod_ref, g_ref, perm_ref, w_ref, u_ref, z_ref = refs[S5_SUB:]
    mod = mod_ref[...]
    lane_blk = _lane_block()

    def normed(j):
        x = h_refs[j][...]
        a = _rms(x) * g_ref[...] * (1.0 + mod[:, 1:2, :]) + mod[:, 0:1, :]
        a = a.reshape(BATCH * S5_TOK, D_MODEL).astype(BF16)
        return _dot(perm_ref[...], a).astype(BF16)

    def finish(j, s, u_seg):
        u_ref[j * S5_TOK:(j + 1) * S5_TOK, :, s * S5_SEG:(s + 1) * S5_SEG] = u_seg.reshape(S5_TOK, BATCH, S5_SEG)
        for kk in range(S5_SEG // LANES):
            k = s * (S5_SEG // LANES) + kk
            u_col = u_seg[:, kk * LANES:(kk + 1) * LANES]
            for hh in range(S5_CHUNK // STEPS_PER_VREG):
                halves = []
                for c2 in range(S5_PAIR):
                    tok0 = c2 * S5_CHUNK + hh * STEPS_PER_VREG
                    steps = [u_col[(tok0 + m) * BATCH:(tok0 + m + 1) * BATCH, :] for m in range(STEPS_PER_VREG)]
                    halves.append(_block_transpose(steps, lane_blk))
                rows = slice(j * S5_PAIR * BATCH, (j + 1) * S5_PAIR * BATCH)
                for r in range(GROUPS_PER_VREG):
                    z_ref[k * GROUPS_PER_VREG + r, rows, hh * LANES:(hh + 1) * LANES] = (
                        jnp.concatenate([h[r] for h in halves], axis=0).astype(BF16))

    pending = None
    for j in range(S5_SUB):
        a = normed(j)
        for s in range(D_MODEL // S5_SEG):
            u_seg = _dot(a, w_ref[:, s * S5_SEG:(s + 1) * S5_SEG])
            if pending is not None:
                finish(*pending)
            pending = (j, s, u_seg)
    finish(*pending)


def _s5_in(h, mods, g, perm, w_in):
    ctx_steps = N_CTX_CHUNKS // (S5_PAIR * S5_SUB)
    n_rows = BATCH * S5_TOK
    return pl.pallas_call(
        _s5_in_kernel,
        out_shape=[jax.ShapeDtypeStruct((TT, BATCH, D_MODEL), F32),
                   jax.ShapeDtypeStruct((S5_GROUPS, ROWS, S5_CW), BF16)],
        grid=(S5_STEPS // S5_SUB,),
        in_specs=[*[pl.BlockSpec((BATCH, S5_TOK, D_MODEL), lambda p, j=j: (0, p * S5_SUB + j, 0))
                    for j in range(S5_SUB)],
                  pl.BlockSpec((None, BATCH, N_MOD, D_MODEL), lambda p: (jnp.minimum(p // ctx_steps, 1), 0, 0, 0)),
                  pl.BlockSpec((1, D_MODEL), lambda p: (0, 0)),
                  pl.BlockSpec((n_rows, n_rows), lambda p: (0, 0)),
                  pl.BlockSpec((D_MODEL, D_MODEL), lambda p: (0, 0))],
        out_specs=[pl.BlockSpec((S5_SUB * S5_TOK, BATCH, D_MODEL), lambda p: (p, 0, 0)),
                   pl.BlockSpec((S5_GROUPS, S5_SUB * S5_PAIR * BATCH, S5_CW), lambda p: (0, p, 0))],
        compiler_params=_params(),
        name="s5_in_proj",
    )(*[h] * S5_SUB, mods, g, perm, w_in)


OPS_G = 4


def _s5_ops_kernel(*refs):
    for j in range(OPS_G):
        _s5_group_ops(*(r.at[j] for r in refs))


def _s5_group_ops(lr_ref, li_ref, ls_ref, btr_ref, bti_ref, cr_ref, ci_ref, m_ref, q_ref, n_ref, a_ref):
    P2 = 2 * S5_STATE

    def cmul(xr, xi, yr, yi):
        return xr * yr - xi * yi, xr * yi + xi * yr

    lr, li = lr_ref[...], li_ref[...]
    dt = jnp.exp(ls_ref[...])
    mag = jnp.exp(lr * dt)
    ar = mag * jnp.cos(li * dt)
    ai = mag * jnp.sin(li * dt)
    den = lr * lr + li * li
    nr = ar - 1.0
    f_re = (nr * lr + ai * li) / den
    f_im = (ai * lr - nr * li) / den
    bt_re = f_re * btr_ref[...] - f_im * bti_ref[...]
    bt_im = f_re * bti_ref[...] + f_im * btr_ref[...]

    powers = [(jnp.ones_like(ar), jnp.zeros_like(ai))]
    for _ in range(S5_CHUNK):
        powers.append(cmul(*powers[-1], ar, ai))
    row_fwd = lax.broadcasted_iota(jnp.int32, (1, P2), 1) < S5_STATE

    def pow_rows(exp_fwd, exp_bwd):
        return (jnp.where(row_fwd, powers[exp_fwd][0], powers[exp_bwd][0]),
                jnp.where(row_fwd, powers[exp_fwd][1], powers[exp_bwd][1]))

    c_re, c_im = cr_ref[...], ci_ref[...]
    last = S5_CHUNK - 1
    cp_blocks = [cmul(c_re, c_im, *pow_rows(blk, last - blk)) for blk in range(S5_CHUNK)]
    cp_re = jnp.concatenate([b[0] for b in cp_blocks], axis=0)
    cp_im = jnp.concatenate([b[1] for b in cp_blocks], axis=0)
    lane_fwd = lax.broadcasted_iota(jnp.int32, (S5_GROUP, P2), 1) < S5_STATE

    def lag_kernels(keep):
        br = jnp.where(keep, bt_re, 0.0)
        bi = jnp.where(keep, bt_im, 0.0)
        dims = (((1,), (1,)), ((), ()))
        hi = lax.Precision.HIGHEST
        return (lax.dot_general(br, cp_re, dims, precision=hi, preferred_element_type=F32)
                - lax.dot_general(bi, cp_im, dims, precision=hi, preferred_element_type=F32))

    kt_f = lag_kernels(lane_fwd)
    kt_b = lag_kernels(jnp.logical_not(lane_fwd))
    lane_w = lax.broadcasted_iota(jnp.int32, (S5_GROUP, S5_CW), 1)
    for s in range(S5_CHUNK):
        f_part = kt_f if s == 0 else jnp.where(lane_w >= S5_GROUP * s, pltpu.roll(kt_f, S5_GROUP * s, 1), 0.0)
        sh = (S5_GROUP * (s + 1)) % S5_CW
        b_roll = kt_b if sh == 0 else pltpu.roll(kt_b, sh, 1)
        b_part = jnp.where(lane_w < S5_GROUP * (s + 1), b_roll, 0.0)
        m_ref[s * S5_GROUP:(s + 1) * S5_GROUP, :] = (f_part + b_part).astype(BF16)

    for blk in range(S5_CHUNK):
        rows = slice(blk * S5_GROUP, (blk + 1) * S5_GROUP)
        q_re, q_im = cmul(bt_re, bt_im, *pow_rows(last - blk, blk))
        q_ref[rows, 0:P2] = q_re.astype(BF16)
        q_ref[rows, P2:2 * P2] = q_im.astype(BF16)

    for blk in range(S5_CHUNK):
        rows = slice(blk * S5_GROUP, (blk + 1) * S5_GROUP)
        n_re, n_im = cmul(c_re, c_im, *pow_rows(blk + 1, S5_CHUNK - blk))
        n_ref[rows, 0:P2] = n_re.astype(BF16)
        n_ref[rows, P2:2 * P2] = (-n_im).astype(BF16)

    a_ref[0:1, :] = powers[S5_CHUNK][0]
    a_ref[1:2, :] = powers[S5_CHUNK][1]


def _s5_ops(row_params, bt, c_nat):
    P2 = 2 * S5_STATE
    row = pl.BlockSpec((OPS_G, 1, P2), lambda g: (g, 0, 0))
    mat = pl.BlockSpec((OPS_G, S5_GROUP, P2), lambda g: (g, 0, 0))
    sq = pl.BlockSpec((OPS_G, S5_CW, S5_CW), lambda g: (g, 0, 0))
    return pl.pallas_call(
        _s5_ops_kernel,
        out_shape=[jax.ShapeDtypeStruct((S5_GROUPS, S5_CW, S5_CW), BF16)] * 3
        + [jax.ShapeDtypeStruct((S5_GROUPS, 2, P2), F32)],
        grid=(S5_GROUPS // OPS_G,),
        in_specs=[row, row, row, mat, mat, mat, mat],
        out_specs=[sq, sq, sq, pl.BlockSpec((OPS_G, 2, P2), lambda g: (g, 0, 0))],
        compiler_params=_params(),
        name="s5_chunk_operators",
    )(*row_params, *bt, *c_nat)


ROWS = N_CHUNKS * BATCH


SCAN_G = 4


def _s5_scan_kernel(u_ref, m_ref, q_ref, n_ref, a_ref, y_ref, pu_ref, sp_ref, yi_ref):
    S = S5_STATE
    lane = lax.broadcasted_iota(jnp.int32, (BATCH, 2 * S), 1)
    lo = lane < S
    for j in range(SCAN_G):
        pu_ref[j] = _dot(u_ref[j], q_ref[j])
    for j in range(SCAN_G):
        yi_ref[j] = _dot(u_ref[j], m_ref[j])
    decay = [(a_ref[j, 0:1, :], a_ref[j, 1:2, :]) for j in range(SCAN_G)]
    zero = jnp.zeros((BATCH, 2 * S), F32)
    state = [(zero, zero)] * SCAN_G
    for k in range(N_CHUNKS):
        cf = k * BATCH
        cb = (N_CTX_CHUNKS - 1 - k if k < N_CTX_CHUNKS else N_CHUNKS + N_CTX_CHUNKS - 1 - k) * BATCH
        for j in range(SCAN_G):
            s_re, s_im = state[j]
            a_re, a_im = decay[j]
            sp_ref[j, cf:cf + BATCH, 0:S] = s_re[:, 0:S]
            sp_ref[j, cb:cb + BATCH, S:2 * S] = s_re[:, S:2 * S]
            sp_ref[j, cf:cf + BATCH, 2 * S:3 * S] = s_im[:, 0:S]
            sp_ref[j, cb:cb + BATCH, 3 * S:4 * S] = s_im[:, S:2 * S]
            x_re = jnp.where(lo, pu_ref[j, cf:cf + BATCH, 0:2 * S], pu_ref[j, cb:cb + BATCH, 0:2 * S])
            x_im = jnp.where(lo, pu_ref[j, cf:cf + BATCH, 2 * S:4 * S], pu_ref[j, cb:cb + BATCH, 2 * S:4 * S])
            state[j] = (a_re * s_re - a_im * s_im + x_re, a_re * s_im + a_im * s_re + x_im)
    for j in range(SCAN_G):
        y_ref[j] = (yi_ref[j] + _dot_nt(sp_ref[j].astype(BF16), n_ref[j])).astype(BF16)


def _s5_scan(u_g, m_op, q_op, n_op, a_vec):
    sq = pl.BlockSpec((SCAN_G, S5_CW, S5_CW), lambda g: (g, 0, 0))
    rows = pl.BlockSpec((SCAN_G, ROWS, S5_CW), lambda g: (g, 0, 0))
    return pl.pallas_call(
        _s5_scan_kernel,
        out_shape=jax.ShapeDtypeStruct((S5_GROUPS, ROWS, S5_CW), BF16),
        grid=(S5_GROUPS // SCAN_G,),
        in_specs=[rows, sq, sq, sq, pl.BlockSpec((SCAN_G, 2, 2 * S5_STATE), lambda g: (g, 0, 0))],
        out_specs=rows,
        scratch_shapes=[pltpu.VMEM((SCAN_G, ROWS, S5_CW), F32)] * 3,
        compiler_params=_params(),
        name="s5_scan",
    )(u_g, m_op, q_op, n_op, a_vec)


TAIL_SUB = 4
TAIL_VMEM_LIMIT = 60 * 1024 * 1024


def _s5_tail_kernel(u_ref, y_ref, d_ref, gw_ref, gb_ref, perm_ref, h_ref, mod_ref, g_ref, wo_ref, w1_ref, w2_ref,
                    o_ref):
    lane_blk = _lane_block()
    n_seg = D_MODEL // S5_SEG
    per_seg = S5_SEG // LANES
    mod = mod_ref[...]

    def regroup(j, s):
        cols = []
        for k in range(s * per_seg, (s + 1) * per_seg):
            by_tok = [None] * S5_TOK
            rows = slice(j * S5_PAIR * BATCH, (j + 1) * S5_PAIR * BATCH)
            for hh in range(S5_CHUNK // STEPS_PER_VREG):
                pairs = [y_ref[k * GROUPS_PER_VREG + r, rows, hh * LANES:(hh + 1) * LANES].astype(F32)
                         for r in range(GROUPS_PER_VREG)]
                for c2 in range(S5_PAIR):
                    groups = [p[c2 * BATCH:(c2 + 1) * BATCH] for p in pairs]
                    steps = _block_transpose(groups, lane_blk)
                    for m in range(STEPS_PER_VREG):
                        by_tok[c2 * S5_CHUNK + hh * STEPS_PER_VREG + m] = steps[m]
            cols.append(jnp.concatenate(by_tok, axis=0))
        return jnp.concatenate(cols, axis=1)

    def gated_mixer(j, out):
        g_parts, pre = [], None
        ys_next = regroup(j, 0)
        for s in range(n_seg):
            ys = ys_next
            if s + 1 < n_seg:
                ys_next = regroup(j, s + 1)
            cols = slice(s * S5_SEG, (s + 1) * S5_SEG)
            u = u_ref[j * S5_TOK:(j + 1) * S5_TOK, :, cols].reshape(BATCH * S5_TOK, S5_SEG)
            g_seg = jax.nn.gelu(u * d_ref[:, cols] + ys)
            g_parts.append(g_seg)
            part = _dot(g_seg.astype(BF16), gw_ref[cols, :])
            pre = part if pre is None else pre + part
            if s + 1 < n_seg:
                yield
        gate = jax.nn.sigmoid(pre + gb_ref[...])
        gated = (jnp.concatenate(g_parts, axis=1) * gate).astype(BF16)
        out[0] = _dot(perm_ref[...], gated).astype(BF16)
        yield

    def mlp(j, gated):
        tok = slice(j * S5_TOK, (j + 1) * S5_TOK)
        y = _dot(gated, wo_ref[...]).reshape(BATCH, S5_TOK, D_MODEL)
        h1 = h_ref[:, tok, :] + mod[:, 2:3, :] * y
        f = (_rms(h1) * g_ref[...] * (1.0 + mod[:, 4:5, :]) + mod[:, 3:4, :])
        f = f.reshape(BATCH * S5_TOK, D_MODEL).astype(BF16)
        acc = jnp.zeros((BATCH * S5_TOK, D_MODEL), F32)
        for c in range(D_FF // FF_CHUNK):
            hid = jnp.maximum(_dot(f, w1_ref[:, c * FF_CHUNK:(c + 1) * FF_CHUNK]), 0.0)
            acc = acc + _dot((hid * hid).astype(BF16), w2_ref[c * FF_CHUNK:(c + 1) * FF_CHUNK, :])
            if c + 1 < D_FF // FF_CHUNK:
                yield
        o_ref[:, tok, :] = h1 + mod[:, 5:6, :] * acc.reshape(BATCH, S5_TOK, D_MODEL)
        yield

    cur = [None]
    for _ in gated_mixer(0, cur):
        pass
    for j in range(TAIL_SUB):
        nxt = [None]
        streams = [mlp(j, cur[0])] + ([gated_mixer(j + 1, nxt)] if j + 1 < TAIL_SUB else [])
        while streams:
            for gen in list(streams):
                if next(gen, StopIteration) is StopIteration:
                    streams.remove(gen)
        cur = nxt


def _s5_tail(u, y_g, d_skip, glu_w, glu_b, perm_t, h, mods, g, wo, w1_all, w2_all, layer):
    ctx_steps = N_CTX_CHUNKS // (S5_PAIR * TAIL_SUB)
    n_rows = BATCH * S5_TOK
    vec = pl.BlockSpec((1, D_MODEL), lambda p: (0, 0))

    def full(shape):
        return pl.BlockSpec(shape, lambda p: (0,) * len(shape), pipeline_mode=pl.Buffered(1))

    return pl.pallas_call(
        _s5_tail_kernel,
        out_shape=jax.ShapeDtypeStruct((BATCH, SEQ, D_MODEL), F32),
        grid=(S5_STEPS // TAIL_SUB - ctx_steps,),
        in_specs=[pl.BlockSpec((TAIL_SUB * S5_TOK, BATCH, D_MODEL), lambda p: (p + ctx_steps, 0, 0)),
                  pl.BlockSpec((S5_GROUPS, TAIL_SUB * S5_PAIR * BATCH, S5_CW), lambda p: (0, p + ctx_steps, 0)),
                  vec, full((D_MODEL, D_MODEL)), vec, full((n_rows, n_rows)),
                  pl.BlockSpec((BATCH, TAIL_SUB * S5_TOK, D_MODEL), lambda p: (0, p + ctx_steps, 0)),
                  pl.BlockSpec((None, BATCH, N_MOD, D_MODEL), lambda p: (1, 0, 0, 0)),
                  vec, full((D_MODEL, D_MODEL)),
                  pl.BlockSpec((None, D_MODEL, D_FF), lambda p: (layer, 0, 0), pipeline_mode=pl.Buffered(1)),
                  pl.BlockSpec((None, D_FF, D_MODEL), lambda p: (layer, 0, 0), pipeline_mode=pl.Buffered(1))],
        out_specs=pl.BlockSpec((BATCH, TAIL_SUB * S5_TOK, D_MODEL), lambda p: (0, p, 0)),
        compiler_params=pltpu.CompilerParams(vmem_limit_bytes=TAIL_VMEM_LIMIT),
        name="s5_tail_mlp",
    )(u, y_g, d_skip, glu_w, glu_b, perm_t, h, mods, g, wo, w1_all, w2_all)


def _rope_tables():
    rows_n = SEQ // GRID_W
    row = np.repeat(np.arange(rows_n, dtype=np.float64), GRID_W)
    col = np.tile(np.arange(GRID_W, dtype=np.float64), rows_n)
    n_freq = HEAD_DIM // 4
    inv = ROPE_BASE ** (-np.arange(n_freq, dtype=np.float64) / n_freq)
    ang = np.concatenate([row[:, None] * inv, col[:, None] * inv], axis=-1)
    reps = LANES // (HEAD_DIM // 2)
    cos_t = np.tile(np.cos(ang), (1, reps))
    sin_t = np.tile(np.sin(ang), (1, reps))
    sign = np.where((np.arange(LANES) % HEAD_DIM) < HEAD_DIM // 2, -1.0, 1.0)
    cos_t = np.concatenate([np.ones((CTX_LEN, LANES)), cos_t], axis=0)
    sin_s = np.concatenate([np.zeros((CTX_LEN, LANES)), sin_t * sign], axis=0)
    return jnp.asarray(cos_t, F32), jnp.asarray(sin_s, F32)


def _fb_rows(x):
    return jnp.transpose(x, (1, 0, 2)).reshape(S5_GROUPS, 1, 2 * S5_STATE)


def _s5_layout(lam_re, lam_im, log_step, b_re, b_im, c_re, c_im):
    ls = jnp.broadcast_to(log_step[:, :, None], lam_re.shape)
    rows = [_fb_rows(v) for v in (lam_re, lam_im, ls)]

    def bt_of(b):
        return jnp.transpose(b, (1, 3, 0, 2)).reshape(S5_GROUPS, S5_GROUP, 2 * S5_STATE)

    def c_of(c):
        return jnp.transpose(c, (1, 2, 0, 3)).reshape(S5_GROUPS, S5_GROUP, 2 * S5_STATE)

    return rows, [bt_of(b_re), bt_of(b_im)], [c_of(c_re), c_of(c_im)]


def kernel(x, c, ctx, c_ctx, norm1_g, norm2_g, mod_w, mod_b, mlp_w1, mlp_w2, attn_w_in, attn_w_out, a_q_norm, a_k_norm, a_sink, b_q_norm, b_k_norm, b_lq1, b_lk1, b_lq2, b_lk2, b_subln, s5_w_in, s5_lambda_re, s5_lambda_im, s5_log_step, s5_b_re, s5_b_im, s5_c_re, s5_c_im, s5_d, s5_glu_w, s5_glu_b, s5_w_out):
    assert x.shape == (BATCH, SEQ, D_MODEL) and ctx.shape == (BATCH, CTX_LEN, D_MODEL)
    stream = (ctx, x)
    s_rows = jnp.concatenate([c, c_ctx[None], jnp.zeros((MOD_ROWS - BATCH - 1, D_MODEL), F32)], axis=0)
    m_all = _modulation(s_rows, mod_w, mod_b)
    cos_t, sin_s = _rope_tables()
    w1_all, w2_all = mlp_w1.astype(BF16), mlp_w2.astype(BF16)
    e_blk = jnp.asarray(np.kron(np.eye(2 * LANES // HEAD_DIM), np.ones((HEAD_DIM, HEAD_DIM))) / HEAD_DIM, BF16)

    for i in range(DEPTH):
        last = i == DEPTH - 1
        j = i // 2
        m_lat = m_all[i, :BATCH].reshape(BATCH, N_MOD, D_MODEL)
        m_ctx = jnp.broadcast_to(m_all[i, BATCH].reshape(1, N_MOD, D_MODEL), (BATCH, N_MOD, D_MODEL))
        mods = jnp.stack([m_ctx, m_lat])
        g1 = norm1_g[i].reshape(1, D_MODEL)
        g2 = norm2_g[i].reshape(1, D_MODEL)
        if i % 2 == 0:
            lambda_init = 0.8 - 0.6 * math.exp(-0.3 * i)
            gains = jnp.stack([jnp.tile(v[j], LANES // HEAD_DIM) for v in (a_q_norm, a_k_norm, b_q_norm, b_k_norm)])
            qa, k2a, v2a, qb, kb, vb = _attn_in(stream, mods, g1, attn_w_in[j].astype(BF16), gains, cos_t, sin_s, e_blk)
            lpar = jnp.stack([b_lq1[j], b_lk1[j], b_lq2[j], b_lk2[j]])
            ya, yb = _attention(a_sink[j], lpar, b_subln[j].reshape(1, LANES), qa, k2a, v2a, qb, kb, vb, lambda_init)
            if last:
                ya, yb = ya[:, CTX_LEN:], yb[:, CTX_LEN:]
            h = _mix_mlp(stream, ya, yb, mods, g2, attn_w_out[j].astype(BF16), w1_all, w2_all, i, latent_only=last)
        else:
            src = np.arange(BATCH * S5_TOK).reshape(BATCH, S5_TOK).T.reshape(-1)
            perm = jnp.asarray(np.eye(BATCH * S5_TOK, dtype=np.float32)[src], BF16)
            h_all = stream[0] if stream[0].shape[1] == TT else jnp.concatenate(stream, axis=1)
            u, u_g = _s5_in(h_all, mods, g1, perm, s5_w_in[j].astype(BF16))
            ops_in = _s5_layout(s5_lambda_re[j], s5_lambda_im[j], s5_log_step[j], s5_b_re[j], s5_b_im[j],
                                s5_c_re[j], s5_c_im[j])
            m_op, q_op, n_op, a_vec = _s5_ops(*ops_in)
            y_g = _s5_scan(u_g, m_op, q_op, n_op, a_vec)
            assert last, "S5 layers before the last one would also need the context rows of the readout"
            h = _s5_tail(u, y_g, s5_d[j].reshape(1, D_MODEL), s5_glu_w[j].astype(BF16),
                         s5_glu_b[j].reshape(1, D_MODEL), perm.T, h_all, mods, g2, s5_w_out[j].astype(BF16),
                         w1_all, w2_all, i)
        stream = (h, h)
    return h
```

```python
import functools
import math

import jax
import jax.numpy as jnp
import numpy as np
from jax import lax
from jax.experimental import pallas as pl
from jax.experimental.pallas import tpu as pltpu

F32 = jnp.float32
BF16 = jnp.bfloat16

D_MODEL = 1024
BATCH = 8
SEQ = 2048
DEPTH = 2
GRID_W = 64
CTX_LEN = 256
HEAD_DIM = 64
WINDOW = 128
A_Q_HEADS = 8
A_KV_HEADS = 2
B_HEADS = 4
A_Q_W = A_Q_HEADS * HEAD_DIM
A_KV_W = A_KV_HEADS * HEAD_DIM
B_QK_W = B_HEADS * 2 * HEAD_DIM
B_V_W = B_HEADS * 2 * HEAD_DIM
ATTN_IN = A_Q_W + 2 * A_KV_W + 2 * B_QK_W + B_V_W
S5_GROUP = 16
S5_GROUPS = D_MODEL // S5_GROUP
S5_STATE = 64
D_FF = 4 * D_MODEL
ROPE_BASE = 10000.0
EPS = 1e-6
NEG_INF = -1e30
LOG2E = math.log2(math.e)

TT = CTX_LEN + SEQ
TM = 256
N_TILES = TT // TM
LANES = 128
S5_CHUNK = 16
S5_CW = S5_CHUNK * S5_GROUP
N_CHUNKS = TT // S5_CHUNK
N_CTX_CHUNKS = CTX_LEN // S5_CHUNK
VMEM_LIMIT = 56 * 1024 * 1024
N_MOD = 6
MOD_ROWS = 16
MOD_COLS = 2 * D_MODEL
GQA = A_Q_HEADS // A_KV_HEADS


def _dot(a, b):
    return jnp.dot(a, b, preferred_element_type=F32)


def _dot_nt(a, b):
    return lax.dot_general(a, b, (((1,), (1,)), ((), ())), preferred_element_type=F32)


def _rms(x):
    return x * lax.rsqrt(jnp.mean(x * x, axis=-1, keepdims=True) + EPS)


def _modnorm(x, g, shift, scale):
    return _rms(x) * g * (1.0 + scale) + shift


def _params(**kw):
    return pltpu.CompilerParams(vmem_limit_bytes=VMEM_LIMIT, **kw)


def _mod_kernel(s_ref, w_ref, b_ref, o_ref):
    s = s_ref[...]
    s = s * jax.nn.sigmoid(s)
    o_ref[...] = _dot(s.astype(BF16), w_ref[...].astype(BF16)) + b_ref[...]


def _modulation(s_rows, mod_w, mod_b):
    return pl.pallas_call(
        _mod_kernel,
        out_shape=jax.ShapeDtypeStruct((DEPTH, MOD_ROWS, N_MOD * D_MODEL), F32),
        grid=(DEPTH, N_MOD * D_MODEL // MOD_COLS),
        in_specs=[
            pl.BlockSpec((MOD_ROWS, D_MODEL), lambda i, j: (0, 0)),
            pl.BlockSpec((None, D_MODEL, MOD_COLS), lambda i, j: (i, 0, j)),
            pl.BlockSpec((None, 1, MOD_COLS), lambda i, j: (i, 0, j)),
        ],
        out_specs=pl.BlockSpec((None, MOD_ROWS, MOD_COLS), lambda i, j: (i, 0, j)),
        compiler_params=_params(),
        name="modulation",
    )(s_rows, mod_w, mod_b.reshape(DEPTH, 1, N_MOD * D_MODEL))


def _stream_specs(stream, n_sub, h_off):
    first_lat = 1 if stream[0].shape[1] == TT else 0

    def tile(t, k):
        return t * n_sub + k + h_off

    ctx_spec = pl.BlockSpec((None, TM, D_MODEL), lambda b, t: (b, 0, 0))
    lat_specs = [pl.BlockSpec((None, TM, D_MODEL),
                              lambda b, t, k=k: (b, jnp.maximum(tile(t, k) - 1 + first_lat, first_lat), 0))
                 for k in range(n_sub)]
    mod_specs = [pl.BlockSpec((None, None, N_MOD, D_MODEL), lambda b, t, k=k: (jnp.minimum(tile(t, k), 1), b, 0, 0))
                 for k in range(n_sub)]
    return ctx_spec, lat_specs, mod_specs


ATTN_IN_SUB = 3


def _attn_in_kernel(*refs):
    n_sub = ATTN_IN_SUB
    hc_ref = refs[0]
    hl_refs = refs[1:1 + n_sub]
    mod_refs = refs[1 + n_sub:1 + 2 * n_sub]
    (g_ref, w_ref, gain_ref, cos_ref, sin_ref, e_ref,
     qa_ref, k2a_ref, v2a_ref, qb_ref, kb_ref, vb_ref) = refs[1 + 2 * n_sub:]
    e = e_ref[...]
    gains = gain_ref[...]
    lane = lax.broadcasted_iota(jnp.int32, (TM, LANES), 1)
    first_half = (lane & (HEAD_DIM - 1)) < HEAD_DIM // 2
    lo = lane < HEAD_DIM
    q_scale = HEAD_DIM ** -0.5 * LOG2E
    ones = jnp.ones((TM, LANES), BF16)

    def tile_segments(k):
        rows = slice(k * TM, (k + 1) * TM)
        cos_t = cos_ref[rows, :]
        sin_s = sin_ref[rows, :]

        def head_mean_sq(z):
            sq = (z * z).astype(BF16)
            width = z.shape[1]
            if width < 2 * LANES:
                return _dot(sq, e[0:width, 0:width])
            return jnp.concatenate([_dot(sq[:, c:c + 2 * LANES], e) for c in range(0, width, 2 * LANES)], axis=1)

        def norm_rope_chunks(z, gain):
            ms = head_mean_sq(z)
            for c in range(z.shape[1] // LANES):
                cols = slice(c * LANES, (c + 1) * LANES)
                cn = z[:, cols] * lax.rsqrt(ms[:, cols] + EPS) * gain
                r_fwd = pltpu.roll(cn, HEAD_DIM // 2, 1)
                r_bwd = pltpu.roll(cn, LANES - HEAD_DIM // 2, 1)
                yield cn * cos_t + jnp.where(first_half, r_bwd, r_fwd) * sin_s

        def dup_halves(x):
            sw = pltpu.roll(x, HEAD_DIM, 1)
            return jnp.where(lo, x, sw).astype(BF16), jnp.where(lo, sw, x).astype(BF16)

        def finish_q(z, ref, gain):
            for c, chunk in enumerate(norm_rope_chunks(z, gain)):
                ref[rows, c * LANES:(c + 1) * LANES] = (chunk * q_scale).astype(BF16)

        def finish_kv_a(z):
            (k_roped,) = norm_rope_chunks(z[:, 0:LANES], gains[1:2])
            for kvh, dup in enumerate(dup_halves(k_roped)):
                k2a_ref[rows, kvh * LANES:(kvh + 1) * LANES] = dup
            for kvh, dup in enumerate(dup_halves(z[:, LANES:2 * LANES])):
                v2a_ref[rows, 2 * kvh * LANES:(2 * kvh + 1) * LANES] = dup
                v2a_ref[rows, (2 * kvh + 1) * LANES:(2 * kvh + 2) * LANES] = ones

        def finish_kb(z):
            for c, chunk in enumerate(norm_rope_chunks(z, gains[3:4])):
                kb_ref[rows, c * LANES:(c + 1) * LANES] = chunk.astype(BF16)

        def finish_vb(z):
            for hd in range(B_HEADS):
                vb_ref[rows, 2 * hd * LANES:(2 * hd + 1) * LANES] = z[:, hd * LANES:(hd + 1) * LANES].astype(BF16)
                vb_ref[rows, (2 * hd + 1) * LANES:(2 * hd + 2) * LANES] = ones

        return [(A_Q_W, lambda z: finish_q(z, qa_ref, gains[0:1])), (2 * A_KV_W, finish_kv_a),
                (B_QK_W, lambda z: finish_q(z, qb_ref, gains[2:3])), (B_QK_W, finish_kb), (B_V_W, finish_vb)]

    def normed(k):
        mod = mod_refs[k][...]
        x = hl_refs[k][...]
        if k == 0:
            x = jnp.where(pl.program_id(1) == 0, hc_ref[...], x)
        return _modnorm(x, g_ref[...], mod[0:1], mod[1:2]).astype(BF16)

    pending = None
    for k in range(n_sub):
        a = normed(k)
        off = 0
        for width, finish in tile_segments(k):
            z = _dot(a, w_ref[:, off:off + width])
            off += width
            if pending is not None:
                pending[1](pending[0])
            pending = (z, finish)
    pending[1](pending[0])


def _attn_in(stream, mods, g, w_in, gains, cos_t, sin_s, e_blk):
    n_sub = ATTN_IN_SUB
    ctx_spec, lat_specs, mod_specs = _stream_specs(stream, n_sub, 0)

    def tok(width):
        return pl.BlockSpec((None, n_sub * TM, width), lambda b, t: (b, t, 0))

    def full(shape):
        return pl.BlockSpec(shape, lambda b, t: (0,) * len(shape))

    out_shapes = [jax.ShapeDtypeStruct((BATCH, TT, w), BF16)
                  for w in (A_Q_W, 2 * A_KV_W, 4 * A_KV_W, B_QK_W, B_QK_W, 2 * B_V_W)]
    return pl.pallas_call(
        _attn_in_kernel,
        out_shape=out_shapes,
        grid=(BATCH, N_TILES // n_sub),
        in_specs=[
            ctx_spec, *lat_specs, *mod_specs,
            full((1, D_MODEL)), full((D_MODEL, ATTN_IN)), full((4, LANES)),
            pl.BlockSpec((n_sub * TM, LANES), lambda b, t: (t, 0)),
            pl.BlockSpec((n_sub * TM, LANES), lambda b, t: (t, 0)),
            full((2 * LANES, 2 * LANES)),
        ],
        out_specs=[tok(A_Q_W), tok(2 * A_KV_W), tok(4 * A_KV_W), tok(B_QK_W), tok(B_QK_W), tok(2 * B_V_W)],
        compiler_params=_params(),
        name="attn_in_proj",
    )(stream[0], *[stream[1]] * n_sub, *[mods] * n_sub, g, w_in, gains, cos_t, sin_s, e_blk)


QB = 128


def _run_pipelined(items):
    s_next = items[0][0]()
    for i, (_, finish) in enumerate(items):
        s_cur = s_next
        if i + 1 < len(items):
            s_next = items[i + 1][0]()
        finish(s_cur)


def _win_attn_items(t, sink_ref, q_ref, k2_ref, v2_ref, o_ref, with_window):
    lane = lax.broadcasted_iota(jnp.int32, (QB, LANES), 1)
    lo = lane < HEAD_DIM
    rows = GQA * QB
    row = lax.broadcasted_iota(jnp.int32, (rows, 3 * QB), 0)
    col = lax.broadcasted_iota(jnp.int32, (rows, 3 * QB), 1)
    row_head = lax.broadcasted_iota(jnp.int32, (rows, 1), 0) // QB
    zero = jnp.zeros((QB, LANES), BF16)
    blocks = [(qb, g) for qb in range(TM // QB) for g in range(A_KV_HEADS)]

    def window_start(qb):
        n = (t - 1) * (TM // QB) + qb
        ws = jnp.clip((n - 1) * QB, 0, SEQ - 3 * QB)
        return n, ws

    def scores(qb, g):
        pieces = []
        for p in range(2):
            qp = q_ref[qb * QB:(qb + 1) * QB, g * 2 * LANES + p * LANES: g * 2 * LANES + (p + 1) * LANES]
            pieces.append(jnp.where(lo, qp, zero))
            pieces.append(jnp.where(lo, zero, qp))
        qs = jnp.concatenate(pieces, axis=0)
        s_c = _dot_nt(qs, k2_ref[0:CTX_LEN, g * LANES:(g + 1) * LANES])
        if not with_window:
            return s_c, None
        n, ws = window_start(qb)
        kw = k2_ref[pl.ds(pl.multiple_of(ws + CTX_LEN, QB), 3 * QB), g * LANES:(g + 1) * LANES]
        valid = jnp.abs(n * QB + (row & (QB - 1)) - (ws + col)) <= WINDOW
        return s_c, jnp.where(valid, _dot_nt(qs, kw), NEG_INF)

    def finish(qb, g, s):
        s_c, s_w = s
        sk = jnp.full((rows, 1), sink_ref[GQA * g + GQA - 1], F32)
        for hh in range(GQA - 1):
            sk = jnp.where(row_head == hh, sink_ref[GQA * g + hh], sk)
        sk = sk * LOG2E
        m = jnp.maximum(jnp.max(s_c, axis=-1, keepdims=True), sk)
        if s_w is not None:
            m = jnp.maximum(m, jnp.max(s_w, axis=-1, keepdims=True))
        vcols = slice(2 * g * LANES, (2 * g + 2) * LANES)
        pv = _dot(jnp.exp2(s_c - m).astype(BF16), v2_ref[0:CTX_LEN, vcols])
        if s_w is not None:
            _, ws = window_start(qb)
            vw = v2_ref[pl.ds(pl.multiple_of(ws + CTX_LEN, QB), 3 * QB), vcols]
            pv = pv + _dot(jnp.exp2(s_w - m).astype(BF16), vw)
        o = pv[:, 0:LANES] / (pv[:, LANES:2 * LANES] + jnp.exp2(sk - m))
        for p in range(2):
            o_ref[qb * QB:(qb + 1) * QB, g * 2 * LANES + p * LANES: g * 2 * LANES + (p + 1) * LANES] = jnp.where(
                lo, o[2 * p * QB:(2 * p + 1) * QB], o[(2 * p + 1) * QB:(2 * p + 2) * QB]).astype(BF16)

    return [(functools.partial(scores, qb, g), functools.partial(finish, qb, g)) for qb, g in blocks]


DIFF_ROWS = 256


def _diff_attn_items(lam, subln_ref, q_ref, k_ref, v_ref, o_ref, n_keys, lambda_init):
    R = DIFF_ROWS
    lane = lax.broadcasted_iota(jnp.int32, (R, LANES), 1)
    lo = lane < HEAD_DIM
    zero = jnp.zeros((R, LANES), BF16)
    blocks = [(slice(rb * R, (rb + 1) * R), slice(h * LANES, (h + 1) * LANES))
              for h in range(B_HEADS) for rb in range(TM // R)]

    def scores(rows, cols):
        q = q_ref[rows, cols]
        qs = jnp.concatenate([jnp.where(lo, q, zero), jnp.where(lo, zero, q)], axis=0)
        return _dot_nt(qs, k_ref[0:n_keys, cols])

    def finish(rows, cols, s):
        p = jnp.exp2(s - jnp.max(s, axis=-1, keepdims=True)).astype(BF16)
        vcols = slice(2 * cols.start, 2 * cols.stop)
        pv = _dot(p, v_ref[0:n_keys, vcols])
        sm = pv[:, 0:LANES] / pv[:, LANES:2 * LANES]
        y = sm[0:R] - lam * sm[R:2 * R]
        o_ref[rows, cols] = (_rms(y) * subln_ref[...] * (1.0 - lambda_init)).astype(BF16)

    return [(functools.partial(scores, *blk), functools.partial(finish, *blk)) for blk in blocks]


def _attention_kernel(sink_ref, lpar_ref, subln_ref, qa_ref, k2a_ref, v2a_ref, qb_ref, kb_ref, vb_ref,
                      ya_ref, yb_ref, *, lambda_init):
    t = pl.program_id(1)
    lp = lpar_ref[...]
    lam = (jnp.exp(jnp.sum(lp[0:1] * lp[1:2], axis=-1, keepdims=True))
           - jnp.exp(jnp.sum(lp[2:3] * lp[3:4], axis=-1, keepdims=True)) + lambda_init)

    def attend(is_ctx):
        win = _win_attn_items(t, sink_ref, qa_ref, k2a_ref, v2a_ref, ya_ref, not is_ctx)
        dif = _diff_attn_items(lam, subln_ref, qb_ref, kb_ref, vb_ref, yb_ref, CTX_LEN if is_ctx else TT,
                               lambda_init)
        assert len(win) == len(dif)
        _run_pipelined([item for pair in zip(dif, win) for item in pair])

    @pl.when(t == 0)
    def _():
        attend(True)

    @pl.when(t > 0)
    def _():
        attend(False)


def _attention(sink, lpar, subln, qa, k2a, v2a, qb, kb, vb, lambda_init):
    def tile(width):
        return pl.BlockSpec((None, TM, width), lambda b, t: (b, t, 0))

    def keys(width):
        return pl.BlockSpec((None, TT, width), lambda b, t: (b, 0, 0))

    return pl.pallas_call(
        functools.partial(_attention_kernel, lambda_init=lambda_init),
        out_shape=[jax.ShapeDtypeStruct((BATCH, TT, A_Q_W), BF16), jax.ShapeDtypeStruct((BATCH, TT, B_V_W), BF16)],
        grid=(BATCH, N_TILES),
        in_specs=[
            pl.BlockSpec(memory_space=pltpu.SMEM),
            pl.BlockSpec((4, HEAD_DIM), lambda b, t: (0, 0)),
            pl.BlockSpec((1, LANES), lambda b, t: (0, 0)),
            tile(A_Q_W), keys(2 * A_KV_W), keys(4 * A_KV_W),
            tile(B_QK_W), keys(B_QK_W), keys(2 * B_V_W),
        ],
        out_specs=[tile(A_Q_W), tile(B_V_W)],
        compiler_params=_params(),
        name="attention",
    )(sink, lpar, subln, qa, k2a, v2a, qb, kb, vb)


FF_CHUNK = 1024


def _mix_mlp_kernel(*refs, h_off, n_sub):
    hc_ref = refs[0]
    hl_refs, ua_refs, ub_refs, mod_refs = (refs[1 + i * n_sub:1 + (i + 1) * n_sub] for i in range(4))
    g_ref, wo_ref, w1_ref, w2_ref, o_ref = refs[1 + 4 * n_sub:]
    half = D_MODEL // 2

    def prologue(k):
        mod = mod_refs[k][...]
        y = _dot(ua_refs[k][...], wo_ref[0:half, :]) + _dot(ub_refs[k][...], wo_ref[half:D_MODEL, :])
        x = hl_refs[k][...]
        if h_off == 0 and k == 0:
            x = jnp.where(pl.program_id(1) == 0, hc_ref[...], x)
        h1 = x + mod[2:3] * y
        f = _modnorm(h1, g_ref[...], mod[3:4], mod[4:5]).astype(BF16)
        return h1, f, mod[5:6]

    def mlp(k, h1, f, gate):
        acc = jnp.zeros((TM, D_MODEL), F32)
        for c in range(D_FF // FF_CHUNK):
            hid = jnp.maximum(_dot(f, w1_ref[:, c * FF_CHUNK:(c + 1) * FF_CHUNK]), 0.0)
            acc = acc + _dot((hid * hid).astype(BF16), w2_ref[c * FF_CHUNK:(c + 1) * FF_CHUNK, :])
        o_ref[k * TM:(k + 1) * TM, :] = h1 + gate * acc

    nxt = prologue(0)
    for k in range(n_sub):
        cur = nxt
        if k + 1 < n_sub:
            nxt = prologue(k + 1)
        mlp(k, *cur)


def _mix_mlp(stream, ua, ub, mods, g, wo, w1_all, w2_all, layer, *, latent_only):
    n_tiles = SEQ // TM if latent_only else N_TILES
    h_off = N_TILES - n_tiles
    n_sub = 4 if n_tiles % 4 == 0 else 3
    half = D_MODEL // 2
    ub_col = 1 if ub.shape[-1] == D_MODEL else 0
    ctx_spec, lat_specs, mod_specs = _stream_specs(stream, n_sub, h_off)

    def full(shape):
        return pl.BlockSpec(shape, lambda b, t: (0,) * len(shape), pipeline_mode=pl.Buffered(1))

    def mixer_specs(col):
        return [pl.BlockSpec((None, TM, half), lambda b, t, k=k: (b, t * n_sub + k, col)) for k in range(n_sub)]

    return pl.pallas_call(
        functools.partial(_mix_mlp_kernel, h_off=h_off, n_sub=n_sub),
        out_shape=jax.ShapeDtypeStruct((BATCH, n_tiles * TM, D_MODEL), F32),
        grid=(BATCH, n_tiles // n_sub),
        in_specs=[
            ctx_spec, *lat_specs, *mixer_specs(0), *mixer_specs(ub_col), *mod_specs,
            pl.BlockSpec((1, D_MODEL), lambda b, t: (0, 0)),
            full((D_MODEL, D_MODEL)),
            pl.BlockSpec((None, D_MODEL, D_FF), lambda b, t: (layer, 0, 0), pipeline_mode=pl.Buffered(1)),
            pl.BlockSpec((None, D_FF, D_MODEL), lambda b, t: (layer, 0, 0), pipeline_mode=pl.Buffered(1)),
        ],
        out_specs=pl.BlockSpec((None, n_sub * TM, D_MODEL), lambda b, t: (b, t, 0)),
        compiler_params=_params(),
        name="mixer_out_mlp",
    )(stream[0], *[stream[1]] * n_sub, *[ua] * n_sub, *[ub] * n_sub, *[mods] * n_sub, g, wo, w1_all, w2_all)


S5_PAIR = 2
S5_TOK = S5_PAIR * S5_CHUNK
S5_STEPS = N_CHUNKS // S5_PAIR
GROUPS_PER_VREG = LANES // S5_GROUP
STEPS_PER_VREG = LANES // S5_GROUP
S5_SEG = 2 * LANES


def _lane_block():
    return lax.broadcasted_iota(jnp.int32, (BATCH, LANES), 1) // S5_GROUP


def _block_transpose(xs, lane_blk):
    xs = list(xs)
    n = len(xs)
    d = n // 2
    while d:
        low = (lane_blk & d) == 0
        for i in range(n):
            if i & d:
                continue
            a, b = xs[i], xs[i + d]
            xs[i] = jnp.where(low, a, pltpu.roll(b, S5_GROUP * d, 1))
            xs[i + d] = jnp.where(low, pltpu.roll(a, LANES - S5_GROUP * d, 1), b)
        d //= 2
    return xs


S5_SUB = 4


def _s5_in_kernel(*refs):
    h_refs = refs[:S5_SUB]
    mod_ref, g_ref, perm_ref, w_ref, u_ref, z_ref = refs[S5_SUB:]
    mod = mod_ref[...]
    lane_blk = _lane_block()

    def normed(j):
        x = h_refs[j][...]
        a = _rms(x) * g_ref[...] * (1.0 + mod[:, 1:2, :]) + mod[:, 0:1, :]
        a = a.reshape(BATCH * S5_TOK, D_MODEL).astype(BF16)
        return _dot(perm_ref[...], a).astype(BF16)

    def finish(j, s, u_seg):
        u_ref[j * S5_TOK:(j + 1) * S5_TOK, :, s * S5_SEG:(s + 1) * S5_SEG] = u_seg.reshape(S5_TOK, BATCH, S5_SEG)
        for kk in range(S5_SEG // LANES):
            k = s * (S5_SEG // LANES) + kk
            u_col = u_seg[:, kk * LANES:(kk + 1) * LANES]
            for hh in range(S5_CHUNK // STEPS_PER_VREG):
                halves = []
                for c2 in range(S5_PAIR):
                    tok0 = c2 * S5_CHUNK + hh * STEPS_PER_VREG
                    steps = [u_col[(tok0 + m) * BATCH:(tok0 + m + 1) * BATCH, :] for m in range(STEPS_PER_VREG)]
                    halves.append(_block_transpose(steps, lane_blk))
                rows = slice(j * S5_PAIR * BATCH, (j + 1) * S5_PAIR * BATCH)
                for r in range(GROUPS_PER_VREG):
                    z_ref[k * GROUPS_PER_VREG + r, rows, hh * LANES:(hh + 1) * LANES] = (
                        jnp.concatenate([h[r] for h in halves], axis=0).astype(BF16))

    pending = None
    for j in range(S5_SUB):
        a = normed(j)
        for s in range(D_MODEL // S5_SEG):
            u_seg = _dot(a, w_ref[:, s * S5_SEG:(s + 1) * S5_SEG])
            if pending is not None:
                finish(*pending)
            pending = (j, s, u_seg)
    finish(*pending)


def _s5_in(h, mods, g, perm, w_in):
    ctx_steps = N_CTX_CHUNKS // (S5_PAIR * S5_SUB)
    n_rows = BATCH * S5_TOK
    return pl.pallas_call(
        _s5_in_kernel,
        out_shape=[jax.ShapeDtypeStruct((TT, BATCH, D_MODEL), F32),
                   jax.ShapeDtypeStruct((S5_GROUPS, ROWS, S5_CW), BF16)],
        grid=(S5_STEPS // S5_SUB,),
        in_specs=[*[pl.BlockSpec((BATCH, S5_TOK, D_MODEL), lambda p, j=j: (0, p * S5_SUB + j, 0))
                    for j in range(S5_SUB)],
                  pl.BlockSpec((None, BATCH, N_MOD, D_MODEL), lambda p: (jnp.minimum(p // ctx_steps, 1), 0, 0, 0)),
                  pl.BlockSpec((1, D_MODEL), lambda p: (0, 0)),
                  pl.BlockSpec((n_rows, n_rows), lambda p: (0, 0)),
                  pl.BlockSpec((D_MODEL, D_MODEL), lambda p: (0, 0))],
        out_specs=[pl.BlockSpec((S5_SUB * S5_TOK, BATCH, D_MODEL), lambda p: (p, 0, 0)),
                   pl.BlockSpec((S5_GROUPS, S5_SUB * S5_PAIR * BATCH, S5_CW), lambda p: (0, p, 0))],
        compiler_params=_params(),
        name="s5_in_proj",
    )(*[h] * S5_SUB, mods, g, perm, w_in)


OPS_G = 8


def _s5_ops_kernel(*refs):
    for j in range(OPS_G):
        _s5_group_ops(*(r.at[j] for r in refs))


def _s5_group_ops(lr_ref, li_ref, ls_ref, btr_ref, bti_ref, cr_ref, ci_ref, m_ref, q_ref, n_ref, a_ref):
    P2 = 2 * S5_STATE

    def cmul(xr, xi, yr, yi):
        return xr * yr - xi * yi, xr * yi + xi * yr

    lr, li = lr_ref[...], li_ref[...]
    dt = jnp.exp(ls_ref[...])
    mag = jnp.exp(lr * dt)
    ar = mag * jnp.cos(li * dt)
    ai = mag * jnp.sin(li * dt)
    den = lr * lr + li * li
    nr = ar - 1.0
    f_re = (nr * lr + ai * li) / den
    f_im = (ai * lr - nr * li) / den
    bt_re = f_re * btr_ref[...] - f_im * bti_ref[...]
    bt_im = f_re * bti_ref[...] + f_im * btr_ref[...]

    powers = [(jnp.ones_like(ar), jnp.zeros_like(ai))]
    for _ in range(S5_CHUNK):
        powers.append(cmul(*powers[-1], ar, ai))
    row_fwd = lax.broadcasted_iota(jnp.int32, (1, P2), 1) < S5_STATE

    def pow_rows(exp_fwd, exp_bwd):
        return (jnp.where(row_fwd, powers[exp_fwd][0], powers[exp_bwd][0]),
                jnp.where(row_fwd, powers[exp_fwd][1], powers[exp_bwd][1]))

    c_re, c_im = cr_ref[...], ci_ref[...]
    last = S5_CHUNK - 1
    cp_blocks = [cmul(c_re, c_im, *pow_rows(blk, last - blk)) for blk in range(S5_CHUNK)]
    cp_re = jnp.concatenate([b[0] for b in cp_blocks], axis=0)
    cp_im = jnp.concatenate([b[1] for b in cp_blocks], axis=0)
    lane_fwd = lax.broadcasted_iota(jnp.int32, (S5_GROUP, P2), 1) < S5_STATE

    def lag_kernels(keep):
        br = jnp.where(keep, bt_re, 0.0)
        bi = jnp.where(keep, bt_im, 0.0)
        dims = (((1,), (1,)), ((), ()))
        hi = lax.Precision.HIGHEST
        return (lax.dot_general(br, cp_re, dims, precision=hi, preferred_element_type=F32)
                - lax.dot_general(bi, cp_im, dims, precision=hi, preferred_element_type=F32))

    kt_f = lag_kernels(lane_fwd)
    kt_b = lag_kernels(jnp.logical_not(lane_fwd))
    lane_w = lax.broadcasted_iota(jnp.int32, (S5_GROUP, S5_CW), 1)
    for s in range(S5_CHUNK):
        f_part = kt_f if s == 0 else jnp.where(lane_w >= S5_GROUP * s, pltpu.roll(kt_f, S5_GROUP * s, 1), 0.0)
        sh = (S5_GROUP * (s + 1)) % S5_CW
        b_roll = kt_b if sh == 0 else pltpu.roll(kt_b, sh, 1)
        b_part = jnp.where(lane_w < S5_GROUP * (s + 1), b_roll, 0.0)
        m_ref[s * S5_GROUP:(s + 1) * S5_GROUP, :] = (f_part + b_part).astype(BF16)

    for blk in range(S5_CHUNK):
        rows = slice(blk * S5_GROUP, (blk + 1) * S5_GROUP)
        q_re, q_im = cmul(bt_re, bt_im, *pow_rows(last - blk, blk))
        q_ref[rows, 0:P2] = q_re.astype(BF16)
        q_ref[rows, P2:2 * P2] = q_im.astype(BF16)

    for blk in range(S5_CHUNK):
        rows = slice(blk * S5_GROUP, (blk + 1) * S5_GROUP)
        n_re, n_im = cmul(c_re, c_im, *pow_rows(blk + 1, S5_CHUNK - blk))
        n_ref[rows, 0:P2] = n_re.astype(BF16)
        n_ref[rows, P2:2 * P2] = (-n_im).astype(BF16)

    a_ref[0:1, :] = powers[S5_CHUNK][0]
    a_ref[1:2, :] = powers[S5_CHUNK][1]


def _s5_ops(row_params, bt, c_nat):
    P2 = 2 * S5_STATE
    row = pl.BlockSpec((OPS_G, 1, P2), lambda g: (g, 0, 0))
    mat = pl.BlockSpec((OPS_G, S5_GROUP, P2), lambda g: (g, 0, 0))
    sq = pl.BlockSpec((OPS_G, S5_CW, S5_CW), lambda g: (g, 0, 0))
    return pl.pallas_call(
        _s5_ops_kernel,
        out_shape=[jax.ShapeDtypeStruct((S5_GROUPS, S5_CW, S5_CW), BF16)] * 3
        + [jax.ShapeDtypeStruct((S5_GROUPS, 2, P2), F32)],
        grid=(S5_GROUPS // OPS_G,),
        in_specs=[row, row, row, mat, mat, mat, mat],
        out_specs=[sq, sq, sq, pl.BlockSpec((OPS_G, 2, P2), lambda g: (g, 0, 0))],
        compiler_params=_params(),
        name="s5_chunk_operators",
    )(*row_params, *bt, *c_nat)


ROWS = N_CHUNKS * BATCH


SCAN_G = 4


def _s5_scan_kernel(u_ref, m_ref, q_ref, n_ref, a_ref, y_ref, pu_ref, sp_ref):
    S = S5_STATE
    lane = lax.broadcasted_iota(jnp.int32, (BATCH, 2 * S), 1)
    lo = lane < S
    for j in range(SCAN_G):
        pu_ref[j] = _dot(u_ref[j], q_ref[j])
    for j in range(SCAN_G):
        y_ref[j] = _dot(u_ref[j], m_ref[j])
    decay = [(a_ref[j, 0:1, :], a_ref[j, 1:2, :]) for j in range(SCAN_G)]
    zero = jnp.zeros((BATCH, 2 * S), F32)
    state = [(zero, zero)] * SCAN_G
    for k in range(N_CHUNKS):
        cf = k * BATCH
        cb = (N_CTX_CHUNKS - 1 - k if k < N_CTX_CHUNKS else N_CHUNKS + N_CTX_CHUNKS - 1 - k) * BATCH
        for j in range(SCAN_G):
            s_re, s_im = state[j]
            a_re, a_im = decay[j]
            sp_ref[j, cf:cf + BATCH, 0:S] = s_re[:, 0:S]
            sp_ref[j, cb:cb + BATCH, S:2 * S] = s_re[:, S:2 * S]
            sp_ref[j, cf:cf + BATCH, 2 * S:3 * S] = s_im[:, 0:S]
            sp_ref[j, cb:cb + BATCH, 3 * S:4 * S] = s_im[:, S:2 * S]
            x_re = jnp.where(lo, pu_ref[j, cf:cf + BATCH, 0:2 * S], pu_ref[j, cb:cb + BATCH, 0:2 * S])
            x_im = jnp.where(lo, pu_ref[j, cf:cf + BATCH, 2 * S:4 * S], pu_ref[j, cb:cb + BATCH, 2 * S:4 * S])
            state[j] = (a_re * s_re - a_im * s_im + x_re, a_re * s_im + a_im * s_re + x_im)
    for j in range(SCAN_G):
        y_ref[j] += _dot_nt(sp_ref[j].astype(BF16), n_ref[j])


def _s5_scan(u_g, m_op, q_op, n_op, a_vec):
    sq = pl.BlockSpec((SCAN_G, S5_CW, S5_CW), lambda g: (g, 0, 0))
    rows = pl.BlockSpec((SCAN_G, ROWS, S5_CW), lambda g: (g, 0, 0))
    return pl.pallas_call(
        _s5_scan_kernel,
        out_shape=jax.ShapeDtypeStruct((S5_GROUPS, ROWS, S5_CW), F32),
        grid=(S5_GROUPS // SCAN_G,),
        in_specs=[rows, sq, sq, sq, pl.BlockSpec((SCAN_G, 2, 2 * S5_STATE), lambda g: (g, 0, 0))],
        out_specs=rows,
        scratch_shapes=[pltpu.VMEM((SCAN_G, ROWS, S5_CW), F32), pltpu.VMEM((SCAN_G, ROWS, S5_CW), F32)],
        compiler_params=_params(),
        name="s5_scan",
    )(u_g, m_op, q_op, n_op, a_vec)


TAIL_SUB = 4
TAIL_VMEM_LIMIT = 60 * 1024 * 1024


def _s5_tail_kernel(u_ref, y_ref, d_ref, gw_ref, gb_ref, perm_ref, h_ref, mod_ref, g_ref, wo_ref, w1_ref, w2_ref,
                    o_ref):
    lane_blk = _lane_block()
    n_seg = D_MODEL // S5_SEG
    per_seg = S5_SEG // LANES
    mod = mod_ref[...]

    def regroup(j, s):
        cols = []
        for k in range(s * per_seg, (s + 1) * per_seg):
            by_tok = [None] * S5_TOK
            for c2 in range(S5_PAIR):
                rows = slice((j * S5_PAIR + c2) * BATCH, (j * S5_PAIR + c2 + 1) * BATCH)
                for hh in range(S5_CHUNK // STEPS_PER_VREG):
                    groups = [y_ref[k * GROUPS_PER_VREG + r, rows, hh * LANES:(hh + 1) * LANES]
                              for r in range(GROUPS_PER_VREG)]
                    steps = _block_transpose(groups, lane_blk)
                    for m in range(STEPS_PER_VREG):
                        by_tok[c2 * S5_CHUNK + hh * STEPS_PER_VREG + m] = steps[m]
            cols.append(jnp.concatenate(by_tok, axis=0))
        return jnp.concatenate(cols, axis=1)

    def gated_mixer(j, out):
        g_parts, pre = [], None
        ys_next = regroup(j, 0)
        for s in range(n_seg):
            ys = ys_next
            if s + 1 < n_seg:
                ys_next = regroup(j, s + 1)
            cols = slice(s * S5_SEG, (s + 1) * S5_SEG)
            u = u_ref[j * S5_TOK:(j + 1) * S5_TOK, :, cols].reshape(BATCH * S5_TOK, S5_SEG)
            g_seg = jax.nn.gelu(u * d_ref[:, cols] + ys)
            g_parts.append(g_seg)
            part = _dot(g_seg.astype(BF16), gw_ref[cols, :])
            pre = part if pre is None else pre + part
            if s + 1 < n_seg:
                yield
        gate = jax.nn.sigmoid(pre + gb_ref[...])
        gated = (jnp.concatenate(g_parts, axis=1) * gate).astype(BF16)
        out[0] = _dot(perm_ref[...], gated).astype(BF16)
        yield

    def mlp(j, gated):
        tok = slice(j * S5_TOK, (j + 1) * S5_TOK)
        y = _dot(gated, wo_ref[...]).reshape(BATCH, S5_TOK, D_MODEL)
        h1 = h_ref[:, tok, :] + mod[:, 2:3, :] * y
        f = (_rms(h1) * g_ref[...] * (1.0 + mod[:, 4:5, :]) + mod[:, 3:4, :])
        f = f.reshape(BATCH * S5_TOK, D_MODEL).astype(BF16)
        acc = jnp.zeros((BATCH * S5_TOK, D_MODEL), F32)
        for c in range(D_FF // FF_CHUNK):
            hid = jnp.maximum(_dot(f, w1_ref[:, c * FF_CHUNK:(c + 1) * FF_CHUNK]), 0.0)
            acc = acc + _dot((hid * hid).astype(BF16), w2_ref[c * FF_CHUNK:(c + 1) * FF_CHUNK, :])
            if c + 1 < D_FF // FF_CHUNK:
                yield
        o_ref[:, tok, :] = h1 + mod[:, 5:6, :] * acc.reshape(BATCH, S5_TOK, D_MODEL)
        yield

    cur = [None]
    for _ in gated_mixer(0, cur):
        pass
    for j in range(TAIL_SUB):
        nxt = [None]
        streams = [mlp(j, cur[0])] + ([gated_mixer(j + 1, nxt)] if j + 1 < TAIL_SUB else [])
        while streams:
            for gen in list(streams):
                if next(gen, StopIteration) is StopIteration:
                    streams.remove(gen)
        cur = nxt


def _s5_tail(u, y_g, d_skip, glu_w, glu_b, perm_t, h, mods, g, wo, w1_all, w2_all, layer):
    ctx_steps = N_CTX_CHUNKS // (S5_PAIR * TAIL_SUB)
    n_rows = BATCH * S5_TOK
    vec = pl.BlockSpec((1, D_MODEL), lambda p: (0, 0))

    def full(shape):
        return pl.BlockSpec(shape, lambda p: (0,) * len(shape), pipeline_mode=pl.Buffered(1))

    return pl.pallas_call(
        _s5_tail_kernel,
        out_shape=jax.ShapeDtypeStruct((BATCH, SEQ, D_MODEL), F32),
        grid=(S5_STEPS // TAIL_SUB - ctx_steps,),
        in_specs=[pl.BlockSpec((TAIL_SUB * S5_TOK, BATCH, D_MODEL), lambda p: (p + ctx_steps, 0, 0)),
                  pl.BlockSpec((S5_GROUPS, TAIL_SUB * S5_PAIR * BATCH, S5_CW), lambda p: (0, p + ctx_steps, 0)),
                  vec, full((D_MODEL, D_MODEL)), vec, full((n_rows, n_rows)),
                  pl.BlockSpec((BATCH, TAIL_SUB * S5_TOK, D_MODEL), lambda p: (0, p + ctx_steps, 0)),
                  pl.BlockSpec((None, BATCH, N_MOD, D_MODEL), lambda p: (1, 0, 0, 0)),
                  vec, full((D_MODEL, D_MODEL)),
                  pl.BlockSpec((None, D_MODEL, D_FF), lambda p: (layer, 0, 0), pipeline_mode=pl.Buffered(1)),
                  pl.BlockSpec((None, D_FF, D_MODEL), lambda p: (layer, 0, 0), pipeline_mode=pl.Buffered(1))],
        out_specs=pl.BlockSpec((BATCH, TAIL_SUB * S5_TOK, D_MODEL), lambda p: (0, p, 0)),
        compiler_params=pltpu.CompilerParams(vmem_limit_bytes=TAIL_VMEM_LIMIT),
        name="s5_tail_mlp",
    )(u, y_g, d_skip, glu_w, glu_b, perm_t, h, mods, g, wo, w1_all, w2_all)


def _rope_tables():
    rows_n = SEQ // GRID_W
    row = np.repeat(np.arange(rows_n, dtype=np.float64), GRID_W)
    col = np.tile(np.arange(GRID_W, dtype=np.float64), rows_n)
    n_freq = HEAD_DIM // 4
    inv = ROPE_BASE ** (-np.arange(n_freq, dtype=np.float64) / n_freq)
    ang = np.concatenate([row[:, None] * inv, col[:, None] * inv], axis=-1)
    reps = LANES // (HEAD_DIM // 2)
    cos_t = np.tile(np.cos(ang), (1, reps))
    sin_t = np.tile(np.sin(ang), (1, reps))
    sign = np.where((np.arange(LANES) % HEAD_DIM) < HEAD_DIM // 2, -1.0, 1.0)
    cos_t = np.concatenate([np.ones((CTX_LEN, LANES)), cos_t], axis=0)
    sin_s = np.concatenate([np.zeros((CTX_LEN, LANES)), sin_t * sign], axis=0)
    return jnp.asarray(cos_t, F32), jnp.asarray(sin_s, F32)


def _fb_rows(x):
    return jnp.transpose(x, (1, 0, 2)).reshape(S5_GROUPS, 1, 2 * S5_STATE)


def _s5_layout(lam_re, lam_im, log_step, b_re, b_im, c_re, c_im):
    ls = jnp.broadcast_to(log_step[:, :, None], lam_re.shape)
    rows = [_fb_rows(v) for v in (lam_re, lam_im, ls)]

    def bt_of(b):
        return jnp.transpose(b, (1, 3, 0, 2)).reshape(S5_GROUPS, S5_GROUP, 2 * S5_STATE)

    def c_of(c):
        return jnp.transpose(c, (1, 2, 0, 3)).reshape(S5_GROUPS, S5_GROUP, 2 * S5_STATE)

    return rows, [bt_of(b_re), bt_of(b_im)], [c_of(c_re), c_of(c_im)]


def kernel(x, c, ctx, c_ctx, norm1_g, norm2_g, mod_w, mod_b, mlp_w1, mlp_w2, attn_w_in, attn_w_out, a_q_norm, a_k_norm, a_sink, b_q_norm, b_k_norm, b_lq1, b_lk1, b_lq2, b_lk2, b_subln, s5_w_in, s5_lambda_re, s5_lambda_im, s5_log_step, s5_b_re, s5_b_im, s5_c_re, s5_c_im, s5_d, s5_glu_w, s5_glu_b, s5_w_out):
    assert x.shape == (BATCH, SEQ, D_MODEL) and ctx.shape == (BATCH, CTX_LEN, D_MODEL)
    stream = (ctx, x)
    s_rows = jnp.concatenate([c, c_ctx[None], jnp.zeros((MOD_ROWS - BATCH - 1, D_MODEL), F32)], axis=0)
    m_all = _modulation(s_rows, mod_w, mod_b)
    cos_t, sin_s = _rope_tables()
    w1_all, w2_all = mlp_w1.astype(BF16), mlp_w2.astype(BF16)
    e_blk = jnp.asarray(np.kron(np.eye(2 * LANES // HEAD_DIM), np.ones((HEAD_DIM, HEAD_DIM))) / HEAD_DIM, BF16)

    for i in range(DEPTH):
        last = i == DEPTH - 1
        j = i // 2
        m_lat = m_all[i, :BATCH].reshape(BATCH, N_MOD, D_MODEL)
        m_ctx = jnp.broadcast_to(m_all[i, BATCH].reshape(1, N_MOD, D_MODEL), (BATCH, N_MOD, D_MODEL))
        mods = jnp.stack([m_ctx, m_lat])
        g1 = norm1_g[i].reshape(1, D_MODEL)
        g2 = norm2_g[i].reshape(1, D_MODEL)
        if i % 2 == 0:
            lambda_init = 0.8 - 0.6 * math.exp(-0.3 * i)
            gains = jnp.stack([jnp.tile(v[j], LANES // HEAD_DIM) for v in (a_q_norm, a_k_norm, b_q_norm, b_k_norm)])
            qa, k2a, v2a, qb, kb, vb = _attn_in(stream, mods, g1, attn_w_in[j].astype(BF16), gains, cos_t, sin_s, e_blk)
            lpar = jnp.stack([b_lq1[j], b_lk1[j], b_lq2[j], b_lk2[j]])
            ya, yb = _attention(a_sink[j], lpar, b_subln[j].reshape(1, LANES), qa, k2a, v2a, qb, kb, vb, lambda_init)
            if last:
                ya, yb = ya[:, CTX_LEN:], yb[:, CTX_LEN:]
            h = _mix_mlp(stream, ya, yb, mods, g2, attn_w_out[j].astype(BF16), w1_all, w2_all, i, latent_only=last)
        else:
            src = np.arange(BATCH * S5_TOK).reshape(BATCH, S5_TOK).T.reshape(-1)
            perm = jnp.asarray(np.eye(BATCH * S5_TOK, dtype=np.float32)[src], BF16)
            h_all = stream[0] if stream[0].shape[1] == TT else jnp.concatenate(stream, axis=1)
            u, u_g = _s5_in(h_all, mods, g1, perm, s5_w_in[j].astype(BF16))
            ops_in = _s5_layout(s5_lambda_re[j], s5_lambda_im[j], s5_log_step[j], s5_b_re[j], s5_b_im[j],
                                s5_c_re[j], s5_c_im[j])
            m_op, q_op, n_op, a_vec = _s5_ops(*ops_in)
            y_g = _s5_scan(u_g, m_op, q_op, n_op, a_vec)
            assert last, "S5 layers before the last one would also need the context rows of the readout"
            h = _s5_tail(u, y_g, s5_d[j].reshape(1, D_MODEL), s5_glu_w[j].astype(BF16),
                         s5_glu_b[j].reshape(1, D_MODEL), perm.T, h_all, mods, g2, s5_w_out[j].astype(BF16),
                         w1_all, w2_all, i)
        stream = (h, h)
    return h
```

```python
import functools
import math

import jax
import jax.numpy as jnp
import numpy as np
from jax import lax
from jax.experimental import pallas as pl
from jax.experimental.pallas import tpu as pltpu

F32 = jnp.float32
BF16 = jnp.bfloat16

D_MODEL = 1024
BATCH = 8
SEQ = 2048
DEPTH = 2
GRID_W = 64
CTX_LEN = 256
HEAD_DIM = 64
WINDOW = 128
A_Q_HEADS = 8
A_KV_HEADS = 2
B_HEADS = 4
A_Q_W = A_Q_HEADS * HEAD_DIM
A_KV_W = A_KV_HEADS * HEAD_DIM
B_QK_W = B_HEADS * 2 * HEAD_DIM
B_V_W = B_HEADS * 2 * HEAD_DIM
ATTN_IN = A_Q_W + 2 * A_KV_W + 2 * B_QK_W + B_V_W
S5_GROUP = 16
S5_GROUPS = D_MODEL // S5_GROUP
S5_STATE = 64
D_FF = 4 * D_MODEL
ROPE_BASE = 10000.0
EPS = 1e-6
NEG_INF = -1e30
LOG2E = math.log2(math.e)

TT = CTX_LEN + SEQ
TM = 256
N_TILES = TT // TM
LANES = 128
S5_CHUNK = 16
S5_CW = S5_CHUNK * S5_GROUP
N_CHUNKS = TT // S5_CHUNK
N_CTX_CHUNKS = CTX_LEN // S5_CHUNK
VMEM_LIMIT = 56 * 1024 * 1024
N_MOD = 6
MOD_ROWS = 16
MOD_COLS = 2 * D_MODEL
GQA = A_Q_HEADS // A_KV_HEADS


def _dot(a, b):
    return jnp.dot(a, b, preferred_element_type=F32)


def _dot_nt(a, b):
    return lax.dot_general(a, b, (((1,), (1,)), ((), ())), preferred_element_type=F32)


def _rms(x):
    return x * lax.rsqrt(jnp.mean(x * x, axis=-1, keepdims=True) + EPS)


def _modnorm(x, g, shift, scale):
    return _rms(x) * g * (1.0 + scale) + shift


def _params(**kw):
    return pltpu.CompilerParams(vmem_limit_bytes=VMEM_LIMIT, **kw)


def _mod_kernel(s_ref, w_ref, b_ref, o_ref):
    s = s_ref[...]
    s = s * jax.nn.sigmoid(s)
    o_ref[...] = _dot(s.astype(BF16), w_ref[...].astype(BF16)) + b_ref[...]


def _modulation(s_rows, mod_w, mod_b):
    return pl.pallas_call(
        _mod_kernel,
        out_shape=jax.ShapeDtypeStruct((DEPTH, MOD_ROWS, N_MOD * D_MODEL), F32),
        grid=(DEPTH, N_MOD * D_MODEL // MOD_COLS),
        in_specs=[
            pl.BlockSpec((MOD_ROWS, D_MODEL), lambda i, j: (0, 0)),
            pl.BlockSpec((None, D_MODEL, MOD_COLS), lambda i, j: (i, 0, j)),
            pl.BlockSpec((None, 1, MOD_COLS), lambda i, j: (i, 0, j)),
        ],
        out_specs=pl.BlockSpec((None, MOD_ROWS, MOD_COLS), lambda i, j: (i, 0, j)),
        compiler_params=_params(),
        name="modulation",
    )(s_rows, mod_w, mod_b.reshape(DEPTH, 1, N_MOD * D_MODEL))


def _stream_specs(stream, n_sub, h_off):
    first_lat = 1 if stream[0].shape[1] == TT else 0

    def tile(t, k):
        return t * n_sub + k + h_off

    ctx_spec = pl.BlockSpec((None, TM, D_MODEL), lambda b, t: (b, 0, 0))
    lat_specs = [pl.BlockSpec((None, TM, D_MODEL),
                              lambda b, t, k=k: (b, jnp.maximum(tile(t, k) - 1 + first_lat, first_lat), 0))
                 for k in range(n_sub)]
    mod_specs = [pl.BlockSpec((None, None, N_MOD, D_MODEL), lambda b, t, k=k: (jnp.minimum(tile(t, k), 1), b, 0, 0))
                 for k in range(n_sub)]
    return ctx_spec, lat_specs, mod_specs


ATTN_IN_SUB = 3


def _attn_in_kernel(*refs):
    n_sub = ATTN_IN_SUB
    hc_ref = refs[0]
    hl_refs = refs[1:1 + n_sub]
    mod_refs = refs[1 + n_sub:1 + 2 * n_sub]
    (g_ref, w_ref, gain_ref, cos_ref, sin_ref, e_ref,
     qa_ref, k2a_ref, v2a_ref, qb_ref, kb_ref, vb_ref) = refs[1 + 2 * n_sub:]
    e = e_ref[...]
    gains = gain_ref[...]
    lane = lax.broadcasted_iota(jnp.int32, (TM, LANES), 1)
    first_half = (lane & (HEAD_DIM - 1)) < HEAD_DIM // 2
    lo = lane < HEAD_DIM
    q_scale = HEAD_DIM ** -0.5 * LOG2E
    ones = jnp.ones((TM, LANES), BF16)

    def tile_segments(k):
        rows = slice(k * TM, (k + 1) * TM)
        cos_t = cos_ref[rows, :]
        sin_s = sin_ref[rows, :]

        def head_mean_sq(z):
            sq = (z * z).astype(BF16)
            width = z.shape[1]
            if width < 2 * LANES:
                return _dot(sq, e[0:width, 0:width])
            return jnp.concatenate([_dot(sq[:, c:c + 2 * LANES], e) for c in range(0, width, 2 * LANES)], axis=1)

        def norm_rope_chunks(z, gain):
            ms = head_mean_sq(z)
            for c in range(z.shape[1] // LANES):
                cols = slice(c * LANES, (c + 1) * LANES)
                cn = z[:, cols] * lax.rsqrt(ms[:, cols] + EPS) * gain
                r_fwd = pltpu.roll(cn, HEAD_DIM // 2, 1)
                r_bwd = pltpu.roll(cn, LANES - HEAD_DIM // 2, 1)
                yield cn * cos_t + jnp.where(first_half, r_bwd, r_fwd) * sin_s

        def dup_halves(x):
            sw = pltpu.roll(x, HEAD_DIM, 1)
            return jnp.where(lo, x, sw).astype(BF16), jnp.where(lo, sw, x).astype(BF16)

        def finish_q(z, ref, gain):
            for c, chunk in enumerate(norm_rope_chunks(z, gain)):
                ref[rows, c * LANES:(c + 1) * LANES] = (chunk * q_scale).astype(BF16)

        def finish_kv_a(z):
            (k_roped,) = norm_rope_chunks(z[:, 0:LANES], gains[1:2])
            for kvh, dup in enumerate(dup_halves(k_roped)):
                k2a_ref[rows, kvh * LANES:(kvh + 1) * LANES] = dup
            for kvh, dup in enumerate(dup_halves(z[:, LANES:2 * LANES])):
                v2a_ref[rows, 2 * kvh * LANES:(2 * kvh + 1) * LANES] = dup
                v2a_ref[rows, (2 * kvh + 1) * LANES:(2 * kvh + 2) * LANES] = ones

        def finish_kb(z):
            for c, chunk in enumerate(norm_rope_chunks(z, gains[3:4])):
                kb_ref[rows, c * LANES:(c + 1) * LANES] = chunk.astype(BF16)

        def finish_vb(z):
            for hd in range(B_HEADS):
                vb_ref[rows, 2 * hd * LANES:(2 * hd + 1) * LANES] = z[:, hd * LANES:(hd + 1) * LANES].astype(BF16)
                vb_ref[rows, (2 * hd + 1) * LANES:(2 * hd + 2) * LANES] = ones

        return [(A_Q_W, lambda z: finish_q(z, qa_ref, gains[0:1])), (2 * A_KV_W, finish_kv_a),
                (B_QK_W, lambda z: finish_q(z, qb_ref, gains[2:3])), (B_QK_W, finish_kb), (B_V_W, finish_vb)]

    def normed(k):
        mod = mod_refs[k][...]
        x = hl_refs[k][...]
        if k == 0:
            x = jnp.where(pl.program_id(1) == 0, hc_ref[...], x)
        return _modnorm(x, g_ref[...], mod[0:1], mod[1:2]).astype(BF16)

    pending = None
    for k in range(n_sub):
        a = normed(k)
        off = 0
        for width, finish in tile_segments(k):
            z = _dot(a, w_ref[:, off:off + width])
            off += width
            if pending is not None:
                pending[1](pending[0])
            pending = (z, finish)
    pending[1](pending[0])


def _attn_in(stream, mods, g, w_in, gains, cos_t, sin_s, e_blk):
    n_sub = ATTN_IN_SUB
    ctx_spec, lat_specs, mod_specs = _stream_specs(stream, n_sub, 0)

    def tok(width):
        return pl.BlockSpec((None, n_sub * TM, width), lambda b, t: (b, t, 0))

    def full(shape):
        return pl.BlockSpec(shape, lambda b, t: (0,) * len(shape))

    out_shapes = [jax.ShapeDtypeStruct((BATCH, TT, w), BF16)
                  for w in (A_Q_W, 2 * A_KV_W, 4 * A_KV_W, B_QK_W, B_QK_W, 2 * B_V_W)]
    return pl.pallas_call(
        _attn_in_kernel,
        out_shape=out_shapes,
        grid=(BATCH, N_TILES // n_sub),
        in_specs=[
            ctx_spec, *lat_specs, *mod_specs,
            full((1, D_MODEL)), full((D_MODEL, ATTN_IN)), full((4, LANES)),
            pl.BlockSpec((n_sub * TM, LANES), lambda b, t: (t, 0)),
            pl.BlockSpec((n_sub * TM, LANES), lambda b, t: (t, 0)),
            full((2 * LANES, 2 * LANES)),
        ],
        out_specs=[tok(A_Q_W), tok(2 * A_KV_W), tok(4 * A_KV_W), tok(B_QK_W), tok(B_QK_W), tok(2 * B_V_W)],
        compiler_params=_params(),
        name="attn_in_proj",
    )(stream[0], *[stream[1]] * n_sub, *[mods] * n_sub, g, w_in, gains, cos_t, sin_s, e_blk)


QB = 128


def _run_pipelined(items):
    s_next = items[0][0]()
    for i, (_, finish) in enumerate(items):
        s_cur = s_next
        if i + 1 < len(items):
            s_next = items[i + 1][0]()
        finish(s_cur)


def _win_attn_items(t, sink_ref, q_ref, k2_ref, v2_ref, o_ref, with_window):
    lane = lax.broadcasted_iota(jnp.int32, (QB, LANES), 1)
    lo = lane < HEAD_DIM
    rows = GQA * QB
    row = lax.broadcasted_iota(jnp.int32, (rows, 3 * QB), 0)
    col = lax.broadcasted_iota(jnp.int32, (rows, 3 * QB), 1)
    row_head = lax.broadcasted_iota(jnp.int32, (rows, 1), 0) // QB
    zero = jnp.zeros((QB, LANES), BF16)
    blocks = [(qb, g) for qb in range(TM // QB) for g in range(A_KV_HEADS)]

    def window_start(qb):
        n = (t - 1) * (TM // QB) + qb
        ws = jnp.clip((n - 1) * QB, 0, SEQ - 3 * QB)
        return n, ws

    def scores(qb, g):
        pieces = []
        for p in range(2):
            qp = q_ref[qb * QB:(qb + 1) * QB, g * 2 * LANES + p * LANES: g * 2 * LANES + (p + 1) * LANES]
            pieces.append(jnp.where(lo, qp, zero))
            pieces.append(jnp.where(lo, zero, qp))
        qs = jnp.concatenate(pieces, axis=0)
        s_c = _dot_nt(qs, k2_ref[0:CTX_LEN, g * LANES:(g + 1) * LANES])
        if not with_window:
            return s_c, None
        n, ws = window_start(qb)
        kw = k2_ref[pl.ds(pl.multiple_of(ws + CTX_LEN, QB), 3 * QB), g * LANES:(g + 1) * LANES]
        valid = jnp.abs(n * QB + (row & (QB - 1)) - (ws + col)) <= WINDOW
        return s_c, jnp.where(valid, _dot_nt(qs, kw), NEG_INF)

    def finish(qb, g, s):
        s_c, s_w = s
        sk = jnp.full((rows, 1), sink_ref[GQA * g + GQA - 1], F32)
        for hh in range(GQA - 1):
            sk = jnp.where(row_head == hh, sink_ref[GQA * g + hh], sk)
        sk = sk * LOG2E
        m = jnp.maximum(jnp.max(s_c, axis=-1, keepdims=True), sk)
        if s_w is not None:
            m = jnp.maximum(m, jnp.max(s_w, axis=-1, keepdims=True))
        vcols = slice(2 * g * LANES, (2 * g + 2) * LANES)
        pv = _dot(jnp.exp2(s_c - m).astype(BF16), v2_ref[0:CTX_LEN, vcols])
        if s_w is not None:
            _, ws = window_start(qb)
            vw = v2_ref[pl.ds(pl.multiple_of(ws + CTX_LEN, QB), 3 * QB), vcols]
            pv = pv + _dot(jnp.exp2(s_w - m).astype(BF16), vw)
        o = pv[:, 0:LANES] / (pv[:, LANES:2 * LANES] + jnp.exp2(sk - m))
        for p in range(2):
            o_ref[qb * QB:(qb + 1) * QB, g * 2 * LANES + p * LANES: g * 2 * LANES + (p + 1) * LANES] = jnp.where(
                lo, o[2 * p * QB:(2 * p + 1) * QB], o[(2 * p + 1) * QB:(2 * p + 2) * QB]).astype(BF16)

    return [(functools.partial(scores, qb, g), functools.partial(finish, qb, g)) for qb, g in blocks]


DIFF_ROWS = 256


def _diff_attn_items(lam, subln_ref, q_ref, k_ref, v_ref, o_ref, n_keys, lambda_init):
    R = DIFF_ROWS
    lane = lax.broadcasted_iota(jnp.int32, (R, LANES), 1)
    lo = lane < HEAD_DIM
    zero = jnp.zeros((R, LANES), BF16)
    blocks = [(slice(rb * R, (rb + 1) * R), slice(h * LANES, (h + 1) * LANES))
              for h in range(B_HEADS) for rb in range(TM // R)]

    def scores(rows, cols):
        q = q_ref[rows, cols]
        qs = jnp.concatenate([jnp.where(lo, q, zero), jnp.where(lo, zero, q)], axis=0)
        return _dot_nt(qs, k_ref[0:n_keys, cols])

    def finish(rows, cols, s):
        p = jnp.exp2(s - jnp.max(s, axis=-1, keepdims=True)).astype(BF16)
        vcols = slice(2 * cols.start, 2 * cols.stop)
        pv = _dot(p, v_ref[0:n_keys, vcols])
        sm = pv[:, 0:LANES] / pv[:, LANES:2 * LANES]
        y = sm[0:R] - lam * sm[R:2 * R]
        o_ref[rows, cols] = (_rms(y) * subln_ref[...] * (1.0 - lambda_init)).astype(BF16)

    return [(functools.partial(scores, *blk), functools.partial(finish, *blk)) for blk in blocks]


def _attention_kernel(sink_ref, lpar_ref, subln_ref, qa_ref, k2a_ref, v2a_ref, qb_ref, kb_ref, vb_ref,
                      ya_ref, yb_ref, *, lambda_init):
    t = pl.program_id(1)
    lp = lpar_ref[...]
    lam = (jnp.exp(jnp.sum(lp[0:1] * lp[1:2], axis=-1, keepdims=True))
           - jnp.exp(jnp.sum(lp[2:3] * lp[3:4], axis=-1, keepdims=True)) + lambda_init)

    def attend(is_ctx):
        win = _win_attn_items(t, sink_ref, qa_ref, k2a_ref, v2a_ref, ya_ref, not is_ctx)
        dif = _diff_attn_items(lam, subln_ref, qb_ref, kb_ref, vb_ref, yb_ref, CTX_LEN if is_ctx else TT,
                               lambda_init)
        assert len(win) == len(dif)
        _run_pipelined([item for pair in zip(dif, win) for item in pair])

    @pl.when(t == 0)
    def _():
        attend(True)

    @pl.when(t > 0)
    def _():
        attend(False)


def _attention(sink, lpar, subln, qa, k2a, v2a, qb, kb, vb, lambda_init):
    def tile(width):
        return pl.BlockSpec((None, TM, width), lambda b, t: (b, t, 0))

    def keys(width):
        return pl.BlockSpec((None, TT, width), lambda b, t: (b, 0, 0))

    return pl.pallas_call(
        functools.partial(_attention_kernel, lambda_init=lambda_init),
        out_shape=[jax.ShapeDtypeStruct((BATCH, TT, A_Q_W), BF16), jax.ShapeDtypeStruct((BATCH, TT, B_V_W), BF16)],
        grid=(BATCH, N_TILES),
        in_specs=[
            pl.BlockSpec(memory_space=pltpu.SMEM),
            pl.BlockSpec((4, HEAD_DIM), lambda b, t: (0, 0)),
            pl.BlockSpec((1, LANES), lambda b, t: (0, 0)),
            tile(A_Q_W), keys(2 * A_KV_W), keys(4 * A_KV_W),
            tile(B_QK_W), keys(B_QK_W), keys(2 * B_V_W),
        ],
        out_specs=[tile(A_Q_W), tile(B_V_W)],
        compiler_params=_params(),
        name="attention",
    )(sink, lpar, subln, qa, k2a, v2a, qb, kb, vb)


FF_CHUNK = 1024


def _mix_mlp_kernel(*refs, h_off, n_sub):
    hc_ref = refs[0]
    hl_refs, ua_refs, ub_refs, mod_refs = (refs[1 + i * n_sub:1 + (i + 1) * n_sub] for i in range(4))
    g_ref, wo_ref, w1_ref, w2_ref, o_ref = refs[1 + 4 * n_sub:]
    half = D_MODEL // 2

    def prologue(k):
        mod = mod_refs[k][...]
        y = _dot(ua_refs[k][...], wo_ref[0:half, :]) + _dot(ub_refs[k][...], wo_ref[half:D_MODEL, :])
        x = hl_refs[k][...]
        if h_off == 0 and k == 0:
            x = jnp.where(pl.program_id(1) == 0, hc_ref[...], x)
        h1 = x + mod[2:3] * y
        f = _modnorm(h1, g_ref[...], mod[3:4], mod[4:5]).astype(BF16)
        return h1, f, mod[5:6]

    def mlp(k, h1, f, gate):
        acc = jnp.zeros((TM, D_MODEL), F32)
        for c in range(D_FF // FF_CHUNK):
            hid = jnp.maximum(_dot(f, w1_ref[:, c * FF_CHUNK:(c + 1) * FF_CHUNK]), 0.0)
            acc = acc + _dot((hid * hid).astype(BF16), w2_ref[c * FF_CHUNK:(c + 1) * FF_CHUNK, :])
        o_ref[k * TM:(k + 1) * TM, :] = h1 + gate * acc

    nxt = prologue(0)
    for k in range(n_sub):
        cur = nxt
        if k + 1 < n_sub:
            nxt = prologue(k + 1)
        mlp(k, *cur)


def _mix_mlp(stream, ua, ub, mods, g, wo, w1_all, w2_all, layer, *, latent_only):
    n_tiles = SEQ // TM if latent_only else N_TILES
    h_off = N_TILES - n_tiles
    n_sub = 4 if n_tiles % 4 == 0 else 3
    half = D_MODEL // 2
    ub_col = 1 if ub.shape[-1] == D_MODEL else 0
    ctx_spec, lat_specs, mod_specs = _stream_specs(stream, n_sub, h_off)

    def full(shape):
        return pl.BlockSpec(shape, lambda b, t: (0,) * len(shape), pipeline_mode=pl.Buffered(1))

    def mixer_specs(col):
        return [pl.BlockSpec((None, TM, half), lambda b, t, k=k: (b, t * n_sub + k, col)) for k in range(n_sub)]

    return pl.pallas_call(
        functools.partial(_mix_mlp_kernel, h_off=h_off, n_sub=n_sub),
        out_shape=jax.ShapeDtypeStruct((BATCH, n_tiles * TM, D_MODEL), F32),
        grid=(BATCH, n_tiles // n_sub),
        in_specs=[
            ctx_spec, *lat_specs, *mixer_specs(0), *mixer_specs(ub_col), *mod_specs,
            pl.BlockSpec((1, D_MODEL), lambda b, t: (0, 0)),
            full((D_MODEL, D_MODEL)),
            pl.BlockSpec((None, D_MODEL, D_FF), lambda b, t: (layer, 0, 0), pipeline_mode=pl.Buffered(1)),
            pl.BlockSpec((None, D_FF, D_MODEL), lambda b, t: (layer, 0, 0), pipeline_mode=pl.Buffered(1)),
        ],
        out_specs=pl.BlockSpec((None, n_sub * TM, D_MODEL), lambda b, t: (b, t, 0)),
        compiler_params=_params(),
        name="mixer_out_mlp",
    )(stream[0], *[stream[1]] * n_sub, *[ua] * n_sub, *[ub] * n_sub, *[mods] * n_sub, g, wo, w1_all, w2_all)


S5_PAIR = 2
S5_TOK = S5_PAIR * S5_CHUNK
S5_STEPS = N_CHUNKS // S5_PAIR
GROUPS_PER_VREG = LANES // S5_GROUP
STEPS_PER_VREG = LANES // S5_GROUP
S5_SEG = 2 * LANES


def _lane_block():
    return lax.broadcasted_iota(jnp.int32, (BATCH, LANES), 1) // S5_GROUP


def _block_transpose(xs, lane_blk):
    xs = list(xs)
    n = len(xs)
    d = n // 2
    while d:
        low = (lane_blk & d) == 0
        for i in range(n):
            if i & d:
                continue
            a, b = xs[i], xs[i + d]
            xs[i] = jnp.where(low, a, pltpu.roll(b, S5_GROUP * d, 1))
            xs[i + d] = jnp.where(low, pltpu.roll(a, LANES - S5_GROUP * d, 1), b)
        d //= 2
    return xs


S5_SUB = 8


def _s5_in_kernel(*refs):
    h_refs = refs[:S5_SUB]
    mod_ref, g_ref, perm_ref, w_ref, u_ref, z_ref = refs[S5_SUB:]
    mod = mod_ref[...]
    lane_blk = _lane_block()

    def normed(j):
        x = h_refs[j][...]
        a = _rms(x) * g_ref[...] * (1.0 + mod[:, 1:2, :]) + mod[:, 0:1, :]
        a = a.reshape(BATCH * S5_TOK, D_MODEL).astype(BF16)
        return _dot(perm_ref[...], a).astype(BF16)

    def finish(j, s, u_seg):
        u_ref[j * S5_TOK:(j + 1) * S5_TOK, :, s * S5_SEG:(s + 1) * S5_SEG] = u_seg.reshape(S5_TOK, BATCH, S5_SEG)
        for kk in range(S5_SEG // LANES):
            k = s * (S5_SEG // LANES) + kk
            u_col = u_seg[:, kk * LANES:(kk + 1) * LANES]
            for hh in range(S5_CHUNK // STEPS_PER_VREG):
                halves = []
                for c2 in range(S5_PAIR):
                    tok0 = c2 * S5_CHUNK + hh * STEPS_PER_VREG
                    steps = [u_col[(tok0 + m) * BATCH:(tok0 + m + 1) * BATCH, :] for m in range(STEPS_PER_VREG)]
                    halves.append(_block_transpose(steps, lane_blk))
                rows = slice(j * S5_PAIR * BATCH, (j + 1) * S5_PAIR * BATCH)
                for r in range(GROUPS_PER_VREG):
                    z_ref[k * GROUPS_PER_VREG + r, rows, hh * LANES:(hh + 1) * LANES] = (
                        jnp.concatenate([h[r] for h in halves], axis=0).astype(BF16))

    pending = None
    for j in range(S5_SUB):
        a = normed(j)
        for s in range(D_MODEL // S5_SEG):
            u_seg = _dot(a, w_ref[:, s * S5_SEG:(s + 1) * S5_SEG])
            if pending is not None:
                finish(*pending)
            pending = (j, s, u_seg)
    finish(*pending)


def _s5_in(h, mods, g, perm, w_in):
    ctx_steps = N_CTX_CHUNKS // (S5_PAIR * S5_SUB)
    n_rows = BATCH * S5_TOK
    return pl.pallas_call(
        _s5_in_kernel,
        out_shape=[jax.ShapeDtypeStruct((TT, BATCH, D_MODEL), F32),
                   jax.ShapeDtypeStruct((S5_GROUPS, ROWS, S5_CW), BF16)],
        grid=(S5_STEPS // S5_SUB,),
        in_specs=[*[pl.BlockSpec((BATCH, S5_TOK, D_MODEL), lambda p, j=j: (0, p * S5_SUB + j, 0))
                    for j in range(S5_SUB)],
                  pl.BlockSpec((None, BATCH, N_MOD, D_MODEL), lambda p: (jnp.minimum(p // ctx_steps, 1), 0, 0, 0)),
                  pl.BlockSpec((1, D_MODEL), lambda p: (0, 0)),
                  pl.BlockSpec((n_rows, n_rows), lambda p: (0, 0)),
                  pl.BlockSpec((D_MODEL, D_MODEL), lambda p: (0, 0))],
        out_specs=[pl.BlockSpec((S5_SUB * S5_TOK, BATCH, D_MODEL), lambda p: (p, 0, 0)),
                   pl.BlockSpec((S5_GROUPS, S5_SUB * S5_PAIR * BATCH, S5_CW), lambda p: (0, p, 0))],
        compiler_params=_params(),
        name="s5_in_proj",
    )(*[h] * S5_SUB, mods, g, perm, w_in)


OPS_G = 8


def _s5_ops_kernel(*refs):
    for j in range(OPS_G):
        _s5_group_ops(*(r.at[j] for r in refs))


def _s5_group_ops(lr_ref, li_ref, ls_ref, btr_ref, bti_ref, cr_ref, ci_ref, m_ref, q_ref, n_ref, a_ref):
    P2 = 2 * S5_STATE

    def cmul(xr, xi, yr, yi):
        return xr * yr - xi * yi, xr * yi + xi * yr

    lr, li = lr_ref[...], li_ref[...]
    dt = jnp.exp(ls_ref[...])
    mag = jnp.exp(lr * dt)
    ar = mag * jnp.cos(li * dt)
    ai = mag * jnp.sin(li * dt)
    den = lr * lr + li * li
    nr = ar - 1.0
    f_re = (nr * lr + ai * li) / den
    f_im = (ai * lr - nr * li) / den
    bt_re = f_re * btr_ref[...] - f_im * bti_ref[...]
    bt_im = f_re * bti_ref[...] + f_im * btr_ref[...]

    powers = [(jnp.ones_like(ar), jnp.zeros_like(ai))]
    for _ in range(S5_CHUNK):
        powers.append(cmul(*powers[-1], ar, ai))
    row_fwd = lax.broadcasted_iota(jnp.int32, (1, P2), 1) < S5_STATE

    def pow_rows(exp_fwd, exp_bwd):
        return (jnp.where(row_fwd, powers[exp_fwd][0], powers[exp_bwd][0]),
                jnp.where(row_fwd, powers[exp_fwd][1], powers[exp_bwd][1]))

    c_re, c_im = cr_ref[...], ci_ref[...]
    last = S5_CHUNK - 1
    cp_blocks = [cmul(c_re, c_im, *pow_rows(blk, last - blk)) for blk in range(S5_CHUNK)]
    cp_re = jnp.concatenate([b[0] for b in cp_blocks], axis=0)
    cp_im = jnp.concatenate([b[1] for b in cp_blocks], axis=0)
    lane_fwd = lax.broadcasted_iota(jnp.int32, (S5_GROUP, P2), 1) < S5_STATE

    def lag_kernels(keep):
        br = jnp.where(keep, bt_re, 0.0)
        bi = jnp.where(keep, bt_im, 0.0)
        dims = (((1,), (1,)), ((), ()))
        hi = lax.Precision.HIGHEST
        return (lax.dot_general(br, cp_re, dims, precision=hi, preferred_element_type=F32)
                - lax.dot_general(bi, cp_im, dims, precision=hi, preferred_element_type=F32))

    kt_f = lag_kernels(lane_fwd)
    kt_b = lag_kernels(jnp.logical_not(lane_fwd))
    lane_w = lax.broadcasted_iota(jnp.int32, (S5_GROUP, S5_CW), 1)
    for s in range(S5_CHUNK):
        f_part = kt_f if s == 0 else jnp.where(lane_w >= S5_GROUP * s, pltpu.roll(kt_f, S5_GROUP * s, 1), 0.0)
        sh = (S5_GROUP * (s + 1)) % S5_CW
        b_roll = kt_b if sh == 0 else pltpu.roll(kt_b, sh, 1)
        b_part = jnp.where(lane_w < S5_GROUP * (s + 1), b_roll, 0.0)
        m_ref[s * S5_GROUP:(s + 1) * S5_GROUP, :] = (f_part + b_part).astype(BF16)

    for blk in range(S5_CHUNK):
        rows = slice(blk * S5_GROUP, (blk + 1) * S5_GROUP)
        q_re, q_im = cmul(bt_re, bt_im, *pow_rows(last - blk, blk))
        q_ref[rows, 0:P2] = q_re.astype(BF16)
        q_ref[rows, P2:2 * P2] = q_im.astype(BF16)

    for blk in range(S5_CHUNK):
        rows = slice(blk * S5_GROUP, (blk + 1) * S5_GROUP)
        n_re, n_im = cmul(c_re, c_im, *pow_rows(blk + 1, S5_CHUNK - blk))
        n_ref[rows, 0:P2] = n_re.astype(BF16)
        n_ref[rows, P2:2 * P2] = (-n_im).astype(BF16)

    a_ref[0:1, :] = powers[S5_CHUNK][0]
    a_ref[1:2, :] = powers[S5_CHUNK][1]


def _s5_ops(row_params, bt, c_nat):
    P2 = 2 * S5_STATE
    row = pl.BlockSpec((OPS_G, 1, P2), lambda g: (g, 0, 0))
    mat = pl.BlockSpec((OPS_G, S5_GROUP, P2), lambda g: (g, 0, 0))
    sq = pl.BlockSpec((OPS_G, S5_CW, S5_CW), lambda g: (g, 0, 0))
    return pl.pallas_call(
        _s5_ops_kernel,
        out_shape=[jax.ShapeDtypeStruct((S5_GROUPS, S5_CW, S5_CW), BF16)] * 3
        + [jax.ShapeDtypeStruct((S5_GROUPS, 2, P2), F32)],
        grid=(S5_GROUPS // OPS_G,),
        in_specs=[row, row, row, mat, mat, mat, mat],
        out_specs=[sq, sq, sq, pl.BlockSpec((OPS_G, 2, P2), lambda g: (g, 0, 0))],
        compiler_params=_params(),
        name="s5_chunk_operators",
    )(*row_params, *bt, *c_nat)


ROWS = N_CHUNKS * BATCH


SCAN_G = 4


def _s5_scan_kernel(u_ref, m_ref, q_ref, n_ref, a_ref, y_ref, pu_ref, sp_ref):
    S = S5_STATE
    lane = lax.broadcasted_iota(jnp.int32, (BATCH, 2 * S), 1)
    lo = lane < S
    for j in range(SCAN_G):
        pu_ref[j] = _dot(u_ref[j], q_ref[j])
    for j in range(SCAN_G):
        y_ref[j] = _dot(u_ref[j], m_ref[j])
    decay = [(a_ref[j, 0:1, :], a_ref[j, 1:2, :]) for j in range(SCAN_G)]
    zero = jnp.zeros((BATCH, 2 * S), F32)
    state = [(zero, zero)] * SCAN_G
    for k in range(N_CHUNKS):
        cf = k * BATCH
        cb = (N_CTX_CHUNKS - 1 - k if k < N_CTX_CHUNKS else N_CHUNKS + N_CTX_CHUNKS - 1 - k) * BATCH
        for j in range(SCAN_G):
            s_re, s_im = state[j]
            a_re, a_im = decay[j]
            sp_ref[j, cf:cf + BATCH, 0:S] = s_re[:, 0:S]
            sp_ref[j, cb:cb + BATCH, S:2 * S] = s_re[:, S:2 * S]
            sp_ref[j, cf:cf + BATCH, 2 * S:3 * S] = s_im[:, 0:S]
            sp_ref[j, cb:cb + BATCH, 3 * S:4 * S] = s_im[:, S:2 * S]
            x_re = jnp.where(lo, pu_ref[j, cf:cf + BATCH, 0:2 * S], pu_ref[j, cb:cb + BATCH, 0:2 * S])
            x_im = jnp.where(lo, pu_ref[j, cf:cf + BATCH, 2 * S:4 * S], pu_ref[j, cb:cb + BATCH, 2 * S:4 * S])
            state[j] = (a_re * s_re - a_im * s_im + x_re, a_re * s_im + a_im * s_re + x_im)
    for j in range(SCAN_G):
        y_ref[j] += _dot_nt(sp_ref[j].astype(BF16), n_ref[j])


def _s5_scan(u_g, m_op, q_op, n_op, a_vec):
    sq = pl.BlockSpec((SCAN_G, S5_CW, S5_CW), lambda g: (g, 0, 0))
    rows = pl.BlockSpec((SCAN_G, ROWS, S5_CW), lambda g: (g, 0, 0))
    return pl.pallas_call(
        _s5_scan_kernel,
        out_shape=jax.ShapeDtypeStruct((S5_GROUPS, ROWS, S5_CW), F32),
        grid=(S5_GROUPS // SCAN_G,),
        in_specs=[rows, sq, sq, sq, pl.BlockSpec((SCAN_G, 2, 2 * S5_STATE), lambda g: (g, 0, 0))],
        out_specs=rows,
        scratch_shapes=[pltpu.VMEM((SCAN_G, ROWS, S5_CW), F32), pltpu.VMEM((SCAN_G, ROWS, S5_CW), F32)],
        compiler_params=_params(),
        name="s5_scan",
    )(u_g, m_op, q_op, n_op, a_vec)


TAIL_SUB = 4
TAIL_VMEM_LIMIT = 60 * 1024 * 1024


def _s5_tail_kernel(u_ref, y_ref, d_ref, gw_ref, gb_ref, perm_ref, h_ref, mod_ref, g_ref, wo_ref, w1_ref, w2_ref,
                    o_ref):
    lane_blk = _lane_block()
    n_seg = D_MODEL // S5_SEG
    per_seg = S5_SEG // LANES
    mod = mod_ref[...]

    def regroup(j, s):
        cols = []
        for k in range(s * per_seg, (s + 1) * per_seg):
            by_tok = [None] * S5_TOK
            for c2 in range(S5_PAIR):
                rows = slice((j * S5_PAIR + c2) * BATCH, (j * S5_PAIR + c2 + 1) * BATCH)
                for hh in range(S5_CHUNK // STEPS_PER_VREG):
                    groups = [y_ref[k * GROUPS_PER_VREG + r, rows, hh * LANES:(hh + 1) * LANES]
                              for r in range(GROUPS_PER_VREG)]
                    steps = _block_transpose(groups, lane_blk)
                    for m in range(STEPS_PER_VREG):
                        by_tok[c2 * S5_CHUNK + hh * STEPS_PER_VREG + m] = steps[m]
            cols.append(jnp.concatenate(by_tok, axis=0))
        return jnp.concatenate(cols, axis=1)

    def gated_mixer(j, out):
        g_parts, pre = [], None
        ys_next = regroup(j, 0)
        for s in range(n_seg):
            ys = ys_next
            if s + 1 < n_seg:
                ys_next = regroup(j, s + 1)
            cols = slice(s * S5_SEG, (s + 1) * S5_SEG)
            u = u_ref[j * S5_TOK:(j + 1) * S5_TOK, :, cols].reshape(BATCH * S5_TOK, S5_SEG)
            g_seg = jax.nn.gelu(u * d_ref[:, cols] + ys)
            g_parts.append(g_seg)
            part = _dot(g_seg.astype(BF16), gw_ref[cols, :])
            pre = part if pre is None else pre + part
            if s + 1 < n_seg:
                yield
        gate = jax.nn.sigmoid(pre + gb_ref[...])
        gated = (jnp.concatenate(g_parts, axis=1) * gate).astype(BF16)
        out[0] = _dot(perm_ref[...], gated).astype(BF16)
        yield

    def mlp(j, gated):
        tok = slice(j * S5_TOK, (j + 1) * S5_TOK)
        y = _dot(gated, wo_ref[...]).reshape(BATCH, S5_TOK, D_MODEL)
        h1 = h_ref[:, tok, :] + mod[:, 2:3, :] * y
        f = (_rms(h1) * g_ref[...] * (1.0 + mod[:, 4:5, :]) + mod[:, 3:4, :])
        f = f.reshape(BATCH * S5_TOK, D_MODEL).astype(BF16)
        acc = jnp.zeros((BATCH * S5_TOK, D_MODEL), F32)
        for c in range(D_FF // FF_CHUNK):
            hid = jnp.maximum(_dot(f, w1_ref[:, c * FF_CHUNK:(c + 1) * FF_CHUNK]), 0.0)
            acc = acc + _dot((hid * hid).astype(BF16), w2_ref[c * FF_CHUNK:(c + 1) * FF_CHUNK, :])
            if c + 1 < D_FF // FF_CHUNK:
                yield
        o_ref[:, tok, :] = h1 + mod[:, 5:6, :] * acc.reshape(BATCH, S5_TOK, D_MODEL)
        yield

    cur = [None]
    for _ in gated_mixer(0, cur):
        pass
    for j in range(TAIL_SUB):
        nxt = [None]
        streams = [mlp(j, cur[0])] + ([gated_mixer(j + 1, nxt)] if j + 1 < TAIL_SUB else [])
        while streams:
            for gen in list(streams):
                if next(gen, StopIteration) is StopIteration:
                    streams.remove(gen)
        cur = nxt


def _s5_tail(u, y_g, d_skip, glu_w, glu_b, perm_t, h, mods, g, wo, w1_all, w2_all, layer):
    ctx_steps = N_CTX_CHUNKS // (S5_PAIR * TAIL_SUB)
    n_rows = BATCH * S5_TOK
    vec = pl.BlockSpec((1, D_MODEL), lambda p: (0, 0))

    def full(shape):
        return pl.BlockSpec(shape, lambda p: (0,) * len(shape), pipeline_mode=pl.Buffered(1))

    return pl.pallas_call(
        _s5_tail_kernel,
        out_shape=jax.ShapeDtypeStruct((BATCH, SEQ, D_MODEL), F32),
        grid=(S5_STEPS // TAIL_SUB - ctx_steps,),
        in_specs=[pl.BlockSpec((TAIL_SUB * S5_TOK, BATCH, D_MODEL), lambda p: (p + ctx_steps, 0, 0)),
                  pl.BlockSpec((S5_GROUPS, TAIL_SUB * S5_PAIR * BATCH, S5_CW), lambda p: (0, p + ctx_steps, 0)),
                  vec, full((D_MODEL, D_MODEL)), vec, full((n_rows, n_rows)),
                  pl.BlockSpec((BATCH, TAIL_SUB * S5_TOK, D_MODEL), lambda p: (0, p + ctx_steps, 0)),
                  pl.BlockSpec((None, BATCH, N_MOD, D_MODEL), lambda p: (1, 0, 0, 0)),
                  vec, full((D_MODEL, D_MODEL)),
                  pl.BlockSpec((None, D_MODEL, D_FF), lambda p: (layer, 0, 0), pipeline_mode=pl.Buffered(1)),
                  pl.BlockSpec((None, D_FF, D_MODEL), lambda p: (layer, 0, 0), pipeline_mode=pl.Buffered(1))],
        out_specs=pl.BlockSpec((BATCH, TAIL_SUB * S5_TOK, D_MODEL), lambda p: (0, p, 0)),
        compiler_params=pltpu.CompilerParams(vmem_limit_bytes=TAIL_VMEM_LIMIT),
        name="s5_tail_mlp",
    )(u, y_g, d_skip, glu_w, glu_b, perm_t, h, mods, g, wo, w1_all, w2_all)


def _rope_tables():
    rows_n = SEQ // GRID_W
    row = np.repeat(np.arange(rows_n, dtype=np.float64), GRID_W)
    col = np.tile(np.arange(GRID_W, dtype=np.float64), rows_n)
    n_freq = HEAD_DIM // 4
    inv = ROPE_BASE ** (-np.arange(n_freq, dtype=np.float64) / n_freq)
    ang = np.concatenate([row[:, None] * inv, col[:, None] * inv], axis=-1)
    reps = LANES // (HEAD_DIM // 2)
    cos_t = np.tile(np.cos(ang), (1, reps))
    sin_t = np.tile(np.sin(ang), (1, reps))
    sign = np.where((np.arange(LANES) % HEAD_DIM) < HEAD_DIM // 2, -1.0, 1.0)
    cos_t = np.concatenate([np.ones((CTX_LEN, LANES)), cos_t], axis=0)
    sin_s = np.concatenate([np.zeros((CTX_LEN, LANES)), sin_t * sign], axis=0)
    return jnp.asarray(cos_t, F32), jnp.asarray(sin_s, F32)


def _fb_rows(x):
    return jnp.transpose(x, (1, 0, 2)).reshape(S5_GROUPS, 1, 2 * S5_STATE)


def _s5_layout(lam_re, lam_im, log_step, b_re, b_im, c_re, c_im):
    ls = jnp.broadcast_to(log_step[:, :, None], lam_re.shape)
    rows = [_fb_rows(v) for v in (lam_re, lam_im, ls)]

    def bt_of(b):
        return jnp.transpose(b, (1, 3, 0, 2)).reshape(S5_GROUPS, S5_GROUP, 2 * S5_STATE)

    def c_of(c):
        return jnp.transpose(c, (1, 2, 0, 3)).reshape(S5_GROUPS, S5_GROUP, 2 * S5_STATE)

    return rows, [bt_of(b_re), bt_of(b_im)], [c_of(c_re), c_of(c_im)]


def kernel(x, c, ctx, c_ctx, norm1_g, norm2_g, mod_w, mod_b, mlp_w1, mlp_w2, attn_w_in, attn_w_out, a_q_norm, a_k_norm, a_sink, b_q_norm, b_k_norm, b_lq1, b_lk1, b_lq2, b_lk2, b_subln, s5_w_in, s5_lambda_re, s5_lambda_im, s5_log_step, s5_b_re, s5_b_im, s5_c_re, s5_c_im, s5_d, s5_glu_w, s5_glu_b, s5_w_out):
    assert x.shape == (BATCH, SEQ, D_MODEL) and ctx.shape == (BATCH, CTX_LEN, D_MODEL)
    stream = (ctx, x)
    s_rows = jnp.concatenate([c, c_ctx[None], jnp.zeros((MOD_ROWS - BATCH - 1, D_MODEL), F32)], axis=0)
    m_all = _modulation(s_rows, mod_w, mod_b)
    cos_t, sin_s = _rope_tables()
    w1_all, w2_all = mlp_w1.astype(BF16), mlp_w2.astype(BF16)
    e_blk = jnp.asarray(np.kron(np.eye(2 * LANES // HEAD_DIM), np.ones((HEAD_DIM, HEAD_DIM))) / HEAD_DIM, BF16)

    for i in range(DEPTH):
        last = i == DEPTH - 1
        j = i // 2
        m_lat = m_all[i, :BATCH].reshape(BATCH, N_MOD, D_MODEL)
        m_ctx = jnp.broadcast_to(m_all[i, BATCH].reshape(1, N_MOD, D_MODEL), (BATCH, N_MOD, D_MODEL))
        mods = jnp.stack([m_ctx, m_lat])
        g1 = norm1_g[i].reshape(1, D_MODEL)
        g2 = norm2_g[i].reshape(1, D_MODEL)
        if i % 2 == 0:
            lambda_init = 0.8 - 0.6 * math.exp(-0.3 * i)
            gains = jnp.stack([jnp.tile(v[j], LANES // HEAD_DIM) for v in (a_q_norm, a_k_norm, b_q_norm, b_k_norm)])
            qa, k2a, v2a, qb, kb, vb = _attn_in(stream, mods, g1, attn_w_in[j].astype(BF16), gains, cos_t, sin_s, e_blk)
            lpar = jnp.stack([b_lq1[j], b_lk1[j], b_lq2[j], b_lk2[j]])
            ya, yb = _attention(a_sink[j], lpar, b_subln[j].reshape(1, LANES), qa, k2a, v2a, qb, kb, vb, lambda_init)
            if last:
                ya, yb = ya[:, CTX_LEN:], yb[:, CTX_LEN:]
            h = _mix_mlp(stream, ya, yb, mods, g2, attn_w_out[j].astype(BF16), w1_all, w2_all, i, latent_only=last)
        else:
            src = np.arange(BATCH * S5_TOK).reshape(BATCH, S5_TOK).T.reshape(-1)
            perm = jnp.asarray(np.eye(BATCH * S5_TOK, dtype=np.float32)[src], BF16)
            h_all = stream[0] if stream[0].shape[1] == TT else jnp.concatenate(stream, axis=1)
            u, u_g = _s5_in(h_all, mods, g1, perm, s5_w_in[j].astype(BF16))
            ops_in = _s5_layout(s5_lambda_re[j], s5_lambda_im[j], s5_log_step[j], s5_b_re[j], s5_b_im[j],
                                s5_c_re[j], s5_c_im[j])
            m_op, q_op, n_op, a_vec = _s5_ops(*ops_in)
            y_g = _s5_scan(u_g, m_op, q_op, n_op, a_vec)
            assert last, "S5 layers before the last one would also need the context rows of the readout"
            h = _s5_tail(u, y_g, s5_d[j].reshape(1, D_MODEL), s5_glu_w[j].astype(BF16),
                         s5_glu_b[j].reshape(1, D_MODEL), perm.T, h_all, mods, g2, s5_w_out[j].astype(BF16),
                         w1_all, w2_all, i)
        stream = (h, h)
    return h
```

```python
import functools
import math

import jax
import jax.numpy as jnp
import numpy as np
from jax import lax
from jax.experimental import pallas as pl
from jax.experimental.pallas import tpu as pltpu

F32 = jnp.float32
BF16 = jnp.bfloat16

D_MODEL = 1024
BATCH = 8
SEQ = 2048
DEPTH = 2
GRID_W = 64
CTX_LEN = 256
HEAD_DIM = 64
WINDOW = 128
A_Q_HEADS = 8
A_KV_HEADS = 2
B_HEADS = 4
A_Q_W = A_Q_HEADS * HEAD_DIM
A_KV_W = A_KV_HEADS * HEAD_DIM
B_QK_W = B_HEADS * 2 * HEAD_DIM
B_V_W = B_HEADS * 2 * HEAD_DIM
ATTN_IN = A_Q_W + 2 * A_KV_W + 2 * B_QK_W + B_V_W
S5_GROUP = 16
S5_GROUPS = D_MODEL // S5_GROUP
S5_STATE = 64
D_FF = 4 * D_MODEL
ROPE_BASE = 10000.0
EPS = 1e-6
NEG_INF = -1e30
LOG2E = math.log2(math.e)

TT = CTX_LEN + SEQ
TM = 256
N_TILES = TT // TM
LANES = 128
S5_CHUNK = 16
S5_CW = S5_CHUNK * S5_GROUP
N_CHUNKS = TT // S5_CHUNK
N_CTX_CHUNKS = CTX_LEN // S5_CHUNK
VMEM_LIMIT = 56 * 1024 * 1024
N_MOD = 6
MOD_ROWS = 16
MOD_COLS = 2 * D_MODEL
GQA = A_Q_HEADS // A_KV_HEADS


def _dot(a, b):
    return jnp.dot(a, b, preferred_element_type=F32)


def _dot_nt(a, b):
    return lax.dot_general(a, b, (((1,), (1,)), ((), ())), preferred_element_type=F32)


def _rms(x):
    return x * lax.rsqrt(jnp.mean(x * x, axis=-1, keepdims=True) + EPS)


def _modnorm(x, g, shift, scale):
    return _rms(x) * g * (1.0 + scale) + shift


def _params(**kw):
    return pltpu.CompilerParams(vmem_limit_bytes=VMEM_LIMIT, **kw)


def _mod_kernel(s_ref, w_ref, b_ref, o_ref):
    s = s_ref[...]
    s = s * jax.nn.sigmoid(s)
    o_ref[...] = _dot(s.astype(BF16), w_ref[...].astype(BF16)) + b_ref[...]


def _modulation(s_rows, mod_w, mod_b):
    return pl.pallas_call(
        _mod_kernel,
        out_shape=jax.ShapeDtypeStruct((DEPTH, MOD_ROWS, N_MOD * D_MODEL), F32),
        grid=(DEPTH, N_MOD * D_MODEL // MOD_COLS),
        in_specs=[
            pl.BlockSpec((MOD_ROWS, D_MODEL), lambda i, j: (0, 0)),
            pl.BlockSpec((None, D_MODEL, MOD_COLS), lambda i, j: (i, 0, j)),
            pl.BlockSpec((None, 1, MOD_COLS), lambda i, j: (i, 0, j)),
        ],
        out_specs=pl.BlockSpec((None, MOD_ROWS, MOD_COLS), lambda i, j: (i, 0, j)),
        compiler_params=_params(),
        name="modulation",
    )(s_rows, mod_w, mod_b.reshape(DEPTH, 1, N_MOD * D_MODEL))


def _stream_specs(stream, n_sub, h_off):
    first_lat = 1 if stream[0].shape[1] == TT else 0

    def tile(t, k):
        return t * n_sub + k + h_off

    ctx_spec = pl.BlockSpec((None, TM, D_MODEL), lambda b, t: (b, 0, 0))
    lat_specs = [pl.BlockSpec((None, TM, D_MODEL),
                              lambda b, t, k=k: (b, jnp.maximum(tile(t, k) - 1 + first_lat, first_lat), 0))
                 for k in range(n_sub)]
    mod_specs = [pl.BlockSpec((None, None, N_MOD, D_MODEL), lambda b, t, k=k: (jnp.minimum(tile(t, k), 1), b, 0, 0))
                 for k in range(n_sub)]
    return ctx_spec, lat_specs, mod_specs


ATTN_IN_SUB = 3


def _attn_in_kernel(*refs):
    n_sub = ATTN_IN_SUB
    hc_ref = refs[0]
    hl_refs = refs[1:1 + n_sub]
    mod_refs = refs[1 + n_sub:1 + 2 * n_sub]
    (g_ref, w_ref, gain_ref, cos_ref, sin_ref, e_ref,
     qa_ref, k2a_ref, v2a_ref, qb_ref, kb_ref, vb_ref) = refs[1 + 2 * n_sub:]
    e = e_ref[...]
    gains = gain_ref[...]
    lane = lax.broadcasted_iota(jnp.int32, (TM, LANES), 1)
    first_half = (lane & (HEAD_DIM - 1)) < HEAD_DIM // 2
    lo = lane < HEAD_DIM
    q_scale = HEAD_DIM ** -0.5 * LOG2E
    ones = jnp.ones((TM, LANES), BF16)

    def tile_segments(k):
        rows = slice(k * TM, (k + 1) * TM)
        cos_t = cos_ref[rows, :]
        sin_s = sin_ref[rows, :]

        def head_mean_sq(z):
            sq = (z * z).astype(BF16)
            width = z.shape[1]
            if width < 2 * LANES:
                return _dot(sq, e[0:width, 0:width])
            return jnp.concatenate([_dot(sq[:, c:c + 2 * LANES], e) for c in range(0, width, 2 * LANES)], axis=1)

        def norm_rope_chunks(z, gain):
            ms = head_mean_sq(z)
            for c in range(z.shape[1] // LANES):
                cols = slice(c * LANES, (c + 1) * LANES)
                cn = z[:, cols] * lax.rsqrt(ms[:, cols] + EPS) * gain
                r_fwd = pltpu.roll(cn, HEAD_DIM // 2, 1)
                r_bwd = pltpu.roll(cn, LANES - HEAD_DIM // 2, 1)
                yield cn * cos_t + jnp.where(first_half, r_bwd, r_fwd) * sin_s

        def dup_halves(x):
            sw = pltpu.roll(x, HEAD_DIM, 1)
            return jnp.where(lo, x, sw).astype(BF16), jnp.where(lo, sw, x).astype(BF16)

        def finish_q(z, ref, gain):
            for c, chunk in enumerate(norm_rope_chunks(z, gain)):
                ref[rows, c * LANES:(c + 1) * LANES] = (chunk * q_scale).astype(BF16)

        def finish_kv_a(z):
            (k_roped,) = norm_rope_chunks(z[:, 0:LANES], gains[1:2])
            for kvh, dup in enumerate(dup_halves(k_roped)):
                k2a_ref[rows, kvh * LANES:(kvh + 1) * LANES] = dup
            for kvh, dup in enumerate(dup_halves(z[:, LANES:2 * LANES])):
                v2a_ref[rows, 2 * kvh * LANES:(2 * kvh + 1) * LANES] = dup
                v2a_ref[rows, (2 * kvh + 1) * LANES:(2 * kvh + 2) * LANES] = ones

        def finish_kb(z):
            for c, chunk in enumerate(norm_rope_chunks(z, gains[3:4])):
                kb_ref[rows, c * LANES:(c + 1) * LANES] = chunk.astype(BF16)

        def finish_vb(z):
            for hd in range(B_HEADS):
                vb_ref[rows, 2 * hd * LANES:(2 * hd + 1) * LANES] = z[:, hd * LANES:(hd + 1) * LANES].astype(BF16)
                vb_ref[rows, (2 * hd + 1) * LANES:(2 * hd + 2) * LANES] = ones

        return [(A_Q_W, lambda z: finish_q(z, qa_ref, gains[0:1])), (2 * A_KV_W, finish_kv_a),
                (B_QK_W, lambda z: finish_q(z, qb_ref, gains[2:3])), (B_QK_W, finish_kb), (B_V_W, finish_vb)]

    def normed(k):
        mod = mod_refs[k][...]
        x = hl_refs[k][...]
        if k == 0:
            x = jnp.where(pl.program_id(1) == 0, hc_ref[...], x)
        return _modnorm(x, g_ref[...], mod[0:1], mod[1:2]).astype(BF16)

    pending = None
    for k in range(n_sub):
        a = normed(k)
        off = 0
        for width, finish in tile_segments(k):
            z = _dot(a, w_ref[:, off:off + width])
            off += width
            if pending is not None:
                pending[1](pending[0])
            pending = (z, finish)
    pending[1](pending[0])


def _attn_in(stream, mods, g, w_in, gains, cos_t, sin_s, e_blk):
    n_sub = ATTN_IN_SUB
    ctx_spec, lat_specs, mod_specs = _stream_specs(stream, n_sub, 0)

    def tok(width):
        return pl.BlockSpec((None, n_sub * TM, width), lambda b, t: (b, t, 0))

    def full(shape):
        return pl.BlockSpec(shape, lambda b, t: (0,) * len(shape))

    out_shapes = [jax.ShapeDtypeStruct((BATCH, TT, w), BF16)
                  for w in (A_Q_W, 2 * A_KV_W, 4 * A_KV_W, B_QK_W, B_QK_W, 2 * B_V_W)]
    return pl.pallas_call(
        _attn_in_kernel,
        out_shape=out_shapes,
        grid=(BATCH, N_TILES // n_sub),
        in_specs=[
            ctx_spec, *lat_specs, *mod_specs,
            full((1, D_MODEL)), full((D_MODEL, ATTN_IN)), full((4, LANES)),
            pl.BlockSpec((n_sub * TM, LANES), lambda b, t: (t, 0)),
            pl.BlockSpec((n_sub * TM, LANES), lambda b, t: (t, 0)),
            full((2 * LANES, 2 * LANES)),
        ],
        out_specs=[tok(A_Q_W), tok(2 * A_KV_W), tok(4 * A_KV_W), tok(B_QK_W), tok(B_QK_W), tok(2 * B_V_W)],
        compiler_params=_params(),
        name="attn_in_proj",
    )(stream[0], *[stream[1]] * n_sub, *[mods] * n_sub, g, w_in, gains, cos_t, sin_s, e_blk)


QB = 128


def _run_pipelined(items):
    s_next = items[0][0]()
    for i, (_, finish) in enumerate(items):
        s_cur = s_next
        if i + 1 < len(items):
            s_next = items[i + 1][0]()
        finish(s_cur)


def _win_attn_items(t, sink_ref, q_ref, k2_ref, v2_ref, o_ref, with_window):
    lane = lax.broadcasted_iota(jnp.int32, (QB, LANES), 1)
    lo = lane < HEAD_DIM
    rows = GQA * QB
    row = lax.broadcasted_iota(jnp.int32, (rows, 3 * QB), 0)
    col = lax.broadcasted_iota(jnp.int32, (rows, 3 * QB), 1)
    row_head = lax.broadcasted_iota(jnp.int32, (rows, 1), 0) // QB
    zero = jnp.zeros((QB, LANES), BF16)
    blocks = [(qb, g) for qb in range(TM // QB) for g in range(A_KV_HEADS)]

    def window_start(qb):
        n = (t - 1) * (TM // QB) + qb
        ws = jnp.clip((n - 1) * QB, 0, SEQ - 3 * QB)
        return n, ws

    def scores(qb, g):
        pieces = []
        for p in range(2):
            qp = q_ref[qb * QB:(qb + 1) * QB, g * 2 * LANES + p * LANES: g * 2 * LANES + (p + 1) * LANES]
            pieces.append(jnp.where(lo, qp, zero))
            pieces.append(jnp.where(lo, zero, qp))
        qs = jnp.concatenate(pieces, axis=0)
        s_c = _dot_nt(qs, k2_ref[0:CTX_LEN, g * LANES:(g + 1) * LANES])
        if not with_window:
            return s_c, None
        n, ws = window_start(qb)
        kw = k2_ref[pl.ds(pl.multiple_of(ws + CTX_LEN, QB), 3 * QB), g * LANES:(g + 1) * LANES]
        valid = jnp.abs(n * QB + (row & (QB - 1)) - (ws + col)) <= WINDOW
        return s_c, jnp.where(valid, _dot_nt(qs, kw), NEG_INF)

    def finish(qb, g, s):
        s_c, s_w = s
        sk = jnp.full((rows, 1), sink_ref[GQA * g + GQA - 1], F32)
        for hh in range(GQA - 1):
            sk = jnp.where(row_head == hh, sink_ref[GQA * g + hh], sk)
        sk = sk * LOG2E
        m = jnp.maximum(jnp.max(s_c, axis=-1, keepdims=True), sk)
        if s_w is not None:
            m = jnp.maximum(m, jnp.max(s_w, axis=-1, keepdims=True))
        vcols = slice(2 * g * LANES, (2 * g + 2) * LANES)
        pv = _dot(jnp.exp2(s_c - m).astype(BF16), v2_ref[0:CTX_LEN, vcols])
        if s_w is not None:
            _, ws = window_start(qb)
            vw = v2_ref[pl.ds(pl.multiple_of(ws + CTX_LEN, QB), 3 * QB), vcols]
            pv = pv + _dot(jnp.exp2(s_w - m).astype(BF16), vw)
        o = pv[:, 0:LANES] / (pv[:, LANES:2 * LANES] + jnp.exp2(sk - m))
        for p in range(2):
            o_ref[qb * QB:(qb + 1) * QB, g * 2 * LANES + p * LANES: g * 2 * LANES + (p + 1) * LANES] = jnp.where(
                lo, o[2 * p * QB:(2 * p + 1) * QB], o[(2 * p + 1) * QB:(2 * p + 2) * QB]).astype(BF16)

    return [(functools.partial(scores, qb, g), functools.partial(finish, qb, g)) for qb, g in blocks]


DIFF_ROWS = 256


def _diff_attn_items(lam, subln_ref, q_ref, k_ref, v_ref, o_ref, n_keys, lambda_init):
    R = DIFF_ROWS
    lane = lax.broadcasted_iota(jnp.int32, (R, LANES), 1)
    lo = lane < HEAD_DIM
    zero = jnp.zeros((R, LANES), BF16)
    blocks = [(slice(rb * R, (rb + 1) * R), slice(h * LANES, (h + 1) * LANES))
              for h in range(B_HEADS) for rb in range(TM // R)]

    def scores(rows, cols):
        q = q_ref[rows, cols]
        qs = jnp.concatenate([jnp.where(lo, q, zero), jnp.where(lo, zero, q)], axis=0)
        return _dot_nt(qs, k_ref[0:n_keys, cols])

    def finish(rows, cols, s):
        p = jnp.exp2(s - jnp.max(s, axis=-1, keepdims=True)).astype(BF16)
        vcols = slice(2 * cols.start, 2 * cols.stop)
        pv = _dot(p, v_ref[0:n_keys, vcols])
        sm = pv[:, 0:LANES] / pv[:, LANES:2 * LANES]
        y = sm[0:R] - lam * sm[R:2 * R]
        o_ref[rows, cols] = (_rms(y) * subln_ref[...] * (1.0 - lambda_init)).astype(BF16)

    return [(functools.partial(scores, *blk), functools.partial(finish, *blk)) for blk in blocks]


def _attention_kernel(sink_ref, lpar_ref, subln_ref, qa_ref, k2a_ref, v2a_ref, qb_ref, kb_ref, vb_ref,
                      ya_ref, yb_ref, *, lambda_init):
    step = pl.program_id(1)
    lp = lpar_ref[...]
    lam = (jnp.exp(jnp.sum(lp[0:1] * lp[1:2], axis=-1, keepdims=True))
           - jnp.exp(jnp.sum(lp[2:3] * lp[3:4], axis=-1, keepdims=True)) + lambda_init)

    def attend(k, is_ctx):
        rows = pl.ds(k * TM, TM)
        t = step * ATT_SUB + k
        win = _win_attn_items(t, sink_ref, qa_ref.at[rows], k2a_ref, v2a_ref, ya_ref.at[rows], not is_ctx)
        dif = _diff_attn_items(lam, subln_ref, qb_ref.at[rows], kb_ref, vb_ref, yb_ref.at[rows],
                               CTX_LEN if is_ctx else TT, lambda_init)
        assert len(win) == len(dif)
        _run_pipelined([item for pair in zip(dif, win) for item in pair])

    pl.when(step == 0)(functools.partial(attend, 0, True))
    pl.when(step > 0)(functools.partial(attend, 0, False))
    for k in range(1, ATT_SUB):
        attend(k, False)


ATT_SUB = 3


def _attention(sink, lpar, subln, qa, k2a, v2a, qb, kb, vb, lambda_init):
    def tile(width):
        return pl.BlockSpec((None, ATT_SUB * TM, width), lambda b, t: (b, t, 0))

    def keys(width):
        return pl.BlockSpec((None, TT, width), lambda b, t: (b, 0, 0))

    return pl.pallas_call(
        functools.partial(_attention_kernel, lambda_init=lambda_init),
        out_shape=[jax.ShapeDtypeStruct((BATCH, TT, A_Q_W), BF16), jax.ShapeDtypeStruct((BATCH, TT, B_V_W), BF16)],
        grid=(BATCH, N_TILES // ATT_SUB),
        in_specs=[
            pl.BlockSpec(memory_space=pltpu.SMEM),
            pl.BlockSpec((4, HEAD_DIM), lambda b, t: (0, 0)),
            pl.BlockSpec((1, LANES), lambda b, t: (0, 0)),
            tile(A_Q_W), keys(2 * A_KV_W), keys(4 * A_KV_W),
            tile(B_QK_W), keys(B_QK_W), keys(2 * B_V_W),
        ],
        out_specs=[tile(A_Q_W), tile(B_V_W)],
        compiler_params=_params(),
        name="attention",
    )(sink, lpar, subln, qa, k2a, v2a, qb, kb, vb)


FF_CHUNK = 1024


def _mix_mlp_kernel(*refs, h_off, n_sub):
    hc_ref = refs[0]
    hl_refs, ua_refs, ub_refs, mod_refs = (refs[1 + i * n_sub:1 + (i + 1) * n_sub] for i in range(4))
    g_ref, wo_ref, w1_ref, w2_ref, o_ref = refs[1 + 4 * n_sub:]
    half = D_MODEL // 2

    def prologue(k):
        mod = mod_refs[k][...]
        y = _dot(ua_refs[k][...], wo_ref[0:half, :]) + _dot(ub_refs[k][...], wo_ref[half:D_MODEL, :])
        x = hl_refs[k][...]
        if h_off == 0 and k == 0:
            x = jnp.where(pl.program_id(1) == 0, hc_ref[...], x)
        h1 = x + mod[2:3] * y
        f = _modnorm(h1, g_ref[...], mod[3:4], mod[4:5]).astype(BF16)
        return h1, f, mod[5:6]

    def mlp(k, h1, f, gate):
        acc = jnp.zeros((TM, D_MODEL), F32)
        for c in range(D_FF // FF_CHUNK):
            hid = jnp.maximum(_dot(f, w1_ref[:, c * FF_CHUNK:(c + 1) * FF_CHUNK]), 0.0)
            acc = acc + _dot((hid * hid).astype(BF16), w2_ref[c * FF_CHUNK:(c + 1) * FF_CHUNK, :])
        o_ref[k * TM:(k + 1) * TM, :] = h1 + gate * acc

    nxt = prologue(0)
    for k in range(n_sub):
        cur = nxt
        if k + 1 < n_sub:
            nxt = prologue(k + 1)
        mlp(k, *cur)


def _mix_mlp(stream, ua, ub, mods, g, wo, w1_all, w2_all, layer, *, latent_only):
    n_tiles = SEQ // TM if latent_only else N_TILES
    h_off = N_TILES - n_tiles
    n_sub = 4 if n_tiles % 4 == 0 else 3
    half = D_MODEL // 2
    ub_col = 1 if ub.shape[-1] == D_MODEL else 0
    ctx_spec, lat_specs, mod_specs = _stream_specs(stream, n_sub, h_off)

    def full(shape):
        return pl.BlockSpec(shape, lambda b, t: (0,) * len(shape), pipeline_mode=pl.Buffered(1))

    def mixer_specs(col):
        return [pl.BlockSpec((None, TM, half), lambda b, t, k=k: (b, t * n_sub + k, col)) for k in range(n_sub)]

    return pl.pallas_call(
        functools.partial(_mix_mlp_kernel, h_off=h_off, n_sub=n_sub),
        out_shape=jax.ShapeDtypeStruct((BATCH, n_tiles * TM, D_MODEL), F32),
        grid=(BATCH, n_tiles // n_sub),
        in_specs=[
            ctx_spec, *lat_specs, *mixer_specs(0), *mixer_specs(ub_col), *mod_specs,
            pl.BlockSpec((1, D_MODEL), lambda b, t: (0, 0)),
            full((D_MODEL, D_MODEL)),
            pl.BlockSpec((None, D_MODEL, D_FF), lambda b, t: (layer, 0, 0), pipeline_mode=pl.Buffered(1)),
            pl.BlockSpec((None, D_FF, D_MODEL), lambda b, t: (layer, 0, 0), pipeline_mode=pl.Buffered(1)),
        ],
        out_specs=pl.BlockSpec((None, n_sub * TM, D_MODEL), lambda b, t: (b, t, 0)),
        compiler_params=_params(),
        name="mixer_out_mlp",
    )(stream[0], *[stream[1]] * n_sub, *[ua] * n_sub, *[ub] * n_sub, *[mods] * n_sub, g, wo, w1_all, w2_all)


S5_PAIR = 2
S5_TOK = S5_PAIR * S5_CHUNK
S5_STEPS = N_CHUNKS // S5_PAIR
GROUPS_PER_VREG = LANES // S5_GROUP
STEPS_PER_VREG = LANES // S5_GROUP
S5_SEG = 2 * LANES


def _lane_block():
    return lax.broadcasted_iota(jnp.int32, (BATCH, LANES), 1) // S5_GROUP


def _block_transpose(xs, lane_blk):
    xs = list(xs)
    n = len(xs)
    d = n // 2
    while d:
        low = (lane_blk & d) == 0
        for i in range(n):
            if i & d:
                continue
            a, b = xs[i], xs[i + d]
            xs[i] = jnp.where(low, a, pltpu.roll(b, S5_GROUP * d, 1))
            xs[i + d] = jnp.where(low, pltpu.roll(a, LANES - S5_GROUP * d, 1), b)
        d //= 2
    return xs


S5_SUB = 4


def _s5_in_kernel(*refs):
    h_refs = refs[:S5_SUB]
    mod_ref, g_ref, perm_ref, w_ref, u_ref, z_ref = refs[S5_SUB:]
    mod = mod_ref[...]
    lane_blk = _lane_block()

    def normed(j):
        x = h_refs[j][...]
        a = _rms(x) * g_ref[...] * (1.0 + mod[:, 1:2, :]) + mod[:, 0:1, :]
        a = a.reshape(BATCH * S5_TOK, D_MODEL).astype(BF16)
        return _dot(perm_ref[...], a).astype(BF16)

    def finish(j, s, u_seg):
        u_ref[j * S5_TOK:(j + 1) * S5_TOK, :, s * S5_SEG:(s + 1) * S5_SEG] = u_seg.reshape(S5_TOK, BATCH, S5_SEG)
        for kk in range(S5_SEG // LANES):
            k = s * (S5_SEG // LANES) + kk
            u_col = u_seg[:, kk * LANES:(kk + 1) * LANES]
            for hh in range(S5_CHUNK // STEPS_PER_VREG):
                halves = []
                for c2 in range(S5_PAIR):
                    tok0 = c2 * S5_CHUNK + hh * STEPS_PER_VREG
                    steps = [u_col[(tok0 + m) * BATCH:(tok0 + m + 1) * BATCH, :] for m in range(STEPS_PER_VREG)]
                    halves.append(_block_transpose(steps, lane_blk))
                rows = slice(j * S5_PAIR * BATCH, (j + 1) * S5_PAIR * BATCH)
                for r in range(GROUPS_PER_VREG):
                    z_ref[k * GROUPS_PER_VREG + r, rows, hh * LANES:(hh + 1) * LANES] = (
                        jnp.concatenate([h[r] for h in halves], axis=0).astype(BF16))

    pending = None
    for j in range(S5_SUB):
        a = normed(j)
        for s in range(D_MODEL // S5_SEG):
            u_seg = _dot(a, w_ref[:, s * S5_SEG:(s + 1) * S5_SEG])
            if pending is not None:
                finish(*pending)
            pending = (j, s, u_seg)
    finish(*pending)


def _s5_in(h, mods, g, perm, w_in):
    ctx_steps = N_CTX_CHUNKS // (S5_PAIR * S5_SUB)
    n_rows = BATCH * S5_TOK
    return pl.pallas_call(
        _s5_in_kernel,
        out_shape=[jax.ShapeDtypeStruct((TT, BATCH, D_MODEL), F32),
                   jax.ShapeDtypeStruct((S5_GROUPS, ROWS, S5_CW), BF16)],
        grid=(S5_STEPS // S5_SUB,),
        in_specs=[*[pl.BlockSpec((BATCH, S5_TOK, D_MODEL), lambda p, j=j: (0, p * S5_SUB + j, 0))
                    for j in range(S5_SUB)],
                  pl.BlockSpec((None, BATCH, N_MOD, D_MODEL), lambda p: (jnp.minimum(p // ctx_steps, 1), 0, 0, 0)),
                  pl.BlockSpec((1, D_MODEL), lambda p: (0, 0)),
                  pl.BlockSpec((n_rows, n_rows), lambda p: (0, 0)),
                  pl.BlockSpec((D_MODEL, D_MODEL), lambda p: (0, 0))],
        out_specs=[pl.BlockSpec((S5_SUB * S5_TOK, BATCH, D_MODEL), lambda p: (p, 0, 0)),
                   pl.BlockSpec((S5_GROUPS, S5_SUB * S5_PAIR * BATCH, S5_CW), lambda p: (0, p, 0))],
        compiler_params=_params(),
        name="s5_in_proj",
    )(*[h] * S5_SUB, mods, g, perm, w_in)


OPS_G = 8


def _s5_ops_kernel(*refs):
    for j in range(OPS_G):
        _s5_group_ops(*(r.at[j] for r in refs))


def _s5_group_ops(lr_ref, li_ref, ls_ref, btr_ref, bti_ref, cr_ref, ci_ref, m_ref, q_ref, n_ref, a_ref):
    P2 = 2 * S5_STATE

    def cmul(xr, xi, yr, yi):
        return xr * yr - xi * yi, xr * yi + xi * yr

    lr, li = lr_ref[...], li_ref[...]
    dt = jnp.exp(ls_ref[...])
    mag = jnp.exp(lr * dt)
    ar = mag * jnp.cos(li * dt)
    ai = mag * jnp.sin(li * dt)
    den = lr * lr + li * li
    nr = ar - 1.0
    f_re = (nr * lr + ai * li) / den
    f_im = (ai * lr - nr * li) / den
    bt_re = f_re * btr_ref[...] - f_im * bti_ref[...]
    bt_im = f_re * bti_ref[...] + f_im * btr_ref[...]

    powers = [(jnp.ones_like(ar), jnp.zeros_like(ai))]
    for _ in range(S5_CHUNK):
        powers.append(cmul(*powers[-1], ar, ai))
    row_fwd = lax.broadcasted_iota(jnp.int32, (1, P2), 1) < S5_STATE

    def pow_rows(exp_fwd, exp_bwd):
        return (jnp.where(row_fwd, powers[exp_fwd][0], powers[exp_bwd][0]),
                jnp.where(row_fwd, powers[exp_fwd][1], powers[exp_bwd][1]))

    c_re, c_im = cr_ref[...], ci_ref[...]
    last = S5_CHUNK - 1
    cp_blocks = [cmul(c_re, c_im, *pow_rows(blk, last - blk)) for blk in range(S5_CHUNK)]
    cp_re = jnp.concatenate([b[0] for b in cp_blocks], axis=0)
    cp_im = jnp.concatenate([b[1] for b in cp_blocks], axis=0)
    lane_fwd = lax.broadcasted_iota(jnp.int32, (S5_GROUP, P2), 1) < S5_STATE

    def lag_kernels(keep):
        br = jnp.where(keep, bt_re, 0.0)
        bi = jnp.where(keep, bt_im, 0.0)
        dims = (((1,), (1,)), ((), ()))
        hi = lax.Precision.HIGHEST
        return (lax.dot_general(br, cp_re, dims, precision=hi, preferred_element_type=F32)
                - lax.dot_general(bi, cp_im, dims, precision=hi, preferred_element_type=F32))

    kt_f = lag_kernels(lane_fwd)
    kt_b = lag_kernels(jnp.logical_not(lane_fwd))
    lane_w = lax.broadcasted_iota(jnp.int32, (S5_GROUP, S5_CW), 1)
    for s in range(S5_CHUNK):
        f_part = kt_f if s == 0 else jnp.where(lane_w >= S5_GROUP * s, pltpu.roll(kt_f, S5_GROUP * s, 1), 0.0)
        sh = (S5_GROUP * (s + 1)) % S5_CW
        b_roll = kt_b if sh == 0 else pltpu.roll(kt_b, sh, 1)
        b_part = jnp.where(lane_w < S5_GROUP * (s + 1), b_roll, 0.0)
        m_ref[s * S5_GROUP:(s + 1) * S5_GROUP, :] = (f_part + b_part).astype(BF16)

    for blk in range(S5_CHUNK):
        rows = slice(blk * S5_GROUP, (blk + 1) * S5_GROUP)
        q_re, q_im = cmul(bt_re, bt_im, *pow_rows(last - blk, blk))
        q_ref[rows, 0:P2] = q_re.astype(BF16)
        q_ref[rows, P2:2 * P2] = q_im.astype(BF16)

    for blk in range(S5_CHUNK):
        rows = slice(blk * S5_GROUP, (blk + 1) * S5_GROUP)
        n_re, n_im = cmul(c_re, c_im, *pow_rows(blk + 1, S5_CHUNK - blk))
        n_ref[rows, 0:P2] = n_re.astype(BF16)
        n_ref[rows, P2:2 * P2] = (-n_im).astype(BF16)

    a_ref[0:1, :] = powers[S5_CHUNK][0]
    a_ref[1:2, :] = powers[S5_CHUNK][1]


def _s5_ops(row_params, bt, c_nat):
    P2 = 2 * S5_STATE
    row = pl.BlockSpec((OPS_G, 1, P2), lambda g: (g, 0, 0))
    mat = pl.BlockSpec((OPS_G, S5_GROUP, P2), lambda g: (g, 0, 0))
    sq = pl.BlockSpec((OPS_G, S5_CW, S5_CW), lambda g: (g, 0, 0))
    return pl.pallas_call(
        _s5_ops_kernel,
        out_shape=[jax.ShapeDtypeStruct((S5_GROUPS, S5_CW, S5_CW), BF16)] * 3
        + [jax.ShapeDtypeStruct((S5_GROUPS, 2, P2), F32)],
        grid=(S5_GROUPS // OPS_G,),
        in_specs=[row, row, row, mat, mat, mat, mat],
        out_specs=[sq, sq, sq, pl.BlockSpec((OPS_G, 2, P2), lambda g: (g, 0, 0))],
        compiler_params=_params(),
        name="s5_chunk_operators",
    )(*row_params, *bt, *c_nat)


ROWS = N_CHUNKS * BATCH


SCAN_G = 4


def _s5_scan_kernel(u_ref, m_ref, q_ref, n_ref, a_ref, y_ref, pu_ref, sp_ref):
    S = S5_STATE
    lane = lax.broadcasted_iota(jnp.int32, (BATCH, 2 * S), 1)
    lo = lane < S
    for j in range(SCAN_G):
        pu_ref[j] = _dot(u_ref[j], q_ref[j])
    for j in range(SCAN_G):
        y_ref[j] = _dot(u_ref[j], m_ref[j])
    decay = [(a_ref[j, 0:1, :], a_ref[j, 1:2, :]) for j in range(SCAN_G)]
    zero = jnp.zeros((BATCH, 2 * S), F32)
    state = [(zero, zero)] * SCAN_G
    for k in range(N_CHUNKS):
        cf = k * BATCH
        cb = (N_CTX_CHUNKS - 1 - k if k < N_CTX_CHUNKS else N_CHUNKS + N_CTX_CHUNKS - 1 - k) * BATCH
        for j in range(SCAN_G):
            s_re, s_im = state[j]
            a_re, a_im = decay[j]
            sp_ref[j, cf:cf + BATCH, 0:S] = s_re[:, 0:S]
            sp_ref[j, cb:cb + BATCH, S:2 * S] = s_re[:, S:2 * S]
            sp_ref[j, cf:cf + BATCH, 2 * S:3 * S] = s_im[:, 0:S]
            sp_ref[j, cb:cb + BATCH, 3 * S:4 * S] = s_im[:, S:2 * S]
            x_re = jnp.where(lo, pu_ref[j, cf:cf + BATCH, 0:2 * S], pu_ref[j, cb:cb + BATCH, 0:2 * S])
            x_im = jnp.where(lo, pu_ref[j, cf:cf + BATCH, 2 * S:4 * S], pu_ref[j, cb:cb + BATCH, 2 * S:4 * S])
            state[j] = (a_re * s_re - a_im * s_im + x_re, a_re * s_im + a_im * s_re + x_im)
    for j in range(SCAN_G):
        y_ref[j] += _dot_nt(sp_ref[j].astype(BF16), n_ref[j])


def _s5_scan(u_g, m_op, q_op, n_op, a_vec):
    sq = pl.BlockSpec((SCAN_G, S5_CW, S5_CW), lambda g: (g, 0, 0))
    rows = pl.BlockSpec((SCAN_G, ROWS, S5_CW), lambda g: (g, 0, 0))
    return pl.pallas_call(
        _s5_scan_kernel,
        out_shape=jax.ShapeDtypeStruct((S5_GROUPS, ROWS, S5_CW), F32),
        grid=(S5_GROUPS // SCAN_G,),
        in_specs=[rows, sq, sq, sq, pl.BlockSpec((SCAN_G, 2, 2 * S5_STATE), lambda g: (g, 0, 0))],
        out_specs=rows,
        scratch_shapes=[pltpu.VMEM((SCAN_G, ROWS, S5_CW), F32), pltpu.VMEM((SCAN_G, ROWS, S5_CW), F32)],
        compiler_params=_params(),
        name="s5_scan",
    )(u_g, m_op, q_op, n_op, a_vec)


TAIL_SUB = 4
TAIL_VMEM_LIMIT = 60 * 1024 * 1024


def _s5_tail_kernel(u_ref, y_ref, d_ref, gw_ref, gb_ref, perm_ref, h_ref, mod_ref, g_ref, wo_ref, w1_ref, w2_ref,
                    o_ref):
    lane_blk = _lane_block()
    n_seg = D_MODEL // S5_SEG
    per_seg = S5_SEG // LANES
    mod = mod_ref[...]

    def regroup(j, s):
        cols = []
        for k in range(s * per_seg, (s + 1) * per_seg):
            by_tok = [None] * S5_TOK
            for c2 in range(S5_PAIR):
                rows = slice((j * S5_PAIR + c2) * BATCH, (j * S5_PAIR + c2 + 1) * BATCH)
                for hh in range(S5_CHUNK // STEPS_PER_VREG):
                    groups = [y_ref[k * GROUPS_PER_VREG + r, rows, hh * LANES:(hh + 1) * LANES]
                              for r in range(GROUPS_PER_VREG)]
                    steps = _block_transpose(groups, lane_blk)
                    for m in range(STEPS_PER_VREG):
                        by_tok[c2 * S5_CHUNK + hh * STEPS_PER_VREG + m] = steps[m]
            cols.append(jnp.concatenate(by_tok, axis=0))
        return jnp.concatenate(cols, axis=1)

    def gated_mixer(j, out):
        g_parts, pre = [], None
        ys_next = regroup(j, 0)
        for s in range(n_seg):
            ys = ys_next
            if s + 1 < n_seg:
                ys_next = regroup(j, s + 1)
            cols = slice(s * S5_SEG, (s + 1) * S5_SEG)
            u = u_ref[j * S5_TOK:(j + 1) * S5_TOK, :, cols].reshape(BATCH * S5_TOK, S5_SEG)
            g_seg = jax.nn.gelu(u * d_ref[:, cols] + ys)
            g_parts.append(g_seg)
            part = _dot(g_seg.astype(BF16), gw_ref[cols, :])
            pre = part if pre is None else pre + part
            if s + 1 < n_seg:
                yield
        gate = jax.nn.sigmoid(pre + gb_ref[...])
        gated = (jnp.concatenate(g_parts, axis=1) * gate).astype(BF16)
        out[0] = _dot(perm_ref[...], gated).astype(BF16)
        yield

    def mlp(j, gated):
        tok = slice(j * S5_TOK, (j + 1) * S5_TOK)
        y = _dot(gated, wo_ref[...]).reshape(BATCH, S5_TOK, D_MODEL)
        h1 = h_ref[:, tok, :] + mod[:, 2:3, :] * y
        f = (_rms(h1) * g_ref[...] * (1.0 + mod[:, 4:5, :]) + mod[:, 3:4, :])
        f = f.reshape(BATCH * S5_TOK, D_MODEL).astype(BF16)
        acc = jnp.zeros((BATCH * S5_TOK, D_MODEL), F32)
        for c in range(D_FF // FF_CHUNK):
            hid = jnp.maximum(_dot(f, w1_ref[:, c * FF_CHUNK:(c + 1) * FF_CHUNK]), 0.0)
            acc = acc + _dot((hid * hid).astype(BF16), w2_ref[c * FF_CHUNK:(c + 1) * FF_CHUNK, :])
            if c + 1 < D_FF // FF_CHUNK:
                yield
        o_ref[:, tok, :] = h1 + mod[:, 5:6, :] * acc.reshape(BATCH, S5_TOK, D_MODEL)
        yield

    cur = [None]
    for _ in gated_mixer(0, cur):
        pass
    for j in range(TAIL_SUB):
        nxt = [None]
        streams = [mlp(j, cur[0])] + ([gated_mixer(j + 1, nxt)] if j + 1 < TAIL_SUB else [])
        while streams:
            for gen in list(streams):
                if next(gen, StopIteration) is StopIteration:
                    streams.remove(gen)
        cur = nxt


def _s5_tail(u, y_g, d_skip, glu_w, glu_b, perm_t, h, mods, g, wo, w1_all, w2_all, layer):
    ctx_steps = N_CTX_CHUNKS // (S5_PAIR * TAIL_SUB)
    n_rows = BATCH * S5_TOK
    vec = pl.BlockSpec((1, D_MODEL), lambda p: (0, 0))

    def full(shape):
        return pl.BlockSpec(shape, lambda p: (0,) * len(shape), pipeline_mode=pl.Buffered(1))

    return pl.pallas_call(
        _s5_tail_kernel,
        out_shape=jax.ShapeDtypeStruct((BATCH, SEQ, D_MODEL), F32),
        grid=(S5_STEPS // TAIL_SUB - ctx_steps,),
        in_specs=[pl.BlockSpec((TAIL_SUB * S5_TOK, BATCH, D_MODEL), lambda p: (p + ctx_steps, 0, 0)),
                  pl.BlockSpec((S5_GROUPS, TAIL_SUB * S5_PAIR * BATCH, S5_CW), lambda p: (0, p + ctx_steps, 0)),
                  vec, full((D_MODEL, D_MODEL)), vec, full((n_rows, n_rows)),
                  pl.BlockSpec((BATCH, TAIL_SUB * S5_TOK, D_MODEL), lambda p: (0, p + ctx_steps, 0)),
                  pl.BlockSpec((None, BATCH, N_MOD, D_MODEL), lambda p: (1, 0, 0, 0)),
                  vec, full((D_MODEL, D_MODEL)),
                  pl.BlockSpec((None, D_MODEL, D_FF), lambda p: (layer, 0, 0), pipeline_mode=pl.Buffered(1)),
                  pl.BlockSpec((None, D_FF, D_MODEL), lambda p: (layer, 0, 0), pipeline_mode=pl.Buffered(1))],
        out_specs=pl.BlockSpec((BATCH, TAIL_SUB * S5_TOK, D_MODEL), lambda p: (0, p, 0)),
        compiler_params=pltpu.CompilerParams(vmem_limit_bytes=TAIL_VMEM_LIMIT),
        name="s5_tail_mlp",
    )(u, y_g, d_skip, glu_w, glu_b, perm_t, h, mods, g, wo, w1_all, w2_all)


def _rope_tables():
    rows_n = SEQ // GRID_W
    row = np.repeat(np.arange(rows_n, dtype=np.float64), GRID_W)
    col = np.tile(np.arange(GRID_W, dtype=np.float64), rows_n)
    n_freq = HEAD_DIM // 4
    inv = ROPE_BASE ** (-np.arange(n_freq, dtype=np.float64) / n_freq)
    ang = np.concatenate([row[:, None] * inv, col[:, None] * inv], axis=-1)
    reps = LANES // (HEAD_DIM // 2)
    cos_t = np.tile(np.cos(ang), (1, reps))
    sin_t = np.tile(np.sin(ang), (1, reps))
    sign = np.where((np.arange(LANES) % HEAD_DIM) < HEAD_DIM // 2, -1.0, 1.0)
    cos_t = np.concatenate([np.ones((CTX_LEN, LANES)), cos_t], axis=0)
    sin_s = np.concatenate([np.zeros((CTX_LEN, LANES)), sin_t * sign], axis=0)
    return jnp.asarray(cos_t, F32), jnp.asarray(sin_s, F32)


def _fb_rows(x):
    return jnp.transpose(x, (1, 0, 2)).reshape(S5_GROUPS, 1, 2 * S5_STATE)


def _s5_layout(lam_re, lam_im, log_step, b_re, b_im, c_re, c_im):
    ls = jnp.broadcast_to(log_step[:, :, None], lam_re.shape)
    rows = [_fb_rows(v) for v in (lam_re, lam_im, ls)]

    def bt_of(b):
        return jnp.transpose(b, (1, 3, 0, 2)).reshape(S5_GROUPS, S5_GROUP, 2 * S5_STATE)

    def c_of(c):
        return jnp.transpose(c, (1, 2, 0, 3)).reshape(S5_GROUPS, S5_GROUP, 2 * S5_STATE)

    return rows, [bt_of(b_re), bt_of(b_im)], [c_of(c_re), c_of(c_im)]


def kernel(x, c, ctx, c_ctx, norm1_g, norm2_g, mod_w, mod_b, mlp_w1, mlp_w2, attn_w_in, attn_w_out, a_q_norm, a_k_norm, a_sink, b_q_norm, b_k_norm, b_lq1, b_lk1, b_lq2, b_lk2, b_subln, s5_w_in, s5_lambda_re, s5_lambda_im, s5_log_step, s5_b_re, s5_b_im, s5_c_re, s5_c_im, s5_d, s5_glu_w, s5_glu_b, s5_w_out):
    assert x.shape == (BATCH, SEQ, D_MODEL) and ctx.shape == (BATCH, CTX_LEN, D_MODEL)
    stream = (ctx, x)
    s_rows = jnp.concatenate([c, c_ctx[None], jnp.zeros((MOD_ROWS - BATCH - 1, D_MODEL), F32)], axis=0)
    m_all = _modulation(s_rows, mod_w, mod_b)
    cos_t, sin_s = _rope_tables()
    w1_all, w2_all = mlp_w1.astype(BF16), mlp_w2.astype(BF16)
    e_blk = jnp.asarray(np.kron(np.eye(2 * LANES // HEAD_DIM), np.ones((HEAD_DIM, HEAD_DIM))) / HEAD_DIM, BF16)

    for i in range(DEPTH):
        last = i == DEPTH - 1
        j = i // 2
        m_lat = m_all[i, :BATCH].reshape(BATCH, N_MOD, D_MODEL)
        m_ctx = jnp.broadcast_to(m_all[i, BATCH].reshape(1, N_MOD, D_MODEL), (BATCH, N_MOD, D_MODEL))
        mods = jnp.stack([m_ctx, m_lat])
        g1 = norm1_g[i].reshape(1, D_MODEL)
        g2 = norm2_g[i].reshape(1, D_MODEL)
        if i % 2 == 0:
            lambda_init = 0.8 - 0.6 * math.exp(-0.3 * i)
            gains = jnp.stack([jnp.tile(v[j], LANES // HEAD_DIM) for v in (a_q_norm, a_k_norm, b_q_norm, b_k_norm)])
            qa, k2a, v2a, qb, kb, vb = _attn_in(stream, mods, g1, attn_w_in[j].astype(BF16), gains, cos_t, sin_s, e_blk)
            lpar = jnp.stack([b_lq1[j], b_lk1[j], b_lq2[j], b_lk2[j]])
            ya, yb = _attention(a_sink[j], lpar, b_subln[j].reshape(1, LANES), qa, k2a, v2a, qb, kb, vb, lambda_init)
            if last:
                ya, yb = ya[:, CTX_LEN:], yb[:, CTX_LEN:]
            h = _mix_mlp(stream, ya, yb, mods, g2, attn_w_out[j].astype(BF16), w1_all, w2_all, i, latent_only=last)
        else:
            src = np.arange(BATCH * S5_TOK).reshape(BATCH, S5_TOK).T.reshape(-1)
            perm = jnp.asarray(np.eye(BATCH * S5_TOK, dtype=np.float32)[src], BF16)
            h_all = stream[0] if stream[0].shape[1] == TT else jnp.concatenate(stream, axis=1)
            u, u_g = _s5_in(h_all, mods, g1, perm, s5_w_in[j].astype(BF16))
            ops_in = _s5_layout(s5_lambda_re[j], s5_lambda_im[j], s5_log_step[j], s5_b_re[j], s5_b_im[j],
                                s5_c_re[j], s5_c_im[j])
            m_op, q_op, n_op, a_vec = _s5_ops(*ops_in)
            y_g = _s5_scan(u_g, m_op, q_op, n_op, a_vec)
            assert last, "S5 layers before the last one would also need the context rows of the readout"
            h = _s5_tail(u, y_g, s5_d[j].reshape(1, D_MODEL), s5_glu_w[j].astype(BF16),
                         s5_glu_b[j].reshape(1, D_MODEL), perm.T, h_all, mods, g2, s5_w_out[j].astype(BF16),
                         w1_all, w2_all, i)
        stream = (h, h)
    return h
```

```python
import functools
import math

import jax
import jax.numpy as jnp
import numpy as np
from jax import lax
from jax.experimental import pallas as pl
from jax.experimental.pallas import tpu as pltpu

F32 = jnp.float32
BF16 = jnp.bfloat16

D_MODEL = 1024
BATCH = 8
SEQ = 2048
DEPTH = 2
GRID_W = 64
CTX_LEN = 256
HEAD_DIM = 64
WINDOW = 128
A_Q_HEADS = 8
A_KV_HEADS = 2
B_HEADS = 4
A_Q_W = A_Q_HEADS * HEAD_DIM
A_KV_W = A_KV_HEADS * HEAD_DIM
B_QK_W = B_HEADS * 2 * HEAD_DIM
B_V_W = B_HEADS * 2 * HEAD_DIM
ATTN_IN = A_Q_W + 2 * A_KV_W + 2 * B_QK_W + B_V_W
S5_GROUP = 16
S5_GROUPS = D_MODEL // S5_GROUP
S5_STATE = 64
D_FF = 4 * D_MODEL
ROPE_BASE = 10000.0
EPS = 1e-6
NEG_INF = -1e30
LOG2E = math.log2(math.e)

TT = CTX_LEN + SEQ
TM = 256
N_TILES = TT // TM
LANES = 128
S5_CHUNK = 16
S5_CW = S5_CHUNK * S5_GROUP
N_CHUNKS = TT // S5_CHUNK
N_CTX_CHUNKS = CTX_LEN // S5_CHUNK
VMEM_LIMIT = 56 * 1024 * 1024
N_MOD = 6
MOD_ROWS = 16
MOD_COLS = 2 * D_MODEL
GQA = A_Q_HEADS // A_KV_HEADS


def _dot(a, b):
    return jnp.dot(a, b, preferred_element_type=F32)


def _dot_nt(a, b):
    return lax.dot_general(a, b, (((1,), (1,)), ((), ())), preferred_element_type=F32)


def _rms(x):
    return x * lax.rsqrt(jnp.mean(x * x, axis=-1, keepdims=True) + EPS)


def _modnorm(x, g, shift, scale):
    return _rms(x) * g * (1.0 + scale) + shift


def _params(**kw):
    return pltpu.CompilerParams(vmem_limit_bytes=VMEM_LIMIT, **kw)


def _mod_kernel(s_ref, w_ref, b_ref, o_ref):
    s = s_ref[...]
    s = s * jax.nn.sigmoid(s)
    o_ref[...] = _dot(s.astype(BF16), w_ref[...].astype(BF16)) + b_ref[...]


def _modulation(s_rows, mod_w, mod_b):
    return pl.pallas_call(
        _mod_kernel,
        out_shape=jax.ShapeDtypeStruct((DEPTH, MOD_ROWS, N_MOD * D_MODEL), F32),
        grid=(DEPTH, N_MOD * D_MODEL // MOD_COLS),
        in_specs=[
            pl.BlockSpec((MOD_ROWS, D_MODEL), lambda i, j: (0, 0)),
            pl.BlockSpec((None, D_MODEL, MOD_COLS), lambda i, j: (i, 0, j)),
            pl.BlockSpec((None, 1, MOD_COLS), lambda i, j: (i, 0, j)),
        ],
        out_specs=pl.BlockSpec((None, MOD_ROWS, MOD_COLS), lambda i, j: (i, 0, j)),
        compiler_params=_params(),
        name="modulation",
    )(s_rows, mod_w, mod_b.reshape(DEPTH, 1, N_MOD * D_MODEL))


def _stream_specs(stream, n_sub, h_off):
    first_lat = 1 if stream[0].shape[1] == TT else 0

    def tile(t, k):
        return t * n_sub + k + h_off

    ctx_spec = pl.BlockSpec((None, TM, D_MODEL), lambda b, t: (b, 0, 0))
    lat_specs = [pl.BlockSpec((None, TM, D_MODEL),
                              lambda b, t, k=k: (b, jnp.maximum(tile(t, k) - 1 + first_lat, first_lat), 0))
                 for k in range(n_sub)]
    mod_specs = [pl.BlockSpec((None, None, N_MOD, D_MODEL), lambda b, t, k=k: (jnp.minimum(tile(t, k), 1), b, 0, 0))
                 for k in range(n_sub)]
    return ctx_spec, lat_specs, mod_specs


ATTN_IN_SUB = 3


def _attn_in_kernel(*refs):
    n_sub = ATTN_IN_SUB
    hc_ref = refs[0]
    hl_refs = refs[1:1 + n_sub]
    mod_refs = refs[1 + n_sub:1 + 2 * n_sub]
    (g_ref, w_ref, gain_ref, cos_ref, sin_ref, e_ref,
     qa_ref, k2a_ref, v2a_ref, qb_ref, kb_ref, vb_ref) = refs[1 + 2 * n_sub:]
    e = e_ref[...]
    gains = gain_ref[...]
    lane = lax.broadcasted_iota(jnp.int32, (TM, LANES), 1)
    first_half = (lane & (HEAD_DIM - 1)) < HEAD_DIM // 2
    lo = lane < HEAD_DIM
    q_scale = HEAD_DIM ** -0.5 * LOG2E
    ones = jnp.ones((TM, LANES), BF16)

    def tile_segments(k):
        rows = slice(k * TM, (k + 1) * TM)
        cos_t = cos_ref[rows, :]
        sin_s = sin_ref[rows, :]

        def head_mean_sq(z):
            sq = (z * z).astype(BF16)
            width = z.shape[1]
            if width < 2 * LANES:
                return _dot(sq, e[0:width, 0:width])
            return jnp.concatenate([_dot(sq[:, c:c + 2 * LANES], e) for c in range(0, width, 2 * LANES)], axis=1)

        def norm_rope_chunks(z, gain):
            ms = head_mean_sq(z)
            for c in range(z.shape[1] // LANES):
                cols = slice(c * LANES, (c + 1) * LANES)
                cn = z[:, cols] * lax.rsqrt(ms[:, cols] + EPS) * gain
                r_fwd = pltpu.roll(cn, HEAD_DIM // 2, 1)
                r_bwd = pltpu.roll(cn, LANES - HEAD_DIM // 2, 1)
                yield cn * cos_t + jnp.where(first_half, r_bwd, r_fwd) * sin_s

        def dup_halves(x):
            sw = pltpu.roll(x, HEAD_DIM, 1)
            return jnp.where(lo, x, sw).astype(BF16), jnp.where(lo, sw, x).astype(BF16)

        def finish_q(z, ref, gain):
            for c, chunk in enumerate(norm_rope_chunks(z, gain)):
                ref[rows, c * LANES:(c + 1) * LANES] = (chunk * q_scale).astype(BF16)

        def finish_kv_a(z):
            (k_roped,) = norm_rope_chunks(z[:, 0:LANES], gains[1:2])
            for kvh, dup in enumerate(dup_halves(k_roped)):
                k2a_ref[rows, kvh * LANES:(kvh + 1) * LANES] = dup
            for kvh, dup in enumerate(dup_halves(z[:, LANES:2 * LANES])):
                v2a_ref[rows, 2 * kvh * LANES:(2 * kvh + 1) * LANES] = dup
                v2a_ref[rows, (2 * kvh + 1) * LANES:(2 * kvh + 2) * LANES] = ones

        def finish_kb(z):
            for c, chunk in enumerate(norm_rope_chunks(z, gains[3:4])):
                kb_ref[rows, c * LANES:(c + 1) * LANES] = chunk.astype(BF16)

        def finish_vb(z):
            for hd in range(B_HEADS):
                vb_ref[rows, 2 * hd * LANES:(2 * hd + 1) * LANES] = z[:, hd * LANES:(hd + 1) * LANES].astype(BF16)
                vb_ref[rows, (2 * hd + 1) * LANES:(2 * hd + 2) * LANES] = ones

        return [(A_Q_W, lambda z: finish_q(z, qa_ref, gains[0:1])), (2 * A_KV_W, finish_kv_a),
                (B_QK_W, lambda z: finish_q(z, qb_ref, gains[2:3])), (B_QK_W, finish_kb), (B_V_W, finish_vb)]

    def normed(k):
        mod = mod_refs[k][...]
        x = hl_refs[k][...]
        if k == 0:
            x = jnp.where(pl.program_id(1) == 0, hc_ref[...], x)
        return _modnorm(x, g_ref[...], mod[0:1], mod[1:2]).astype(BF16)

    pending = None
    for k in range(n_sub):
        a = normed(k)
        off = 0
        for width, finish in tile_segments(k):
            z = _dot(a, w_ref[:, off:off + width])
            off += width
            if pending is not None:
                pending[1](pending[0])
            pending = (z, finish)
    pending[1](pending[0])


def _attn_in(stream, mods, g, w_in, gains, cos_t, sin_s, e_blk):
    n_sub = ATTN_IN_SUB
    ctx_spec, lat_specs, mod_specs = _stream_specs(stream, n_sub, 0)

    def tok(width):
        return pl.BlockSpec((None, n_sub * TM, width), lambda b, t: (b, t, 0))

    def full(shape):
        return pl.BlockSpec(shape, lambda b, t: (0,) * len(shape))

    out_shapes = [jax.ShapeDtypeStruct((BATCH, TT, w), BF16)
                  for w in (A_Q_W, 2 * A_KV_W, 4 * A_KV_W, B_QK_W, B_QK_W, 2 * B_V_W)]
    return pl.pallas_call(
        _attn_in_kernel,
        out_shape=out_shapes,
        grid=(BATCH, N_TILES // n_sub),
        in_specs=[
            ctx_spec, *lat_specs, *mod_specs,
            full((1, D_MODEL)), full((D_MODEL, ATTN_IN)), full((4, LANES)),
            pl.BlockSpec((n_sub * TM, LANES), lambda b, t: (t, 0)),
            pl.BlockSpec((n_sub * TM, LANES), lambda b, t: (t, 0)),
            full((2 * LANES, 2 * LANES)),
        ],
        out_specs=[tok(A_Q_W), tok(2 * A_KV_W), tok(4 * A_KV_W), tok(B_QK_W), tok(B_QK_W), tok(2 * B_V_W)],
        compiler_params=_params(),
        name="attn_in_proj",
    )(stream[0], *[stream[1]] * n_sub, *[mods] * n_sub, g, w_in, gains, cos_t, sin_s, e_blk)


QB = 128


def _run_pipelined(items):
    s_next = items[0][0]()
    for i, (_, finish) in enumerate(items):
        s_cur = s_next
        if i + 1 < len(items):
            s_next = items[i + 1][0]()
        finish(s_cur)


def _win_attn_items(t, sink_ref, q_ref, k2_ref, v2_ref, o_ref, with_window):
    lane = lax.broadcasted_iota(jnp.int32, (QB, LANES), 1)
    lo = lane < HEAD_DIM
    rows = GQA * QB
    row = lax.broadcasted_iota(jnp.int32, (rows, 3 * QB), 0)
    col = lax.broadcasted_iota(jnp.int32, (rows, 3 * QB), 1)
    row_head = lax.broadcasted_iota(jnp.int32, (rows, 1), 0) // QB
    zero = jnp.zeros((QB, LANES), BF16)
    blocks = [(qb, g) for qb in range(TM // QB) for g in range(A_KV_HEADS)]

    def window_start(qb):
        n = (t - 1) * (TM // QB) + qb
        ws = jnp.clip((n - 1) * QB, 0, SEQ - 3 * QB)
        return n, ws

    def scores(qb, g):
        pieces = []
        for p in range(2):
            qp = q_ref[qb * QB:(qb + 1) * QB, g * 2 * LANES + p * LANES: g * 2 * LANES + (p + 1) * LANES]
            pieces.append(jnp.where(lo, qp, zero))
            pieces.append(jnp.where(lo, zero, qp))
        qs = jnp.concatenate(pieces, axis=0)
        s_c = _dot_nt(qs, k2_ref[0:CTX_LEN, g * LANES:(g + 1) * LANES])
        if not with_window:
            return s_c, None
        n, ws = window_start(qb)
        kw = k2_ref[pl.ds(pl.multiple_of(ws + CTX_LEN, QB), 3 * QB), g * LANES:(g + 1) * LANES]
        valid = jnp.abs(n * QB + (row & (QB - 1)) - (ws + col)) <= WINDOW
        return s_c, jnp.where(valid, _dot_nt(qs, kw), NEG_INF)

    def finish(qb, g, s):
        s_c, s_w = s
        sk = jnp.full((rows, 1), sink_ref[GQA * g + GQA - 1], F32)
        for hh in range(GQA - 1):
            sk = jnp.where(row_head == hh, sink_ref[GQA * g + hh], sk)
        sk = sk * LOG2E
        m = jnp.maximum(jnp.max(s_c, axis=-1, keepdims=True), sk)
        if s_w is not None:
            m = jnp.maximum(m, jnp.max(s_w, axis=-1, keepdims=True))
        vcols = slice(2 * g * LANES, (2 * g + 2) * LANES)
        pv = _dot(jnp.exp2(s_c - m).astype(BF16), v2_ref[0:CTX_LEN, vcols])
        if s_w is not None:
            _, ws = window_start(qb)
            vw = v2_ref[pl.ds(pl.multiple_of(ws + CTX_LEN, QB), 3 * QB), vcols]
            pv = pv + _dot(jnp.exp2(s_w - m).astype(BF16), vw)
        o = pv[:, 0:LANES] / (pv[:, LANES:2 * LANES] + jnp.exp2(sk - m))
        for p in range(2):
            o_ref[qb * QB:(qb + 1) * QB, g * 2 * LANES + p * LANES: g * 2 * LANES + (p + 1) * LANES] = jnp.where(
                lo, o[2 * p * QB:(2 * p + 1) * QB], o[(2 * p + 1) * QB:(2 * p + 2) * QB]).astype(BF16)

    return [(functools.partial(scores, qb, g), functools.partial(finish, qb, g)) for qb, g in blocks]


DIFF_ROWS = 256


def _diff_attn_items(lam, subln_ref, q_ref, k_ref, v_ref, o_ref, n_keys, lambda_init):
    R = DIFF_ROWS
    lane = lax.broadcasted_iota(jnp.int32, (R, LANES), 1)
    lo = lane < HEAD_DIM
    zero = jnp.zeros((R, LANES), BF16)
    blocks = [(slice(rb * R, (rb + 1) * R), slice(h * LANES, (h + 1) * LANES))
              for h in range(B_HEADS) for rb in range(TM // R)]

    def scores(rows, cols):
        q = q_ref[rows, cols]
        qs = jnp.concatenate([jnp.where(lo, q, zero), jnp.where(lo, zero, q)], axis=0)
        return _dot_nt(qs, k_ref[0:n_keys, cols])

    def finish(rows, cols, s):
        p = jnp.exp2(s - jnp.max(s, axis=-1, keepdims=True)).astype(BF16)
        vcols = slice(2 * cols.start, 2 * cols.stop)
        pv = _dot(p, v_ref[0:n_keys, vcols])
        sm = pv[:, 0:LANES] / pv[:, LANES:2 * LANES]
        y = sm[0:R] - lam * sm[R:2 * R]
        o_ref[rows, cols] = (_rms(y) * subln_ref[...] * (1.0 - lambda_init)).astype(BF16)

    return [(functools.partial(scores, *blk), functools.partial(finish, *blk)) for blk in blocks]


def _attention_kernel(sink_ref, lpar_ref, subln_ref, qa_ref, k2a_ref, v2a_ref, qb_ref, kb_ref, vb_ref,
                      ya_ref, yb_ref, *, lambda_init):
    step = pl.program_id(1)
    lp = lpar_ref[...]
    lam = (jnp.exp(jnp.sum(lp[0:1] * lp[1:2], axis=-1, keepdims=True))
           - jnp.exp(jnp.sum(lp[2:3] * lp[3:4], axis=-1, keepdims=True)) + lambda_init)

    def tile_items(k, is_ctx):
        rows = pl.ds(k * TM, TM)
        t = step * ATT_SUB + k
        win = _win_attn_items(t, sink_ref, qa_ref.at[rows], k2a_ref, v2a_ref, ya_ref.at[rows], not is_ctx)
        dif = _diff_attn_items(lam, subln_ref, qb_ref.at[rows], kb_ref, vb_ref, yb_ref.at[rows],
                               CTX_LEN if is_ctx else TT, lambda_init)
        assert len(win) == len(dif)
        return [item for pair in zip(dif, win) for item in pair]

    pl.when(step == 0)(lambda: _run_pipelined(tile_items(0, True)))
    pl.when(step > 0)(lambda: _run_pipelined(tile_items(0, False)))
    _run_pipelined([item for k in range(1, ATT_SUB) for item in tile_items(k, False)])


ATT_SUB = 3


def _attention(sink, lpar, subln, qa, k2a, v2a, qb, kb, vb, lambda_init):
    def tile(width):
        return pl.BlockSpec((None, ATT_SUB * TM, width), lambda b, t: (b, t, 0))

    def keys(width):
        return pl.BlockSpec((None, TT, width), lambda b, t: (b, 0, 0))

    return pl.pallas_call(
        functools.partial(_attention_kernel, lambda_init=lambda_init),
        out_shape=[jax.ShapeDtypeStruct((BATCH, TT, A_Q_W), BF16), jax.ShapeDtypeStruct((BATCH, TT, B_V_W), BF16)],
        grid=(BATCH, N_TILES // ATT_SUB),
        in_specs=[
            pl.BlockSpec(memory_space=pltpu.SMEM),
            pl.BlockSpec((4, HEAD_DIM), lambda b, t: (0, 0)),
            pl.BlockSpec((1, LANES), lambda b, t: (0, 0)),
            tile(A_Q_W), keys(2 * A_KV_W), keys(4 * A_KV_W),
            tile(B_QK_W), keys(B_QK_W), keys(2 * B_V_W),
        ],
        out_specs=[tile(A_Q_W), tile(B_V_W)],
        compiler_params=_params(),
        name="attention",
    )(sink, lpar, subln, qa, k2a, v2a, qb, kb, vb)


FF_CHUNK = 1024


def _mix_mlp_kernel(*refs, h_off, n_sub):
    hc_ref = refs[0]
    hl_refs, ua_refs, ub_refs, mod_refs = (refs[1 + i * n_sub:1 + (i + 1) * n_sub] for i in range(4))
    g_ref, wo_ref, w1_ref, w2_ref, o_ref = refs[1 + 4 * n_sub:]
    half = D_MODEL // 2

    def prologue(k):
        mod = mod_refs[k][...]
        y = _dot(ua_refs[k][...], wo_ref[0:half, :]) + _dot(ub_refs[k][...], wo_ref[half:D_MODEL, :])
        x = hl_refs[k][...]
        if h_off == 0 and k == 0:
            x = jnp.where(pl.program_id(1) == 0, hc_ref[...], x)
        h1 = x + mod[2:3] * y
        f = _modnorm(h1, g_ref[...], mod[3:4], mod[4:5]).astype(BF16)
        return h1, f, mod[5:6]

    def mlp(k, h1, f, gate):
        acc = jnp.zeros((TM, D_MODEL), F32)
        for c in range(D_FF // FF_CHUNK):
            hid = jnp.maximum(_dot(f, w1_ref[:, c * FF_CHUNK:(c + 1) * FF_CHUNK]), 0.0)
            acc = acc + _dot((hid * hid).astype(BF16), w2_ref[c * FF_CHUNK:(c + 1) * FF_CHUNK, :])
        o_ref[k * TM:(k + 1) * TM, :] = h1 + gate * acc

    nxt = prologue(0)
    for k in range(n_sub):
        cur = nxt
        if k + 1 < n_sub:
            nxt = prologue(k + 1)
        mlp(k, *cur)


def _mix_mlp(stream, ua, ub, mods, g, wo, w1_all, w2_all, layer, *, latent_only):
    n_tiles = SEQ // TM if latent_only else N_TILES
    h_off = N_TILES - n_tiles
    n_sub = 4 if n_tiles % 4 == 0 else 3
    half = D_MODEL // 2
    ub_col = 1 if ub.shape[-1] == D_MODEL else 0
    ctx_spec, lat_specs, mod_specs = _stream_specs(stream, n_sub, h_off)

    def full(shape):
        return pl.BlockSpec(shape, lambda b, t: (0,) * len(shape), pipeline_mode=pl.Buffered(1))

    def mixer_specs(col):
        return [pl.BlockSpec((None, TM, half), lambda b, t, k=k: (b, t * n_sub + k, col)) for k in range(n_sub)]

    return pl.pallas_call(
        functools.partial(_mix_mlp_kernel, h_off=h_off, n_sub=n_sub),
        out_shape=jax.ShapeDtypeStruct((BATCH, n_tiles * TM, D_MODEL), F32),
        grid=(BATCH, n_tiles // n_sub),
        in_specs=[
            ctx_spec, *lat_specs, *mixer_specs(0), *mixer_specs(ub_col), *mod_specs,
            pl.BlockSpec((1, D_MODEL), lambda b, t: (0, 0)),
            full((D_MODEL, D_MODEL)),
            pl.BlockSpec((None, D_MODEL, D_FF), lambda b, t: (layer, 0, 0), pipeline_mode=pl.Buffered(1)),
            pl.BlockSpec((None, D_FF, D_MODEL), lambda b, t: (layer, 0, 0), pipeline_mode=pl.Buffered(1)),
        ],
        out_specs=pl.BlockSpec((None, n_sub * TM, D_MODEL), lambda b, t: (b, t, 0)),
        compiler_params=_params(),
        name="mixer_out_mlp",
    )(stream[0], *[stream[1]] * n_sub, *[ua] * n_sub, *[ub] * n_sub, *[mods] * n_sub, g, wo, w1_all, w2_all)


S5_PAIR = 2
S5_TOK = S5_PAIR * S5_CHUNK
S5_STEPS = N_CHUNKS // S5_PAIR
GROUPS_PER_VREG = LANES // S5_GROUP
STEPS_PER_VREG = LANES // S5_GROUP
S5_SEG = 2 * LANES


def _lane_block():
    return lax.broadcasted_iota(jnp.int32, (BATCH, LANES), 1) // S5_GROUP


def _block_transpose(xs, lane_blk):
    xs = list(xs)
    n = len(xs)
    d = n // 2
    while d:
        low = (lane_blk & d) == 0
        for i in range(n):
            if i & d:
                continue
            a, b = xs[i], xs[i + d]
            xs[i] = jnp.where(low, a, pltpu.roll(b, S5_GROUP * d, 1))
            xs[i + d] = jnp.where(low, pltpu.roll(a, LANES - S5_GROUP * d, 1), b)
        d //= 2
    return xs


S5_SUB = 4


def _s5_in_kernel(*refs):
    h_refs = refs[:S5_SUB]
    mod_ref, g_ref, perm_ref, w_ref, u_ref, z_ref = refs[S5_SUB:]
    mod = mod_ref[...]
    lane_blk = _lane_block()

    def normed(j):
        x = h_refs[j][...]
        a = _rms(x) * g_ref[...] * (1.0 + mod[:, 1:2, :]) + mod[:, 0:1, :]
        a = a.reshape(BATCH * S5_TOK, D_MODEL).astype(BF16)
        return _dot(perm_ref[...], a).astype(BF16)

    def finish(j, s, u_seg):
        u_ref[j * S5_TOK:(j + 1) * S5_TOK, :, s * S5_SEG:(s + 1) * S5_SEG] = u_seg.reshape(S5_TOK, BATCH, S5_SEG)
        for kk in range(S5_SEG // LANES):
            k = s * (S5_SEG // LANES) + kk
            u_col = u_seg[:, kk * LANES:(kk + 1) * LANES]
            for hh in range(S5_CHUNK // STEPS_PER_VREG):
                halves = []
                for c2 in range(S5_PAIR):
                    tok0 = c2 * S5_CHUNK + hh * STEPS_PER_VREG
                    steps = [u_col[(tok0 + m) * BATCH:(tok0 + m + 1) * BATCH, :] for m in range(STEPS_PER_VREG)]
                    halves.append(_block_transpose(steps, lane_blk))
                rows = slice(j * S5_PAIR * BATCH, (j + 1) * S5_PAIR * BATCH)
                for r in range(GROUPS_PER_VREG):
                    z_ref[k * GROUPS_PER_VREG + r, rows, hh * LANES:(hh + 1) * LANES] = (
                        jnp.concatenate([h[r] for h in halves], axis=0).astype(BF16))

    pending = None
    for j in range(S5_SUB):
        a = normed(j)
        for s in range(D_MODEL // S5_SEG):
            u_seg = _dot(a, w_ref[:, s * S5_SEG:(s + 1) * S5_SEG])
            if pending is not None:
                finish(*pending)
            pending = (j, s, u_seg)
    finish(*pending)


def _s5_in(h, mods, g, perm, w_in):
    ctx_steps = N_CTX_CHUNKS // (S5_PAIR * S5_SUB)
    n_rows = BATCH * S5_TOK
    return pl.pallas_call(
        _s5_in_kernel,
        out_shape=[jax.ShapeDtypeStruct((TT, BATCH, D_MODEL), F32),
                   jax.ShapeDtypeStruct((S5_GROUPS, ROWS, S5_CW), BF16)],
        grid=(S5_STEPS // S5_SUB,),
        in_specs=[*[pl.BlockSpec((BATCH, S5_TOK, D_MODEL), lambda p, j=j: (0, p * S5_SUB + j, 0))
                    for j in range(S5_SUB)],
                  pl.BlockSpec((None, BATCH, N_MOD, D_MODEL), lambda p: (jnp.minimum(p // ctx_steps, 1), 0, 0, 0)),
                  pl.BlockSpec((1, D_MODEL), lambda p: (0, 0)),
                  pl.BlockSpec((n_rows, n_rows), lambda p: (0, 0)),
                  pl.BlockSpec((D_MODEL, D_MODEL), lambda p: (0, 0))],
        out_specs=[pl.BlockSpec((S5_SUB * S5_TOK, BATCH, D_MODEL), lambda p: (p, 0, 0)),
                   pl.BlockSpec((S5_GROUPS, S5_SUB * S5_PAIR * BATCH, S5_CW), lambda p: (0, p, 0))],
        compiler_params=_params(),
        name="s5_in_proj",
    )(*[h] * S5_SUB, mods, g, perm, w_in)


OPS_G = 8


def _s5_ops_kernel(*refs):
    for j in range(OPS_G):
        _s5_group_ops(*(r.at[j] for r in refs))


def _s5_group_ops(lr_ref, li_ref, ls_ref, btr_ref, bti_ref, cr_ref, ci_ref, m_ref, q_ref, n_ref, a_ref):
    P2 = 2 * S5_STATE

    def cmul(xr, xi, yr, yi):
        return xr * yr - xi * yi, xr * yi + xi * yr

    lr, li = lr_ref[...], li_ref[...]
    dt = jnp.exp(ls_ref[...])
    mag = jnp.exp(lr * dt)
    ar = mag * jnp.cos(li * dt)
    ai = mag * jnp.sin(li * dt)
    den = lr * lr + li * li
    nr = ar - 1.0
    f_re = (nr * lr + ai * li) / den
    f_im = (ai * lr - nr * li) / den
    bt_re = f_re * btr_ref[...] - f_im * bti_ref[...]
    bt_im = f_re * bti_ref[...] + f_im * btr_ref[...]

    powers = [(jnp.ones_like(ar), jnp.zeros_like(ai))]
    for _ in range(S5_CHUNK):
        powers.append(cmul(*powers[-1], ar, ai))
    row_fwd = lax.broadcasted_iota(jnp.int32, (1, P2), 1) < S5_STATE

    def pow_rows(exp_fwd, exp_bwd):
        return (jnp.where(row_fwd, powers[exp_fwd][0], powers[exp_bwd][0]),
                jnp.where(row_fwd, powers[exp_fwd][1], powers[exp_bwd][1]))

    c_re, c_im = cr_ref[...], ci_ref[...]
    last = S5_CHUNK - 1
    cp_blocks = [cmul(c_re, c_im, *pow_rows(blk, last - blk)) for blk in range(S5_CHUNK)]
    cp_re = jnp.concatenate([b[0] for b in cp_blocks], axis=0)
    cp_im = jnp.concatenate([b[1] for b in cp_blocks], axis=0)
    lane_fwd = lax.broadcasted_iota(jnp.int32, (S5_GROUP, P2), 1) < S5_STATE

    def lag_kernels(keep):
        br = jnp.where(keep, bt_re, 0.0)
        bi = jnp.where(keep, bt_im, 0.0)
        dims = (((1,), (1,)), ((), ()))
        hi = lax.Precision.HIGHEST
        return (lax.dot_general(br, cp_re, dims, precision=hi, preferred_element_type=F32)
                - lax.dot_general(bi, cp_im, dims, precision=hi, preferred_element_type=F32))

    kt_f = lag_kernels(lane_fwd)
    kt_b = lag_kernels(jnp.logical_not(lane_fwd))
    lane_w = lax.broadcasted_iota(jnp.int32, (S5_GROUP, S5_CW), 1)
    for s in range(S5_CHUNK):
        f_part = kt_f if s == 0 else jnp.where(lane_w >= S5_GROUP * s, pltpu.roll(kt_f, S5_GROUP * s, 1), 0.0)
        sh = (S5_GROUP * (s + 1)) % S5_CW
        b_roll = kt_b if sh == 0 else pltpu.roll(kt_b, sh, 1)
        b_part = jnp.where(lane_w < S5_GROUP * (s + 1), b_roll, 0.0)
        m_ref[s * S5_GROUP:(s + 1) * S5_GROUP, :] = (f_part + b_part).astype(BF16)

    for blk in range(S5_CHUNK):
        rows = slice(blk * S5_GROUP, (blk + 1) * S5_GROUP)
        q_re, q_im = cmul(bt_re, bt_im, *pow_rows(last - blk, blk))
        q_ref[rows, 0:P2] = q_re.astype(BF16)
        q_ref[rows, P2:2 * P2] = q_im.astype(BF16)

    for blk in range(S5_CHUNK):
        rows = slice(blk * S5_GROUP, (blk + 1) * S5_GROUP)
        n_re, n_im = cmul(c_re, c_im, *pow_rows(blk + 1, S5_CHUNK - blk))
        n_ref[rows, 0:P2] = n_re.astype(BF16)
        n_ref[rows, P2:2 * P2] = (-n_im).astype(BF16)

    a_ref[0:1, :] = powers[S5_CHUNK][0]
    a_ref[1:2, :] = powers[S5_CHUNK][1]


def _s5_ops(row_params, bt, c_nat):
    P2 = 2 * S5_STATE
    row = pl.BlockSpec((OPS_G, 1, P2), lambda g: (g, 0, 0))
    mat = pl.BlockSpec((OPS_G, S5_GROUP, P2), lambda g: (g, 0, 0))
    sq = pl.BlockSpec((OPS_G, S5_CW, S5_CW), lambda g: (g, 0, 0))
    return pl.pallas_call(
        _s5_ops_kernel,
        out_shape=[jax.ShapeDtypeStruct((S5_GROUPS, S5_CW, S5_CW), BF16)] * 3
        + [jax.ShapeDtypeStruct((S5_GROUPS, 2, P2), F32)],
        grid=(S5_GROUPS // OPS_G,),
        in_specs=[row, row, row, mat, mat, mat, mat],
        out_specs=[sq, sq, sq, pl.BlockSpec((OPS_G, 2, P2), lambda g: (g, 0, 0))],
        compiler_params=_params(),
        name="s5_chunk_operators",
    )(*row_params, *bt, *c_nat)


ROWS = N_CHUNKS * BATCH


SCAN_G = 4


def _s5_scan_kernel(u_ref, m_ref, q_ref, n_ref, a_ref, y_ref, pu_ref, sp_ref):
    S = S5_STATE
    lane = lax.broadcasted_iota(jnp.int32, (BATCH, 2 * S), 1)
    lo = lane < S
    for j in range(SCAN_G):
        pu_ref[j] = _dot(u_ref[j], q_ref[j])
    for j in range(SCAN_G):
        y_ref[j] = _dot(u_ref[j], m_ref[j])
    decay = [(a_ref[j, 0:1, :], a_ref[j, 1:2, :]) for j in range(SCAN_G)]
    zero = jnp.zeros((BATCH, 2 * S), F32)
    state = [(zero, zero)] * SCAN_G
    for k in range(N_CHUNKS):
        cf = k * BATCH
        cb = (N_CTX_CHUNKS - 1 - k if k < N_CTX_CHUNKS else N_CHUNKS + N_CTX_CHUNKS - 1 - k) * BATCH
        for j in range(SCAN_G):
            s_re, s_im = state[j]
            a_re, a_im = decay[j]
            sp_ref[j, cf:cf + BATCH, 0:S] = s_re[:, 0:S]
            sp_ref[j, cb:cb + BATCH, S:2 * S] = s_re[:, S:2 * S]
            sp_ref[j, cf:cf + BATCH, 2 * S:3 * S] = s_im[:, 0:S]
            sp_ref[j, cb:cb + BATCH, 3 * S:4 * S] = s_im[:, S:2 * S]
            x_re = jnp.where(lo, pu_ref[j, cf:cf + BATCH, 0:2 * S], pu_ref[j, cb:cb + BATCH, 0:2 * S])
            x_im = jnp.where(lo, pu_ref[j, cf:cf + BATCH, 2 * S:4 * S], pu_ref[j, cb:cb + BATCH, 2 * S:4 * S])
            state[j] = (a_re * s_re - a_im * s_im + x_re, a_re * s_im + a_im * s_re + x_im)
    for j in range(SCAN_G):
        y_ref[j] += _dot_nt(sp_ref[j].astype(BF16), n_ref[j])


def _s5_scan(u_g, m_op, q_op, n_op, a_vec):
    sq = pl.BlockSpec((SCAN_G, S5_CW, S5_CW), lambda g: (g, 0, 0))
    rows = pl.BlockSpec((SCAN_G, ROWS, S5_CW), lambda g: (g, 0, 0))
    return pl.pallas_call(
        _s5_scan_kernel,
        out_shape=jax.ShapeDtypeStruct((S5_GROUPS, ROWS, S5_CW), F32),
        grid=(S5_GROUPS // SCAN_G,),
        in_specs=[rows, sq, sq, sq, pl.BlockSpec((SCAN_G, 2, 2 * S5_STATE), lambda g: (g, 0, 0))],
        out_specs=rows,
        scratch_shapes=[pltpu.VMEM((SCAN_G, ROWS, S5_CW), F32), pltpu.VMEM((SCAN_G, ROWS, S5_CW), F32)],
        compiler_params=_params(),
        name="s5_scan",
    )(u_g, m_op, q_op, n_op, a_vec)


TAIL_SUB = 4
TAIL_VMEM_LIMIT = 60 * 1024 * 1024


def _s5_tail_kernel(u_ref, y_ref, d_ref, gw_ref, gb_ref, perm_ref, h_ref, mod_ref, g_ref, wo_ref, w1_ref, w2_ref,
                    o_ref):
    lane_blk = _lane_block()
    n_seg = D_MODEL // S5_SEG
    per_seg = S5_SEG // LANES
    mod = mod_ref[...]

    def regroup(j, s):
        cols = []
        for k in range(s * per_seg, (s + 1) * per_seg):
            by_tok = [None] * S5_TOK
            for c2 in range(S5_PAIR):
                rows = slice((j * S5_PAIR + c2) * BATCH, (j * S5_PAIR + c2 + 1) * BATCH)
                for hh in range(S5_CHUNK // STEPS_PER_VREG):
                    groups = [y_ref[k * GROUPS_PER_VREG + r, rows, hh * LANES:(hh + 1) * LANES]
                              for r in range(GROUPS_PER_VREG)]
                    steps = _block_transpose(groups, lane_blk)
                    for m in range(STEPS_PER_VREG):
                        by_tok[c2 * S5_CHUNK + hh * STEPS_PER_VREG + m] = steps[m]
            cols.append(jnp.concatenate(by_tok, axis=0))
        return jnp.concatenate(cols, axis=1)

    def gated_mixer(j, out):
        g_parts, pre = [], None
        ys_next = regroup(j, 0)
        for s in range(n_seg):
            ys = ys_next
            if s + 1 < n_seg:
                ys_next = regroup(j, s + 1)
            cols = slice(s * S5_SEG, (s + 1) * S5_SEG)
            u = u_ref[j * S5_TOK:(j + 1) * S5_TOK, :, cols].reshape(BATCH * S5_TOK, S5_SEG)
            g_seg = jax.nn.gelu(u * d_ref[:, cols] + ys)
            g_parts.append(g_seg)
            part = _dot(g_seg.astype(BF16), gw_ref[cols, :])
            pre = part if pre is None else pre + part
            if s + 1 < n_seg:
                yield
        gate = jax.nn.sigmoid(pre + gb_ref[...])
        gated = (jnp.concatenate(g_parts, axis=1) * gate).astype(BF16)
        out[0] = _dot(perm_ref[...], gated).astype(BF16)
        yield

    def mlp(j, gated):
        tok = slice(j * S5_TOK, (j + 1) * S5_TOK)
        y = _dot(gated, wo_ref[...]).reshape(BATCH, S5_TOK, D_MODEL)
        h1 = h_ref[:, tok, :] + mod[:, 2:3, :] * y
        f = (_rms(h1) * g_ref[...] * (1.0 + mod[:, 4:5, :]) + mod[:, 3:4, :])
        f = f.reshape(BATCH * S5_TOK, D_MODEL).astype(BF16)
        acc = jnp.zeros((BATCH * S5_TOK, D_MODEL), F32)
        for c in range(D_FF // FF_CHUNK):
            hid = jnp.maximum(_dot(f, w1_ref[:, c * FF_CHUNK:(c + 1) * FF_CHUNK]), 0.0)
            acc = acc + _dot((hid * hid).astype(BF16), w2_ref[c * FF_CHUNK:(c + 1) * FF_CHUNK, :])
            if c + 1 < D_FF // FF_CHUNK:
                yield
        o_ref[:, tok, :] = h1 + mod[:, 5:6, :] * acc.reshape(BATCH, S5_TOK, D_MODEL)
        yield

    cur = [None]
    for _ in gated_mixer(0, cur):
        pass
    for j in range(TAIL_SUB):
        nxt = [None]
        streams = [mlp(j, cur[0])] + ([gated_mixer(j + 1, nxt)] if j + 1 < TAIL_SUB else [])
        while streams:
            for gen in list(streams):
                if next(gen, StopIteration) is StopIteration:
                    streams.remove(gen)
        cur = nxt


def _s5_tail(u, y_g, d_skip, glu_w, glu_b, perm_t, h, mods, g, wo, w1_all, w2_all, layer):
    ctx_steps = N_CTX_CHUNKS // (S5_PAIR * TAIL_SUB)
    n_rows = BATCH * S5_TOK
    vec = pl.BlockSpec((1, D_MODEL), lambda p: (0, 0))

    def full(shape):
        return pl.BlockSpec(shape, lambda p: (0,) * len(shape), pipeline_mode=pl.Buffered(1))

    return pl.pallas_call(
        _s5_tail_kernel,
        out_shape=jax.ShapeDtypeStruct((BATCH, SEQ, D_MODEL), F32),
        grid=(S5_STEPS // TAIL_SUB - ctx_steps,),
        in_specs=[pl.BlockSpec((TAIL_SUB * S5_TOK, BATCH, D_MODEL), lambda p: (p + ctx_steps, 0, 0)),
                  pl.BlockSpec((S5_GROUPS, TAIL_SUB * S5_PAIR * BATCH, S5_CW), lambda p: (0, p + ctx_steps, 0)),
                  vec, full((D_MODEL, D_MODEL)), vec, full((n_rows, n_rows)),
                  pl.BlockSpec((BATCH, TAIL_SUB * S5_TOK, D_MODEL), lambda p: (0, p + ctx_steps, 0)),
                  pl.BlockSpec((None, BATCH, N_MOD, D_MODEL), lambda p: (1, 0, 0, 0)),
                  vec, full((D_MODEL, D_MODEL)),
                  pl.BlockSpec((None, D_MODEL, D_FF), lambda p: (layer, 0, 0), pipeline_mode=pl.Buffered(1)),
                  pl.BlockSpec((None, D_FF, D_MODEL), lambda p: (layer, 0, 0), pipeline_mode=pl.Buffered(1))],
        out_specs=pl.BlockSpec((BATCH, TAIL_SUB * S5_TOK, D_MODEL), lambda p: (0, p, 0)),
        compiler_params=pltpu.CompilerParams(vmem_limit_bytes=TAIL_VMEM_LIMIT),
        name="s5_tail_mlp",
    )(u, y_g, d_skip, glu_w, glu_b, perm_t, h, mods, g, wo, w1_all, w2_all)


def _rope_tables():
    rows_n = SEQ // GRID_W
    row = np.repeat(np.arange(rows_n, dtype=np.float64), GRID_W)
    col = np.tile(np.arange(GRID_W, dtype=np.float64), rows_n)
    n_freq = HEAD_DIM // 4
    inv = ROPE_BASE ** (-np.arange(n_freq, dtype=np.float64) / n_freq)
    ang = np.concatenate([row[:, None] * inv, col[:, None] * inv], axis=-1)
    reps = LANES // (HEAD_DIM // 2)
    cos_t = np.tile(np.cos(ang), (1, reps))
    sin_t = np.tile(np.sin(ang), (1, reps))
    sign = np.where((np.arange(LANES) % HEAD_DIM) < HEAD_DIM // 2, -1.0, 1.0)
    cos_t = np.concatenate([np.ones((CTX_LEN, LANES)), cos_t], axis=0)
    sin_s = np.concatenate([np.zeros((CTX_LEN, LANES)), sin_t * sign], axis=0)
    return jnp.asarray(cos_t, F32), jnp.asarray(sin_s, F32)


def _fb_rows(x):
    return jnp.transpose(x, (1, 0, 2)).reshape(S5_GROUPS, 1, 2 * S5_STATE)


def _s5_layout(lam_re, lam_im, log_step, b_re, b_im, c_re, c_im):
    ls = jnp.broadcast_to(log_step[:, :, None], lam_re.shape)
    rows = [_fb_rows(v) for v in (lam_re, lam_im, ls)]

    def bt_of(b):
        return jnp.transpose(b, (1, 3, 0, 2)).reshape(S5_GROUPS, S5_GROUP, 2 * S5_STATE)

    def c_of(c):
        return jnp.transpose(c, (1, 2, 0, 3)).reshape(S5_GROUPS, S5_GROUP, 2 * S5_STATE)

    return rows, [bt_of(b_re), bt_of(b_im)], [c_of(c_re), c_of(c_im)]


def kernel(x, c, ctx, c_ctx, norm1_g, norm2_g, mod_w, mod_b, mlp_w1, mlp_w2, attn_w_in, attn_w_out, a_q_norm, a_k_norm, a_sink, b_q_norm, b_k_norm, b_lq1, b_lk1, b_lq2, b_lk2, b_subln, s5_w_in, s5_lambda_re, s5_lambda_im, s5_log_step, s5_b_re, s5_b_im, s5_c_re, s5_c_im, s5_d, s5_glu_w, s5_glu_b, s5_w_out):
    assert x.shape == (BATCH, SEQ, D_MODEL) and ctx.shape == (BATCH, CTX_LEN, D_MODEL)
    stream = (ctx, x)
    s_rows = jnp.concatenate([c, c_ctx[None], jnp.zeros((MOD_ROWS - BATCH - 1, D_MODEL), F32)], axis=0)
    m_all = _modulation(s_rows, mod_w, mod_b)
    cos_t, sin_s = _rope_tables()
    w1_all, w2_all = mlp_w1.astype(BF16), mlp_w2.astype(BF16)
    e_blk = jnp.asarray(np.kron(np.eye(2 * LANES // HEAD_DIM), np.ones((HEAD_DIM, HEAD_DIM))) / HEAD_DIM, BF16)

    for i in range(DEPTH):
        last = i == DEPTH - 1
        j = i // 2
        m_lat = m_all[i, :BATCH].reshape(BATCH, N_MOD, D_MODEL)
        m_ctx = jnp.broadcast_to(m_all[i, BATCH].reshape(1, N_MOD, D_MODEL), (BATCH, N_MOD, D_MODEL))
        mods = jnp.stack([m_ctx, m_lat])
        g1 = norm1_g[i].reshape(1, D_MODEL)
        g2 = norm2_g[i].reshape(1, D_MODEL)
        if i % 2 == 0:
            lambda_init = 0.8 - 0.6 * math.exp(-0.3 * i)
            gains = jnp.stack([jnp.tile(v[j], LANES // HEAD_DIM) for v in (a_q_norm, a_k_norm, b_q_norm, b_k_norm)])
            qa, k2a, v2a, qb, kb, vb = _attn_in(stream, mods, g1, attn_w_in[j].astype(BF16), gains, cos_t, sin_s, e_blk)
            lpar = jnp.stack([b_lq1[j], b_lk1[j], b_lq2[j], b_lk2[j]])
            ya, yb = _attention(a_sink[j], lpar, b_subln[j].reshape(1, LANES), qa, k2a, v2a, qb, kb, vb, lambda_init)
            if last:
                ya, yb = ya[:, CTX_LEN:], yb[:, CTX_LEN:]
            h = _mix_mlp(stream, ya, yb, mods, g2, attn_w_out[j].astype(BF16), w1_all, w2_all, i, latent_only=last)
        else:
            src = np.arange(BATCH * S5_TOK).reshape(BATCH, S5_TOK).T.reshape(-1)
            perm = jnp.asarray(np.eye(BATCH * S5_TOK, dtype=np.float32)[src], BF16)
            h_all = stream[0] if stream[0].shape[1] == TT else jnp.concatenate(stream, axis=1)
            u, u_g = _s5_in(h_all, mods, g1, perm, s5_w_in[j].astype(BF16))
            ops_in = _s5_layout(s5_lambda_re[j], s5_lambda_im[j], s5_log_step[j], s5_b_re[j], s5_b_im[j],
                                s5_c_re[j], s5_c_im[j])
            m_op, q_op, n_op, a_vec = _s5_ops(*ops_in)
            y_g = _s5_scan(u_g, m_op, q_op, n_op, a_vec)
            assert last, "S5 layers before the last one would also need the context rows of the readout"
            h = _s5_tail(u, y_g, s5_d[j].reshape(1, D_MODEL), s5_glu_w[j].astype(BF16),
                         s5_glu_b[j].reshape(1, D_MODEL), perm.T, h_all, mods, g2, s5_w_out[j].astype(BF16),
                         w1_all, w2_all, i)
        stream = (h, h)
    return h
```

```python
import functools
import math

import jax
import jax.numpy as jnp
import numpy as np
from jax import lax
from jax.experimental import pallas as pl
from jax.experimental.pallas import tpu as pltpu

F32 = jnp.float32
BF16 = jnp.bfloat16

D_MODEL = 1024
BATCH = 8
SEQ = 2048
DEPTH = 2
GRID_W = 64
CTX_LEN = 256
HEAD_DIM = 64
WINDOW = 128
A_Q_HEADS = 8
A_KV_HEADS = 2
B_HEADS = 4
A_Q_W = A_Q_HEADS * HEAD_DIM
A_KV_W = A_KV_HEADS * HEAD_DIM
B_QK_W = B_HEADS * 2 * HEAD_DIM
B_V_W = B_HEADS * 2 * HEAD_DIM
ATTN_IN = A_Q_W + 2 * A_KV_W + 2 * B_QK_W + B_V_W
S5_GROUP = 16
S5_GROUPS = D_MODEL // S5_GROUP
S5_STATE = 64
D_FF = 4 * D_MODEL
ROPE_BASE = 10000.0
EPS = 1e-6
NEG_INF = -1e30
LOG2E = math.log2(math.e)

TT = CTX_LEN + SEQ
TM = 256
N_TILES = TT // TM
LANES = 128
S5_CHUNK = 16
S5_CW = S5_CHUNK * S5_GROUP
N_CHUNKS = TT // S5_CHUNK
N_CTX_CHUNKS = CTX_LEN // S5_CHUNK
VMEM_LIMIT = 56 * 1024 * 1024
N_MOD = 6
MOD_ROWS = 16
MOD_COLS = 2 * D_MODEL
GQA = A_Q_HEADS // A_KV_HEADS


def _dot(a, b):
    return jnp.dot(a, b, preferred_element_type=F32)


def _dot_nt(a, b):
    return lax.dot_general(a, b, (((1,), (1,)), ((), ())), preferred_element_type=F32)


def _rms(x):
    return x * lax.rsqrt(jnp.mean(x * x, axis=-1, keepdims=True) + EPS)


def _modnorm(x, g, shift, scale):
    return _rms(x) * g * (1.0 + scale) + shift


def _params(**kw):
    return pltpu.CompilerParams(vmem_limit_bytes=VMEM_LIMIT, **kw)


def _mod_kernel(s_ref, w_ref, b_ref, o_ref):
    s = s_ref[...]
    s = s * jax.nn.sigmoid(s)
    o_ref[...] = _dot(s.astype(BF16), w_ref[...].astype(BF16)) + b_ref[...]


def _modulation(s_rows, mod_w, mod_b):
    return pl.pallas_call(
        _mod_kernel,
        out_shape=jax.ShapeDtypeStruct((DEPTH, MOD_ROWS, N_MOD * D_MODEL), F32),
        grid=(DEPTH, N_MOD * D_MODEL // MOD_COLS),
        in_specs=[
            pl.BlockSpec((MOD_ROWS, D_MODEL), lambda i, j: (0, 0)),
            pl.BlockSpec((None, D_MODEL, MOD_COLS), lambda i, j: (i, 0, j)),
            pl.BlockSpec((None, 1, MOD_COLS), lambda i, j: (i, 0, j)),
        ],
        out_specs=pl.BlockSpec((None, MOD_ROWS, MOD_COLS), lambda i, j: (i, 0, j)),
        compiler_params=_params(),
        name="modulation",
    )(s_rows, mod_w, mod_b.reshape(DEPTH, 1, N_MOD * D_MODEL))


def _stream_specs(stream, n_sub, h_off):
    first_lat = 1 if stream[0].shape[1] == TT else 0

    def tile(t, k):
        return t * n_sub + k + h_off

    ctx_spec = pl.BlockSpec((None, TM, D_MODEL), lambda b, t: (b, 0, 0))
    lat_specs = [pl.BlockSpec((None, TM, D_MODEL),
                              lambda b, t, k=k: (b, jnp.maximum(tile(t, k) - 1 + first_lat, first_lat), 0))
                 for k in range(n_sub)]
    mod_specs = [pl.BlockSpec((None, None, N_MOD, D_MODEL), lambda b, t, k=k: (jnp.minimum(tile(t, k), 1), b, 0, 0))
                 for k in range(n_sub)]
    return ctx_spec, lat_specs, mod_specs


ATTN_IN_SUB = 3


def _attn_in_kernel(*refs):
    n_sub = ATTN_IN_SUB
    hc_ref = refs[0]
    hl_refs = refs[1:1 + n_sub]
    mod_refs = refs[1 + n_sub:1 + 2 * n_sub]
    (g_ref, w_ref, gain_ref, cos_ref, sin_ref, e_ref,
     qa_ref, k2a_ref, v2a_ref, qb_ref, kb_ref, vb_ref) = refs[1 + 2 * n_sub:]
    e = e_ref[...]
    gains = gain_ref[...]
    lane = lax.broadcasted_iota(jnp.int32, (TM, LANES), 1)
    first_half = (lane & (HEAD_DIM - 1)) < HEAD_DIM // 2
    lo = lane < HEAD_DIM
    q_scale = HEAD_DIM ** -0.5 * LOG2E
    ones = jnp.ones((TM, LANES), BF16)

    def tile_segments(k):
        rows = slice(k * TM, (k + 1) * TM)
        cos_t = cos_ref[rows, :]
        sin_s = sin_ref[rows, :]

        def head_mean_sq(z):
            sq = (z * z).astype(BF16)
            width = z.shape[1]
            if width < 2 * LANES:
                return _dot(sq, e[0:width, 0:width])
            return jnp.concatenate([_dot(sq[:, c:c + 2 * LANES], e) for c in range(0, width, 2 * LANES)], axis=1)

        def norm_rope_chunks(z, gain):
            ms = head_mean_sq(z)
            for c in range(z.shape[1] // LANES):
                cols = slice(c * LANES, (c + 1) * LANES)
                cn = z[:, cols] * lax.rsqrt(ms[:, cols] + EPS) * gain
                r_fwd = pltpu.roll(cn, HEAD_DIM // 2, 1)
                r_bwd = pltpu.roll(cn, LANES - HEAD_DIM // 2, 1)
                yield cn * cos_t + jnp.where(first_half, r_bwd, r_fwd) * sin_s

        def dup_halves(x):
            sw = pltpu.roll(x, HEAD_DIM, 1)
            return jnp.where(lo, x, sw).astype(BF16), jnp.where(lo, sw, x).astype(BF16)

        def finish_q(z, ref, gain):
            for c, chunk in enumerate(norm_rope_chunks(z, gain)):
                ref[rows, c * LANES:(c + 1) * LANES] = (chunk * q_scale).astype(BF16)

        def finish_kv_a(z):
            (k_roped,) = norm_rope_chunks(z[:, 0:LANES], gains[1:2])
            for kvh, dup in enumerate(dup_halves(k_roped)):
                k2a_ref[rows, kvh * LANES:(kvh + 1) * LANES] = dup
            for kvh, dup in enumerate(dup_halves(z[:, LANES:2 * LANES])):
                v2a_ref[rows, 2 * kvh * LANES:(2 * kvh + 1) * LANES] = dup
                v2a_ref[rows, (2 * kvh + 1) * LANES:(2 * kvh + 2) * LANES] = ones

        def finish_kb(z):
            for c, chunk in enumerate(norm_rope_chunks(z, gains[3:4])):
                kb_ref[rows, c * LANES:(c + 1) * LANES] = chunk.astype(BF16)

        def finish_vb(z):
            for hd in range(B_HEADS):
                vb_ref[rows, 2 * hd * LANES:(2 * hd + 1) * LANES] = z[:, hd * LANES:(hd + 1) * LANES].astype(BF16)
                vb_ref[rows, (2 * hd + 1) * LANES:(2 * hd + 2) * LANES] = ones

        return [(A_Q_W, lambda z: finish_q(z, qa_ref, gains[0:1])), (2 * A_KV_W, finish_kv_a),
                (B_QK_W, lambda z: finish_q(z, qb_ref, gains[2:3])), (B_QK_W, finish_kb), (B_V_W, finish_vb)]

    def normed(k):
        mod = mod_refs[k][...]
        x = hl_refs[k][...]
        if k == 0:
            x = jnp.where(pl.program_id(1) == 0, hc_ref[...], x)
        return _modnorm(x, g_ref[...], mod[0:1], mod[1:2]).astype(BF16)

    pending = None
    for k in range(n_sub):
        a = normed(k)
        off = 0
        for width, finish in tile_segments(k):
            z = _dot(a, w_ref[:, off:off + width])
            off += width
            if pending is not None:
                pending[1](pending[0])
            pending = (z, finish)
    pending[1](pending[0])


def _attn_in(stream, mods, g, w_in, gains, cos_t, sin_s, e_blk):
    n_sub = ATTN_IN_SUB
    ctx_spec, lat_specs, mod_specs = _stream_specs(stream, n_sub, 0)

    def tok(width):
        return pl.BlockSpec((None, n_sub * TM, width), lambda b, t: (b, t, 0))

    def full(shape):
        return pl.BlockSpec(shape, lambda b, t: (0,) * len(shape))

    out_shapes = [jax.ShapeDtypeStruct((BATCH, TT, w), BF16)
                  for w in (A_Q_W, 2 * A_KV_W, 4 * A_KV_W, B_QK_W, B_QK_W, 2 * B_V_W)]
    return pl.pallas_call(
        _attn_in_kernel,
        out_shape=out_shapes,
        grid=(BATCH, N_TILES // n_sub),
        in_specs=[
            ctx_spec, *lat_specs, *mod_specs,
            full((1, D_MODEL)), full((D_MODEL, ATTN_IN)), full((4, LANES)),
            pl.BlockSpec((n_sub * TM, LANES), lambda b, t: (t, 0)),
            pl.BlockSpec((n_sub * TM, LANES), lambda b, t: (t, 0)),
            full((2 * LANES, 2 * LANES)),
        ],
        out_specs=[tok(A_Q_W), tok(2 * A_KV_W), tok(4 * A_KV_W), tok(B_QK_W), tok(B_QK_W), tok(2 * B_V_W)],
        compiler_params=_params(),
        name="attn_in_proj",
    )(stream[0], *[stream[1]] * n_sub, *[mods] * n_sub, g, w_in, gains, cos_t, sin_s, e_blk)


QB = 128


def _run_pipelined(items):
    s_next = items[0][0]()
    for i, (_, finish) in enumerate(items):
        s_cur = s_next
        if i + 1 < len(items):
            s_next = items[i + 1][0]()
        finish(s_cur)


def _win_attn_items(t, sink_ref, q_ref, k2_ref, v2_ref, o_ref, with_window):
    lane = lax.broadcasted_iota(jnp.int32, (QB, LANES), 1)
    lo = lane < HEAD_DIM
    rows = GQA * QB
    row = lax.broadcasted_iota(jnp.int32, (rows, 3 * QB), 0)
    col = lax.broadcasted_iota(jnp.int32, (rows, 3 * QB), 1)
    row_head = lax.broadcasted_iota(jnp.int32, (rows, 1), 0) // QB
    zero = jnp.zeros((QB, LANES), BF16)
    blocks = [(qb, g) for qb in range(TM // QB) for g in range(A_KV_HEADS)]

    def window_start(qb):
        n = (t - 1) * (TM // QB) + qb
        ws = jnp.clip((n - 1) * QB, 0, SEQ - 3 * QB)
        return n, ws

    def scores(qb, g):
        pieces = []
        for p in range(2):
            qp = q_ref[qb * QB:(qb + 1) * QB, g * 2 * LANES + p * LANES: g * 2 * LANES + (p + 1) * LANES]
            pieces.append(jnp.where(lo, qp, zero))
            pieces.append(jnp.where(lo, zero, qp))
        qs = jnp.concatenate(pieces, axis=0)
        s_c = _dot_nt(qs, k2_ref[0:CTX_LEN, g * LANES:(g + 1) * LANES])
        if not with_window:
            return s_c, None
        n, ws = window_start(qb)
        kw = k2_ref[pl.ds(pl.multiple_of(ws + CTX_LEN, QB), 3 * QB), g * LANES:(g + 1) * LANES]
        valid = jnp.abs(n * QB + (row & (QB - 1)) - (ws + col)) <= WINDOW
        return s_c, jnp.where(valid, _dot_nt(qs, kw), NEG_INF)

    def finish(qb, g, s):
        s_c, s_w = s
        sk = jnp.full((rows, 1), sink_ref[GQA * g + GQA - 1], F32)
        for hh in range(GQA - 1):
            sk = jnp.where(row_head == hh, sink_ref[GQA * g + hh], sk)
        sk = sk * LOG2E
        m = jnp.maximum(jnp.max(s_c, axis=-1, keepdims=True), sk)
        if s_w is not None:
            m = jnp.maximum(m, jnp.max(s_w, axis=-1, keepdims=True))
        vcols = slice(2 * g * LANES, (2 * g + 2) * LANES)
        pv = _dot(jnp.exp2(s_c - m).astype(BF16), v2_ref[0:CTX_LEN, vcols])
        if s_w is not None:
            _, ws = window_start(qb)
            vw = v2_ref[pl.ds(pl.multiple_of(ws + CTX_LEN, QB), 3 * QB), vcols]
            pv = pv + _dot(jnp.exp2(s_w - m).astype(BF16), vw)
        o = pv[:, 0:LANES] / (pv[:, LANES:2 * LANES] + jnp.exp2(sk - m))
        for p in range(2):
            o_ref[qb * QB:(qb + 1) * QB, g * 2 * LANES + p * LANES: g * 2 * LANES + (p + 1) * LANES] = jnp.where(
                lo, o[2 * p * QB:(2 * p + 1) * QB], o[(2 * p + 1) * QB:(2 * p + 2) * QB]).astype(BF16)

    return [(functools.partial(scores, qb, g), functools.partial(finish, qb, g)) for qb, g in blocks]


DIFF_ROWS = 256


def _diff_attn_items(lam, subln_ref, q_ref, k_ref, v_ref, o_ref, n_keys, lambda_init):
    R = DIFF_ROWS
    lane = lax.broadcasted_iota(jnp.int32, (R, LANES), 1)
    lo = lane < HEAD_DIM
    zero = jnp.zeros((R, LANES), BF16)
    blocks = [(slice(rb * R, (rb + 1) * R), slice(h * LANES, (h + 1) * LANES))
              for h in range(B_HEADS) for rb in range(TM // R)]

    def scores(rows, cols):
        q = q_ref[rows, cols]
        qs = jnp.concatenate([jnp.where(lo, q, zero), jnp.where(lo, zero, q)], axis=0)
        return _dot_nt(qs, k_ref[0:n_keys, cols])

    def finish(rows, cols, s):
        p = jnp.exp2(s - jnp.max(s, axis=-1, keepdims=True)).astype(BF16)
        vcols = slice(2 * cols.start, 2 * cols.stop)
        pv = _dot(p, v_ref[0:n_keys, vcols])
        sm = pv[:, 0:LANES] / pv[:, LANES:2 * LANES]
        y = sm[0:R] - lam * sm[R:2 * R]
        o_ref[rows, cols] = (_rms(y) * subln_ref[...] * (1.0 - lambda_init)).astype(BF16)

    return [(functools.partial(scores, *blk), functools.partial(finish, *blk)) for blk in blocks]


def _attention_kernel(sink_ref, lpar_ref, subln_ref, qa_ref, k2a_ref, v2a_ref, qb_ref, kb_ref, vb_ref,
                      ya_ref, yb_ref, *, lambda_init):
    step = pl.program_id(1)
    lp = lpar_ref[...]
    lam = (jnp.exp(jnp.sum(lp[0:1] * lp[1:2], axis=-1, keepdims=True))
           - jnp.exp(jnp.sum(lp[2:3] * lp[3:4], axis=-1, keepdims=True)) + lambda_init)

    def attend(k, is_ctx):
        rows = pl.ds(k * TM, TM)
        t = step * ATT_SUB + k
        win = _win_attn_items(t, sink_ref, qa_ref.at[rows], k2a_ref, v2a_ref, ya_ref.at[rows], not is_ctx)
        dif = _diff_attn_items(lam, subln_ref, qb_ref.at[rows], kb_ref, vb_ref, yb_ref.at[rows],
                               CTX_LEN if is_ctx else TT, lambda_init)
        assert len(win) == len(dif)
        _run_pipelined([item for pair in zip(dif, win) for item in pair])

    pl.when(step == 0)(functools.partial(attend, 0, True))
    pl.when(step > 0)(functools.partial(attend, 0, False))
    for k in range(1, ATT_SUB):
        attend(k, False)


ATT_SUB = 3


def _attention(sink, lpar, subln, qa, k2a, v2a, qb, kb, vb, lambda_init):
    def tile(width):
        return pl.BlockSpec((None, ATT_SUB * TM, width), lambda b, t: (b, t, 0))

    def keys(width):
        return pl.BlockSpec((None, TT, width), lambda b, t: (b, 0, 0))

    return pl.pallas_call(
        functools.partial(_attention_kernel, lambda_init=lambda_init),
        out_shape=[jax.ShapeDtypeStruct((BATCH, TT, A_Q_W), BF16), jax.ShapeDtypeStruct((BATCH, TT, B_V_W), BF16)],
        grid=(BATCH, N_TILES // ATT_SUB),
        in_specs=[
            pl.BlockSpec(memory_space=pltpu.SMEM),
            pl.BlockSpec((4, HEAD_DIM), lambda b, t: (0, 0)),
            pl.BlockSpec((1, LANES), lambda b, t: (0, 0)),
            tile(A_Q_W), keys(2 * A_KV_W), keys(4 * A_KV_W),
            tile(B_QK_W), keys(B_QK_W), keys(2 * B_V_W),
        ],
        out_specs=[tile(A_Q_W), tile(B_V_W)],
        compiler_params=_params(),
        name="attention",
    )(sink, lpar, subln, qa, k2a, v2a, qb, kb, vb)


FF_CHUNK = 1024


def _mix_mlp_kernel(*refs, h_off, n_sub):
    hc_ref = refs[0]
    hl_refs, ua_refs, ub_refs, mod_refs = (refs[1 + i * n_sub:1 + (i + 1) * n_sub] for i in range(4))
    g_ref, wo_ref, w1_ref, w2_ref, o_ref = refs[1 + 4 * n_sub:]
    half = D_MODEL // 2

    def prologue(k):
        mod = mod_refs[k][...]
        y = _dot(ua_refs[k][...], wo_ref[0:half, :]) + _dot(ub_refs[k][...], wo_ref[half:D_MODEL, :])
        x = hl_refs[k][...]
        if h_off == 0 and k == 0:
            x = jnp.where(pl.program_id(1) == 0, hc_ref[...], x)
        h1 = x + mod[2:3] * y
        f = _modnorm(h1, g_ref[...], mod[3:4], mod[4:5]).astype(BF16)
        return h1, f, mod[5:6]

    def mlp(k, h1, f, gate):
        acc = jnp.zeros((TM, D_MODEL), F32)
        for c in range(D_FF // FF_CHUNK):
            hid = jnp.maximum(_dot(f, w1_ref[:, c * FF_CHUNK:(c + 1) * FF_CHUNK]), 0.0)
            acc = acc + _dot((hid * hid).astype(BF16), w2_ref[c * FF_CHUNK:(c + 1) * FF_CHUNK, :])
        o_ref[k * TM:(k + 1) * TM, :] = h1 + gate * acc

    nxt = prologue(0)
    for k in range(n_sub):
        cur = nxt
        if k + 1 < n_sub:
            nxt = prologue(k + 1)
        mlp(k, *cur)


def _mix_mlp(stream, ua, ub, mods, g, wo, w1_all, w2_all, layer, *, latent_only):
    n_tiles = SEQ // TM if latent_only else N_TILES
    h_off = N_TILES - n_tiles
    n_sub = 4 if n_tiles % 4 == 0 else 3
    half = D_MODEL // 2
    ub_col = 1 if ub.shape[-1] == D_MODEL else 0
    ctx_spec, lat_specs, mod_specs = _stream_specs(stream, n_sub, h_off)

    def full(shape):
        return pl.BlockSpec(shape, lambda b, t: (0,) * len(shape), pipeline_mode=pl.Buffered(1))

    def mixer_specs(col):
        return [pl.BlockSpec((None, TM, half), lambda b, t, k=k: (b, t * n_sub + k, col)) for k in range(n_sub)]

    return pl.pallas_call(
        functools.partial(_mix_mlp_kernel, h_off=h_off, n_sub=n_sub),
        out_shape=jax.ShapeDtypeStruct((BATCH, n_tiles * TM, D_MODEL), F32),
        grid=(BATCH, n_tiles // n_sub),
        in_specs=[
            ctx_spec, *lat_specs, *mixer_specs(0), *mixer_specs(ub_col), *mod_specs,
            pl.BlockSpec((1, D_MODEL), lambda b, t: (0, 0)),
            full((D_MODEL, D_MODEL)),
            pl.BlockSpec((None, D_MODEL, D_FF), lambda b, t: (layer, 0, 0), pipeline_mode=pl.Buffered(1)),
            pl.BlockSpec((None, D_FF, D_MODEL), lambda b, t: (layer, 0, 0), pipeline_mode=pl.Buffered(1)),
        ],
        out_specs=pl.BlockSpec((None, n_sub * TM, D_MODEL), lambda b, t: (b, t, 0)),
        compiler_params=_params(),
        name="mixer_out_mlp",
    )(stream[0], *[stream[1]] * n_sub, *[ua] * n_sub, *[ub] * n_sub, *[mods] * n_sub, g, wo, w1_all, w2_all)


S5_PAIR = 2
S5_TOK = S5_PAIR * S5_CHUNK
S5_STEPS = N_CHUNKS // S5_PAIR
GROUPS_PER_VREG = LANES // S5_GROUP
STEPS_PER_VREG = LANES // S5_GROUP
S5_SEG = 2 * LANES


def _lane_block():
    return lax.broadcasted_iota(jnp.int32, (BATCH, LANES), 1) // S5_GROUP


def _block_transpose(xs, lane_blk):
    xs = list(xs)
    n = len(xs)
    d = n // 2
    while d:
        low = (lane_blk & d) == 0
        for i in range(n):
            if i & d:
                continue
            a, b = xs[i], xs[i + d]
            xs[i] = jnp.where(low, a, pltpu.roll(b, S5_GROUP * d, 1))
            xs[i + d] = jnp.where(low, pltpu.roll(a, LANES - S5_GROUP * d, 1), b)
        d //= 2
    return xs


S5_SUB = 4


def _s5_in_kernel(*refs):
    h_refs = refs[:S5_SUB]
    mod_ref, g_ref, perm_ref, w_ref, u_ref, z_ref = refs[S5_SUB:]
    mod = mod_ref[...]
    lane_blk = _lane_block()

    def normed(j):
        x = h_refs[j][...]
        a = _rms(x) * g_ref[...] * (1.0 + mod[:, 1:2, :]) + mod[:, 0:1, :]
        a = a.reshape(BATCH * S5_TOK, D_MODEL).astype(BF16)
        return _dot(perm_ref[...], a).astype(BF16)

    def finish(j, s, u_seg):
        u_ref[j * S5_TOK:(j + 1) * S5_TOK, :, s * S5_SEG:(s + 1) * S5_SEG] = u_seg.reshape(S5_TOK, BATCH, S5_SEG)
        for kk in range(S5_SEG // LANES):
            k = s * (S5_SEG // LANES) + kk
            u_col = u_seg[:, kk * LANES:(kk + 1) * LANES]
            for hh in range(S5_CHUNK // STEPS_PER_VREG):
                halves = []
                for c2 in range(S5_PAIR):
                    tok0 = c2 * S5_CHUNK + hh * STEPS_PER_VREG
                    steps = [u_col[(tok0 + m) * BATCH:(tok0 + m + 1) * BATCH, :] for m in range(STEPS_PER_VREG)]
                    halves.append(_block_transpose(steps, lane_blk))
                rows = slice(j * S5_PAIR * BATCH, (j + 1) * S5_PAIR * BATCH)
                for r in range(GROUPS_PER_VREG):
                    z_ref[k * GROUPS_PER_VREG + r, rows, hh * LANES:(hh + 1) * LANES] = (
                        jnp.concatenate([h[r] for h in halves], axis=0).astype(BF16))

    pending = None
    for j in range(S5_SUB):
        a = normed(j)
        for s in range(D_MODEL // S5_SEG):
            u_seg = _dot(a, w_ref[:, s * S5_SEG:(s + 1) * S5_SEG])
            if pending is not None:
                finish(*pending)
            pending = (j, s, u_seg)
    finish(*pending)


def _s5_in(h, mods, g, perm, w_in):
    ctx_steps = N_CTX_CHUNKS // (S5_PAIR * S5_SUB)
    n_rows = BATCH * S5_TOK
    return pl.pallas_call(
        _s5_in_kernel,
        out_shape=[jax.ShapeDtypeStruct((TT, BATCH, D_MODEL), F32),
                   jax.ShapeDtypeStruct((S5_GROUPS, ROWS, S5_CW), BF16)],
        grid=(S5_STEPS // S5_SUB,),
        in_specs=[*[pl.BlockSpec((BATCH, S5_TOK, D_MODEL), lambda p, j=j: (0, p * S5_SUB + j, 0))
                    for j in range(S5_SUB)],
                  pl.BlockSpec((None, BATCH, N_MOD, D_MODEL), lambda p: (jnp.minimum(p // ctx_steps, 1), 0, 0, 0)),
                  pl.BlockSpec((1, D_MODEL), lambda p: (0, 0)),
                  pl.BlockSpec((n_rows, n_rows), lambda p: (0, 0)),
                  pl.BlockSpec((D_MODEL, D_MODEL), lambda p: (0, 0))],
        out_specs=[pl.BlockSpec((S5_SUB * S5_TOK, BATCH, D_MODEL), lambda p: (p, 0, 0)),
                   pl.BlockSpec((S5_GROUPS, S5_SUB * S5_PAIR * BATCH, S5_CW), lambda p: (0, p, 0))],
        compiler_params=_params(),
        name="s5_in_proj",
    )(*[h] * S5_SUB, mods, g, perm, w_in)


OPS_G = 8


def _s5_ops_kernel(*refs):
    for j in range(OPS_G):
        _s5_group_ops(*(r.at[j] for r in refs))


def _s5_group_ops(lr_ref, li_ref, ls_ref, btr_ref, bti_ref, cr_ref, ci_ref, m_ref, q_ref, n_ref, a_ref):
    P2 = 2 * S5_STATE

    def cmul(xr, xi, yr, yi):
        return xr * yr - xi * yi, xr * yi + xi * yr

    lr, li = lr_ref[...], li_ref[...]
    dt = jnp.exp(ls_ref[...])
    mag = jnp.exp(lr * dt)
    ar = mag * jnp.cos(li * dt)
    ai = mag * jnp.sin(li * dt)
    den = lr * lr + li * li
    nr = ar - 1.0
    f_re = (nr * lr + ai * li) / den
    f_im = (ai * lr - nr * li) / den
    bt_re = f_re * btr_ref[...] - f_im * bti_ref[...]
    bt_im = f_re * bti_ref[...] + f_im * btr_ref[...]

    powers = [(jnp.ones_like(ar), jnp.zeros_like(ai))]
    for _ in range(S5_CHUNK):
        powers.append(cmul(*powers[-1], ar, ai))
    row_fwd = lax.broadcasted_iota(jnp.int32, (1, P2), 1) < S5_STATE

    def pow_rows(exp_fwd, exp_bwd):
        return (jnp.where(row_fwd, powers[exp_fwd][0], powers[exp_bwd][0]),
                jnp.where(row_fwd, powers[exp_fwd][1], powers[exp_bwd][1]))

    c_re, c_im = cr_ref[...], ci_ref[...]
    last = S5_CHUNK - 1
    cp_blocks = [cmul(c_re, c_im, *pow_rows(blk, last - blk)) for blk in range(S5_CHUNK)]
    cp_re = jnp.concatenate([b[0] for b in cp_blocks], axis=0)
    cp_im = jnp.concatenate([b[1] for b in cp_blocks], axis=0)
    lane_fwd = lax.broadcasted_iota(jnp.int32, (S5_GROUP, P2), 1) < S5_STATE

    def lag_kernels(keep):
        br = jnp.where(keep, bt_re, 0.0)
        bi = jnp.where(keep, bt_im, 0.0)
        dims = (((1,), (1,)), ((), ()))
        hi = lax.Precision.HIGHEST
        return (lax.dot_general(br, cp_re, dims, precision=hi, preferred_element_type=F32)
                - lax.dot_general(bi, cp_im, dims, precision=hi, preferred_element_type=F32))

    kt_f = lag_kernels(lane_fwd)
    kt_b = lag_kernels(jnp.logical_not(lane_fwd))
    lane_w = lax.broadcasted_iota(jnp.int32, (S5_GROUP, S5_CW), 1)
    for s in range(S5_CHUNK):
        f_part = kt_f if s == 0 else jnp.where(lane_w >= S5_GROUP * s, pltpu.roll(kt_f, S5_GROUP * s, 1), 0.0)
        sh = (S5_GROUP * (s + 1)) % S5_CW
        b_roll = kt_b if sh == 0 else pltpu.roll(kt_b, sh, 1)
        b_part = jnp.where(lane_w < S5_GROUP * (s + 1), b_roll, 0.0)
        m_ref[s * S5_GROUP:(s + 1) * S5_GROUP, :] = (f_part + b_part).astype(BF16)

    for blk in range(S5_CHUNK):
        rows = slice(blk * S5_GROUP, (blk + 1) * S5_GROUP)
        q_re, q_im = cmul(bt_re, bt_im, *pow_rows(last - blk, blk))
        q_ref[rows, 0:P2] = q_re.astype(BF16)
        q_ref[rows, P2:2 * P2] = q_im.astype(BF16)

    for blk in range(S5_CHUNK):
        rows = slice(blk * S5_GROUP, (blk + 1) * S5_GROUP)
        n_re, n_im = cmul(c_re, c_im, *pow_rows(blk + 1, S5_CHUNK - blk))
        n_ref[rows, 0:P2] = n_re.astype(BF16)
        n_ref[rows, P2:2 * P2] = (-n_im).astype(BF16)

    a_ref[0:1, :] = powers[S5_CHUNK][0]
    a_ref[1:2, :] = powers[S5_CHUNK][1]


def _s5_ops(row_params, bt, c_nat):
    P2 = 2 * S5_STATE
    row = pl.BlockSpec((OPS_G, 1, P2), lambda g: (g, 0, 0))
    mat = pl.BlockSpec((OPS_G, S5_GROUP, P2), lambda g: (g, 0, 0))
    sq = pl.BlockSpec((OPS_G, S5_CW, S5_CW), lambda g: (g, 0, 0))
    return pl.pallas_call(
        _s5_ops_kernel,
        out_shape=[jax.ShapeDtypeStruct((S5_GROUPS, S5_CW, S5_CW), BF16)] * 3
        + [jax.ShapeDtypeStruct((S5_GROUPS, 2, P2), F32)],
        grid=(S5_GROUPS // OPS_G,),
        in_specs=[row, row, row, mat, mat, mat, mat],
        out_specs=[sq, sq, sq, pl.BlockSpec((OPS_G, 2, P2), lambda g: (g, 0, 0))],
        compiler_params=_params(),
        name="s5_chunk_operators",
    )(*row_params, *bt, *c_nat)


ROWS = N_CHUNKS * BATCH


SCAN_G = 4


def _s5_scan_kernel(lr_ref, li_ref, ls_ref, btr_ref, bti_ref, cr_ref, ci_ref, u_ref, y_ref,
                    pu_ref, sp_ref, m_ref, q_ref, n_ref, a_ref):
    S = S5_STATE
    lane = lax.broadcasted_iota(jnp.int32, (BATCH, 2 * S), 1)
    lo = lane < S

    def operators(j):
        _s5_group_ops(*(r.at[j] for r in (lr_ref, li_ref, ls_ref, btr_ref, bti_ref, cr_ref, ci_ref,
                                          m_ref, q_ref, n_ref, a_ref)))

    operators(0)
    for j in range(SCAN_G):
        if j + 1 < SCAN_G:
            operators(j + 1)
        pu_ref[j] = _dot(u_ref[j], q_ref[j])
        y_ref[j] = _dot(u_ref[j], m_ref[j])
    decay = [(a_ref[j, 0:1, :], a_ref[j, 1:2, :]) for j in range(SCAN_G)]
    zero = jnp.zeros((BATCH, 2 * S), F32)
    state = [(zero, zero)] * SCAN_G
    for k in range(N_CHUNKS):
        cf = k * BATCH
        cb = (N_CTX_CHUNKS - 1 - k if k < N_CTX_CHUNKS else N_CHUNKS + N_CTX_CHUNKS - 1 - k) * BATCH
        for j in range(SCAN_G):
            s_re, s_im = state[j]
            a_re, a_im = decay[j]
            sp_ref[j, cf:cf + BATCH, 0:S] = s_re[:, 0:S]
            sp_ref[j, cb:cb + BATCH, S:2 * S] = s_re[:, S:2 * S]
            sp_ref[j, cf:cf + BATCH, 2 * S:3 * S] = s_im[:, 0:S]
            sp_ref[j, cb:cb + BATCH, 3 * S:4 * S] = s_im[:, S:2 * S]
            x_re = jnp.where(lo, pu_ref[j, cf:cf + BATCH, 0:2 * S], pu_ref[j, cb:cb + BATCH, 0:2 * S])
            x_im = jnp.where(lo, pu_ref[j, cf:cf + BATCH, 2 * S:4 * S], pu_ref[j, cb:cb + BATCH, 2 * S:4 * S])
            state[j] = (a_re * s_re - a_im * s_im + x_re, a_re * s_im + a_im * s_re + x_im)
    for j in range(SCAN_G):
        y_ref[j] += _dot_nt(sp_ref[j].astype(BF16), n_ref[j])


def _s5_scan(u_g, row_params, bt, c_nat):
    P2 = 2 * S5_STATE
    row = pl.BlockSpec((SCAN_G, 1, P2), lambda g: (g, 0, 0))
    mat = pl.BlockSpec((SCAN_G, S5_GROUP, P2), lambda g: (g, 0, 0))
    rows = pl.BlockSpec((SCAN_G, ROWS, S5_CW), lambda g: (g, 0, 0))
    return pl.pallas_call(
        _s5_scan_kernel,
        out_shape=jax.ShapeDtypeStruct((S5_GROUPS, ROWS, S5_CW), F32),
        grid=(S5_GROUPS // SCAN_G,),
        in_specs=[row, row, row, mat, mat, mat, mat, rows],
        out_specs=rows,
        scratch_shapes=[pltpu.VMEM((SCAN_G, ROWS, S5_CW), F32), pltpu.VMEM((SCAN_G, ROWS, S5_CW), F32),
                        pltpu.VMEM((SCAN_G, S5_CW, S5_CW), BF16), pltpu.VMEM((SCAN_G, S5_CW, S5_CW), BF16),
                        pltpu.VMEM((SCAN_G, S5_CW, S5_CW), BF16), pltpu.VMEM((SCAN_G, 2, P2), F32)],
        compiler_params=_params(),
        name="s5_scan",
    )(*row_params, *bt, *c_nat, u_g)


TAIL_SUB = 4
TAIL_VMEM_LIMIT = 60 * 1024 * 1024


def _s5_tail_kernel(u_ref, y_ref, d_ref, gw_ref, gb_ref, perm_ref, h_ref, mod_ref, g_ref, wo_ref, w1_ref, w2_ref,
                    o_ref):
    lane_blk = _lane_block()
    n_seg = D_MODEL // S5_SEG
    per_seg = S5_SEG // LANES
    mod = mod_ref[...]

    def regroup(j, s):
        cols = []
        for k in range(s * per_seg, (s + 1) * per_seg):
            by_tok = [None] * S5_TOK
            for c2 in range(S5_PAIR):
                rows = slice((j * S5_PAIR + c2) * BATCH, (j * S5_PAIR + c2 + 1) * BATCH)
                for hh in range(S5_CHUNK // STEPS_PER_VREG):
                    groups = [y_ref[k * GROUPS_PER_VREG + r, rows, hh * LANES:(hh + 1) * LANES]
                              for r in range(GROUPS_PER_VREG)]
                    steps = _block_transpose(groups, lane_blk)
                    for m in range(STEPS_PER_VREG):
                        by_tok[c2 * S5_CHUNK + hh * STEPS_PER_VREG + m] = steps[m]
            cols.append(jnp.concatenate(by_tok, axis=0))
        return jnp.concatenate(cols, axis=1)

    def gated_mixer(j, out):
        g_parts, pre = [], None
        ys_next = regroup(j, 0)
        for s in range(n_seg):
            ys = ys_next
            if s + 1 < n_seg:
                ys_next = regroup(j, s + 1)
            cols = slice(s * S5_SEG, (s + 1) * S5_SEG)
            u = u_ref[j * S5_TOK:(j + 1) * S5_TOK, :, cols].reshape(BATCH * S5_TOK, S5_SEG)
            g_seg = jax.nn.gelu(u * d_ref[:, cols] + ys)
            g_parts.append(g_seg)
            part = _dot(g_seg.astype(BF16), gw_ref[cols, :])
            pre = part if pre is None else pre + part
            if s + 1 < n_seg:
                yield
        gate = jax.nn.sigmoid(pre + gb_ref[...])
        gated = (jnp.concatenate(g_parts, axis=1) * gate).astype(BF16)
        out[0] = _dot(perm_ref[...], gated).astype(BF16)
        yield

    def mlp(j, gated):
        tok = slice(j * S5_TOK, (j + 1) * S5_TOK)
        y = _dot(gated, wo_ref[...]).reshape(BATCH, S5_TOK, D_MODEL)
        h1 = h_ref[:, tok, :] + mod[:, 2:3, :] * y
        f = (_rms(h1) * g_ref[...] * (1.0 + mod[:, 4:5, :]) + mod[:, 3:4, :])
        f = f.reshape(BATCH * S5_TOK, D_MODEL).astype(BF16)
        acc = jnp.zeros((BATCH * S5_TOK, D_MODEL), F32)
        for c in range(D_FF // FF_CHUNK):
            hid = jnp.maximum(_dot(f, w1_ref[:, c * FF_CHUNK:(c + 1) * FF_CHUNK]), 0.0)
            acc = acc + _dot((hid * hid).astype(BF16), w2_ref[c * FF_CHUNK:(c + 1) * FF_CHUNK, :])
            if c + 1 < D_FF // FF_CHUNK:
                yield
        o_ref[:, tok, :] = h1 + mod[:, 5:6, :] * acc.reshape(BATCH, S5_TOK, D_MODEL)
        yield

    cur = [None]
    for _ in gated_mixer(0, cur):
        pass
    for j in range(TAIL_SUB):
        nxt = [None]
        streams = [mlp(j, cur[0])] + ([gated_mixer(j + 1, nxt)] if j + 1 < TAIL_SUB else [])
        while streams:
            for gen in list(streams):
                if next(gen, StopIteration) is StopIteration:
                    streams.remove(gen)
        cur = nxt


def _s5_tail(u, y_g, d_skip, glu_w, glu_b, perm_t, h, mods, g, wo, w1_all, w2_all, layer):
    ctx_steps = N_CTX_CHUNKS // (S5_PAIR * TAIL_SUB)
    n_rows = BATCH * S5_TOK
    vec = pl.BlockSpec((1, D_MODEL), lambda p: (0, 0))

    def full(shape):
        return pl.BlockSpec(shape, lambda p: (0,) * len(shape), pipeline_mode=pl.Buffered(1))

    return pl.pallas_call(
        _s5_tail_kernel,
        out_shape=jax.ShapeDtypeStruct((BATCH, SEQ, D_MODEL), F32),
        grid=(S5_STEPS // TAIL_SUB - ctx_steps,),
        in_specs=[pl.BlockSpec((TAIL_SUB * S5_TOK, BATCH, D_MODEL), lambda p: (p + ctx_steps, 0, 0)),
                  pl.BlockSpec((S5_GROUPS, TAIL_SUB * S5_PAIR * BATCH, S5_CW), lambda p: (0, p + ctx_steps, 0)),
                  vec, full((D_MODEL, D_MODEL)), vec, full((n_rows, n_rows)),
                  pl.BlockSpec((BATCH, TAIL_SUB * S5_TOK, D_MODEL), lambda p: (0, p + ctx_steps, 0)),
                  pl.BlockSpec((None, BATCH, N_MOD, D_MODEL), lambda p: (1, 0, 0, 0)),
                  vec, full((D_MODEL, D_MODEL)),
                  pl.BlockSpec((None, D_MODEL, D_FF), lambda p: (layer, 0, 0), pipeline_mode=pl.Buffered(1)),
                  pl.BlockSpec((None, D_FF, D_MODEL), lambda p: (layer, 0, 0), pipeline_mode=pl.Buffered(1))],
        out_specs=pl.BlockSpec((BATCH, TAIL_SUB * S5_TOK, D_MODEL), lambda p: (0, p, 0)),
        compiler_params=pltpu.CompilerParams(vmem_limit_bytes=TAIL_VMEM_LIMIT),
        name="s5_tail_mlp",
    )(u, y_g, d_skip, glu_w, glu_b, perm_t, h, mods, g, wo, w1_all, w2_all)


def _rope_tables():
    rows_n = SEQ // GRID_W
    row = np.repeat(np.arange(rows_n, dtype=np.float64), GRID_W)
    col = np.tile(np.arange(GRID_W, dtype=np.float64), rows_n)
    n_freq = HEAD_DIM // 4
    inv = ROPE_BASE ** (-np.arange(n_freq, dtype=np.float64) / n_freq)
    ang = np.concatenate([row[:, None] * inv, col[:, None] * inv], axis=-1)
    reps = LANES // (HEAD_DIM // 2)
    cos_t = np.tile(np.cos(ang), (1, reps))
    sin_t = np.tile(np.sin(ang), (1, reps))
    sign = np.where((np.arange(LANES) % HEAD_DIM) < HEAD_DIM // 2, -1.0, 1.0)
    cos_t = np.concatenate([np.ones((CTX_LEN, LANES)), cos_t], axis=0)
    sin_s = np.concatenate([np.zeros((CTX_LEN, LANES)), sin_t * sign], axis=0)
    return jnp.asarray(cos_t, F32), jnp.asarray(sin_s, F32)


def _fb_rows(x):
    return jnp.transpose(x, (1, 0, 2)).reshape(S5_GROUPS, 1, 2 * S5_STATE)


def _s5_layout(lam_re, lam_im, log_step, b_re, b_im, c_re, c_im):
    ls = jnp.broadcast_to(log_step[:, :, None], lam_re.shape)
    rows = [_fb_rows(v) for v in (lam_re, lam_im, ls)]

    def bt_of(b):
        return jnp.transpose(b, (1, 3, 0, 2)).reshape(S5_GROUPS, S5_GROUP, 2 * S5_STATE)

    def c_of(c):
        return jnp.transpose(c, (1, 2, 0, 3)).reshape(S5_GROUPS, S5_GROUP, 2 * S5_STATE)

    return rows, [bt_of(b_re), bt_of(b_im)], [c_of(c_re), c_of(c_im)]


def kernel(x, c, ctx, c_ctx, norm1_g, norm2_g, mod_w, mod_b, mlp_w1, mlp_w2, attn_w_in, attn_w_out, a_q_norm, a_k_norm, a_sink, b_q_norm, b_k_norm, b_lq1, b_lk1, b_lq2, b_lk2, b_subln, s5_w_in, s5_lambda_re, s5_lambda_im, s5_log_step, s5_b_re, s5_b_im, s5_c_re, s5_c_im, s5_d, s5_glu_w, s5_glu_b, s5_w_out):
    assert x.shape == (BATCH, SEQ, D_MODEL) and ctx.shape == (BATCH, CTX_LEN, D_MODEL)
    stream = (ctx, x)
    s_rows = jnp.concatenate([c, c_ctx[None], jnp.zeros((MOD_ROWS - BATCH - 1, D_MODEL), F32)], axis=0)
    m_all = _modulation(s_rows, mod_w, mod_b)
    cos_t, sin_s = _rope_tables()
    w1_all, w2_all = mlp_w1.astype(BF16), mlp_w2.astype(BF16)
    e_blk = jnp.asarray(np.kron(np.eye(2 * LANES // HEAD_DIM), np.ones((HEAD_DIM, HEAD_DIM))) / HEAD_DIM, BF16)

    for i in range(DEPTH):
        last = i == DEPTH - 1
        j = i // 2
        m_lat = m_all[i, :BATCH].reshape(BATCH, N_MOD, D_MODEL)
        m_ctx = jnp.broadcast_to(m_all[i, BATCH].reshape(1, N_MOD, D_MODEL), (BATCH, N_MOD, D_MODEL))
        mods = jnp.stack([m_ctx, m_lat])
        g1 = norm1_g[i].reshape(1, D_MODEL)
        g2 = norm2_g[i].reshape(1, D_MODEL)
        if i % 2 == 0:
            lambda_init = 0.8 - 0.6 * math.exp(-0.3 * i)
            gains = jnp.stack([jnp.tile(v[j], LANES // HEAD_DIM) for v in (a_q_norm, a_k_norm, b_q_norm, b_k_norm)])
            qa, k2a, v2a, qb, kb, vb = _attn_in(stream, mods, g1, attn_w_in[j].astype(BF16), gains, cos_t, sin_s, e_blk)
            lpar = jnp.stack([b_lq1[j], b_lk1[j], b_lq2[j], b_lk2[j]])
            ya, yb = _attention(a_sink[j], lpar, b_subln[j].reshape(1, LANES), qa, k2a, v2a, qb, kb, vb, lambda_init)
            if last:
                ya, yb = ya[:, CTX_LEN:], yb[:, CTX_LEN:]
            h = _mix_mlp(stream, ya, yb, mods, g2, attn_w_out[j].astype(BF16), w1_all, w2_all, i, latent_only=last)
        else:
            src = np.arange(BATCH * S5_TOK).reshape(BATCH, S5_TOK).T.reshape(-1)
            perm = jnp.asarray(np.eye(BATCH * S5_TOK, dtype=np.float32)[src], BF16)
            h_all = stream[0] if stream[0].shape[1] == TT else jnp.concatenate(stream, axis=1)
            u, u_g = _s5_in(h_all, mods, g1, perm, s5_w_in[j].astype(BF16))
            ops_in = _s5_layout(s5_lambda_re[j], s5_lambda_im[j], s5_log_step[j], s5_b_re[j], s5_b_im[j],
                                s5_c_re[j], s5_c_im[j])
            y_g = _s5_scan(u_g, *ops_in)
            assert last, "S5 layers before the last one would also need the context rows of the readout"
            h = _s5_tail(u, y_g, s5_d[j].reshape(1, D_MODEL), s5_glu_w[j].astype(BF16),
                         s5_glu_b[j].reshape(1, D_MODEL), perm.T, h_all, mods, g2, s5_w_out[j].astype(BF16),
                         w1_all, w2_all, i)
        stream = (h, h)
    return h
```
